```python
import math
import jax, jax.numpy as jnp
from jax import lax
import numpy as np

D_MODEL = 1024
BATCH = 8
SEQ = 8192
DEPTH = 1

D_MIX = D_MODEL
HEAD_DIM = 64
ATTN_WIDTH = D_MIX // 2
N_Q_HEADS = ATTN_WIDTH // HEAD_DIM
N_KV_HEADS = 2
GQA_GROUP = N_Q_HEADS // N_KV_HEADS
WINDOW = 128
BLOCK = 128
CONV_WIDTH = D_MIX - ATTN_WIDTH
CONV_GROUPS = 8
CONV_KERNEL = 31
D_FF = 2816
FFN_KERNEL = 3
Q_COLS = N_Q_HEADS * HEAD_DIM
KV_COLS = N_KV_HEADS * HEAD_DIM
CONV_IN_COLS = 2 * CONV_WIDTH
IN_COLS = Q_COLS + 2 * KV_COLS + CONV_IN_COLS
EPS = 1e-6
NEG_INF = -1e30

kernel_name = "hymba_swa_sink_conformer_conv_hybrid"


def rmsnorm(x, g):
    xf = x.astype(jnp.float32)
    xf = xf * lax.rsqrt(jnp.mean(xf * xf, axis=-1, keepdims=True) + EPS)
    return (xf * g.astype(jnp.float32)).astype(x.dtype)


def group_layernorm(x, g, b, groups):
    shp = x.shape
    xf = x.astype(jnp.float32).reshape(shp[:-1] + (groups, shp[-1] // groups))
    mu = jnp.mean(xf, axis=-1, keepdims=True)
    var = jnp.mean(jnp.square(xf - mu), axis=-1, keepdims=True)
    xf = ((xf - mu) * lax.rsqrt(var + EPS)).reshape(shp)
    return (xf * g.astype(jnp.float32) + b.astype(jnp.float32)).astype(x.dtype)


def causal_dwconv(x, w, b):
    k_len, ch = w.shape
    y = lax.conv_general_dilated(
        x, w[:, None, :].astype(x.dtype), window_strides=(1,), padding=[(k_len - 1, 0)],
        dimension_numbers=("NWC", "WIO", "NWC"), feature_group_count=ch)
    return y + b.astype(x.dtype)


def alibi_slopes(n_heads):
    return 2.0 ** (-8.0 * (np.arange(n_heads, dtype=np.float32) + 1.0) / n_heads)


def swa_gqa_sinks(q, k, v, sinks):
    b_, s_, _, d_ = q.shape
    nb = s_ // BLOCK
    qb = q.reshape(b_, nb, BLOCK, N_KV_HEADS, GQA_GROUP, d_)
    kb = k.reshape(b_, nb, BLOCK, N_KV_HEADS, d_)
    vb = v.reshape(b_, nb, BLOCK, N_KV_HEADS, d_)
    pad = ((0, 0), (1, 0), (0, 0), (0, 0), (0, 0))
    kk = jnp.concatenate([jnp.pad(kb, pad)[:, :-1], kb], axis=2)
    vv = jnp.concatenate([jnp.pad(vb, pad)[:, :-1], vb], axis=2)
    scores = jnp.einsum("bnqkgd,bnskd->bnkgqs", qb, kk).astype(jnp.float32) / math.sqrt(d_)
    qi = jnp.arange(BLOCK)[:, None]
    kj = jnp.arange(2 * BLOCK)[None, :]
    rel = (qi + BLOCK - kj).astype(jnp.float32)
    band = (rel >= 0) & (rel < WINDOW)
    key_ok = (jnp.arange(nb)[:, None] * BLOCK - BLOCK + jnp.arange(2 * BLOCK)[None, :]) >= 0
    valid = band[None] & key_ok[:, None, :]
    slopes = jnp.asarray(alibi_slopes(N_Q_HEADS)).reshape(N_KV_HEADS, GQA_GROUP)
    scores = scores - slopes[:, :, None, None] * rel
    scores = jnp.where(valid[None, :, None, None], scores, NEG_INF)
    sink = jnp.broadcast_to(sinks.astype(jnp.float32).reshape(N_KV_HEADS, GQA_GROUP, 1, 1),
                            scores.shape[:-1] + (1,))
    probs = jax.nn.softmax(jnp.concatenate([scores, sink], axis=-1), axis=-1)[..., :-1]
    out = jnp.einsum("bnkgqs,bnskd->bnqkgd", probs.astype(v.dtype), vv)
    return out.reshape(b_, s_, N_Q_HEADS * d_)


def _fwd_setup_inputs(seed: int = 0) -> dict:
    key = jax.random.key(seed)
    ks = jax.random.split(key, 20)
    f32 = jnp.float32
    nrm = lambda k, shp, sc: jax.random.normal(k, shp, f32) * sc
    return {
        "x": nrm(ks[0], (BATCH, SEQ, D_MODEL), 1.0),
        "mix_norm_gain": 1.0 + nrm(ks[1], (D_MODEL,), 0.01),
        "w_in": nrm(ks[2], (D_MODEL, IN_COLS), D_MODEL ** -0.5),
        "b_in": nrm(ks[3], (IN_COLS,), 0.01),
        "q_norm_gain": 1.0 + nrm(ks[4], (HEAD_DIM,), 0.01),
        "k_norm_gain": 1.0 + nrm(ks[5], (HEAD_DIM,), 0.01),
        "attn_sinks": nrm(ks[6], (N_Q_HEADS,), 0.5),
        "conv_dw_w": nrm(ks[7], (CONV_KERNEL, CONV_WIDTH), CONV_KERNEL ** -0.5),
        "conv_dw_b": nrm(ks[8], (CONV_WIDTH,), 0.01),
        "conv_norm_gain": 1.0 + nrm(ks[9], (CONV_WIDTH,), 0.01),
        "conv_norm_bias": nrm(ks[10], (CONV_WIDTH,), 0.01),
        "w_out": nrm(ks[11], (D_MIX, D_MODEL), D_MIX ** -0.5),
        "b_out": nrm(ks[12], (D_MODEL,), 0.01),
        "ffn_norm_gain": 1.0 + nrm(ks[13], (D_MODEL,), 0.01),
        "w_up": nrm(ks[14], (D_MODEL, 2 * D_FF), D_MODEL ** -0.5),
        "ffn_dw_w": nrm(ks[15], (FFN_KERNEL, 2 * D_FF), FFN_KERNEL ** -0.5),
        "ffn_dw_b": nrm(ks[16], (2 * D_FF,), 0.01),
        "w_down": nrm(ks[17], (D_FF, D_MODEL), D_FF ** -0.5),
    }


def _fwd_reference(x, mix_norm_gain, w_in, b_in, q_norm_gain, k_norm_gain, attn_sinks,
              conv_dw_w, conv_dw_b, conv_norm_gain, conv_norm_bias, w_out, b_out,
              ffn_norm_gain, w_up, ffn_dw_w, ffn_dw_b, w_down):
    b_, s_, _ = x.shape
    for _layer in range(DEPTH):
        h = rmsnorm(x, mix_norm_gain)
        proj = h @ w_in + b_in
        q, k, v, conv_in = jnp.split(
            proj, [Q_COLS, Q_COLS + KV_COLS, Q_COLS + 2 * KV_COLS], axis=-1)
        q = rmsnorm(q.reshape(b_, s_, N_Q_HEADS, HEAD_DIM), q_norm_gain)
        k = rmsnorm(k.reshape(b_, s_, N_KV_HEADS, HEAD_DIM), k_norm_gain)
        v = v.reshape(b_, s_, N_KV_HEADS, HEAD_DIM)
        attn_out = swa_gqa_sinks(q, k, v, attn_sinks)
        a, gate = jnp.split(conv_in, 2, axis=-1)
        c = a * jax.nn.sigmoid(gate)
        c = causal_dwconv(c, conv_dw_w, conv_dw_b)
        c = jax.nn.silu(group_layernorm(c, conv_norm_gain, conv_norm_bias, CONV_GROUPS))
        mixed = jnp.concatenate([attn_out, c], axis=-1)
        x = x + mixed @ w_out + b_out
        h = rmsnorm(x, ffn_norm_gain)
        up = causal_dwconv(h @ w_up, ffn_dw_w, ffn_dw_b)
        g, u = jnp.split(up, 2, axis=-1)
        x = x + (jax.nn.silu(g) * u) @ w_down
    return x


import jax as _jax
import jax.numpy as _jnp

TWIN_FORMAT = 'train_step'
FWD_PARAMS = ['x', 'mix_norm_gain', 'w_in', 'b_in', 'q_norm_gain', 'k_norm_gain', 'attn_sinks', 'conv_dw_w', 'conv_dw_b', 'conv_norm_gain', 'conv_norm_bias', 'w_out', 'b_out', 'ffn_norm_gain', 'w_up', 'ffn_dw_w', 'ffn_dw_b', 'w_down']
TWIN_WEIGHTS = ['mix_norm_gain', 'w_in', 'b_in', 'q_norm_gain', 'k_norm_gain', 'attn_sinks', 'conv_dw_w', 'conv_dw_b', 'conv_norm_gain', 'conv_norm_bias', 'w_out', 'b_out', 'ffn_norm_gain', 'w_up', 'ffn_dw_w', 'ffn_dw_b', 'w_down']
TWIN_DIFF_INPUT = 'x'
TWIN_INPUTS = ['x', 'mix_norm_gain', 'w_in', 'b_in', 'q_norm_gain', 'k_norm_gain', 'attn_sinks', 'conv_dw_w', 'conv_dw_b', 'conv_norm_gain', 'conv_norm_bias', 'w_out', 'b_out', 'ffn_norm_gain', 'w_up', 'ffn_dw_w', 'ffn_dw_b', 'w_down', 'loss_target', 'm_mix_norm_gain', 'm_w_in', 'm_b_in', 'm_q_norm_gain', 'm_k_norm_gain', 'm_attn_sinks', 'm_conv_dw_w', 'm_conv_dw_b', 'm_conv_norm_gain', 'm_conv_norm_bias', 'm_w_out', 'm_b_out', 'm_ffn_norm_gain', 'm_w_up', 'm_ffn_dw_w', 'm_ffn_dw_b', 'm_w_down', 'v_mix_norm_gain', 'v_w_in', 'v_b_in', 'v_q_norm_gain', 'v_k_norm_gain', 'v_attn_sinks', 'v_conv_dw_w', 'v_conv_dw_b', 'v_conv_norm_gain', 'v_conv_norm_bias', 'v_w_out', 'v_b_out', 'v_ffn_norm_gain', 'v_w_up', 'v_ffn_dw_w', 'v_ffn_dw_b', 'v_w_down']
TWIN_OUTPUTS = ['loss', 'grad_x', 'grad_mix_norm_gain', 'grad_w_in', 'grad_b_in', 'grad_q_norm_gain', 'grad_k_norm_gain', 'grad_attn_sinks', 'grad_conv_dw_w', 'grad_conv_dw_b', 'grad_conv_norm_gain', 'grad_conv_norm_bias', 'grad_w_out', 'grad_b_out', 'grad_ffn_norm_gain', 'grad_w_up', 'grad_ffn_dw_w', 'grad_ffn_dw_b', 'grad_w_down', 'delta_mix_norm_gain', 'delta_w_in', 'delta_b_in', 'delta_q_norm_gain', 'delta_k_norm_gain', 'delta_attn_sinks', 'delta_conv_dw_w', 'delta_conv_dw_b', 'delta_conv_norm_gain', 'delta_conv_norm_bias', 'delta_w_out', 'delta_b_out', 'delta_ffn_norm_gain', 'delta_w_up', 'delta_ffn_dw_w', 'delta_ffn_dw_b', 'delta_w_down', 'new_m_mix_norm_gain', 'new_m_w_in', 'new_m_b_in', 'new_m_q_norm_gain', 'new_m_k_norm_gain', 'new_m_attn_sinks', 'new_m_conv_dw_w', 'new_m_conv_dw_b', 'new_m_conv_norm_gain', 'new_m_conv_norm_bias', 'new_m_w_out', 'new_m_b_out', 'new_m_ffn_norm_gain', 'new_m_w_up', 'new_m_ffn_dw_w', 'new_m_ffn_dw_b', 'new_m_w_down', 'new_v_mix_norm_gain', 'new_v_w_in', 'new_v_b_in', 'new_v_q_norm_gain', 'new_v_k_norm_gain', 'new_v_attn_sinks', 'new_v_conv_dw_w', 'new_v_conv_dw_b', 'new_v_conv_norm_gain', 'new_v_conv_norm_bias', 'new_v_w_out', 'new_v_b_out', 'new_v_ffn_norm_gain', 'new_v_w_up', 'new_v_ffn_dw_w', 'new_v_ffn_dw_b', 'new_v_w_down']
TWIN_LEAF_KINDS = {'loss': 'loss', 'grad_x': 'grad_x', 'grad_mix_norm_gain': 'grad_w', 'grad_w_in': 'grad_w', 'grad_b_in': 'grad_w', 'grad_q_norm_gain': 'grad_w', 'grad_k_norm_gain': 'grad_w', 'grad_attn_sinks': 'grad_w', 'grad_conv_dw_w': 'grad_w', 'grad_conv_dw_b': 'grad_w', 'grad_conv_norm_gain': 'grad_w', 'grad_conv_norm_bias': 'grad_w', 'grad_w_out': 'grad_w', 'grad_b_out': 'grad_w', 'grad_ffn_norm_gain': 'grad_w', 'grad_w_up': 'grad_w', 'grad_ffn_dw_w': 'grad_w', 'grad_ffn_dw_b': 'grad_w', 'grad_w_down': 'grad_w', 'delta_mix_norm_gain': 'delta_w', 'delta_w_in': 'delta_w', 'delta_b_in': 'delta_w', 'delta_q_norm_gain': 'delta_w', 'delta_k_norm_gain': 'delta_w', 'delta_attn_sinks': 'delta_w', 'delta_conv_dw_w': 'delta_w', 'delta_conv_dw_b': 'delta_w', 'delta_conv_norm_gain': 'delta_w', 'delta_conv_norm_bias': 'delta_w', 'delta_w_out': 'delta_w', 'delta_b_out': 'delta_w', 'delta_ffn_norm_gain': 'delta_w', 'delta_w_up': 'delta_w', 'delta_ffn_dw_w': 'delta_w', 'delta_ffn_dw_b': 'delta_w', 'delta_w_down': 'delta_w', 'new_m_mix_norm_gain': 'new_m', 'new_m_w_in': 'new_m', 'new_m_b_in': 'new_m', 'new_m_q_norm_gain': 'new_m', 'new_m_k_norm_gain': 'new_m', 'new_m_attn_sinks': 'new_m', 'new_m_conv_dw_w': 'new_m', 'new_m_conv_dw_b': 'new_m', 'new_m_conv_norm_gain': 'new_m', 'new_m_conv_norm_bias': 'new_m', 'new_m_w_out': 'new_m', 'new_m_b_out': 'new_m', 'new_m_ffn_norm_gain': 'new_m', 'new_m_w_up': 'new_m', 'new_m_ffn_dw_w': 'new_m', 'new_m_ffn_dw_b': 'new_m', 'new_m_w_down': 'new_m', 'new_v_mix_norm_gain': 'new_v', 'new_v_w_in': 'new_v', 'new_v_b_in': 'new_v', 'new_v_q_norm_gain': 'new_v', 'new_v_k_norm_gain': 'new_v', 'new_v_attn_sinks': 'new_v', 'new_v_conv_dw_w': 'new_v', 'new_v_conv_dw_b': 'new_v', 'new_v_conv_norm_gain': 'new_v', 'new_v_conv_norm_bias': 'new_v', 'new_v_w_out': 'new_v', 'new_v_b_out': 'new_v', 'new_v_ffn_norm_gain': 'new_v', 'new_v_w_up': 'new_v', 'new_v_ffn_dw_w': 'new_v', 'new_v_ffn_dw_b': 'new_v', 'new_v_w_down': 'new_v'}


def _forward(args):
    return _fwd_reference(*[args[k] for k in FWD_PARAMS])


def _output_shape():
    def fwd():
        inp = _fwd_setup_inputs(0)
        return _fwd_reference(*[inp[k] for k in FWD_PARAMS])
    out = _jax.eval_shape(fwd)
    return out.shape, out.dtype

N_MICROBATCH = 1
ADAM_LR = 0.001
ADAM_B1 = 0.9
ADAM_B2 = 0.999
ADAM_EPS = 1e-08
ADAM_WD = 0.01
ADAM_STEP = 10
PER_EXAMPLE_BATCH_AXIS = {'x': 0, 'loss_target': 0}
SHARED_INPUTS = []
_WEIGHT_DTYPES = {'mix_norm_gain': _jnp.float32, 'w_in': _jnp.float32, 'b_in': _jnp.float32, 'q_norm_gain': _jnp.float32, 'k_norm_gain': _jnp.float32, 'attn_sinks': _jnp.float32, 'conv_dw_w': _jnp.float32, 'conv_dw_b': _jnp.float32, 'conv_norm_gain': _jnp.float32, 'conv_norm_bias': _jnp.float32, 'w_out': _jnp.float32, 'b_out': _jnp.float32, 'ffn_norm_gain': _jnp.float32, 'w_up': _jnp.float32, 'ffn_dw_w': _jnp.float32, 'ffn_dw_b': _jnp.float32, 'w_down': _jnp.float32}
MOMENT_SCALE = {'mix_norm_gain': 1.842434e+00, 'w_in': 3.442260e-01, 'b_in': 1.335926e+01, 'q_norm_gain': 1.717602e+01, 'k_norm_gain': 1.712246e+01, 'attn_sinks': 6.196932e+01, 'conv_dw_w': 1.087572e+00, 'conv_dw_b': 2.123893e+01, 'conv_norm_gain': 2.963349e+01, 'conv_norm_bias': 2.254686e+01, 'w_out': 3.530829e+00, 'b_out': 2.475399e+01, 'ffn_norm_gain': 5.399320e+01, 'w_up': 1.322521e+00, 'ffn_dw_w': 7.845491e+00, 'ffn_dw_b': 7.072740e+00, 'w_down': 7.522953e-01}


def _to_microbatches(a, axis):
    t = _jnp.moveaxis(a, axis, 0)
    t = t.reshape((N_MICROBATCH, t.shape[0] // N_MICROBATCH) + t.shape[1:])
    return _jnp.moveaxis(t, 1, axis + 1)


def setup_inputs(seed: int = 0) -> dict:
    inp = _fwd_setup_inputs(seed)
    key = _jax.random.fold_in(_jax.random.key(seed), 7919)
    shape, _ = _output_shape()
    out = dict(inp)
    out["loss_target"] = _jax.random.normal(_jax.random.fold_in(key, 0), shape, _jnp.float32)
    for i, name in enumerate(TWIN_WEIGHTS):
        w = inp[name].astype(_jnp.float32)
        if MOMENT_SCALE is None:
            s = _jnp.sqrt(_jnp.mean(_jnp.square(w)) + 1e-30)
        else:
            s = MOMENT_SCALE[name]
        km, kv = _jax.random.split(_jax.random.fold_in(key, i + 1))
        out[name] = w
        out["m_" + name] = s * _jax.random.normal(km, w.shape, _jnp.float32)
        out["v_" + name] = (s * s) * _jax.random.uniform(kv, w.shape, _jnp.float32, 0.5, 1.5)
    if N_MICROBATCH > 1:
        for name, axis in PER_EXAMPLE_BATCH_AXIS.items():
            out[name] = _to_microbatches(out[name], axis)
    return {'x': out['x'], 'mix_norm_gain': out['mix_norm_gain'], 'w_in': out['w_in'], 'b_in': out['b_in'], 'q_norm_gain': out['q_norm_gain'], 'k_norm_gain': out['k_norm_gain'], 'attn_sinks': out['attn_sinks'], 'conv_dw_w': out['conv_dw_w'], 'conv_dw_b': out['conv_dw_b'], 'conv_norm_gain': out['conv_norm_gain'], 'conv_norm_bias': out['conv_norm_bias'], 'w_out': out['w_out'], 'b_out': out['b_out'], 'ffn_norm_gain': out['ffn_norm_gain'], 'w_up': out['w_up'], 'ffn_dw_w': out['ffn_dw_w'], 'ffn_dw_b': out['ffn_dw_b'], 'w_down': out['w_down'], 'loss_target': out['loss_target'], 'm_mix_norm_gain': out['m_mix_norm_gain'], 'm_w_in': out['m_w_in'], 'm_b_in': out['m_b_in'], 'm_q_norm_gain': out['m_q_norm_gain'], 'm_k_norm_gain': out['m_k_norm_gain'], 'm_attn_sinks': out['m_attn_sinks'], 'm_conv_dw_w': out['m_conv_dw_w'], 'm_conv_dw_b': out['m_conv_dw_b'], 'm_conv_norm_gain': out['m_conv_norm_gain'], 'm_conv_norm_bias': out['m_conv_norm_bias'], 'm_w_out': out['m_w_out'], 'm_b_out': out['m_b_out'], 'm_ffn_norm_gain': out['m_ffn_norm_gain'], 'm_w_up': out['m_w_up'], 'm_ffn_dw_w': out['m_ffn_dw_w'], 'm_ffn_dw_b': out['m_ffn_dw_b'], 'm_w_down': out['m_w_down'], 'v_mix_norm_gain': out['v_mix_norm_gain'], 'v_w_in': out['v_w_in'], 'v_b_in': out['v_b_in'], 'v_q_norm_gain': out['v_q_norm_gain'], 'v_k_norm_gain': out['v_k_norm_gain'], 'v_attn_sinks': out['v_attn_sinks'], 'v_conv_dw_w': out['v_conv_dw_w'], 'v_conv_dw_b': out['v_conv_dw_b'], 'v_conv_norm_gain': out['v_conv_norm_gain'], 'v_conv_norm_bias': out['v_conv_norm_bias'], 'v_w_out': out['v_w_out'], 'v_b_out': out['v_b_out'], 'v_ffn_norm_gain': out['v_ffn_norm_gain'], 'v_w_up': out['v_w_up'], 'v_ffn_dw_w': out['v_ffn_dw_w'], 'v_ffn_dw_b': out['v_ffn_dw_b'], 'v_w_down': out['v_w_down']}


def _loss(weights, diff, rest, loss_target):
    with _jax.named_scope("forward"):
        args = {**rest, TWIN_DIFF_INPUT: diff, **{k: w.astype(_WEIGHT_DTYPES[k]) for k, w in weights.items()}}
        y = _forward(args)
    with _jax.named_scope("loss_head"):
        err = _jnp.square(y.astype(_jnp.float32) - loss_target)
        return 0.5 * _jnp.sum(_jnp.mean(err, axis=-1)) if err.ndim else 0.5 * err


def _adamw(w, g, m, v):
    m = ADAM_B1 * m + (1.0 - ADAM_B1) * g
    v = ADAM_B2 * v + (1.0 - ADAM_B2) * _jnp.square(g)
    m_hat = m / (1.0 - ADAM_B1 ** ADAM_STEP)
    v_hat = v / (1.0 - ADAM_B2 ** ADAM_STEP)
    delta = -ADAM_LR * (m_hat / (_jnp.sqrt(v_hat) + ADAM_EPS) + ADAM_WD * w)
    return delta, m, v


def reference(x, mix_norm_gain, w_in, b_in, q_norm_gain, k_norm_gain, attn_sinks, conv_dw_w, conv_dw_b, conv_norm_gain, conv_norm_bias, w_out, b_out, ffn_norm_gain, w_up, ffn_dw_w, ffn_dw_b, w_down, loss_target, m_mix_norm_gain, m_w_in, m_b_in, m_q_norm_gain, m_k_norm_gain, m_attn_sinks, m_conv_dw_w, m_conv_dw_b, m_conv_norm_gain, m_conv_norm_bias, m_w_out, m_b_out, m_ffn_norm_gain, m_w_up, m_ffn_dw_w, m_ffn_dw_b, m_w_down, v_mix_norm_gain, v_w_in, v_b_in, v_q_norm_gain, v_k_norm_gain, v_attn_sinks, v_conv_dw_w, v_conv_dw_b, v_conv_norm_gain, v_conv_norm_bias, v_w_out, v_b_out, v_ffn_norm_gain, v_w_up, v_ffn_dw_w, v_ffn_dw_b, v_w_down):
    given = dict(x=x, mix_norm_gain=mix_norm_gain, w_in=w_in, b_in=b_in, q_norm_gain=q_norm_gain, k_norm_gain=k_norm_gain, attn_sinks=attn_sinks, conv_dw_w=conv_dw_w, conv_dw_b=conv_dw_b, conv_norm_gain=conv_norm_gain, conv_norm_bias=conv_norm_bias, w_out=w_out, b_out=b_out, ffn_norm_gain=ffn_norm_gain, w_up=w_up, ffn_dw_w=ffn_dw_w, ffn_dw_b=ffn_dw_b, w_down=w_down, loss_target=loss_target, m_mix_norm_gain=m_mix_norm_gain, m_w_in=m_w_in, m_b_in=m_b_in, m_q_norm_gain=m_q_norm_gain, m_k_norm_gain=m_k_norm_gain, m_attn_sinks=m_attn_sinks, m_conv_dw_w=m_conv_dw_w, m_conv_dw_b=m_conv_dw_b, m_conv_norm_gain=m_conv_norm_gain, m_conv_norm_bias=m_conv_norm_bias, m_w_out=m_w_out, m_b_out=m_b_out, m_ffn_norm_gain=m_ffn_norm_gain, m_w_up=m_w_up, m_ffn_dw_w=m_ffn_dw_w, m_ffn_dw_b=m_ffn_dw_b, m_w_down=m_w_down, v_mix_norm_gain=v_mix_norm_gain, v_w_in=v_w_in, v_b_in=v_b_in, v_q_norm_gain=v_q_norm_gain, v_k_norm_gain=v_k_norm_gain, v_attn_sinks=v_attn_sinks, v_conv_dw_w=v_conv_dw_w, v_conv_dw_b=v_conv_dw_b, v_conv_norm_gain=v_conv_norm_gain, v_conv_norm_bias=v_conv_norm_bias, v_w_out=v_w_out, v_b_out=v_b_out, v_ffn_norm_gain=v_ffn_norm_gain, v_w_up=v_w_up, v_ffn_dw_w=v_ffn_dw_w, v_ffn_dw_b=v_ffn_dw_b, v_w_down=v_w_down)
    weights = {n: given[n] for n in TWIN_WEIGHTS}
    shared = {n: given[n] for n in SHARED_INPUTS}
    per_example = {n: given[n] for n in ['x']}
    grad_fn = _jax.value_and_grad(_loss, argnums=(0, 1))

    def one_microbatch(ex, loss_target):
        ex = dict(ex)
        diff = ex.pop(TWIN_DIFF_INPUT)
        return grad_fn(weights, diff, {**shared, **ex}, loss_target)

    if N_MICROBATCH == 1:
        loss, (grad_w, grad_x) = one_microbatch(per_example, given["loss_target"])
    else:
        def body(carry, xs):
            loss_sum, grad_sum = carry
            l_k, (gw_k, gx_k) = one_microbatch(xs[0], xs[1])
            with _jax.named_scope("update"):
                return (loss_sum + l_k, _jax.tree.map(_jnp.add, grad_sum, gw_k)), gx_k

        init = (_jnp.zeros((), _jnp.float32), _jax.tree.map(_jnp.zeros_like, weights))
        (loss, grad_w), grad_x = _jax.lax.scan(body, init, (per_example, given["loss_target"]))
    with _jax.named_scope("update"):
        delta_w, new_m, new_v = {}, {}, {}
        for n in TWIN_WEIGHTS:
            delta_w[n], new_m[n], new_v[n] = _adamw(weights[n], grad_w[n], given["m_" + n], given["v_" + n])
    return (loss, grad_x, *[grad_w[n] for n in TWIN_WEIGHTS], *[delta_w[n] for n in TWIN_WEIGHTS],
            *[new_m[n] for n in TWIN_WEIGHTS], *[new_v[n] for n in TWIN_WEIGHTS])
```

```python
import functools

import jax
import jax.numpy as jnp
from jax import lax
from jax.experimental import pallas as pl
from jax.experimental.pallas import tpu as pltpu

F32 = jnp.float32
BF16 = jnp.bfloat16
MESH = pl.DeviceIdType.MESH

D_MODEL = 1024
HEAD_DIM = 64
N_HEADS = 8
Q_COLS = 512
QKV_COLS = 768
CONV_W = 512
CONV_K = 31
IN_COLS = 1792
D_FF = 2816
FFN_CB = 1408
BLOCK = 128
LANES = 128
EPS = 1e-6
NEG_INF = -1e30
SLOPES = tuple(float(2.0 ** (-(h + 1.0))) for h in range(N_HEADS))
HALO = 32
FHALO = 16
ROW_CHUNK = 64
VMEM_LIMIT = 56 * 1024 * 1024

ADAM_LR = 0.001
ADAM_B1 = 0.9
ADAM_B2 = 0.999
ADAM_EPS = 1e-08
ADAM_WD = 0.01
ADAM_STEP = 10


def _params(sem=None):
    kw = dict(vmem_limit_bytes=VMEM_LIMIT)
    if sem is not None:
        kw["dimension_semantics"] = sem
    return pltpu.CompilerParams(**kw)


def _token_tile(t):
    return 512 if t % 512 == 0 and t >= 2048 else 128


def _sig(v):
    return 1.0 / (1.0 + jnp.exp(-v))


def _lo_mask():
    return lax.broadcasted_iota(jnp.int32, (1, LANES), 1) < HEAD_DIM


def _half_mean(v, lo):
    s_lo = jnp.sum(jnp.where(lo, v, 0.0), axis=-1, keepdims=True)
    s_hi = jnp.sum(jnp.where(lo, 0.0, v), axis=-1, keepdims=True)
    return jnp.where(lo, s_lo, s_hi) * (1.0 / HEAD_DIM)


def _dot(a, b):
    return jnp.dot(a, b, preferred_element_type=F32)


def _dot_nt(a, b):
    return lax.dot_general(a, b, (((1,), (1,)), ((), ())), preferred_element_type=F32)


def _dot_tn(a, b):
    return lax.dot_general(a, b, (((0,), (0,)), ((), ())), preferred_element_type=F32)


def _full(shape):
    nd = len(shape)
    return pl.BlockSpec(shape, lambda *_: (0,) * nd)


def _rows_to_tile(rows, n_rows):
    c = rows[0].shape[-1]
    rid = lax.broadcasted_iota(jnp.int32, (n_rows, c), 0)
    out = jnp.zeros((n_rows, c), F32)
    for k, r in enumerate(rows):
        out = jnp.where(rid == k, r, out)
    return out


def _fwd_in(x, g_mix, w_in, b_in, gq2, gk2):
    t = x.shape[0]
    tt = _token_tile(t)

    def body(x_ref, g_ref, w_ref, b_ref, gq_ref, gk_ref, h1_ref, pq_ref, pc_ref, qkvn_ref):
        xv = x_ref[...]
        r = lax.rsqrt(jnp.mean(xv * xv, axis=-1, keepdims=True) + EPS)
        h = (xv * r * g_ref[...]).astype(BF16)
        h1_ref[...] = h
        proj = _dot(h, w_ref[...]) + b_ref[...]
        pq_ref[...] = proj[:, :QKV_COLS]
        pc_ref[...] = proj[:, QKV_COLS:]
        lo = _lo_mask()
        for p in range(5):
            seg = proj[:, p * LANES:(p + 1) * LANES]
            rr = lax.rsqrt(_half_mean(seg * seg, lo) + EPS)
            gain = gq_ref[...] if p < 4 else gk_ref[...]
            qkvn_ref[:, p * LANES:(p + 1) * LANES] = (seg * rr * gain).astype(BF16)
        qkvn_ref[:, 640:768] = proj[:, 640:768].astype(BF16)

    return pl.pallas_call(
        body, name="fwd_in", grid=(t // tt,),
        in_specs=[pl.BlockSpec((tt, D_MODEL), lambda i: (i, 0)), _full((1, D_MODEL)), _full((D_MODEL, IN_COLS)),
                  _full((1, IN_COLS)), _full((1, LANES)), _full((1, LANES))],
        out_specs=[pl.BlockSpec((tt, D_MODEL), lambda i: (i, 0)), pl.BlockSpec((tt, QKV_COLS), lambda i: (i, 0)),
                   pl.BlockSpec((tt, 2 * CONV_W), lambda i: (i, 0)), pl.BlockSpec((tt, QKV_COLS), lambda i: (i, 0))],
        out_shape=[jax.ShapeDtypeStruct((t, D_MODEL), BF16), jax.ShapeDtypeStruct((t, QKV_COLS), F32),
                   jax.ShapeDtypeStruct((t, 2 * CONV_W), F32), jax.ShapeDtypeStruct((t, QKV_COLS), BF16)],
        compiler_params=_params(("parallel",)),
    )(x, g_mix, w_in, b_in, gq2, gk2)


def _attn_keys(i, kvp_ref, kvc_ref):
    kv = jnp.concatenate([kvp_ref[...], kvc_ref[...]], axis=0)
    kp = kv[:, :LANES]
    vp = kv[:, LANES:]
    kr = pltpu.roll(kp.astype(F32), HEAD_DIM, 1).astype(BF16)
    vr = pltpu.roll(vp.astype(F32), HEAD_DIM, 1).astype(BF16)
    qi = lax.broadcasted_iota(jnp.int32, (BLOCK, 2 * BLOCK), 0)
    kj = lax.broadcasted_iota(jnp.int32, (BLOCK, 2 * BLOCK), 1)
    rel = qi + BLOCK - kj
    valid = (rel >= 0) & (rel < BLOCK) & ((kj >= BLOCK) | (i > 0))
    return kp, vp, kr, vr, rel.astype(F32), valid


def _head_scores(q_ref, h, lo, kp, kr, relf, valid):
    p, e, kk = h // 2, h % 2, h // 4
    qp = q_ref[:, p * LANES:(p + 1) * LANES]
    hm = lo if e == 0 else jnp.logical_not(lo)
    qm = jnp.where(hm, qp, jnp.zeros_like(qp))
    s = _dot_nt(qm, kp if kk == e else kr)
    s = s * 0.125 - SLOPES[h] * relf
    return jnp.where(valid, s, NEG_INF), qm, hm, kk == e


def _attn_fwd(qkvn, sinks):
    t = qkvn.shape[0]
    nb = t // BLOCK

    def body(sink_ref, q_ref, kvc_ref, kvp_ref, o_ref, lse_ref):
        i = pl.program_id(0)
        kp, vp, kr, vr, relf, valid = _attn_keys(i, kvp_ref, kvc_ref)
        lo = _lo_mask()
        lane = lax.broadcasted_iota(jnp.int32, (BLOCK, LANES), 1)
        lse_t = jnp.zeros((BLOCK, LANES), F32)
        outs = [jnp.zeros((BLOCK, LANES), F32) for _ in range(4)]
        for h in range(N_HEADS):
            s, _, hm, same = _head_scores(q_ref, h, lo, kp, kr, relf, valid)
            sink = sink_ref[h]
            m = jnp.maximum(jnp.max(s, axis=-1, keepdims=True), sink)
            pe = jnp.exp(s - m)
            l = jnp.sum(pe, axis=-1, keepdims=True) + jnp.exp(sink - m)
            lse_t = jnp.where(lane == h, m + jnp.log(l), lse_t)
            o = _dot((pe / l).astype(BF16), vp if same else vr)
            outs[h // 2] = outs[h // 2] + jnp.where(hm, o, 0.0)
        for p in range(4):
            o_ref[:, p * LANES:(p + 1) * LANES] = outs[p].astype(BF16)
        lse_ref[...] = lse_t

    return pl.pallas_call(
        body, name="attn_fwd", grid=(nb,),
        in_specs=[pl.BlockSpec(memory_space=pltpu.SMEM),
                  pl.BlockSpec((BLOCK, Q_COLS), lambda i: (i, 0)),
                  pl.BlockSpec((BLOCK, 2 * LANES), lambda i: (i, 2)),
                  pl.BlockSpec((BLOCK, 2 * LANES), lambda i: (jnp.maximum(i - 1, 0), 2))],
        out_specs=[pl.BlockSpec((BLOCK, Q_COLS), lambda i: (i, 0)), pl.BlockSpec((BLOCK, LANES), lambda i: (i, 0))],
        out_shape=[jax.ShapeDtypeStruct((t, Q_COLS), BF16), jax.ShapeDtypeStruct((t, LANES), F32)],
        compiler_params=_params(("parallel",)),
    )(sinks, qkvn, qkvn, qkvn)


def _glu(pc):
    return pc[:, :CONV_W] * _sig(pc[:, CONV_W:])


def _group_norm_stats(seg, lo):
    mu = _half_mean(seg, lo)
    d = seg - mu
    rstd = lax.rsqrt(_half_mean(d * d, lo) + EPS)
    return d * rstd, rstd


def _conv_fwd(pc, cw, cb, ln_g, ln_b):
    t = pc.shape[0]
    tt = _token_tile(t)

    def body(cur_ref, prev_ref, w_ref, b_ref, g_ref, bb_ref, y_ref, c_ref, scr):
        i = pl.program_id(0)
        scr[0:HALO, :] = _glu(prev_ref[...]) * (i > 0).astype(F32)
        scr[HALO:HALO + tt, :] = _glu(cur_ref[...])
        for cbk in range(CONV_W // LANES):
            cs = slice(cbk * LANES, (cbk + 1) * LANES)
            for rb in range(tt // ROW_CHUNK):
                r0 = rb * ROW_CHUNK
                acc = jnp.zeros((ROW_CHUNK, LANES), F32) + b_ref[:, cs]
                for k in range(CONV_K):
                    acc = acc + w_ref[k:k + 1, cs] * scr[r0 + 2 + k:r0 + 2 + k + ROW_CHUNK, cs]
                y_ref[r0:r0 + ROW_CHUNK, cs] = acc
        lo = _lo_mask()
        for p in range(CONV_W // LANES):
            cs = slice(p * LANES, (p + 1) * LANES)
            yh, _ = _group_norm_stats(y_ref[:, cs], lo)
            z = yh * g_ref[:, cs] + bb_ref[:, cs]
            c_ref[:, cs] = (z * _sig(z)).astype(BF16)

    hb = tt // HALO
    return pl.pallas_call(
        body, name="conv_fwd", grid=(t // tt,),
        in_specs=[pl.BlockSpec((tt, 2 * CONV_W), lambda i: (i, 0)),
                  pl.BlockSpec((HALO, 2 * CONV_W), lambda i: (jnp.maximum(i * hb - 1, 0), 0)),
                  _full((CONV_K, CONV_W)), _full((1, CONV_W)), _full((1, CONV_W)), _full((1, CONV_W))],
        out_specs=[pl.BlockSpec((tt, CONV_W), lambda i: (i, 0)), pl.BlockSpec((tt, CONV_W), lambda i: (i, 0))],
        out_shape=[jax.ShapeDtypeStruct((t, CONV_W), F32), jax.ShapeDtypeStruct((t, CONV_W), BF16)],
        scratch_shapes=[pltpu.VMEM((tt + HALO, CONV_W), F32)],
        compiler_params=_params(("parallel",)),
    )(pc, pc, cw, cb, ln_g, ln_b)


def _out_proj(x, attn, c, w_out, b_out, g_ffn):
    t = x.shape[0]
    tt = _token_tile(t)

    def body(x_ref, a_ref, c_ref, w_ref, b_ref, g_ref, x1_ref, h2_ref):
        x1 = x_ref[...] + _dot(a_ref[...], w_ref[0:Q_COLS, :]) + _dot(c_ref[...], w_ref[Q_COLS:, :]) + b_ref[...]
        x1_ref[...] = x1
        r = lax.rsqrt(jnp.mean(x1 * x1, axis=-1, keepdims=True) + EPS)
        h2_ref[...] = (x1 * r * g_ref[...]).astype(BF16)

    row = lambda w: pl.BlockSpec((tt, w), lambda i: (i, 0))
    return pl.pallas_call(
        body, name="out_proj", grid=(t // tt,),
        in_specs=[row(D_MODEL), row(Q_COLS), row(CONV_W), _full((D_MODEL, D_MODEL)), _full((1, D_MODEL)), _full((1, D_MODEL))],
        out_specs=[row(D_MODEL), row(D_MODEL)],
        out_shape=[jax.ShapeDtypeStruct((t, D_MODEL), F32), jax.ShapeDtypeStruct((t, D_MODEL), BF16)],
        compiler_params=_params(("parallel",)),
    )(x, attn, c, w_out, b_out, g_ffn)


def _ffn_conv(scr, cur, dw_ref, db_ref, tt):
    return (dw_ref[0:1, :] * scr[FHALO - 2:FHALO - 2 + tt, :] + dw_ref[1:2, :] * scr[FHALO - 1:FHALO - 1 + tt, :]
            + dw_ref[2:3, :] * cur + db_ref[...])


def _ffn_up(h2, w_up, dw, db):
    t = h2.shape[0]
    tt = _token_tile(t)
    nj = D_FF // FFN_CB

    def body(hc_ref, hp_ref, wg_ref, wu_ref, dwg_ref, dwu_ref, dbg_ref, dbu_ref, hg_ref, hu_ref, act_ref, sg, su):
        i = pl.program_id(1)
        hc = hc_ref[...]
        hp = hp_ref[...] * (i > 0).astype(BF16)
        ups = []
        for w_ref, dw_ref, db_ref, o_ref, scr in ((wg_ref, dwg_ref, dbg_ref, hg_ref, sg), (wu_ref, dwu_ref, dbu_ref, hu_ref, su)):
            cur = _dot(hc, w_ref[...])
            o_ref[...] = cur
            scr[0:FHALO, :] = _dot(hp, w_ref[...])
            scr[FHALO:FHALO + tt, :] = cur
            ups.append(_ffn_conv(scr, cur, dw_ref, db_ref, tt))
        g, u = ups
        act_ref[...] = (g * _sig(g) * u).astype(BF16)

    fb = tt // FHALO
    colg = lambda r: pl.BlockSpec((r, FFN_CB), lambda j, i: (0, j))
    colu = lambda r: pl.BlockSpec((r, FFN_CB), lambda j, i: (0, j + nj))
    tile = pl.BlockSpec((tt, FFN_CB), lambda j, i: (i, j))
    return pl.pallas_call(
        body, name="ffn_up", grid=(nj, t // tt),
        in_specs=[pl.BlockSpec((tt, D_MODEL), lambda j, i: (i, 0)),
                  pl.BlockSpec((FHALO, D_MODEL), lambda j, i: (jnp.maximum(i * fb - 1, 0), 0)),
                  colg(D_MODEL), colu(D_MODEL), colg(3), colu(3), colg(1), colu(1)],
        out_specs=[tile, tile, tile],
        out_shape=[jax.ShapeDtypeStruct((t, D_FF), F32), jax.ShapeDtypeStruct((t, D_FF), F32),
                   jax.ShapeDtypeStruct((t, D_FF), BF16)],
        scratch_shapes=[pltpu.VMEM((tt + FHALO, FFN_CB), F32), pltpu.VMEM((tt + FHALO, FFN_CB), F32)],
        compiler_params=_params(("parallel", "parallel")),
    )(h2, h2, w_up, w_up, dw, dw, db, db)


def _ffn_down(act, w_down, x1, target):
    t = act.shape[0]
    tt = _token_tile(t)

    def body(a_ref, w_ref, x1_ref, t_ref, dy_ref, loss_ref):
        err = x1_ref[...] + _dot(a_ref[...], w_ref[...]) - t_ref[...]
        dy_ref[...] = err * (1.0 / D_MODEL)

        @pl.when(pl.program_id(0) == 0)
        def _():
            loss_ref[...] = jnp.zeros_like(loss_ref)

        loss_ref[...] += jnp.sum(err * err, axis=0, keepdims=True)

    row = lambda w: pl.BlockSpec((tt, w), lambda i: (i, 0))
    return pl.pallas_call(
        body, name="ffn_down", grid=(t // tt,),
        in_specs=[row(D_FF), _full((D_FF, D_MODEL)), row(D_MODEL), row(D_MODEL)],
        out_specs=[row(D_MODEL), _full((1, D_MODEL))],
        out_shape=[jax.ShapeDtypeStruct((t, D_MODEL), F32), jax.ShapeDtypeStruct((1, D_MODEL), F32)],
        compiler_params=_params(("arbitrary",)),
    )(act, w_down, x1, target)


def _ffn_bwd_act(dy, w_down, hg, hu, dw, db):
    t = dy.shape[0]
    tt = _token_tile(t)
    nj = D_FF // FFN_CB

    def body(dy_ref, wd_ref, gc_ref, gp_ref, uc_ref, up_ref, dwg_ref, dwu_ref, dbg_ref, dbu_ref,
             dg_ref, du_ref, gwg_ref, gwu_ref, gbg_ref, gbu_ref, sg, su):
        i = pl.program_id(1)
        first = (i > 0).astype(F32)
        d_act = _dot_nt(dy_ref[...].astype(BF16), wd_ref[...])
        ups = []
        for c_ref, p_ref, dw_ref, db_ref, scr in ((gc_ref, gp_ref, dwg_ref, dbg_ref, sg), (uc_ref, up_ref, dwu_ref, dbu_ref, su)):
            scr[0:FHALO, :] = p_ref[...] * first
            scr[FHALO:FHALO + tt, :] = c_ref[...]
            ups.append(_ffn_conv(scr, c_ref[...], dw_ref, db_ref, tt))
        g, u = ups
        s = _sig(g)
        d_u = d_act * (g * s)
        d_g = d_act * u * (s * (1.0 + g * (1.0 - s)))

        @pl.when(i == 0)
        def _():
            for r in (gwg_ref, gwu_ref, gbg_ref, gbu_ref):
                r[...] = jnp.zeros_like(r)

        for d, o_ref, gw_ref, gb_ref, scr in ((d_g, dg_ref, gwg_ref, gbg_ref, sg), (d_u, du_ref, gwu_ref, gbu_ref, su)):
            o_ref[...] = d.astype(BF16)
            gb_ref[...] += jnp.sum(d, axis=0, keepdims=True)
            rows = [jnp.sum(d * scr[FHALO - 2 + k:FHALO - 2 + k + tt, :], axis=0, keepdims=True) for k in range(3)]
            gw_ref[...] += _rows_to_tile(rows, 8)

    fb = tt // FHALO
    tile = pl.BlockSpec((tt, FFN_CB), lambda j, i: (i, j))
    prev = pl.BlockSpec((FHALO, FFN_CB), lambda j, i: (jnp.maximum(i * fb - 1, 0), j))
    colg = lambda r: pl.BlockSpec((r, FFN_CB), lambda j, i: (0, j))
    colu = lambda r: pl.BlockSpec((r, FFN_CB), lambda j, i: (0, j + nj))
    acc = lambda r: pl.BlockSpec((r, FFN_CB), lambda j, i: (0, j))
    return pl.pallas_call(
        body, name="ffn_bwd_act", grid=(nj, t // tt),
        in_specs=[pl.BlockSpec((tt, D_MODEL), lambda j, i: (i, 0)), pl.BlockSpec((FFN_CB, D_MODEL), lambda j, i: (j, 0)),
                  tile, prev, tile, prev, colg(3), colu(3), colg(1), colu(1)],
        out_specs=[tile, tile, acc(8), acc(8), acc(1), acc(1)],
        out_shape=[jax.ShapeDtypeStruct((t, D_FF), BF16), jax.ShapeDtypeStruct((t, D_FF), BF16),
                   jax.ShapeDtypeStruct((8, D_FF), F32), jax.ShapeDtypeStruct((8, D_FF), F32),
                   jax.ShapeDtypeStruct((1, D_FF), F32), jax.ShapeDtypeStruct((1, D_FF), F32)],
        scratch_shapes=[pltpu.VMEM((tt + FHALO, FFN_CB), F32), pltpu.VMEM((tt + FHALO, FFN_CB), F32)],
        compiler_params=_params(("parallel", "arbitrary")),
    )(dy, w_down, hg, hg, hu, hu, dw, dw, db, db)


def _ffn_bwd_conv(dg, du, dw):
    t = dg.shape[0]
    tt = _token_tile(t)
    nj = D_FF // FFN_CB
    ni = t // tt

    def body(gc_ref, gn_ref, uc_ref, un_ref, dwg_ref, dwu_ref, og_ref, ou_ref, scr):
        i = pl.program_id(1)
        last = (i < ni - 1).astype(F32)
        for c_ref, n_ref, dw_ref, o_ref in ((gc_ref, gn_ref, dwg_ref, og_ref), (uc_ref, un_ref, dwu_ref, ou_ref)):
            cur = c_ref[...].astype(F32)
            scr[0:tt, :] = cur
            scr[tt:tt + FHALO, :] = n_ref[...].astype(F32) * last
            o_ref[...] = (dw_ref[2:3, :] * cur + dw_ref[1:2, :] * scr[1:1 + tt, :]
                          + dw_ref[0:1, :] * scr[2:2 + tt, :]).astype(BF16)

    fb = tt // FHALO
    tile = pl.BlockSpec((tt, FFN_CB), lambda j, i: (i, j))
    nxt = pl.BlockSpec((FHALO, FFN_CB), lambda j, i: (jnp.minimum((i + 1) * fb, t // FHALO - 1), j))
    return pl.pallas_call(
        body, name="ffn_bwd_conv", grid=(nj, ni),
        in_specs=[tile, nxt, tile, nxt, pl.BlockSpec((3, FFN_CB), lambda j, i: (0, j)),
                  pl.BlockSpec((3, FFN_CB), lambda j, i: (0, j + nj))],
        out_specs=[tile, tile],
        out_shape=[jax.ShapeDtypeStruct((t, D_FF), BF16), jax.ShapeDtypeStruct((t, D_FF), BF16)],
        scratch_shapes=[pltpu.VMEM((tt + FHALO, FFN_CB), F32)],
        compiler_params=_params(("parallel", "parallel")),
    )(dg, dg, du, du, dw, dw)


def _ffn_bwd_in(dhg, dhu, w_up, x1, dy, g_ffn):
    t = x1.shape[0]
    tt = _token_tile(t)

    def body(dg_ref, du_ref, w_ref, x1_ref, dy_ref, g_ref, dx_ref, gg_ref):
        d_h2 = _dot_nt(dg_ref[...], w_ref[:, 0:D_FF]) + _dot_nt(du_ref[...], w_ref[:, D_FF:])
        x1 = x1_ref[...]
        r = lax.rsqrt(jnp.mean(x1 * x1, axis=-1, keepdims=True) + EPS)
        xh = x1 * r
        gd = d_h2 * g_ref[...]
        dx_ref[...] = dy_ref[...] + r * (gd - xh * jnp.mean(gd * xh, axis=-1, keepdims=True))

        @pl.when(pl.program_id(0) == 0)
        def _():
            gg_ref[...] = jnp.zeros_like(gg_ref)

        gg_ref[...] += jnp.sum(d_h2 * xh, axis=0, keepdims=True)

    row = lambda w: pl.BlockSpec((tt, w), lambda i: (i, 0))
    return pl.pallas_call(
        body, name="ffn_bwd_in", grid=(t // tt,),
        in_specs=[row(D_FF), row(D_FF), _full((D_MODEL, 2 * D_FF)), row(D_MODEL), row(D_MODEL), _full((1, D_MODEL))],
        out_specs=[row(D_MODEL), _full((1, D_MODEL))],
        out_shape=[jax.ShapeDtypeStruct((t, D_MODEL), F32), jax.ShapeDtypeStruct((1, D_MODEL), F32)],
        compiler_params=_params(("arbitrary",)),
    )(dhg, dhu, w_up, x1, dy, g_ffn)


def _grad_weight(a, b, nj, mb, name):
    t, m = a.shape
    n = b.shape[1]
    nb_ = n // nj
    tt = _token_tile(t)

    def body(a_ref, b_ref, o_ref):
        @pl.when(pl.program_id(2) == 0)
        def _():
            o_ref[...] = jnp.zeros_like(o_ref)

        o_ref[0] += _dot_tn(a_ref[...].astype(BF16), b_ref[...].astype(BF16))

    return pl.pallas_call(
        body, name=name, grid=(nj, m // mb, t // tt),
        in_specs=[pl.BlockSpec((tt, mb), lambda j, mi, i: (i, mi)), pl.BlockSpec((tt, nb_), lambda j, mi, i: (i, j))],
        out_specs=pl.BlockSpec((1, mb, nb_), lambda j, mi, i: (j, mi, 0)),
        out_shape=jax.ShapeDtypeStruct((nj, m, nb_), F32),
        compiler_params=_params(("parallel", "parallel", "arbitrary")),
    )(a, b)


def _out_proj_bwd(dx1, w_out):
    t = dx1.shape[0]
    tt = _token_tile(t)

    def body(d_ref, w_ref, dm_ref, gb_ref):
        d = d_ref[...]
        dm_ref[...] = _dot_nt(d.astype(BF16), w_ref[...])

        @pl.when(pl.program_id(0) == 0)
        def _():
            gb_ref[...] = jnp.zeros_like(gb_ref)

        gb_ref[...] += jnp.sum(d, axis=0, keepdims=True)

    row = pl.BlockSpec((tt, D_MODEL), lambda i: (i, 0))
    return pl.pallas_call(
        body, name="out_proj_bwd", grid=(t // tt,),
        in_specs=[row, _full((D_MODEL, D_MODEL))],
        out_specs=[row, _full((1, D_MODEL))],
        out_shape=[jax.ShapeDtypeStruct((t, D_MODEL), F32), jax.ShapeDtypeStruct((1, D_MODEL), F32)],
        compiler_params=_params(("arbitrary",)),
    )(dx1, w_out)


def _conv_bwd(dmix, y, pc, cw, ln_g, ln_b):
    t = y.shape[0]
    tt = _token_tile(t)
    ni = t // tt
    ncb = CONV_W // LANES

    def body(dc_ref, dcn_ref, y_ref, yn_ref, pc_ref, pcp_ref, w_ref, g_ref, bb_ref,
             dp_ref, gw_ref, gb_ref, gg_ref, gbb_ref, gbin_ref, scr_d, scr_c, scr_o):
        i = pl.program_id(0)
        lo = _lo_mask()

        @pl.when(i == 0)
        def _():
            for r in (gw_ref, gb_ref, gg_ref, gbb_ref, gbin_ref):
                r[...] = jnp.zeros_like(r)

        def norm_bwd(dc, yv, cs):
            yh, rstd = _group_norm_stats(yv, lo)
            z = yh * g_ref[:, cs] + bb_ref[:, cs]
            s = _sig(z)
            dz = dc * (s * (1.0 + z * (1.0 - s)))
            dyh = dz * g_ref[:, cs]
            d_y = rstd * (dyh - _half_mean(dyh, lo) - yh * _half_mean(dyh * yh, lo))
            return d_y, dz, yh

        for p in range(ncb):
            cs = slice(p * LANES, (p + 1) * LANES)
            d_y, dz, yh = norm_bwd(dc_ref[:, cs], y_ref[:, cs], cs)
            scr_d[0:tt, cs] = d_y
            gg_ref[:, cs] += jnp.sum(dz * yh, axis=0, keepdims=True)
            gbb_ref[:, cs] += jnp.sum(dz, axis=0, keepdims=True)
            gb_ref[:, cs] += jnp.sum(d_y, axis=0, keepdims=True)
            d_yn, _, _ = norm_bwd(dcn_ref[:, cs], yn_ref[:, cs], cs)
            scr_d[tt:tt + HALO, cs] = d_yn * (i < ni - 1).astype(F32)
        scr_c[0:HALO, :] = _glu(pcp_ref[...]) * (i > 0).astype(F32)
        scr_c[HALO:HALO + tt, :] = _glu(pc_ref[...])

        rid = lax.broadcasted_iota(jnp.int32, (HALO, LANES), 0)
        for cbk in range(ncb):
            cs = slice(cbk * LANES, (cbk + 1) * LANES)
            for rb in range(tt // ROW_CHUNK):
                r0 = rb * ROW_CHUNK
                acc = jnp.zeros((ROW_CHUNK, LANES), F32)
                for k in range(CONV_K):
                    acc = acc + w_ref[k:k + 1, cs] * scr_d[r0 + 30 - k:r0 + 30 - k + ROW_CHUNK, cs]
                scr_o[r0:r0 + ROW_CHUNK, cs] = acc
            gwt = jnp.zeros((HALO, LANES), F32)
            for k in range(CONV_K):
                acc = jnp.zeros((ROW_CHUNK, LANES), F32)
                for rb in range(tt // ROW_CHUNK):
                    r0 = rb * ROW_CHUNK
                    acc = acc + scr_d[r0:r0 + ROW_CHUNK, cs] * scr_c[r0 + 2 + k:r0 + 2 + k + ROW_CHUNK, cs]
                gwt = jnp.where(rid == k, jnp.sum(acc, axis=0, keepdims=True), gwt)
            gw_ref[:, cs] += gwt
        d_c0 = scr_o[...]
        a = pc_ref[:, 0:CONV_W]
        s = _sig(pc_ref[:, CONV_W:])
        d_a = d_c0 * s
        d_gate = d_c0 * a * s * (1.0 - s)
        dp_ref[:, 0:CONV_W] = d_a.astype(BF16)
        dp_ref[:, CONV_W:] = d_gate.astype(BF16)
        gbin_ref[:, 0:CONV_W] += jnp.sum(d_a, axis=0, keepdims=True)
        gbin_ref[:, CONV_W:] += jnp.sum(d_gate, axis=0, keepdims=True)

    hb = tt // HALO
    nxt = lambda col: pl.BlockSpec((HALO, CONV_W), lambda i: (jnp.minimum((i + 1) * hb, t // HALO - 1), col))
    return pl.pallas_call(
        body, name="conv_bwd", grid=(ni,),
        in_specs=[pl.BlockSpec((tt, CONV_W), lambda i: (i, 1)), nxt(1),
                  pl.BlockSpec((tt, CONV_W), lambda i: (i, 0)), nxt(0),
                  pl.BlockSpec((tt, 2 * CONV_W), lambda i: (i, 0)),
                  pl.BlockSpec((HALO, 2 * CONV_W), lambda i: (jnp.maximum(i * hb - 1, 0), 0)),
                  _full((CONV_K, CONV_W)), _full((1, CONV_W)), _full((1, CONV_W))],
        out_specs=[pl.BlockSpec((tt, 2 * CONV_W), lambda i: (i, 0)), _full((HALO, CONV_W)), _full((1, CONV_W)),
                   _full((1, CONV_W)), _full((1, CONV_W)), _full((1, 2 * CONV_W))],
        out_shape=[jax.ShapeDtypeStruct((t, 2 * CONV_W), BF16), jax.ShapeDtypeStruct((HALO, CONV_W), F32),
                   jax.ShapeDtypeStruct((1, CONV_W), F32), jax.ShapeDtypeStruct((1, CONV_W), F32),
                   jax.ShapeDtypeStruct((1, CONV_W), F32), jax.ShapeDtypeStruct((1, 2 * CONV_W), F32)],
        scratch_shapes=[pltpu.VMEM((tt + HALO, CONV_W), F32), pltpu.VMEM((tt + HALO, CONV_W), F32),
                        pltpu.VMEM((tt, CONV_W), F32)],
        compiler_params=_params(("arbitrary",)),
    )(dmix, dmix, y, y, pc, pc, cw, ln_g, ln_b)


def _attn_bwd(qkvn, dmix, lse, sinks):
    t = qkvn.shape[0]
    nb = t // BLOCK

    def body(sink_ref, q_ref, kvc_ref, kvp_ref, do_ref, lse_ref, dq_ref, dcur_ref, dprev_ref, ds_ref):
        i = pl.program_id(0)
        kp, vp, kr, vr, relf, valid = _attn_keys(i, kvp_ref, kvc_ref)
        lo = _lo_mask()
        lane1 = lax.broadcasted_iota(jnp.int32, (1, LANES), 1)
        lane = lax.broadcasted_iota(jnp.int32, (BLOCK, LANES), 1)
        lse_t = lse_ref[...]
        zkv = lambda: jnp.zeros((2 * BLOCK, LANES), F32)
        dk_in, dk_roll, dv_in, dv_roll = zkv(), zkv(), zkv(), zkv()
        dqs = [jnp.zeros((BLOCK, LANES), F32) for _ in range(4)]
        dsink = jnp.zeros((1, LANES), F32)
        for h in range(N_HEADS):
            s, qm, hm, same = _head_scores(q_ref, h, lo, kp, kr, relf, valid)
            lse_h = jnp.sum(jnp.where(lane == h, lse_t, 0.0), axis=-1, keepdims=True)
            prob = jnp.exp(s - lse_h)
            do_m = jnp.where(hm, do_ref[:, (h // 2) * LANES:(h // 2 + 1) * LANES], 0.0).astype(BF16)
            dp = _dot_nt(do_m, vp if same else vr)
            dsum = jnp.sum(prob * dp, axis=-1, keepdims=True)
            dsc = prob * (dp - dsum)
            psink = jnp.exp(sink_ref[h] - lse_h)
            dsink = dsink + jnp.where(lane1 == h, jnp.sum(-psink * dsum, axis=0, keepdims=True), 0.0)
            dsb = (dsc * 0.125).astype(BF16)
            dqs[h // 2] = dqs[h // 2] + jnp.where(hm, _dot(dsb, kp if same else kr), 0.0)
            dk_c = _dot_tn(dsb, qm)
            dv_c = _dot_tn(prob.astype(BF16), do_m)
            if same:
                dk_in, dv_in = dk_in + dk_c, dv_in + dv_c
            else:
                dk_roll, dv_roll = dk_roll + dk_c, dv_roll + dv_c
        for p in range(4):
            dq_ref[:, p * LANES:(p + 1) * LANES] = dqs[p]
        dk = dk_in + pltpu.roll(dk_roll, HEAD_DIM, 1)
        dv = dv_in + pltpu.roll(dv_roll, HEAD_DIM, 1)
        dprev_ref[:, 0:LANES] = dk[0:BLOCK]
        dprev_ref[:, LANES:] = dv[0:BLOCK]
        dcur_ref[:, 0:LANES] = dk[BLOCK:]
        dcur_ref[:, LANES:] = dv[BLOCK:]

        @pl.when(i == 0)
        def _():
            ds_ref[...] = jnp.zeros_like(ds_ref)

        ds_ref[...] += dsink

    blk = lambda w: pl.BlockSpec((BLOCK, w), lambda i: (i, 0))
    return pl.pallas_call(
        body, name="attn_bwd", grid=(nb,),
        in_specs=[pl.BlockSpec(memory_space=pltpu.SMEM), blk(Q_COLS),
                  pl.BlockSpec((BLOCK, 2 * LANES), lambda i: (i, 2)),
                  pl.BlockSpec((BLOCK, 2 * LANES), lambda i: (jnp.maximum(i - 1, 0), 2)),
                  blk(Q_COLS), blk(LANES)],
        out_specs=[blk(Q_COLS), blk(2 * LANES), blk(2 * LANES), _full((1, LANES))],
        out_shape=[jax.ShapeDtypeStruct((t, Q_COLS), F32), jax.ShapeDtypeStruct((t, 2 * LANES), F32),
                   jax.ShapeDtypeStruct((t, 2 * LANES), F32), jax.ShapeDtypeStruct((1, LANES), F32)],
        compiler_params=_params(("arbitrary",)),
    )(sinks, qkvn, qkvn, qkvn, dmix, lse)


def _qk_norm_bwd(dqn, dcur, dprev, pq, gq2, gk2):
    t = dqn.shape[0]
    nb = t // BLOCK

    def body(dq_ref, dc_ref, dn_ref, pq_ref, gq_ref, gk_ref, dp_ref, gbin_ref, gg_ref):
        i = pl.program_id(0)
        lo = _lo_mask()
        dkv = dc_ref[...] + dn_ref[...] * (i < nb - 1).astype(F32)

        @pl.when(i == 0)
        def _():
            gbin_ref[...] = jnp.zeros_like(gbin_ref)
            gg_ref[...] = jnp.zeros_like(gg_ref)

        for p in range(5):
            cs = slice(p * LANES, (p + 1) * LANES)
            seg = pq_ref[:, cs]
            dn = dq_ref[:, cs] if p < 4 else dkv[:, 0:LANES]
            gain = gq_ref[...] if p < 4 else gk_ref[...]
            rr = lax.rsqrt(_half_mean(seg * seg, lo) + EPS)
            xh = seg * rr
            gd = dn * gain
            d = rr * (gd - xh * _half_mean(gd * xh, lo))
            dp_ref[:, cs] = d.astype(BF16)
            gbin_ref[:, cs] += jnp.sum(d, axis=0, keepdims=True)
            gg_ref[:, cs] += jnp.sum(dn * xh, axis=0, keepdims=True)
        dv = dkv[:, LANES:]
        dp_ref[:, 640:768] = dv.astype(BF16)
        gbin_ref[:, 640:768] += jnp.sum(dv, axis=0, keepdims=True)

    blk = lambda w: pl.BlockSpec((BLOCK, w), lambda i: (i, 0))
    return pl.pallas_call(
        body, name="qk_norm_bwd", grid=(nb,),
        in_specs=[blk(Q_COLS), blk(2 * LANES), pl.BlockSpec((BLOCK, 2 * LANES), lambda i: (jnp.minimum(i + 1, nb - 1), 0)),
                  blk(QKV_COLS), _full((1, LANES)), _full((1, LANES))],
        out_specs=[blk(QKV_COLS), _full((1, QKV_COLS)), _full((1, 5 * LANES))],
        out_shape=[jax.ShapeDtypeStruct((t, QKV_COLS), BF16), jax.ShapeDtypeStruct((1, QKV_COLS), F32),
                   jax.ShapeDtypeStruct((1, 5 * LANES), F32)],
        compiler_params=_params(("arbitrary",)),
    )(dqn, dcur, dprev, pq, gq2, gk2)


def _in_proj_bwd(dpq, dpc, w_in, x, dx1, g_mix):
    t = x.shape[0]
    tt = _token_tile(t)

    def body(dq_ref, dc_ref, w_ref, x_ref, d1_ref, g_ref, gx_ref, gg_ref):
        d_h = _dot_nt(dq_ref[...], w_ref[:, 0:QKV_COLS]) + _dot_nt(dc_ref[...], w_ref[:, QKV_COLS:])
        xv = x_ref[...]
        r = lax.rsqrt(jnp.mean(xv * xv, axis=-1, keepdims=True) + EPS)
        xh = xv * r
        gd = d_h * g_ref[...]
        gx_ref[...] = d1_ref[...] + r * (gd - xh * jnp.mean(gd * xh, axis=-1, keepdims=True))

        @pl.when(pl.program_id(0) == 0)
        def _():
            gg_ref[...] = jnp.zeros_like(gg_ref)

        gg_ref[...] += jnp.sum(d_h * xh, axis=0, keepdims=True)

    row = lambda w: pl.BlockSpec((tt, w), lambda i: (i, 0))
    return pl.pallas_call(
        body, name="in_proj_bwd", grid=(t // tt,),
        in_specs=[row(QKV_COLS), row(2 * CONV_W), _full((D_MODEL, IN_COLS)), row(D_MODEL), row(D_MODEL), _full((1, D_MODEL))],
        out_specs=[row(D_MODEL), _full((1, D_MODEL))],
        out_shape=[jax.ShapeDtypeStruct((t, D_MODEL), F32), jax.ShapeDtypeStruct((1, D_MODEL), F32)],
        compiler_params=_params(("arbitrary",)),
    )(dpq, dpc, w_in, x, dx1, g_mix)


def _row_block(r):
    if r <= 256:
        return r
    return max(b for b in range(8, 257, 8) if r % b == 0)


def _adamw(w, g, m, v, name):
    r, c = w.shape
    rb = _row_block(r)

    def body(w_ref, g_ref, m_ref, v_ref, d_ref, nm_ref, nv_ref):
        gv = g_ref[...]
        nm = ADAM_B1 * m_ref[...] + (1.0 - ADAM_B1) * gv
        nv = ADAM_B2 * v_ref[...] + (1.0 - ADAM_B2) * (gv * gv)
        m_hat = nm / (1.0 - ADAM_B1 ** ADAM_STEP)
        v_hat = nv / (1.0 - ADAM_B2 ** ADAM_STEP)
        d_ref[...] = -ADAM_LR * (m_hat / (jnp.sqrt(v_hat) + ADAM_EPS) + ADAM_WD * w_ref[...])
        nm_ref[...] = nm
        nv_ref[...] = nv

    blk = pl.BlockSpec((rb, c), lambda i: (i, 0))
    shp = jax.ShapeDtypeStruct((r, c), F32)
    return pl.pallas_call(
        body, name=name, grid=(r // rb,), in_specs=[blk] * 4, out_specs=[blk] * 3, out_shape=[shp] * 3,
        compiler_params=_params(("parallel",)),
    )(w, g, m, v)


ANY = pl.BlockSpec(memory_space=pl.ANY)


def _place():
    x, y, c = lax.axis_index("x"), lax.axis_index("y"), lax.axis_index("c")
    chips = [(1 - x, y), (x, 1 - y), (1 - x, 1 - y)]
    return x, y, c, chips


def _gather_all(v):
    r = v.shape[0]

    def body(v_ref, all_ref, sum_ref, send_sems, recv_sems):
        x, y, c, _ = _place()
        me = 4 * x + 2 * y + c
        all_ref[me] = v_ref[...]
        copies = []
        for k in range(1, 8):
            kx, ky, kc = (k >> 2) & 1, (k >> 1) & 1, k & 1
            peer = (x ^ kx, y ^ ky, c ^ kc)
            cp = pltpu.make_async_remote_copy(src_ref=v_ref, dst_ref=all_ref.at[me], send_sem=send_sems.at[k - 1],
                                              recv_sem=recv_sems.at[k - 1], device_id=peer, device_id_type=MESH)
            cp.start()
            copies.append((cp, 4 * peer[0] + 2 * peer[1] + peer[2]))
        for k, (cp, src_idx) in enumerate(copies):
            pltpu.make_async_remote_copy(src_ref=v_ref, dst_ref=all_ref.at[src_idx], send_sem=send_sems.at[k],
                                         recv_sem=recv_sems.at[k], device_id=(x, y, c), device_id_type=MESH).wait_recv()
        for cp, _ in copies:
            cp.wait_send()
        tot = all_ref[0]
        for d in range(1, 8):
            tot = tot + all_ref[d]
        sum_ref[...] = tot

    vm = pl.BlockSpec(memory_space=pltpu.VMEM)
    return pl.pallas_call(
        body, name="gather_all", in_specs=[vm], out_specs=[vm, vm],
        out_shape=[jax.ShapeDtypeStruct((8, r, LANES), v.dtype), jax.ShapeDtypeStruct((r, LANES), v.dtype)],
        scratch_shapes=[pltpu.SemaphoreType.DMA((7,)), pltpu.SemaphoreType.DMA((7,))],
        compiler_params=pltpu.CompilerParams(vmem_limit_bytes=VMEM_LIMIT),
    )(v)


def _gather_weights(shards):
    n = len(shards)

    def body(*refs):
        ins, outs = refs[:n], refs[n:2 * n]
        send_sems, recv_sems, local_sems = refs[2 * n:]
        x, y, c, chips = _place()
        me = 2 * x + y
        sib = (x, y, 1 - c)

        def copy(a, k, src, dst, to):
            return pltpu.make_async_remote_copy(src_ref=src, dst_ref=dst, send_sem=send_sems.at[a, k],
                                                recv_sem=recv_sems.at[a, k], device_id=to, device_id_type=MESH)

        started = []
        for a in range(n):
            loc = pltpu.make_async_copy(ins[a], outs[a].at[me], local_sems.at[a])
            loc.start()
            started.append(loc)
        sends = []
        for a in range(n):
            for j, chip in enumerate(chips):
                cp = copy(a, j, ins[a].at[c], outs[a].at[me, c], (*chip, c))
                cp.start()
                sends.append(cp)
        for a in range(n):
            for j, chip in enumerate(chips):
                land = outs[a].at[2 * chip[0] + chip[1], c]
                copy(a, j, land, land, (x, y, c)).wait_recv()
                cp = copy(a, 3 + j, land, land, sib)
                cp.start()
                sends.append(cp)
        for a in range(n):
            for j, chip in enumerate(chips):
                land = outs[a].at[2 * chip[0] + chip[1], 1 - c]
                copy(a, 3 + j, land, land, (x, y, c)).wait_recv()
        for cp in sends:
            cp.wait_send()
        for loc in started:
            loc.wait()

    return pl.pallas_call(
        body, name="gather_weights", in_specs=[ANY] * n, out_specs=[ANY] * n,
        out_shape=[jax.ShapeDtypeStruct((4,) + s.shape, s.dtype) for s in shards],
        scratch_shapes=[pltpu.SemaphoreType.DMA((n, 6)), pltpu.SemaphoreType.DMA((n, 6)), pltpu.SemaphoreType.DMA((n,))],
    )(*shards)


def _swap_halves(grads):
    n = len(grads)

    def body(*refs):
        ins, outs = refs[:n], refs[n:2 * n]
        send_sems, recv_sems = refs[2 * n:]
        x, y, c, _ = _place()
        sends = []
        for a in range(n):
            for j in range(4):
                cp = pltpu.make_async_remote_copy(src_ref=ins[a].at[j, 1 - c], dst_ref=outs[a].at[j],
                                                  send_sem=send_sems.at[a, j], recv_sem=recv_sems.at[a, j],
                                                  device_id=(x, y, 1 - c), device_id_type=MESH)
                cp.start()
                sends.append(cp)
        for cp in sends:
            cp.wait()

    return pl.pallas_call(
        body, name="swap_halves", in_specs=[ANY] * n, out_specs=[ANY] * n,
        out_shape=[jax.ShapeDtypeStruct((4,) + g.shape[2:], g.dtype) for g in grads],
        scratch_shapes=[pltpu.SemaphoreType.DMA((n, 4)), pltpu.SemaphoreType.DMA((n, 4))],
    )(*grads)


def _add_sibling(g, got, c_idx, name):
    _, _, h, c = g.shape

    def body(s_ref, a_ref, b_ref, o_ref):
        o_ref[...] = a_ref[...] + b_ref[...]

    return pl.pallas_call(
        body, name=name,
        grid_spec=pltpu.PrefetchScalarGridSpec(
            num_scalar_prefetch=1, grid=(4,),
            in_specs=[pl.BlockSpec((None, None, h, c), lambda j, s: (j, s[0], 0, 0)),
                      pl.BlockSpec((None, h, c), lambda j, s: (j, 0, 0))],
            out_specs=pl.BlockSpec((None, h, c), lambda j, s: (j, 0, 0))),
        out_shape=jax.ShapeDtypeStruct((4, h, c), F32),
        compiler_params=_params(("parallel",)),
    )(c_idx, g, got)


def _exchange_chips(parts):
    n = len(parts)

    def body(*refs):
        ins, outs = refs[:n], refs[n:2 * n]
        send_sems, recv_sems = refs[2 * n:]
        x, y, c, chips = _place()
        sends = []
        for a in range(n):
            for j, chip in enumerate(chips):
                cp = pltpu.make_async_remote_copy(src_ref=ins[a].at[2 * chip[0] + chip[1]], dst_ref=outs[a].at[j],
                                                  send_sem=send_sems.at[a, j], recv_sem=recv_sems.at[a, j],
                                                  device_id=(*chip, c), device_id_type=MESH)
                cp.start()
                sends.append(cp)
        for cp in sends:
            cp.wait()

    return pl.pallas_call(
        body, name="exchange_chips", in_specs=[ANY] * n, out_specs=[ANY] * n,
        out_shape=[jax.ShapeDtypeStruct((3,) + p.shape[1:], p.dtype) for p in parts],
        scratch_shapes=[pltpu.SemaphoreType.DMA((n, 3)), pltpu.SemaphoreType.DMA((n, 3))],
    )(*parts)


def _add_chips(part, got, chip_idx, name):
    _, h, c = part.shape

    def body(s_ref, a_ref, b_ref, o_ref):
        o_ref[...] = ((a_ref[...] + b_ref[0]) + b_ref[1]) + b_ref[2]

    return pl.pallas_call(
        body, name=name,
        grid_spec=pltpu.PrefetchScalarGridSpec(
            num_scalar_prefetch=1, grid=(1,),
            in_specs=[pl.BlockSpec((None, h, c), lambda i, s: (s[0], 0, 0)),
                      pl.BlockSpec((3, h, c), lambda i, s: (0, 0, 0))],
            out_specs=pl.BlockSpec((h, c), lambda i, s: (0, 0))),
        out_shape=jax.ShapeDtypeStruct((h, c), F32),
        compiler_params=_params(("arbitrary",)),
    )(chip_idx, part, got)


def _join_halves(halves):
    n = len(halves)

    def body(*refs):
        ins, outs = refs[:n], refs[n:2 * n]
        send_sems, recv_sems, local_sems = refs[2 * n:]
        x, y, c, _ = _place()
        pend = []
        for a in range(n):
            loc = pltpu.make_async_copy(ins[a], outs[a].at[c], local_sems.at[a])
            loc.start()
            cp = pltpu.make_async_remote_copy(src_ref=ins[a], dst_ref=outs[a].at[c], send_sem=send_sems.at[a],
                                              recv_sem=recv_sems.at[a], device_id=(x, y, 1 - c), device_id_type=MESH)
            cp.start()
            pend.append((loc, cp))
        for a, (loc, cp) in enumerate(pend):
            loc.wait()
            cp.wait_send()
            pltpu.make_async_remote_copy(src_ref=ins[a], dst_ref=outs[a].at[1 - c], send_sem=send_sems.at[a],
                                         recv_sem=recv_sems.at[a], device_id=(x, y, c), device_id_type=MESH).wait_recv()

    return pl.pallas_call(
        body, name="join_halves", in_specs=[ANY] * n, out_specs=[ANY] * n,
        out_shape=[jax.ShapeDtypeStruct((2,) + h.shape, h.dtype) for h in halves],
        scratch_shapes=[pltpu.SemaphoreType.DMA((n,)), pltpu.SemaphoreType.DMA((n,)), pltpu.SemaphoreType.DMA((n,))],
    )(*halves)


def _pack(parts):
    flat = []
    for p in parts:
        p = p.reshape(-1).astype(F32)
        flat.append(jnp.pad(p, (0, (-p.shape[0]) % LANES)))
    v = jnp.concatenate(flat)
    v = jnp.pad(v, (0, (-v.shape[0]) % (8 * LANES)))
    return v.reshape(-1, LANES)


def _unpack(v, shapes):
    flat = v.reshape(-1)
    out, off = [], 0
    for s in shapes:
        n = 1
        for d in s:
            n *= d
        out.append(flat[off:off + n].reshape(s))
        off += n + (-n) % LANES
    return out


def kernel(x, mix_norm_gain, w_in, b_in, q_norm_gain, k_norm_gain, attn_sinks, conv_dw_w, conv_dw_b, conv_norm_gain, conv_norm_bias, w_out, b_out, ffn_norm_gain, w_up, ffn_dw_w, ffn_dw_b, w_down, loss_target, m_mix_norm_gain, m_w_in, m_b_in, m_q_norm_gain, m_k_norm_gain, m_attn_sinks, m_conv_dw_w, m_conv_dw_b, m_conv_norm_gain, m_conv_norm_bias, m_w_out, m_b_out, m_ffn_norm_gain, m_w_up, m_ffn_dw_w, m_ffn_dw_b, m_w_down, v_mix_norm_gain, v_w_in, v_b_in, v_q_norm_gain, v_k_norm_gain, v_attn_sinks, v_conv_dw_w, v_conv_dw_b, v_conv_norm_gain, v_conv_norm_bias, v_w_out, v_b_out, v_ffn_norm_gain, v_w_up, v_ffn_dw_w, v_ffn_dw_b, v_w_down):
    t = x.shape[1]
    xi, yi, ci = lax.axis_index("x"), lax.axis_index("y"), lax.axis_index("c")
    chip = 2 * xi + yi
    c_idx = jnp.reshape(ci, (1,)).astype(jnp.int32)
    chip_idx = jnp.reshape(chip, (1,)).astype(jnp.int32)
    x2 = x.reshape(t, D_MODEL)
    tgt = loss_target.reshape(t, D_MODEL)

    big = [w_in, w_out, w_up, w_down]
    halves = [w.astype(BF16).reshape(2, w.shape[0] // 2, w.shape[1]) for w in big]
    g_wi, g_wo, g_wu, g_wd = _gather_weights(halves)
    wi = jnp.concatenate([g_wi[j].reshape(D_MODEL, IN_COLS // 4) for j in range(4)], axis=1)
    wo = g_wo.reshape(D_MODEL, D_MODEL)
    wu = jnp.concatenate([g_wu[j].reshape(D_MODEL, 2 * D_FF // 4) for j in range(4)], axis=1)
    wd = g_wd.reshape(D_FF, D_MODEL)
    small_w, _ = _gather_all(_pack([conv_dw_w, ffn_dw_w]))
    per_chip = [_unpack(small_w[4 * (j // 2) + 2 * (j % 2)], [conv_dw_w.shape, ffn_dw_w.shape]) for j in range(4)]
    cw = jnp.concatenate([p[0] for p in per_chip], axis=1)
    fw = jnp.concatenate([p[1] for p in per_chip], axis=1)

    row = lambda a: a.reshape(1, -1)
    gq2 = row(jnp.concatenate([q_norm_gain, q_norm_gain]))
    gk2 = row(jnp.concatenate([k_norm_gain, k_norm_gain]))

    h1, pq, pc, qkvn = _fwd_in(x2, row(mix_norm_gain), wi, row(b_in), gq2, gk2)
    attn, lse = _attn_fwd(qkvn, attn_sinks)
    y_conv, c_act = _conv_fwd(pc, cw, row(conv_dw_b), row(conv_norm_gain), row(conv_norm_bias))
    x1, h2 = _out_proj(x2, attn, c_act, wo, row(b_out), row(ffn_norm_gain))
    hg, hu, act = _ffn_up(h2, wu, fw, row(ffn_dw_b))
    dy, loss_cols = _ffn_down(act, wd, x1, tgt)

    dg, du, gfw_g, gfw_u, gfb_g, gfb_u = _ffn_bwd_act(dy, wd, hg, hu, fw, row(ffn_dw_b))
    dhg, dhu = _ffn_bwd_conv(dg, du, fw)
    dx1, g_ffn_gain = _ffn_bwd_in(dhg, dhu, wu, x1, dy, row(ffn_norm_gain))
    gw_down = _grad_weight(act, dy, 1, FFN_CB, "grad_w_down")
    gw_up = jnp.concatenate([_grad_weight(h2, dhg, 2, D_MODEL, "grad_w_up_gate"),
                             _grad_weight(h2, dhu, 2, D_MODEL, "grad_w_up_lin")], axis=0)
    dmix, g_b_out = _out_proj_bwd(dx1, wo)
    gw_out = jnp.concatenate([_grad_weight(attn, dx1, 1, Q_COLS, "grad_w_out_attn")[0],
                              _grad_weight(c_act, dx1, 1, CONV_W, "grad_w_out_conv")[0]], axis=0)
    dpc, g_cw, g_cb, g_lng, g_lnb, gbin_c = _conv_bwd(dmix, y_conv, pc, cw, row(conv_norm_gain), row(conv_norm_bias))
    dqn, dcur, dprev, g_sink = _attn_bwd(qkvn, dmix, lse, attn_sinks)
    dpq, gbin_q, g_qk = _qk_norm_bwd(dqn, dcur, dprev, pq, gq2, gk2)
    grad_x, g_mix_gain = _in_proj_bwd(dpq, dpc, wi, x2, dx1, row(mix_norm_gain))
    gw_in = jnp.concatenate([_grad_weight(h1, dpq, 1, D_MODEL, "grad_w_in_qkv")[0],
                             _grad_weight(h1, dpc, 1, D_MODEL, "grad_w_in_conv")[0]], axis=1)

    g_qk = g_qk.reshape(5, 2, HEAD_DIM)
    small = [g_mix_gain, jnp.concatenate([gbin_q, gbin_c], axis=1), g_qk[:4].sum(axis=(0, 1)), g_qk[4].sum(axis=0),
             g_sink[0, :N_HEADS], g_cb, g_lng, g_lnb, g_b_out, g_ffn_gain, jnp.concatenate([gfb_g, gfb_u], axis=1),
             loss_cols, g_cw[:CONV_K], jnp.concatenate([gfw_g[:3], gfw_u[:3]], axis=1)]
    _, tot = _gather_all(_pack(small))
    rep_names = ["mix_norm_gain", "b_in", "q_norm_gain", "k_norm_gain", "attn_sinks", "conv_dw_b", "conv_norm_gain",
                 "conv_norm_bias", "b_out", "ffn_norm_gain", "ffn_dw_b"]
    rep_w = [mix_norm_gain, b_in, q_norm_gain, k_norm_gain, attn_sinks, conv_dw_b, conv_norm_gain, conv_norm_bias,
             b_out, ffn_norm_gain, ffn_dw_b]
    rep_m = [m_mix_norm_gain, m_b_in, m_q_norm_gain, m_k_norm_gain, m_attn_sinks, m_conv_dw_b, m_conv_norm_gain,
             m_conv_norm_bias, m_b_out, m_ffn_norm_gain, m_ffn_dw_b]
    rep_v = [v_mix_norm_gain, v_b_in, v_q_norm_gain, v_k_norm_gain, v_attn_sinks, v_conv_dw_b, v_conv_norm_gain,
             v_conv_norm_bias, v_b_out, v_ffn_norm_gain, v_ffn_dw_b]
    shapes = [w.shape for w in rep_w] + [(D_MODEL,), (CONV_K, CONV_W), (3, 2 * D_FF)]
    tot_parts = _unpack(tot, shapes)
    loss = (0.5 / D_MODEL) * jnp.sum(tot_parts[len(rep_w)])
    g_cw_full, g_fw_full = tot_parts[len(rep_w) + 1], tot_parts[len(rep_w) + 2]
    n_rep_rows = _pack(rep_w).shape[0]
    rep_d, rep_nm, rep_nv = _adamw(_pack(rep_w), tot[:n_rep_rows], _pack(rep_m), _pack(rep_v), "adamw_small")
    rep_shapes = [w.shape for w in rep_w]
    res = {}
    for nm_, g_, d_, m_, v_ in zip(rep_names, tot_parts, _unpack(rep_d, rep_shapes), _unpack(rep_nm, rep_shapes),
                                   _unpack(rep_nv, rep_shapes)):
        res[nm_] = (g_, d_, m_, v_)

    g_cw_mine = lax.dynamic_slice_in_dim(g_cw_full, chip * (CONV_W // 4), CONV_W // 4, axis=1)
    g_fw_mine = lax.dynamic_slice_in_dim(g_fw_full, chip * (2 * D_FF // 4), 2 * D_FF // 4, axis=1)
    res["conv_dw_w"] = (g_cw_mine, *_adamw(conv_dw_w, g_cw_mine, m_conv_dw_w, v_conv_dw_w, "adamw_conv_dw_w"))
    res["ffn_dw_w"] = (g_fw_mine, *_adamw(ffn_dw_w, g_fw_mine, m_ffn_dw_w, v_ffn_dw_w, "adamw_ffn_dw_w"))

    gw_in4 = gw_in.reshape(D_MODEL, 4, IN_COLS // 4).transpose(1, 0, 2)
    partial = [gw_in4, gw_out.reshape(4, D_MODEL // 4, D_MODEL), gw_up, gw_down.reshape(4, D_FF // 4, D_MODEL)]
    partial = [g.reshape(4, 2, g.shape[1] // 2, g.shape[2]) for g in partial]
    got = _swap_halves(partial)
    names = ["w_in", "w_out", "w_up", "w_down"]
    chip_part = [_add_sibling(g, r, c_idx, "add_sibling_" + nm_) for g, r, nm_ in zip(partial, got, names)]
    got2 = _exchange_chips(chip_part)
    reduced = [_add_chips(p, r, chip_idx, "add_chips_" + nm_) for p, r, nm_ in zip(chip_part, got2, names)]
    shard_g = _join_halves(reduced)
    for nm_, w_, g_, m_, v_ in zip(names, big, shard_g, [m_w_in, m_w_out, m_w_up, m_w_down], [v_w_in, v_w_out, v_w_up, v_w_down]):
        g_ = g_.reshape(w_.shape)
        res[nm_] = (g_, *_adamw(w_, g_, m_, v_, "adamw_" + nm_))

    order = ["mix_norm_gain", "w_in", "b_in", "q_norm_gain", "k_norm_gain", "attn_sinks", "conv_dw_w", "conv_dw_b",
             "conv_norm_gain", "conv_norm_bias", "w_out", "b_out", "ffn_norm_gain", "w_up", "ffn_dw_w", "ffn_dw_b", "w_down"]
    return (loss, grad_x.reshape(x.shape), *[res[n][0] for n in order], *[res[n][1] for n in order],
            *[res[n][2] for n in order], *[res[n][3] for n in order])
```

```python
import functools

import jax
import jax.numpy as jnp
from jax import lax
from jax.experimental import pallas as pl
from jax.experimental.pallas import tpu as pltpu

F32 = jnp.float32
BF16 = jnp.bfloat16
MESH = pl.DeviceIdType.MESH

D_MODEL = 1024
HEAD_DIM = 64
N_HEADS = 8
Q_COLS = 512
QKV_COLS = 768
CONV_W = 512
CONV_K = 31
IN_COLS = 1792
D_FF = 2816
FFN_CB = 1408
BLOCK = 128
LANES = 128
EPS = 1e-6
NEG_INF = -1e30
SLOPES = tuple(float(2.0 ** (-(h + 1.0))) for h in range(N_HEADS))
HALO = 32
FHALO = 16
ROW_CHUNK = 64
VMEM_LIMIT = 56 * 1024 * 1024

ADAM_LR = 0.001
ADAM_B1 = 0.9
ADAM_B2 = 0.999
ADAM_EPS = 1e-08
ADAM_WD = 0.01
ADAM_STEP = 10


def _params(sem=None):
    kw = dict(vmem_limit_bytes=VMEM_LIMIT)
    if sem is not None:
        kw["dimension_semantics"] = sem
    return pltpu.CompilerParams(**kw)


def _token_tile(t):
    return 512 if t % 512 == 0 and t >= 2048 else 128


def _sig(v):
    return 1.0 / (1.0 + jnp.exp(-v))


def _lo_mask():
    return lax.broadcasted_iota(jnp.int32, (1, LANES), 1) < HEAD_DIM


def _half_mean(v, lo):
    s_lo = jnp.sum(jnp.where(lo, v, 0.0), axis=-1, keepdims=True)
    s_hi = jnp.sum(jnp.where(lo, 0.0, v), axis=-1, keepdims=True)
    return jnp.where(lo, s_lo, s_hi) * (1.0 / HEAD_DIM)


def _dot(a, b):
    return jnp.dot(a, b, preferred_element_type=F32)


def _dot_nt(a, b):
    return lax.dot_general(a, b, (((1,), (1,)), ((), ())), preferred_element_type=F32)


def _dot_tn(a, b):
    return lax.dot_general(a, b, (((0,), (0,)), ((), ())), preferred_element_type=F32)


def _full(shape):
    nd = len(shape)
    return pl.BlockSpec(shape, lambda *_: (0,) * nd)


def _rows_to_tile(rows, n_rows):
    c = rows[0].shape[-1]
    rid = lax.broadcasted_iota(jnp.int32, (n_rows, c), 0)
    out = jnp.zeros((n_rows, c), F32)
    for k, r in enumerate(rows):
        out = jnp.where(rid == k, r, out)
    return out


ANY = pl.BlockSpec(memory_space=pl.ANY)


class _Rider:
    def __init__(self, ins, outs, sems, start, finish, aliases=None):
        self.ins, self.outs, self.sems = list(ins), list(outs), list(sems)
        self.start, self.finish, self.aliases = start, finish, dict(aliases or {})


def _join_riders(riders):
    ins, outs, sems, aliases, spans = [], [], [], {}, []
    for r in riders:
        spans.append((len(ins), len(outs), len(sems), r))
        for a, b in r.aliases.items():
            aliases[len(ins) + a] = len(outs) + b
        ins += r.ins
        outs += r.outs
        sems += r.sems

    def each(which):
        def run(i_refs, o_refs, s_refs):
            for i0, o0, s0, r in spans:
                getattr(r, which)(i_refs[i0:i0 + len(r.ins)], o_refs[o0:o0 + len(r.outs)], s_refs[s0:s0 + len(r.sems)])
        return run

    return _Rider(ins, outs, sems, each("start"), each("finish"), aliases)


def _call(body, args, *, name, grid, in_specs, out_specs, out_shape, scratch=(), riders=()):
    in_specs, out_specs, out_shape, scratch = list(in_specs), list(out_specs), list(out_shape), list(scratch)
    sem = ("arbitrary",) * len(grid)
    if not riders:
        outs = pl.pallas_call(body, name=name, grid=grid, in_specs=in_specs, out_specs=out_specs, out_shape=out_shape,
                              scratch_shapes=scratch, compiler_params=_params(sem))(*args)
        return list(outs), []
    r = _join_riders(riders)
    n_in, n_out, n_scr = len(in_specs), len(out_specs), len(scratch)
    nri, nro = len(r.ins), len(r.outs)

    def full(*refs):
        ins, rin = refs[:n_in], refs[n_in:n_in + nri]
        o0 = n_in + nri
        outs, rout = refs[o0:o0 + n_out], refs[o0 + n_out:o0 + n_out + nro]
        s0 = o0 + n_out + nro
        scr, rsem = refs[s0:s0 + n_scr], refs[s0 + n_scr:]
        first = functools.reduce(jnp.logical_and, [pl.program_id(k) == 0 for k in range(len(grid))])
        last = functools.reduce(jnp.logical_and, [pl.program_id(k) == grid[k] - 1 for k in range(len(grid))])

        @pl.when(first)
        def _():
            r.start(rin, rout, rsem)

        body(*ins, *outs, *scr)

        @pl.when(last)
        def _():
            r.finish(rin, rout, rsem)

    outs = pl.pallas_call(
        full, name=name, grid=grid, in_specs=in_specs + [ANY] * nri, out_specs=out_specs + [ANY] * nro,
        out_shape=out_shape + r.outs, scratch_shapes=scratch + r.sems,
        input_output_aliases={n_in + a: n_out + b for a, b in r.aliases.items()},
        compiler_params=_params(sem))(*args, *r.ins)
    return list(outs[:n_out]), list(outs[n_out:])


def _run_riders(riders, name):
    r = _join_riders(riders)
    nri, nro = len(r.ins), len(r.outs)

    def body(*refs):
        rin, rout, rsem = refs[:nri], refs[nri:nri + nro], refs[nri + nro:]
        r.start(rin, rout, rsem)
        r.finish(rin, rout, rsem)

    outs = pl.pallas_call(body, name=name, in_specs=[ANY] * nri, out_specs=[ANY] * nro, out_shape=r.outs,
                          scratch_shapes=r.sems, input_output_aliases=r.aliases)(*r.ins)
    return list(outs)


def _fwd_in(x, g_mix, w_in, b_in, gq2, gk2, riders=()):
    t = x.shape[0]
    tt = _token_tile(t)

    def body(x_ref, g_ref, w_ref, b_ref, gq_ref, gk_ref, h1_ref, pq_ref, pc_ref, qkvn_ref):
        xv = x_ref[...]
        r = lax.rsqrt(jnp.mean(xv * xv, axis=-1, keepdims=True) + EPS)
        h = (xv * r * g_ref[...]).astype(BF16)
        h1_ref[...] = h
        proj = _dot(h, w_ref[...]) + b_ref[...]
        pq_ref[...] = proj[:, :QKV_COLS]
        pc_ref[...] = proj[:, QKV_COLS:]
        lo = _lo_mask()
        for p in range(5):
            seg = proj[:, p * LANES:(p + 1) * LANES]
            rr = lax.rsqrt(_half_mean(seg * seg, lo) + EPS)
            gain = gq_ref[...] if p < 4 else gk_ref[...]
            qkvn_ref[:, p * LANES:(p + 1) * LANES] = (seg * rr * gain).astype(BF16)
        qkvn_ref[:, 640:768] = proj[:, 640:768].astype(BF16)

    return _call(
        body, (x, g_mix, w_in, b_in, gq2, gk2), name="fwd_in", grid=(t // tt,), riders=riders,
        in_specs=[pl.BlockSpec((tt, D_MODEL), lambda i: (i, 0)), _full((1, D_MODEL)), _full((D_MODEL, IN_COLS)),
                  _full((1, IN_COLS)), _full((1, LANES)), _full((1, LANES))],
        out_specs=[pl.BlockSpec((tt, D_MODEL), lambda i: (i, 0)), pl.BlockSpec((tt, QKV_COLS), lambda i: (i, 0)),
                   pl.BlockSpec((tt, 2 * CONV_W), lambda i: (i, 0)), pl.BlockSpec((tt, QKV_COLS), lambda i: (i, 0))],
        out_shape=[jax.ShapeDtypeStruct((t, D_MODEL), BF16), jax.ShapeDtypeStruct((t, QKV_COLS), F32),
                   jax.ShapeDtypeStruct((t, 2 * CONV_W), F32), jax.ShapeDtypeStruct((t, QKV_COLS), BF16)])


def _attn_keys(i, kvp_ref, kvc_ref):
    kv = jnp.concatenate([kvp_ref[...], kvc_ref[...]], axis=0)
    kp = kv[:, :LANES]
    vp = kv[:, LANES:]
    kr = pltpu.roll(kp.astype(F32), HEAD_DIM, 1).astype(BF16)
    vr = pltpu.roll(vp.astype(F32), HEAD_DIM, 1).astype(BF16)
    qi = lax.broadcasted_iota(jnp.int32, (BLOCK, 2 * BLOCK), 0)
    kj = lax.broadcasted_iota(jnp.int32, (BLOCK, 2 * BLOCK), 1)
    rel = qi + BLOCK - kj
    valid = (rel >= 0) & (rel < BLOCK) & ((kj >= BLOCK) | (i > 0))
    return kp, vp, kr, vr, rel.astype(F32), valid


def _head_scores(q_ref, h, lo, kp, kr, relf, valid):
    p, e, kk = h // 2, h % 2, h // 4
    qp = q_ref[:, p * LANES:(p + 1) * LANES]
    hm = lo if e == 0 else jnp.logical_not(lo)
    qm = jnp.where(hm, qp, jnp.zeros_like(qp))
    s = _dot_nt(qm, kp if kk == e else kr)
    s = s * 0.125 - SLOPES[h] * relf
    return jnp.where(valid, s, NEG_INF), qm, hm, kk == e


def _attn_fwd(qkvn, sinks, riders=()):
    t = qkvn.shape[0]
    nb = t // BLOCK

    def body(sink_ref, q_ref, kvc_ref, kvp_ref, o_ref, lse_ref):
        i = pl.program_id(0)
        kp, vp, kr, vr, relf, valid = _attn_keys(i, kvp_ref, kvc_ref)
        lo = _lo_mask()
        lane = lax.broadcasted_iota(jnp.int32, (BLOCK, LANES), 1)
        lse_t = jnp.zeros((BLOCK, LANES), F32)
        outs = [jnp.zeros((BLOCK, LANES), F32) for _ in range(4)]
        for h in range(N_HEADS):
            s, _, hm, same = _head_scores(q_ref, h, lo, kp, kr, relf, valid)
            sink = sink_ref[h]
            m = jnp.maximum(jnp.max(s, axis=-1, keepdims=True), sink)
            pe = jnp.exp(s - m)
            l = jnp.sum(pe, axis=-1, keepdims=True) + jnp.exp(sink - m)
            lse_t = jnp.where(lane == h, m + jnp.log(l), lse_t)
            o = _dot((pe / l).astype(BF16), vp if same else vr)
            outs[h // 2] = outs[h // 2] + jnp.where(hm, o, 0.0)
        for p in range(4):
            o_ref[:, p * LANES:(p + 1) * LANES] = outs[p].astype(BF16)
        lse_ref[...] = lse_t

    return _call(
        body, (sinks, qkvn, qkvn, qkvn), name="attn_fwd", grid=(nb,), riders=riders,
        in_specs=[pl.BlockSpec(memory_space=pltpu.SMEM),
                  pl.BlockSpec((BLOCK, Q_COLS), lambda i: (i, 0)),
                  pl.BlockSpec((BLOCK, 2 * LANES), lambda i: (i, 2)),
                  pl.BlockSpec((BLOCK, 2 * LANES), lambda i: (jnp.maximum(i - 1, 0), 2))],
        out_specs=[pl.BlockSpec((BLOCK, Q_COLS), lambda i: (i, 0)), pl.BlockSpec((BLOCK, LANES), lambda i: (i, 0))],
        out_shape=[jax.ShapeDtypeStruct((t, Q_COLS), BF16), jax.ShapeDtypeStruct((t, LANES), F32)])


def _glu(pc):
    return pc[:, :CONV_W] * _sig(pc[:, CONV_W:])


def _group_norm_stats(seg, lo):
    mu = _half_mean(seg, lo)
    d = seg - mu
    rstd = lax.rsqrt(_half_mean(d * d, lo) + EPS)
    return d * rstd, rstd


def _conv_fwd(pc, cw, cb, ln_g, ln_b, riders=()):
    t = pc.shape[0]
    tt = _token_tile(t)

    def body(cur_ref, prev_ref, w_ref, b_ref, g_ref, bb_ref, y_ref, c_ref, scr):
        i = pl.program_id(0)
        scr[0:HALO, :] = _glu(prev_ref[...]) * (i > 0).astype(F32)
        scr[HALO:HALO + tt, :] = _glu(cur_ref[...])
        for cbk in range(CONV_W // LANES):
            cs = slice(cbk * LANES, (cbk + 1) * LANES)
            for rb in range(tt // ROW_CHUNK):
                r0 = rb * ROW_CHUNK
                acc = jnp.zeros((ROW_CHUNK, LANES), F32) + b_ref[:, cs]
                for k in range(CONV_K):
                    acc = acc + w_ref[k:k + 1, cs] * scr[r0 + 2 + k:r0 + 2 + k + ROW_CHUNK, cs]
                y_ref[r0:r0 + ROW_CHUNK, cs] = acc
        lo = _lo_mask()
        for p in range(CONV_W // LANES):
            cs = slice(p * LANES, (p + 1) * LANES)
            yh, _ = _group_norm_stats(y_ref[:, cs], lo)
            z = yh * g_ref[:, cs] + bb_ref[:, cs]
            c_ref[:, cs] = (z * _sig(z)).astype(BF16)

    hb = tt // HALO
    return _call(
        body, (pc, pc, cw, cb, ln_g, ln_b), name="conv_fwd", grid=(t // tt,), riders=riders,
        in_specs=[pl.BlockSpec((tt, 2 * CONV_W), lambda i: (i, 0)),
                  pl.BlockSpec((HALO, 2 * CONV_W), lambda i: (jnp.maximum(i * hb - 1, 0), 0)),
                  _full((CONV_K, CONV_W)), _full((1, CONV_W)), _full((1, CONV_W)), _full((1, CONV_W))],
        out_specs=[pl.BlockSpec((tt, CONV_W), lambda i: (i, 0)), pl.BlockSpec((tt, CONV_W), lambda i: (i, 0))],
        out_shape=[jax.ShapeDtypeStruct((t, CONV_W), F32), jax.ShapeDtypeStruct((t, CONV_W), BF16)],
        scratch=[pltpu.VMEM((tt + HALO, CONV_W), F32)])


def _out_proj(x, attn, c, w_out, b_out, g_ffn, riders=()):
    t = x.shape[0]
    tt = _token_tile(t)

    def body(x_ref, a_ref, c_ref, w_ref, b_ref, g_ref, x1_ref, h2_ref):
        x1 = x_ref[...] + _dot(a_ref[...], w_ref[0:Q_COLS, :]) + _dot(c_ref[...], w_ref[Q_COLS:, :]) + b_ref[...]
        x1_ref[...] = x1
        r = lax.rsqrt(jnp.mean(x1 * x1, axis=-1, keepdims=True) + EPS)
        h2_ref[...] = (x1 * r * g_ref[...]).astype(BF16)

    row = lambda w: pl.BlockSpec((tt, w), lambda i: (i, 0))
    return _call(
        body, (x, attn, c, w_out, b_out, g_ffn), name="out_proj", grid=(t // tt,), riders=riders,
        in_specs=[row(D_MODEL), row(Q_COLS), row(CONV_W), _full((D_MODEL, D_MODEL)), _full((1, D_MODEL)), _full((1, D_MODEL))],
        out_specs=[row(D_MODEL), row(D_MODEL)],
        out_shape=[jax.ShapeDtypeStruct((t, D_MODEL), F32), jax.ShapeDtypeStruct((t, D_MODEL), BF16)])


def _ffn_conv(scr, cur, dw_ref, db_ref, tt):
    return (dw_ref[0:1, :] * scr[FHALO - 2:FHALO - 2 + tt, :] + dw_ref[1:2, :] * scr[FHALO - 1:FHALO - 1 + tt, :]
            + dw_ref[2:3, :] * cur + db_ref[...])


def _ffn_up(h2, w_up, dw, db):
    t = h2.shape[0]
    tt = _token_tile(t)
    nj = D_FF // FFN_CB

    def body(hc_ref, hp_ref, wg_ref, wu_ref, dwg_ref, dwu_ref, dbg_ref, dbu_ref, hg_ref, hu_ref, act_ref, sg, su):
        i = pl.program_id(1)
        hc = hc_ref[...]
        hp = hp_ref[...] * (i > 0).astype(BF16)
        ups = []
        for w_ref, dw_ref, db_ref, o_ref, scr in ((wg_ref, dwg_ref, dbg_ref, hg_ref, sg), (wu_ref, dwu_ref, dbu_ref, hu_ref, su)):
            cur = _dot(hc, w_ref[...])
            o_ref[...] = cur
            scr[0:FHALO, :] = _dot(hp, w_ref[...])
            scr[FHALO:FHALO + tt, :] = cur
            ups.append(_ffn_conv(scr, cur, dw_ref, db_ref, tt))
        g, u = ups
        act_ref[...] = (g * _sig(g) * u).astype(BF16)

    fb = tt // FHALO
    colg = lambda r: pl.BlockSpec((r, FFN_CB), lambda j, i: (0, j))
    colu = lambda r: pl.BlockSpec((r, FFN_CB), lambda j, i: (0, j + nj))
    tile = pl.BlockSpec((tt, FFN_CB), lambda j, i: (i, j))
    return pl.pallas_call(
        body, name="ffn_up", grid=(nj, t // tt),
        in_specs=[pl.BlockSpec((tt, D_MODEL), lambda j, i: (i, 0)),
                  pl.BlockSpec((FHALO, D_MODEL), lambda j, i: (jnp.maximum(i * fb - 1, 0), 0)),
                  colg(D_MODEL), colu(D_MODEL), colg(3), colu(3), colg(1), colu(1)],
        out_specs=[tile, tile, tile],
        out_shape=[jax.ShapeDtypeStruct((t, D_FF), F32), jax.ShapeDtypeStruct((t, D_FF), F32),
                   jax.ShapeDtypeStruct((t, D_FF), BF16)],
        scratch_shapes=[pltpu.VMEM((tt + FHALO, FFN_CB), F32), pltpu.VMEM((tt + FHALO, FFN_CB), F32)],
        compiler_params=_params(("parallel", "parallel")),
    )(h2, h2, w_up, w_up, dw, dw, db, db)


def _ffn_down(act, w_down, x1, target):
    t = act.shape[0]
    tt = _token_tile(t)

    def body(a_ref, w_ref, x1_ref, t_ref, dy_ref, loss_ref):
        err = x1_ref[...] + _dot(a_ref[...], w_ref[...]) - t_ref[...]
        dy_ref[...] = err * (1.0 / D_MODEL)

        @pl.when(pl.program_id(0) == 0)
        def _():
            loss_ref[...] = jnp.zeros_like(loss_ref)

        loss_ref[...] += jnp.sum(err * err, axis=0, keepdims=True)

    row = lambda w: pl.BlockSpec((tt, w), lambda i: (i, 0))
    return pl.pallas_call(
        body, name="ffn_down", grid=(t // tt,),
        in_specs=[row(D_FF), _full((D_FF, D_MODEL)), row(D_MODEL), row(D_MODEL)],
        out_specs=[row(D_MODEL), _full((1, D_MODEL))],
        out_shape=[jax.ShapeDtypeStruct((t, D_MODEL), F32), jax.ShapeDtypeStruct((1, D_MODEL), F32)],
        compiler_params=_params(("arbitrary",)),
    )(act, w_down, x1, target)


def _ffn_bwd_act(dy, w_down, hg, hu, dw, db):
    t = dy.shape[0]
    tt = _token_tile(t)
    nj = D_FF // FFN_CB

    def body(dy_ref, wd_ref, gc_ref, gp_ref, uc_ref, up_ref, dwg_ref, dwu_ref, dbg_ref, dbu_ref,
             dg_ref, du_ref, gwg_ref, gwu_ref, gbg_ref, gbu_ref, sg, su):
        i = pl.program_id(1)
        first = (i > 0).astype(F32)
        d_act = _dot_nt(dy_ref[...].astype(BF16), wd_ref[...])
        ups = []
        for c_ref, p_ref, dw_ref, db_ref, scr in ((gc_ref, gp_ref, dwg_ref, dbg_ref, sg), (uc_ref, up_ref, dwu_ref, dbu_ref, su)):
            scr[0:FHALO, :] = p_ref[...] * first
            scr[FHALO:FHALO + tt, :] = c_ref[...]
            ups.append(_ffn_conv(scr, c_ref[...], dw_ref, db_ref, tt))
        g, u = ups
        s = _sig(g)
        d_u = d_act * (g * s)
        d_g = d_act * u * (s * (1.0 + g * (1.0 - s)))

        @pl.when(i == 0)
        def _():
            for r in (gwg_ref, gwu_ref, gbg_ref, gbu_ref):
                r[...] = jnp.zeros_like(r)

        for d, o_ref, gw_ref, gb_ref, scr in ((d_g, dg_ref, gwg_ref, gbg_ref, sg), (d_u, du_ref, gwu_ref, gbu_ref, su)):
            o_ref[...] = d.astype(BF16)
            gb_ref[...] += jnp.sum(d, axis=0, keepdims=True)
            rows = [jnp.sum(d * scr[FHALO - 2 + k:FHALO - 2 + k + tt, :], axis=0, keepdims=True) for k in range(3)]
            gw_ref[...] += _rows_to_tile(rows, 8)

    fb = tt // FHALO
    tile = pl.BlockSpec((tt, FFN_CB), lambda j, i: (i, j))
    prev = pl.BlockSpec((FHALO, FFN_CB), lambda j, i: (jnp.maximum(i * fb - 1, 0), j))
    colg = lambda r: pl.BlockSpec((r, FFN_CB), lambda j, i: (0, j))
    colu = lambda r: pl.BlockSpec((r, FFN_CB), lambda j, i: (0, j + nj))
    acc = lambda r: pl.BlockSpec((r, FFN_CB), lambda j, i: (0, j))
    return pl.pallas_call(
        body, name="ffn_bwd_act", grid=(nj, t // tt),
        in_specs=[pl.BlockSpec((tt, D_MODEL), lambda j, i: (i, 0)), pl.BlockSpec((FFN_CB, D_MODEL), lambda j, i: (j, 0)),
                  tile, prev, tile, prev, colg(3), colu(3), colg(1), colu(1)],
        out_specs=[tile, tile, acc(8), acc(8), acc(1), acc(1)],
        out_shape=[jax.ShapeDtypeStruct((t, D_FF), BF16), jax.ShapeDtypeStruct((t, D_FF), BF16),
                   jax.ShapeDtypeStruct((8, D_FF), F32), jax.ShapeDtypeStruct((8, D_FF), F32),
                   jax.ShapeDtypeStruct((1, D_FF), F32), jax.ShapeDtypeStruct((1, D_FF), F32)],
        scratch_shapes=[pltpu.VMEM((tt + FHALO, FFN_CB), F32), pltpu.VMEM((tt + FHALO, FFN_CB), F32)],
        compiler_params=_params(("parallel", "arbitrary")),
    )(dy, w_down, hg, hg, hu, hu, dw, dw, db, db)


def _ffn_bwd_conv(dg, du, dw):
    t = dg.shape[0]
    tt = _token_tile(t)
    nj = D_FF // FFN_CB
    ni = t // tt

    def body(gc_ref, gn_ref, uc_ref, un_ref, dwg_ref, dwu_ref, og_ref, ou_ref, scr):
        i = pl.program_id(1)
        last = (i < ni - 1).astype(F32)
        for c_ref, n_ref, dw_ref, o_ref in ((gc_ref, gn_ref, dwg_ref, og_ref), (uc_ref, un_ref, dwu_ref, ou_ref)):
            cur = c_ref[...].astype(F32)
            scr[0:tt, :] = cur
            scr[tt:tt + FHALO, :] = n_ref[...].astype(F32) * last
            o_ref[...] = (dw_ref[2:3, :] * cur + dw_ref[1:2, :] * scr[1:1 + tt, :]
                          + dw_ref[0:1, :] * scr[2:2 + tt, :]).astype(BF16)

    fb = tt // FHALO
    tile = pl.BlockSpec((tt, FFN_CB), lambda j, i: (i, j))
    nxt = pl.BlockSpec((FHALO, FFN_CB), lambda j, i: (jnp.minimum((i + 1) * fb, t // FHALO - 1), j))
    return pl.pallas_call(
        body, name="ffn_bwd_conv", grid=(nj, ni),
        in_specs=[tile, nxt, tile, nxt, pl.BlockSpec((3, FFN_CB), lambda j, i: (0, j)),
                  pl.BlockSpec((3, FFN_CB), lambda j, i: (0, j + nj))],
        out_specs=[tile, tile],
        out_shape=[jax.ShapeDtypeStruct((t, D_FF), BF16), jax.ShapeDtypeStruct((t, D_FF), BF16)],
        scratch_shapes=[pltpu.VMEM((tt + FHALO, FFN_CB), F32)],
        compiler_params=_params(("parallel", "parallel")),
    )(dg, dg, du, du, dw, dw)


def _ffn_bwd_in(dhg, dhu, w_up, x1, dy, g_ffn, riders=()):
    t = x1.shape[0]
    tt = _token_tile(t)

    def body(dg_ref, du_ref, w_ref, x1_ref, dy_ref, g_ref, dx_ref, gg_ref):
        d_h2 = _dot_nt(dg_ref[...], w_ref[:, 0:D_FF]) + _dot_nt(du_ref[...], w_ref[:, D_FF:])
        x1 = x1_ref[...]
        r = lax.rsqrt(jnp.mean(x1 * x1, axis=-1, keepdims=True) + EPS)
        xh = x1 * r
        gd = d_h2 * g_ref[...]
        dx_ref[...] = dy_ref[...] + r * (gd - xh * jnp.mean(gd * xh, axis=-1, keepdims=True))

        @pl.when(pl.program_id(0) == 0)
        def _():
            gg_ref[...] = jnp.zeros_like(gg_ref)

        gg_ref[...] += jnp.sum(d_h2 * xh, axis=0, keepdims=True)

    row = lambda w: pl.BlockSpec((tt, w), lambda i: (i, 0))
    return _call(
        body, (dhg, dhu, w_up, x1, dy, g_ffn), name="ffn_bwd_in", grid=(t // tt,), riders=riders,
        in_specs=[row(D_FF), row(D_FF), _full((D_MODEL, 2 * D_FF)), row(D_MODEL), row(D_MODEL), _full((1, D_MODEL))],
        out_specs=[row(D_MODEL), _full((1, D_MODEL))],
        out_shape=[jax.ShapeDtypeStruct((t, D_MODEL), F32), jax.ShapeDtypeStruct((1, D_MODEL), F32)])


def _grad_weight(a, b, nj, mb, name):
    t, m = a.shape
    n = b.shape[1]
    nb_ = n // nj
    tt = _token_tile(t)

    def body(a_ref, b_ref, o_ref):
        @pl.when(pl.program_id(2) == 0)
        def _():
            o_ref[...] = jnp.zeros_like(o_ref)

        o_ref[0] += _dot_tn(a_ref[...].astype(BF16), b_ref[...].astype(BF16))

    return pl.pallas_call(
        body, name=name, grid=(nj, m // mb, t // tt),
        in_specs=[pl.BlockSpec((tt, mb), lambda j, mi, i: (i, mi)), pl.BlockSpec((tt, nb_), lambda j, mi, i: (i, j))],
        out_specs=pl.BlockSpec((1, mb, nb_), lambda j, mi, i: (j, mi, 0)),
        out_shape=jax.ShapeDtypeStruct((nj, m, nb_), F32),
        compiler_params=_params(("parallel", "parallel", "arbitrary")),
    )(a, b)


def _out_proj_bwd(dx1, w_out):
    t = dx1.shape[0]
    tt = _token_tile(t)

    def body(d_ref, w_ref, dm_ref, gb_ref):
        d = d_ref[...]
        dm_ref[...] = _dot_nt(d.astype(BF16), w_ref[...])

        @pl.when(pl.program_id(0) == 0)
        def _():
            gb_ref[...] = jnp.zeros_like(gb_ref)

        gb_ref[...] += jnp.sum(d, axis=0, keepdims=True)

    row = pl.BlockSpec((tt, D_MODEL), lambda i: (i, 0))
    return pl.pallas_call(
        body, name="out_proj_bwd", grid=(t // tt,),
        in_specs=[row, _full((D_MODEL, D_MODEL))],
        out_specs=[row, _full((1, D_MODEL))],
        out_shape=[jax.ShapeDtypeStruct((t, D_MODEL), F32), jax.ShapeDtypeStruct((1, D_MODEL), F32)],
        compiler_params=_params(("arbitrary",)),
    )(dx1, w_out)


def _conv_bwd(dmix, y, pc, cw, ln_g, ln_b, riders=()):
    t = y.shape[0]
    tt = _token_tile(t)
    ni = t // tt
    ncb = CONV_W // LANES

    def body(dc_ref, dcn_ref, y_ref, yn_ref, pc_ref, pcp_ref, w_ref, g_ref, bb_ref,
             dp_ref, gw_ref, gb_ref, gg_ref, gbb_ref, gbin_ref, scr_d, scr_c, scr_o):
        i = pl.program_id(0)
        lo = _lo_mask()

        @pl.when(i == 0)
        def _():
            for r in (gw_ref, gb_ref, gg_ref, gbb_ref, gbin_ref):
                r[...] = jnp.zeros_like(r)

        def norm_bwd(dc, yv, cs):
            yh, rstd = _group_norm_stats(yv, lo)
            z = yh * g_ref[:, cs] + bb_ref[:, cs]
            s = _sig(z)
            dz = dc * (s * (1.0 + z * (1.0 - s)))
            dyh = dz * g_ref[:, cs]
            d_y = rstd * (dyh - _half_mean(dyh, lo) - yh * _half_mean(dyh * yh, lo))
            return d_y, dz, yh

        for p in range(ncb):
            cs = slice(p * LANES, (p + 1) * LANES)
            d_y, dz, yh = norm_bwd(dc_ref[:, cs], y_ref[:, cs], cs)
            scr_d[0:tt, cs] = d_y
            gg_ref[:, cs] += jnp.sum(dz * yh, axis=0, keepdims=True)
            gbb_ref[:, cs] += jnp.sum(dz, axis=0, keepdims=True)
            gb_ref[:, cs] += jnp.sum(d_y, axis=0, keepdims=True)
            d_yn, _, _ = norm_bwd(dcn_ref[:, cs], yn_ref[:, cs], cs)
            scr_d[tt:tt + HALO, cs] = d_yn * (i < ni - 1).astype(F32)
        scr_c[0:HALO, :] = _glu(pcp_ref[...]) * (i > 0).astype(F32)
        scr_c[HALO:HALO + tt, :] = _glu(pc_ref[...])

        rid = lax.broadcasted_iota(jnp.int32, (HALO, LANES), 0)
        for cbk in range(ncb):
            cs = slice(cbk * LANES, (cbk + 1) * LANES)
            for rb in range(tt // ROW_CHUNK):
                r0 = rb * ROW_CHUNK
                acc = jnp.zeros((ROW_CHUNK, LANES), F32)
                for k in range(CONV_K):
                    acc = acc + w_ref[k:k + 1, cs] * scr_d[r0 + 30 - k:r0 + 30 - k + ROW_CHUNK, cs]
                scr_o[r0:r0 + ROW_CHUNK, cs] = acc
            gwt = jnp.zeros((HALO, LANES), F32)
            for k in range(CONV_K):
                acc = jnp.zeros((ROW_CHUNK, LANES), F32)
                for rb in range(tt // ROW_CHUNK):
                    r0 = rb * ROW_CHUNK
                    acc = acc + scr_d[r0:r0 + ROW_CHUNK, cs] * scr_c[r0 + 2 + k:r0 + 2 + k + ROW_CHUNK, cs]
                gwt = jnp.where(rid == k, jnp.sum(acc, axis=0, keepdims=True), gwt)
            gw_ref[:, cs] += gwt
        d_c0 = scr_o[...]
        a = pc_ref[:, 0:CONV_W]
        s = _sig(pc_ref[:, CONV_W:])
        d_a = d_c0 * s
        d_gate = d_c0 * a * s * (1.0 - s)
        dp_ref[:, 0:CONV_W] = d_a.astype(BF16)
        dp_ref[:, CONV_W:] = d_gate.astype(BF16)
        gbin_ref[:, 0:CONV_W] += jnp.sum(d_a, axis=0, keepdims=True)
        gbin_ref[:, CONV_W:] += jnp.sum(d_gate, axis=0, keepdims=True)

    hb = tt // HALO
    nxt = lambda col: pl.BlockSpec((HALO, CONV_W), lambda i: (jnp.minimum((i + 1) * hb, t // HALO - 1), col))
    return _call(
        body, (dmix, dmix, y, y, pc, pc, cw, ln_g, ln_b), name="conv_bwd", grid=(ni,), riders=riders,
        in_specs=[pl.BlockSpec((tt, CONV_W), lambda i: (i, 1)), nxt(1),
                  pl.BlockSpec((tt, CONV_W), lambda i: (i, 0)), nxt(0),
                  pl.BlockSpec((tt, 2 * CONV_W), lambda i: (i, 0)),
                  pl.BlockSpec((HALO, 2 * CONV_W), lambda i: (jnp.maximum(i * hb - 1, 0), 0)),
                  _full((CONV_K, CONV_W)), _full((1, CONV_W)), _full((1, CONV_W))],
        out_specs=[pl.BlockSpec((tt, 2 * CONV_W), lambda i: (i, 0)), _full((HALO, CONV_W)), _full((1, CONV_W)),
                   _full((1, CONV_W)), _full((1, CONV_W)), _full((1, 2 * CONV_W))],
        out_shape=[jax.ShapeDtypeStruct((t, 2 * CONV_W), BF16), jax.ShapeDtypeStruct((HALO, CONV_W), F32),
                   jax.ShapeDtypeStruct((1, CONV_W), F32), jax.ShapeDtypeStruct((1, CONV_W), F32),
                   jax.ShapeDtypeStruct((1, CONV_W), F32), jax.ShapeDtypeStruct((1, 2 * CONV_W), F32)],
        scratch=[pltpu.VMEM((tt + HALO, CONV_W), F32), pltpu.VMEM((tt + HALO, CONV_W), F32),
                 pltpu.VMEM((tt, CONV_W), F32)])


def _attn_bwd(qkvn, dmix, lse, sinks, riders=()):
    t = qkvn.shape[0]
    nb = t // BLOCK

    def body(sink_ref, q_ref, kvc_ref, kvp_ref, do_ref, lse_ref, dq_ref, dcur_ref, dprev_ref, ds_ref):
        i = pl.program_id(0)
        kp, vp, kr, vr, relf, valid = _attn_keys(i, kvp_ref, kvc_ref)
        lo = _lo_mask()
        lane1 = lax.broadcasted_iota(jnp.int32, (1, LANES), 1)
        lane = lax.broadcasted_iota(jnp.int32, (BLOCK, LANES), 1)
        lse_t = lse_ref[...]
        zkv = lambda: jnp.zeros((2 * BLOCK, LANES), F32)
        dk_in, dk_roll, dv_in, dv_roll = zkv(), zkv(), zkv(), zkv()
        dqs = [jnp.zeros((BLOCK, LANES), F32) for _ in range(4)]
        dsink = jnp.zeros((1, LANES), F32)
        for h in range(N_HEADS):
            s, qm, hm, same = _head_scores(q_ref, h, lo, kp, kr, relf, valid)
            lse_h = jnp.sum(jnp.where(lane == h, lse_t, 0.0), axis=-1, keepdims=True)
            prob = jnp.exp(s - lse_h)
            do_m = jnp.where(hm, do_ref[:, (h // 2) * LANES:(h // 2 + 1) * LANES], 0.0).astype(BF16)
            dp = _dot_nt(do_m, vp if same else vr)
            dsum = jnp.sum(prob * dp, axis=-1, keepdims=True)
            dsc = prob * (dp - dsum)
            psink = jnp.exp(sink_ref[h] - lse_h)
            dsink = dsink + jnp.where(lane1 == h, jnp.sum(-psink * dsum, axis=0, keepdims=True), 0.0)
            dsb = (dsc * 0.125).astype(BF16)
            dqs[h // 2] = dqs[h // 2] + jnp.where(hm, _dot(dsb, kp if same else kr), 0.0)
            dk_c = _dot_tn(dsb, qm)
            dv_c = _dot_tn(prob.astype(BF16), do_m)
            if same:
                dk_in, dv_in = dk_in + dk_c, dv_in + dv_c
            else:
                dk_roll, dv_roll = dk_roll + dk_c, dv_roll + dv_c
        for p in range(4):
            dq_ref[:, p * LANES:(p + 1) * LANES] = dqs[p]
        dk = dk_in + pltpu.roll(dk_roll, HEAD_DIM, 1)
        dv = dv_in + pltpu.roll(dv_roll, HEAD_DIM, 1)
        dprev_ref[:, 0:LANES] = dk[0:BLOCK]
        dprev_ref[:, LANES:] = dv[0:BLOCK]
        dcur_ref[:, 0:LANES] = dk[BLOCK:]
        dcur_ref[:, LANES:] = dv[BLOCK:]

        @pl.when(i == 0)
        def _():
            ds_ref[...] = jnp.zeros_like(ds_ref)

        ds_ref[...] += dsink

    blk = lambda w: pl.BlockSpec((BLOCK, w), lambda i: (i, 0))
    return _call(
        body, (sinks, qkvn, qkvn, qkvn, dmix, lse), name="attn_bwd", grid=(nb,), riders=riders,
        in_specs=[pl.BlockSpec(memory_space=pltpu.SMEM), blk(Q_COLS),
                  pl.BlockSpec((BLOCK, 2 * LANES), lambda i: (i, 2)),
                  pl.BlockSpec((BLOCK, 2 * LANES), lambda i: (jnp.maximum(i - 1, 0), 2)),
                  blk(Q_COLS), blk(LANES)],
        out_specs=[blk(Q_COLS), blk(2 * LANES), blk(2 * LANES), _full((1, LANES))],
        out_shape=[jax.ShapeDtypeStruct((t, Q_COLS), F32), jax.ShapeDtypeStruct((t, 2 * LANES), F32),
                   jax.ShapeDtypeStruct((t, 2 * LANES), F32), jax.ShapeDtypeStruct((1, LANES), F32)])


def _qk_norm_bwd(dqn, dcur, dprev, pq, gq2, gk2):
    t = dqn.shape[0]
    nb = t // BLOCK

    def body(dq_ref, dc_ref, dn_ref, pq_ref, gq_ref, gk_ref, dp_ref, gbin_ref, gg_ref):
        i = pl.program_id(0)
        lo = _lo_mask()
        dkv = dc_ref[...] + dn_ref[...] * (i < nb - 1).astype(F32)

        @pl.when(i == 0)
        def _():
            gbin_ref[...] = jnp.zeros_like(gbin_ref)
            gg_ref[...] = jnp.zeros_like(gg_ref)

        for p in range(5):
            cs = slice(p * LANES, (p + 1) * LANES)
            seg = pq_ref[:, cs]
            dn = dq_ref[:, cs] if p < 4 else dkv[:, 0:LANES]
            gain = gq_ref[...] if p < 4 else gk_ref[...]
            rr = lax.rsqrt(_half_mean(seg * seg, lo) + EPS)
            xh = seg * rr
            gd = dn * gain
            d = rr * (gd - xh * _half_mean(gd * xh, lo))
            dp_ref[:, cs] = d.astype(BF16)
            gbin_ref[:, cs] += jnp.sum(d, axis=0, keepdims=True)
            gg_ref[:, cs] += jnp.sum(dn * xh, axis=0, keepdims=True)
        dv = dkv[:, LANES:]
        dp_ref[:, 640:768] = dv.astype(BF16)
        gbin_ref[:, 640:768] += jnp.sum(dv, axis=0, keepdims=True)

    blk = lambda w: pl.BlockSpec((BLOCK, w), lambda i: (i, 0))
    return pl.pallas_call(
        body, name="qk_norm_bwd", grid=(nb,),
        in_specs=[blk(Q_COLS), blk(2 * LANES), pl.BlockSpec((BLOCK, 2 * LANES), lambda i: (jnp.minimum(i + 1, nb - 1), 0)),
                  blk(QKV_COLS), _full((1, LANES)), _full((1, LANES))],
        out_specs=[blk(QKV_COLS), _full((1, QKV_COLS)), _full((1, 5 * LANES))],
        out_shape=[jax.ShapeDtypeStruct((t, QKV_COLS), BF16), jax.ShapeDtypeStruct((1, QKV_COLS), F32),
                   jax.ShapeDtypeStruct((1, 5 * LANES), F32)],
        compiler_params=_params(("arbitrary",)),
    )(dqn, dcur, dprev, pq, gq2, gk2)


def _in_proj_bwd(dpq, dpc, w_in, x, dx1, g_mix):
    t = x.shape[0]
    tt = _token_tile(t)

    def body(dq_ref, dc_ref, w_ref, x_ref, d1_ref, g_ref, gx_ref, gg_ref):
        d_h = _dot_nt(dq_ref[...], w_ref[:, 0:QKV_COLS]) + _dot_nt(dc_ref[...], w_ref[:, QKV_COLS:])
        xv = x_ref[...]
        r = lax.rsqrt(jnp.mean(xv * xv, axis=-1, keepdims=True) + EPS)
        xh = xv * r
        gd = d_h * g_ref[...]
        gx_ref[...] = d1_ref[...] + r * (gd - xh * jnp.mean(gd * xh, axis=-1, keepdims=True))

        @pl.when(pl.program_id(0) == 0)
        def _():
            gg_ref[...] = jnp.zeros_like(gg_ref)

        gg_ref[...] += jnp.sum(d_h * xh, axis=0, keepdims=True)

    row = lambda w: pl.BlockSpec((tt, w), lambda i: (i, 0))
    return pl.pallas_call(
        body, name="in_proj_bwd", grid=(t // tt,),
        in_specs=[row(QKV_COLS), row(2 * CONV_W), _full((D_MODEL, IN_COLS)), row(D_MODEL), row(D_MODEL), _full((1, D_MODEL))],
        out_specs=[row(D_MODEL), _full((1, D_MODEL))],
        out_shape=[jax.ShapeDtypeStruct((t, D_MODEL), F32), jax.ShapeDtypeStruct((1, D_MODEL), F32)],
        compiler_params=_params(("arbitrary",)),
    )(dpq, dpc, w_in, x, dx1, g_mix)


def _row_block(r):
    if r <= 256:
        return r
    return max(b for b in range(8, 257, 8) if r % b == 0)


def _adamw(w, g, m, v, name):
    r, c = w.shape
    rb = _row_block(r)

    def body(w_ref, g_ref, m_ref, v_ref, d_ref, nm_ref, nv_ref):
        gv = g_ref[...]
        nm = ADAM_B1 * m_ref[...] + (1.0 - ADAM_B1) * gv
        nv = ADAM_B2 * v_ref[...] + (1.0 - ADAM_B2) * (gv * gv)
        m_hat = nm / (1.0 - ADAM_B1 ** ADAM_STEP)
        v_hat = nv / (1.0 - ADAM_B2 ** ADAM_STEP)
        d_ref[...] = -ADAM_LR * (m_hat / (jnp.sqrt(v_hat) + ADAM_EPS) + ADAM_WD * w_ref[...])
        nm_ref[...] = nm
        nv_ref[...] = nv

    blk = pl.BlockSpec((rb, c), lambda i: (i, 0))
    shp = jax.ShapeDtypeStruct((r, c), F32)
    return pl.pallas_call(
        body, name=name, grid=(r // rb,), in_specs=[blk] * 4, out_specs=[blk] * 3, out_shape=[shp] * 3,
        compiler_params=_params(("parallel",)),
    )(w, g, m, v)


def _place():
    x, y, c = lax.axis_index("x"), lax.axis_index("y"), lax.axis_index("c")
    chips = [(1 - x, y), (x, 1 - y), (1 - x, 1 - y)]
    return x, y, c, chips


def _gather_all(v):
    r = v.shape[0]

    def body(v_ref, all_ref, sum_ref, send_sems, recv_sems):
        x, y, c, _ = _place()
        me = 4 * x + 2 * y + c
        all_ref[me] = v_ref[...]
        copies = []
        for k in range(1, 8):
            kx, ky, kc = (k >> 2) & 1, (k >> 1) & 1, k & 1
            peer = (x ^ kx, y ^ ky, c ^ kc)
            cp = pltpu.make_async_remote_copy(src_ref=v_ref, dst_ref=all_ref.at[me], send_sem=send_sems.at[k - 1],
                                              recv_sem=recv_sems.at[k - 1], device_id=peer, device_id_type=MESH)
            cp.start()
            copies.append((cp, 4 * peer[0] + 2 * peer[1] + peer[2]))
        for k, (cp, src_idx) in enumerate(copies):
            pltpu.make_async_remote_copy(src_ref=v_ref, dst_ref=all_ref.at[src_idx], send_sem=send_sems.at[k],
                                         recv_sem=recv_sems.at[k], device_id=(x, y, c), device_id_type=MESH).wait_recv()
        for cp, _ in copies:
            cp.wait_send()
        tot = all_ref[0]
        for d in range(1, 8):
            tot = tot + all_ref[d]
        sum_ref[...] = tot

    vm = pl.BlockSpec(memory_space=pltpu.VMEM)
    return pl.pallas_call(
        body, name="gather_all", in_specs=[vm], out_specs=[vm, vm],
        out_shape=[jax.ShapeDtypeStruct((8, r, LANES), v.dtype), jax.ShapeDtypeStruct((r, LANES), v.dtype)],
        scratch_shapes=[pltpu.SemaphoreType.DMA((7,)), pltpu.SemaphoreType.DMA((7,))],
        compiler_params=pltpu.CompilerParams(vmem_limit_bytes=VMEM_LIMIT),
    )(v)


def _remote(src, dst, send_sem, recv_sem, to):
    return pltpu.make_async_remote_copy(src_ref=src, dst_ref=dst, send_sem=send_sem, recv_sem=recv_sem,
                                        device_id=to, device_id_type=MESH)


def _dma_sems(*shape):
    return pltpu.SemaphoreType.DMA(shape)


def _gather_first(shards):
    n = len(shards)

    def copies(ins, outs, sems):
        x, y, c, chips = _place()
        me = 2 * x + y
        local = [pltpu.make_async_copy(ins[a], outs[a].at[me], sems[2].at[a]) for a in range(n)]
        sends = [_remote(ins[a].at[c], outs[a].at[me, c], sems[0].at[a, j], sems[1].at[a, j], (*chip, c))
                 for a in range(n) for j, chip in enumerate(chips)]
        lands = [_remote(ins[a].at[c], outs[a].at[2 * chip[0] + chip[1], c], sems[0].at[a, j], sems[1].at[a, j], (x, y, c))
                 for a in range(n) for j, chip in enumerate(chips)]
        return local, sends, lands

    def start(ins, outs, sems):
        local, sends, _ = copies(ins, outs, sems)
        for cp in local + sends:
            cp.start()

    def finish(ins, outs, sems):
        local, sends, lands = copies(ins, outs, sems)
        for cp in lands:
            cp.wait_recv()
        for cp in sends:
            cp.wait_send()
        for cp in local:
            cp.wait()

    return _Rider(shards, [jax.ShapeDtypeStruct((4,) + s.shape, s.dtype) for s in shards],
                  [_dma_sems(n, 3), _dma_sems(n, 3), _dma_sems(n)], start, finish)


def _gather_second(partials):
    n = len(partials)

    def copies(outs, sems):
        x, y, c, chips = _place()
        sends, lands = [], []
        for a in range(n):
            for j, chip in enumerate(chips):
                mine = outs[a].at[2 * chip[0] + chip[1], c]
                theirs = outs[a].at[2 * chip[0] + chip[1], 1 - c]
                sends.append(_remote(mine, mine, sems[0].at[a, j], sems[1].at[a, j], (x, y, 1 - c)))
                lands.append(_remote(theirs, theirs, sems[0].at[a, j], sems[1].at[a, j], (x, y, c)))
        return sends, lands

    def start(ins, outs, sems):
        for cp in copies(outs, sems)[0]:
            cp.start()

    def finish(ins, outs, sems):
        sends, lands = copies(outs, sems)
        for cp in lands:
            cp.wait_recv()
        for cp in sends:
            cp.wait_send()

    return _Rider(partials, [jax.ShapeDtypeStruct(p.shape, p.dtype) for p in partials],
                  [_dma_sems(n, 3), _dma_sems(n, 3)], start, finish, aliases={a: a for a in range(n)})


def _swap_halves(grads):
    n = len(grads)

    def copies(ins, outs, sems):
        x, y, c, _ = _place()
        return [_remote(ins[a].at[j, 1 - c], outs[a].at[j], sems[0].at[a, j], sems[1].at[a, j], (x, y, 1 - c))
                for a in range(n) for j in range(4)]

    def start(ins, outs, sems):
        for cp in copies(ins, outs, sems):
            cp.start()

    def finish(ins, outs, sems):
        for cp in copies(ins, outs, sems):
            cp.wait()

    return _Rider(grads, [jax.ShapeDtypeStruct((4,) + g.shape[2:], g.dtype) for g in grads],
                  [_dma_sems(n, 4), _dma_sems(n, 4)], start, finish)


def _add_sibling(g, got, c_idx, name):
    _, _, h, c = g.shape

    def body(s_ref, a_ref, b_ref, o_ref):
        o_ref[...] = (a_ref[...] + b_ref[...]).astype(BF16)

    return pl.pallas_call(
        body, name=name,
        grid_spec=pltpu.PrefetchScalarGridSpec(
            num_scalar_prefetch=1, grid=(4,),
            in_specs=[pl.BlockSpec((None, None, h, c), lambda j, s: (j, s[0], 0, 0)),
                      pl.BlockSpec((None, h, c), lambda j, s: (j, 0, 0))],
            out_specs=pl.BlockSpec((None, h, c), lambda j, s: (j, 0, 0))),
        out_shape=jax.ShapeDtypeStruct((4, h, c), BF16),
        compiler_params=_params(("parallel",)),
    )(c_idx, g, got)


def _exchange_chips(parts):
    n = len(parts)

    def copies(ins, outs, sems):
        x, y, c, chips = _place()
        return [_remote(ins[a].at[2 * chip[0] + chip[1]], outs[a].at[j], sems[0].at[a, j], sems[1].at[a, j], (*chip, c))
                for a in range(n) for j, chip in enumerate(chips)]

    def start(ins, outs, sems):
        for cp in copies(ins, outs, sems):
            cp.start()

    def finish(ins, outs, sems):
        for cp in copies(ins, outs, sems):
            cp.wait()

    return _Rider(parts, [jax.ShapeDtypeStruct((3,) + p.shape[1:], p.dtype) for p in parts],
                  [_dma_sems(n, 3), _dma_sems(n, 3)], start, finish)


def _add_chips(part, got, chip_idx, name):
    _, h, c = part.shape

    def body(s_ref, a_ref, b_ref, o_ref):
        o_ref[...] = ((a_ref[...].astype(F32) + b_ref[0].astype(F32)) + b_ref[1].astype(F32)) + b_ref[2].astype(F32)

    return pl.pallas_call(
        body, name=name,
        grid_spec=pltpu.PrefetchScalarGridSpec(
            num_scalar_prefetch=1, grid=(1,),
            in_specs=[pl.BlockSpec((None, h, c), lambda i, s: (s[0], 0, 0)),
                      pl.BlockSpec((3, h, c), lambda i, s: (0, 0, 0))],
            out_specs=pl.BlockSpec((h, c), lambda i, s: (0, 0))),
        out_shape=jax.ShapeDtypeStruct((h, c), F32),
        compiler_params=_params(("arbitrary",)),
    )(chip_idx, part, got)


def _join_halves(halves):
    n = len(halves)

    def copies(ins, outs, sems):
        x, y, c, _ = _place()
        local = [pltpu.make_async_copy(ins[a], outs[a].at[c], sems[2].at[a]) for a in range(n)]
        sends = [_remote(ins[a], outs[a].at[c], sems[0].at[a], sems[1].at[a], (x, y, 1 - c)) for a in range(n)]
        lands = [_remote(ins[a], outs[a].at[1 - c], sems[0].at[a], sems[1].at[a], (x, y, c)) for a in range(n)]
        return local, sends, lands

    def start(ins, outs, sems):
        local, sends, _ = copies(ins, outs, sems)
        for cp in local + sends:
            cp.start()

    def finish(ins, outs, sems):
        local, sends, lands = copies(ins, outs, sems)
        for cp in lands:
            cp.wait_recv()
        for cp in sends:
            cp.wait_send()
        for cp in local:
            cp.wait()

    return _Rider(halves, [jax.ShapeDtypeStruct((2,) + h.shape, h.dtype) for h in halves],
                  [_dma_sems(n), _dma_sems(n), _dma_sems(n)], start, finish)


def _pack(parts):
    flat = []
    for p in parts:
        p = p.reshape(-1).astype(F32)
        flat.append(jnp.pad(p, (0, (-p.shape[0]) % LANES)))
    v = jnp.concatenate(flat)
    v = jnp.pad(v, (0, (-v.shape[0]) % (8 * LANES)))
    return v.reshape(-1, LANES)


def _unpack(v, shapes):
    flat = v.reshape(-1)
    out, off = [], 0
    for s in shapes:
        n = 1
        for d in s:
            n *= d
        out.append(flat[off:off + n].reshape(s))
        off += n + (-n) % LANES
    return out


def kernel(x, mix_norm_gain, w_in, b_in, q_norm_gain, k_norm_gain, attn_sinks, conv_dw_w, conv_dw_b, conv_norm_gain, conv_norm_bias, w_out, b_out, ffn_norm_gain, w_up, ffn_dw_w, ffn_dw_b, w_down, loss_target, m_mix_norm_gain, m_w_in, m_b_in, m_q_norm_gain, m_k_norm_gain, m_attn_sinks, m_conv_dw_w, m_conv_dw_b, m_conv_norm_gain, m_conv_norm_bias, m_w_out, m_b_out, m_ffn_norm_gain, m_w_up, m_ffn_dw_w, m_ffn_dw_b, m_w_down, v_mix_norm_gain, v_w_in, v_b_in, v_q_norm_gain, v_k_norm_gain, v_attn_sinks, v_conv_dw_w, v_conv_dw_b, v_conv_norm_gain, v_conv_norm_bias, v_w_out, v_b_out, v_ffn_norm_gain, v_w_up, v_ffn_dw_w, v_ffn_dw_b, v_w_down):
    t = x.shape[1]
    xi, yi, ci = lax.axis_index("x"), lax.axis_index("y"), lax.axis_index("c")
    chip = 2 * xi + yi
    c_idx = jnp.reshape(ci, (1,)).astype(jnp.int32)
    chip_idx = jnp.reshape(chip, (1,)).astype(jnp.int32)
    x2 = x.reshape(t, D_MODEL)
    tgt = loss_target.reshape(t, D_MODEL)

    big = [w_in, w_out, w_up, w_down]
    halves = [w.astype(BF16).reshape(2, w.shape[0] // 2, w.shape[1]) for w in big]
    h_wi, h_wo, h_wu, h_wd = halves
    (p_wi,) = _run_riders([_gather_first([h_wi])], "gather_w_in_first")
    (g_wi,) = _run_riders([_gather_second([p_wi])], "gather_w_in_second")
    wi = jnp.concatenate([g_wi[j].reshape(D_MODEL, IN_COLS // 4) for j in range(4)], axis=1)
    small_w, _ = _gather_all(_pack([conv_dw_w, ffn_dw_w]))
    per_chip = [_unpack(small_w[4 * (j // 2) + 2 * (j % 2)], [conv_dw_w.shape, ffn_dw_w.shape]) for j in range(4)]
    cw = jnp.concatenate([p[0] for p in per_chip], axis=1)
    fw = jnp.concatenate([p[1] for p in per_chip], axis=1)

    row = lambda a: a.reshape(1, -1)
    gq2 = row(jnp.concatenate([q_norm_gain, q_norm_gain]))
    gk2 = row(jnp.concatenate([k_norm_gain, k_norm_gain]))

    (h1, pq, pc, qkvn), (p_wo,) = _fwd_in(x2, row(mix_norm_gain), wi, row(b_in), gq2, gk2, riders=[_gather_first([h_wo])])
    (attn, lse), (g_wo, p_wu) = _attn_fwd(qkvn, attn_sinks, riders=[_gather_second([p_wo]), _gather_first([h_wu])])
    (y_conv, c_act), (g_wu, p_wd) = _conv_fwd(pc, cw, row(conv_dw_b), row(conv_norm_gain), row(conv_norm_bias),
                                              riders=[_gather_second([p_wu]), _gather_first([h_wd])])
    wo = g_wo.reshape(D_MODEL, D_MODEL)
    (x1, h2), (g_wd,) = _out_proj(x2, attn, c_act, wo, row(b_out), row(ffn_norm_gain), riders=[_gather_second([p_wd])])
    wu = jnp.concatenate([g_wu[j].reshape(D_MODEL, 2 * D_FF // 4) for j in range(4)], axis=1)
    wd = g_wd.reshape(D_FF, D_MODEL)
    hg, hu, act = _ffn_up(h2, wu, fw, row(ffn_dw_b))
    dy, loss_cols = _ffn_down(act, wd, x1, tgt)

    split = lambda g: g.reshape(4, 2, g.shape[1] // 2, g.shape[2])
    dg, du, gfw_g, gfw_u, gfb_g, gfb_u = _ffn_bwd_act(dy, wd, hg, hu, fw, row(ffn_dw_b))
    dhg, dhu = _ffn_bwd_conv(dg, du, fw)
    gw_down = _grad_weight(act, dy, 1, FFN_CB, "grad_w_down")
    gw_up = jnp.concatenate([_grad_weight(h2, dhg, 2, D_MODEL, "grad_w_up_gate"),
                             _grad_weight(h2, dhu, 2, D_MODEL, "grad_w_up_lin")], axis=0)
    early = [split(gw_up), split(gw_down.reshape(4, D_FF // 4, D_MODEL))]
    early_names = ["w_up", "w_down"]
    (dx1, g_ffn_gain), got = _ffn_bwd_in(dhg, dhu, wu, x1, dy, row(ffn_norm_gain), riders=[_swap_halves(early)])
    early_part = [_add_sibling(g, r, c_idx, "add_sibling_" + nm_) for g, r, nm_ in zip(early, got, early_names)]
    dmix, g_b_out = _out_proj_bwd(dx1, wo)
    gw_out = jnp.concatenate([_grad_weight(attn, dx1, 1, Q_COLS, "grad_w_out_attn")[0],
                              _grad_weight(c_act, dx1, 1, CONV_W, "grad_w_out_conv")[0]], axis=0)
    (dpc, g_cw, g_cb, g_lng, g_lnb, gbin_c), got2 = _conv_bwd(dmix, y_conv, pc, cw, row(conv_norm_gain), row(conv_norm_bias),
                                                              riders=[_exchange_chips(early_part)])
    early_red = [_add_chips(p, r, chip_idx, "add_chips_" + nm_) for p, r, nm_ in zip(early_part, got2, early_names)]
    (dqn, dcur, dprev, g_sink), early_g = _attn_bwd(qkvn, dmix, lse, attn_sinks, riders=[_join_halves(early_red)])
    dpq, gbin_q, g_qk = _qk_norm_bwd(dqn, dcur, dprev, pq, gq2, gk2)
    grad_x, g_mix_gain = _in_proj_bwd(dpq, dpc, wi, x2, dx1, row(mix_norm_gain))
    gw_in = jnp.concatenate([_grad_weight(h1, dpq, 1, D_MODEL, "grad_w_in_qkv")[0],
                             _grad_weight(h1, dpc, 1, D_MODEL, "grad_w_in_conv")[0]], axis=1)

    g_qk = g_qk.reshape(5, 2, HEAD_DIM)
    small = [g_mix_gain, jnp.concatenate([gbin_q, gbin_c], axis=1), g_qk[:4].sum(axis=(0, 1)), g_qk[4].sum(axis=0),
             g_sink[0, :N_HEADS], g_cb, g_lng, g_lnb, g_b_out, g_ffn_gain, jnp.concatenate([gfb_g, gfb_u], axis=1),
             loss_cols, g_cw[:CONV_K], jnp.concatenate([gfw_g[:3], gfw_u[:3]], axis=1)]
    _, tot = _gather_all(_pack(small))
    rep_names = ["mix_norm_gain", "b_in", "q_norm_gain", "k_norm_gain", "attn_sinks", "conv_dw_b", "conv_norm_gain",
                 "conv_norm_bias", "b_out", "ffn_norm_gain", "ffn_dw_b"]
    rep_w = [mix_norm_gain, b_in, q_norm_gain, k_norm_gain, attn_sinks, conv_dw_b, conv_norm_gain, conv_norm_bias,
             b_out, ffn_norm_gain, ffn_dw_b]
    rep_m = [m_mix_norm_gain, m_b_in, m_q_norm_gain, m_k_norm_gain, m_attn_sinks, m_conv_dw_b, m_conv_norm_gain,
             m_conv_norm_bias, m_b_out, m_ffn_norm_gain, m_ffn_dw_b]
    rep_v = [v_mix_norm_gain, v_b_in, v_q_norm_gain, v_k_norm_gain, v_attn_sinks, v_conv_dw_b, v_conv_norm_gain,
             v_conv_norm_bias, v_b_out, v_ffn_norm_gain, v_ffn_dw_b]
    shapes = [w.shape for w in rep_w] + [(D_MODEL,), (CONV_K, CONV_W), (3, 2 * D_FF)]
    tot_parts = _unpack(tot, shapes)
    loss = (0.5 / D_MODEL) * jnp.sum(tot_parts[len(rep_w)])
    g_cw_full, g_fw_full = tot_parts[len(rep_w) + 1], tot_parts[len(rep_w) + 2]
    n_rep_rows = _pack(rep_w).shape[0]
    rep_d, rep_nm, rep_nv = _adamw(_pack(rep_w), tot[:n_rep_rows], _pack(rep_m), _pack(rep_v), "adamw_small")
    rep_shapes = [w.shape for w in rep_w]
    res = {}
    for nm_, g_, d_, m_, v_ in zip(rep_names, tot_parts, _unpack(rep_d, rep_shapes), _unpack(rep_nm, rep_shapes),
                                   _unpack(rep_nv, rep_shapes)):
        res[nm_] = (g_, d_, m_, v_)

    g_cw_mine = lax.dynamic_slice_in_dim(g_cw_full, chip * (CONV_W // 4), CONV_W // 4, axis=1)
    g_fw_mine = lax.dynamic_slice_in_dim(g_fw_full, chip * (2 * D_FF // 4), 2 * D_FF // 4, axis=1)
    res["conv_dw_w"] = (g_cw_mine, *_adamw(conv_dw_w, g_cw_mine, m_conv_dw_w, v_conv_dw_w, "adamw_conv_dw_w"))
    res["ffn_dw_w"] = (g_fw_mine, *_adamw(ffn_dw_w, g_fw_mine, m_ffn_dw_w, v_ffn_dw_w, "adamw_ffn_dw_w"))

    gw_in4 = gw_in.reshape(D_MODEL, 4, IN_COLS // 4).transpose(1, 0, 2)
    late = [split(gw_in4), split(gw_out.reshape(4, D_MODEL // 4, D_MODEL))]
    late_names = ["w_in", "w_out"]
    got = _run_riders([_swap_halves(late)], "swap_halves_late")
    late_part = [_add_sibling(g, r, c_idx, "add_sibling_" + nm_) for g, r, nm_ in zip(late, got, late_names)]
    got2 = _run_riders([_exchange_chips(late_part)], "exchange_chips_late")
    late_red = [_add_chips(p, r, chip_idx, "add_chips_" + nm_) for p, r, nm_ in zip(late_part, got2, late_names)]
    late_g = _run_riders([_join_halves(late_red)], "join_halves_late")
    names = ["w_in", "w_out", "w_up", "w_down"]
    shard_g = list(late_g) + list(early_g)
    for nm_, w_, g_, m_, v_ in zip(names, big, shard_g, [m_w_in, m_w_out, m_w_up, m_w_down], [v_w_in, v_w_out, v_w_up, v_w_down]):
        g_ = g_.reshape(w_.shape)
        res[nm_] = (g_, *_adamw(w_, g_, m_, v_, "adamw_" + nm_))

    order = ["mix_norm_gain", "w_in", "b_in", "q_norm_gain", "k_norm_gain", "attn_sinks", "conv_dw_w", "conv_dw_b",
             "conv_norm_gain", "conv_norm_bias", "w_out", "b_out", "ffn_norm_gain", "w_up", "ffn_dw_w", "ffn_dw_b", "w_down"]
    return (loss, grad_x.reshape(x.shape), *[res[n][0] for n in order], *[res[n][1] for n in order],
            *[res[n][2] for n in order], *[res[n][3] for n in order])
```

```python
import functools

import jax
import jax.numpy as jnp
from jax import lax
from jax.experimental import pallas as pl
from jax.experimental.pallas import tpu as pltpu

F32 = jnp.float32
BF16 = jnp.bfloat16
MESH = pl.DeviceIdType.MESH

D_MODEL = 1024
HEAD_DIM = 64
N_HEADS = 8
Q_COLS = 512
QKV_COLS = 768
CONV_W = 512
CONV_K = 31
IN_COLS = 1792
D_FF = 2816
FFN_CB = 1408
BLOCK = 128
LANES = 128
EPS = 1e-6
NEG_INF = -1e30
SLOPES = tuple(float(2.0 ** (-(h + 1.0))) for h in range(N_HEADS))
HALO = 32
FHALO = 16
ROW_CHUNK = 64
GRAD_TILE = 2048
VMEM_LIMIT = 56 * 1024 * 1024

ADAM_LR = 0.001
ADAM_B1 = 0.9
ADAM_B2 = 0.999
ADAM_EPS = 1e-08
ADAM_WD = 0.01
ADAM_STEP = 10


def _params(sem=None):
    kw = dict(vmem_limit_bytes=VMEM_LIMIT)
    if sem is not None:
        kw["dimension_semantics"] = sem
    return pltpu.CompilerParams(**kw)


def _token_tile(t):
    return 512 if t % 512 == 0 and t >= 2048 else 128


def _sig(v):
    return 1.0 / (1.0 + jnp.exp(-v))


def _lo_mask():
    return lax.broadcasted_iota(jnp.int32, (1, LANES), 1) < HEAD_DIM


def _half_mean(v, lo):
    s_lo = jnp.sum(jnp.where(lo, v, 0.0), axis=-1, keepdims=True)
    s_hi = jnp.sum(jnp.where(lo, 0.0, v), axis=-1, keepdims=True)
    return jnp.where(lo, s_lo, s_hi) * (1.0 / HEAD_DIM)


def _dot(a, b):
    return jnp.dot(a, b, preferred_element_type=F32)


def _dot_nt(a, b):
    return lax.dot_general(a, b, (((1,), (1,)), ((), ())), preferred_element_type=F32)


def _dot_tn(a, b):
    return lax.dot_general(a, b, (((0,), (0,)), ((), ())), preferred_element_type=F32)


def _full(shape):
    nd = len(shape)
    return pl.BlockSpec(shape, lambda *_: (0,) * nd)


def _rows_to_tile(rows, n_rows):
    c = rows[0].shape[-1]
    rid = lax.broadcasted_iota(jnp.int32, (n_rows, c), 0)
    out = jnp.zeros((n_rows, c), F32)
    for k, r in enumerate(rows):
        out = jnp.where(rid == k, r, out)
    return out


ANY = pl.BlockSpec(memory_space=pl.ANY)


class _Rider:
    def __init__(self, ins, outs, sems, start, finish, aliases=None):
        self.ins, self.outs, self.sems = list(ins), list(outs), list(sems)
        self.start, self.finish, self.aliases = start, finish, dict(aliases or {})


def _join_riders(riders):
    ins, outs, sems, aliases, spans = [], [], [], {}, []
    for r in riders:
        spans.append((len(ins), len(outs), len(sems), r))
        for a, b in r.aliases.items():
            aliases[len(ins) + a] = len(outs) + b
        ins += r.ins
        outs += r.outs
        sems += r.sems

    def each(which):
        def run(i_refs, o_refs, s_refs):
            for i0, o0, s0, r in spans:
                getattr(r, which)(i_refs[i0:i0 + len(r.ins)], o_refs[o0:o0 + len(r.outs)], s_refs[s0:s0 + len(r.sems)])
        return run

    return _Rider(ins, outs, sems, each("start"), each("finish"), aliases)


def _call(body, args, *, name, grid, in_specs, out_specs, out_shape, scratch=(), riders=()):
    in_specs, out_specs, out_shape, scratch = list(in_specs), list(out_specs), list(out_shape), list(scratch)
    sem = ("arbitrary",) * len(grid)
    if not riders:
        outs = pl.pallas_call(body, name=name, grid=grid, in_specs=in_specs, out_specs=out_specs, out_shape=out_shape,
                              scratch_shapes=scratch, compiler_params=_params(sem))(*args)
        return list(outs), []
    r = _join_riders(riders)
    n_in, n_out, n_scr = len(in_specs), len(out_specs), len(scratch)
    nri, nro = len(r.ins), len(r.outs)

    def full(*refs):
        ins, rin = refs[:n_in], refs[n_in:n_in + nri]
        o0 = n_in + nri
        outs, rout = refs[o0:o0 + n_out], refs[o0 + n_out:o0 + n_out + nro]
        s0 = o0 + n_out + nro
        scr, rsem = refs[s0:s0 + n_scr], refs[s0 + n_scr:]
        first = functools.reduce(jnp.logical_and, [pl.program_id(k) == 0 for k in range(len(grid))])
        last = functools.reduce(jnp.logical_and, [pl.program_id(k) == grid[k] - 1 for k in range(len(grid))])

        @pl.when(first)
        def _():
            r.start(rin, rout, rsem)

        body(*ins, *outs, *scr)

        @pl.when(last)
        def _():
            r.finish(rin, rout, rsem)

    outs = pl.pallas_call(
        full, name=name, grid=grid, in_specs=in_specs + [ANY] * nri, out_specs=out_specs + [ANY] * nro,
        out_shape=out_shape + r.outs, scratch_shapes=scratch + r.sems,
        input_output_aliases={n_in + a: n_out + b for a, b in r.aliases.items()},
        compiler_params=_params(sem))(*args, *r.ins)
    return list(outs[:n_out]), list(outs[n_out:])


def _run_riders(riders, name):
    r = _join_riders(riders)
    nri, nro = len(r.ins), len(r.outs)

    def body(*refs):
        rin, rout, rsem = refs[:nri], refs[nri:nri + nro], refs[nri + nro:]
        r.start(rin, rout, rsem)
        r.finish(rin, rout, rsem)

    outs = pl.pallas_call(body, name=name, in_specs=[ANY] * nri, out_specs=[ANY] * nro, out_shape=r.outs,
                          scratch_shapes=r.sems, input_output_aliases=r.aliases)(*r.ins)
    return list(outs)


def _fwd_in(x, g_mix, w_in, b_in, gq2, gk2, riders=()):
    t = x.shape[0]
    tt = _token_tile(t)

    def body(x_ref, g_ref, w_ref, b_ref, gq_ref, gk_ref, h1_ref, pq_ref, pc_ref, qkvn_ref):
        xv = x_ref[...]
        r = lax.rsqrt(jnp.mean(xv * xv, axis=-1, keepdims=True) + EPS)
        h = (xv * r * g_ref[...]).astype(BF16)
        h1_ref[...] = h
        proj = _dot(h, w_ref[...]) + b_ref[...]
        pq_ref[...] = proj[:, :QKV_COLS]
        pc_ref[...] = proj[:, QKV_COLS:]
        lo = _lo_mask()
        for p in range(5):
            seg = proj[:, p * LANES:(p + 1) * LANES]
            rr = lax.rsqrt(_half_mean(seg * seg, lo) + EPS)
            gain = gq_ref[...] if p < 4 else gk_ref[...]
            qkvn_ref[:, p * LANES:(p + 1) * LANES] = (seg * rr * gain).astype(BF16)
        qkvn_ref[:, 640:768] = proj[:, 640:768].astype(BF16)

    return _call(
        body, (x, g_mix, w_in, b_in, gq2, gk2), name="fwd_in", grid=(t // tt,), riders=riders,
        in_specs=[pl.BlockSpec((tt, D_MODEL), lambda i: (i, 0)), _full((1, D_MODEL)), _full((D_MODEL, IN_COLS)),
                  _full((1, IN_COLS)), _full((1, LANES)), _full((1, LANES))],
        out_specs=[pl.BlockSpec((tt, D_MODEL), lambda i: (i, 0)), pl.BlockSpec((tt, QKV_COLS), lambda i: (i, 0)),
                   pl.BlockSpec((tt, 2 * CONV_W), lambda i: (i, 0)), pl.BlockSpec((tt, QKV_COLS), lambda i: (i, 0))],
        out_shape=[jax.ShapeDtypeStruct((t, D_MODEL), BF16), jax.ShapeDtypeStruct((t, QKV_COLS), F32),
                   jax.ShapeDtypeStruct((t, 2 * CONV_W), F32), jax.ShapeDtypeStruct((t, QKV_COLS), BF16)])


def _attn_keys(i, kvp_ref, kvc_ref):
    kv = jnp.concatenate([kvp_ref[...], kvc_ref[...]], axis=0)
    kp = kv[:, :LANES]
    vp = kv[:, LANES:]
    kr = pltpu.roll(kp.astype(F32), HEAD_DIM, 1).astype(BF16)
    vr = pltpu.roll(vp.astype(F32), HEAD_DIM, 1).astype(BF16)
    qi = lax.broadcasted_iota(jnp.int32, (BLOCK, 2 * BLOCK), 0)
    kj = lax.broadcasted_iota(jnp.int32, (BLOCK, 2 * BLOCK), 1)
    rel = qi + BLOCK - kj
    valid = (rel >= 0) & (rel < BLOCK) & ((kj >= BLOCK) | (i > 0))
    return kp, vp, kr, vr, rel.astype(F32), valid


def _head_scores(q_ref, h, lo, kp, kr, relf, valid):
    p, e, kk = h // 2, h % 2, h // 4
    qp = q_ref[:, p * LANES:(p + 1) * LANES]
    hm = lo if e == 0 else jnp.logical_not(lo)
    qm = jnp.where(hm, qp, jnp.zeros_like(qp))
    s = _dot_nt(qm, kp if kk == e else kr)
    s = s * 0.125 - SLOPES[h] * relf
    return jnp.where(valid, s, NEG_INF), qm, hm, kk == e


def _attn_fwd(qkvn, sinks, riders=()):
    t = qkvn.shape[0]
    nb = t // BLOCK

    def body(sink_ref, q_ref, kvc_ref, kvp_ref, o_ref, lse_ref):
        i = pl.program_id(0)
        kp, vp, kr, vr, relf, valid = _attn_keys(i, kvp_ref, kvc_ref)
        lo = _lo_mask()
        lane = lax.broadcasted_iota(jnp.int32, (BLOCK, LANES), 1)
        lse_t = jnp.zeros((BLOCK, LANES), F32)
        outs = [jnp.zeros((BLOCK, LANES), F32) for _ in range(4)]
        for h in range(N_HEADS):
            s, _, hm, same = _head_scores(q_ref, h, lo, kp, kr, relf, valid)
            sink = sink_ref[h]
            m = jnp.maximum(jnp.max(s, axis=-1, keepdims=True), sink)
            pe = jnp.exp(s - m)
            l = jnp.sum(pe, axis=-1, keepdims=True) + jnp.exp(sink - m)
            lse_t = jnp.where(lane == h, m + jnp.log(l), lse_t)
            o = _dot((pe / l).astype(BF16), vp if same else vr)
            outs[h // 2] = outs[h // 2] + jnp.where(hm, o, 0.0)
        for p in range(4):
            o_ref[:, p * LANES:(p + 1) * LANES] = outs[p].astype(BF16)
        lse_ref[...] = lse_t

    return _call(
        body, (sinks, qkvn, qkvn, qkvn), name="attn_fwd", grid=(nb,), riders=riders,
        in_specs=[pl.BlockSpec(memory_space=pltpu.SMEM),
                  pl.BlockSpec((BLOCK, Q_COLS), lambda i: (i, 0)),
                  pl.BlockSpec((BLOCK, 2 * LANES), lambda i: (i, 2)),
                  pl.BlockSpec((BLOCK, 2 * LANES), lambda i: (jnp.maximum(i - 1, 0), 2))],
        out_specs=[pl.BlockSpec((BLOCK, Q_COLS), lambda i: (i, 0)), pl.BlockSpec((BLOCK, LANES), lambda i: (i, 0))],
        out_shape=[jax.ShapeDtypeStruct((t, Q_COLS), BF16), jax.ShapeDtypeStruct((t, LANES), F32)])


def _glu(pc):
    return pc[:, :CONV_W] * _sig(pc[:, CONV_W:])


def _group_norm_stats(seg, lo):
    mu = _half_mean(seg, lo)
    d = seg - mu
    rstd = lax.rsqrt(_half_mean(d * d, lo) + EPS)
    return d * rstd, rstd


def _shifted_copies(src, dst, tt):
    n = tt + HALO - 8
    for s in range(1, 8):
        dst[s - 1, 0:n, :] = src[s:s + n, :]


def _tap_rows(src, shifted, off, r0, cs):
    q, s = divmod(off, 8)
    if s == 0:
        return src[r0 + off:r0 + off + ROW_CHUNK, cs]
    return shifted[s - 1, r0 + 8 * q:r0 + 8 * q + ROW_CHUNK, cs]


def _shift_scratch(tt):
    return pltpu.VMEM((7, tt + HALO - 8, CONV_W), F32)


def _conv_fwd(pc, cw, cb, ln_g, ln_b, riders=()):
    t = pc.shape[0]
    tt = _token_tile(t)

    def body(cur_ref, prev_ref, w_ref, b_ref, g_ref, bb_ref, y_ref, c_ref, scr, shf):
        i = pl.program_id(0)
        scr[0:HALO, :] = _glu(prev_ref[...]) * (i > 0).astype(F32)
        scr[HALO:HALO + tt, :] = _glu(cur_ref[...])
        _shifted_copies(scr, shf, tt)
        for cbk in range(CONV_W // LANES):
            cs = slice(cbk * LANES, (cbk + 1) * LANES)
            for rb in range(tt // ROW_CHUNK):
                r0 = rb * ROW_CHUNK
                acc = jnp.zeros((ROW_CHUNK, LANES), F32) + b_ref[:, cs]
                for k in range(CONV_K):
                    acc = acc + w_ref[k:k + 1, cs] * _tap_rows(scr, shf, 2 + k, r0, cs)
                y_ref[r0:r0 + ROW_CHUNK, cs] = acc
        lo = _lo_mask()
        for p in range(CONV_W // LANES):
            cs = slice(p * LANES, (p + 1) * LANES)
            yh, _ = _group_norm_stats(y_ref[:, cs], lo)
            z = yh * g_ref[:, cs] + bb_ref[:, cs]
            c_ref[:, cs] = (z * _sig(z)).astype(BF16)

    hb = tt // HALO
    return _call(
        body, (pc, pc, cw, cb, ln_g, ln_b), name="conv_fwd", grid=(t // tt,), riders=riders,
        in_specs=[pl.BlockSpec((tt, 2 * CONV_W), lambda i: (i, 0)),
                  pl.BlockSpec((HALO, 2 * CONV_W), lambda i: (jnp.maximum(i * hb - 1, 0), 0)),
                  _full((CONV_K, CONV_W)), _full((1, CONV_W)), _full((1, CONV_W)), _full((1, CONV_W))],
        out_specs=[pl.BlockSpec((tt, CONV_W), lambda i: (i, 0)), pl.BlockSpec((tt, CONV_W), lambda i: (i, 0))],
        out_shape=[jax.ShapeDtypeStruct((t, CONV_W), F32), jax.ShapeDtypeStruct((t, CONV_W), BF16)],
        scratch=[pltpu.VMEM((tt + HALO, CONV_W), F32), _shift_scratch(tt)])


def _out_proj(x, attn, c, w_out, b_out, g_ffn, riders=()):
    t = x.shape[0]
    tt = _token_tile(t)

    def body(x_ref, a_ref, c_ref, w_ref, b_ref, g_ref, x1_ref, h2_ref):
        x1 = x_ref[...] + _dot(a_ref[...], w_ref[0:Q_COLS, :]) + _dot(c_ref[...], w_ref[Q_COLS:, :]) + b_ref[...]
        x1_ref[...] = x1
        r = lax.rsqrt(jnp.mean(x1 * x1, axis=-1, keepdims=True) + EPS)
        h2_ref[...] = (x1 * r * g_ref[...]).astype(BF16)

    row = lambda w: pl.BlockSpec((tt, w), lambda i: (i, 0))
    return _call(
        body, (x, attn, c, w_out, b_out, g_ffn), name="out_proj", grid=(t // tt,), riders=riders,
        in_specs=[row(D_MODEL), row(Q_COLS), row(CONV_W), _full((D_MODEL, D_MODEL)), _full((1, D_MODEL)), _full((1, D_MODEL))],
        out_specs=[row(D_MODEL), row(D_MODEL)],
        out_shape=[jax.ShapeDtypeStruct((t, D_MODEL), F32), jax.ShapeDtypeStruct((t, D_MODEL), BF16)])


def _ffn_conv(scr, cur, dw_ref, db_ref, tt):
    return (dw_ref[0:1, :] * scr[FHALO - 2:FHALO - 2 + tt, :] + dw_ref[1:2, :] * scr[FHALO - 1:FHALO - 1 + tt, :]
            + dw_ref[2:3, :] * cur + db_ref[...])


def _ffn_up(h2, w_up, dw, db):
    t = h2.shape[0]
    tt = _token_tile(t)
    nj = D_FF // FFN_CB

    def body(hc_ref, hp_ref, wg_ref, wu_ref, dwg_ref, dwu_ref, dbg_ref, dbu_ref, hg_ref, hu_ref, act_ref, sg, su):
        i = pl.program_id(1)
        hc = hc_ref[...]
        hp = hp_ref[...] * (i > 0).astype(BF16)
        ups = []
        for w_ref, dw_ref, db_ref, o_ref, scr in ((wg_ref, dwg_ref, dbg_ref, hg_ref, sg), (wu_ref, dwu_ref, dbu_ref, hu_ref, su)):
            cur = _dot(hc, w_ref[...])
            o_ref[...] = cur
            scr[0:FHALO, :] = _dot(hp, w_ref[...])
            scr[FHALO:FHALO + tt, :] = cur
            ups.append(_ffn_conv(scr, cur, dw_ref, db_ref, tt))
        g, u = ups
        act_ref[...] = (g * _sig(g) * u).astype(BF16)

    fb = tt // FHALO
    colg = lambda r: pl.BlockSpec((r, FFN_CB), lambda j, i: (0, j))
    colu = lambda r: pl.BlockSpec((r, FFN_CB), lambda j, i: (0, j + nj))
    tile = pl.BlockSpec((tt, FFN_CB), lambda j, i: (i, j))
    return pl.pallas_call(
        body, name="ffn_up", grid=(nj, t // tt),
        in_specs=[pl.BlockSpec((tt, D_MODEL), lambda j, i: (i, 0)),
                  pl.BlockSpec((FHALO, D_MODEL), lambda j, i: (jnp.maximum(i * fb - 1, 0), 0)),
                  colg(D_MODEL), colu(D_MODEL), colg(3), colu(3), colg(1), colu(1)],
        out_specs=[tile, tile, tile],
        out_shape=[jax.ShapeDtypeStruct((t, D_FF), F32), jax.ShapeDtypeStruct((t, D_FF), F32),
                   jax.ShapeDtypeStruct((t, D_FF), BF16)],
        scratch_shapes=[pltpu.VMEM((tt + FHALO, FFN_CB), F32), pltpu.VMEM((tt + FHALO, FFN_CB), F32)],
        compiler_params=_params(("parallel", "parallel")),
    )(h2, h2, w_up, w_up, dw, dw, db, db)


def _ffn_down(act, w_down, x1, target):
    t = act.shape[0]
    tt = _token_tile(t)

    def body(a_ref, w_ref, x1_ref, t_ref, dy_ref, loss_ref):
        err = x1_ref[...] + _dot(a_ref[...], w_ref[...]) - t_ref[...]
        dy_ref[...] = err * (1.0 / D_MODEL)

        @pl.when(pl.program_id(0) == 0)
        def _():
            loss_ref[...] = jnp.zeros_like(loss_ref)

        loss_ref[...] += jnp.sum(err * err, axis=0, keepdims=True)

    row = lambda w: pl.BlockSpec((tt, w), lambda i: (i, 0))
    return pl.pallas_call(
        body, name="ffn_down", grid=(t // tt,),
        in_specs=[row(D_FF), _full((D_FF, D_MODEL)), row(D_MODEL), row(D_MODEL)],
        out_specs=[row(D_MODEL), _full((1, D_MODEL))],
        out_shape=[jax.ShapeDtypeStruct((t, D_MODEL), F32), jax.ShapeDtypeStruct((1, D_MODEL), F32)],
        compiler_params=_params(("arbitrary",)),
    )(act, w_down, x1, target)


def _ffn_bwd_act(dy, w_down, hg, hu, dw, db):
    t = dy.shape[0]
    tt = _token_tile(t)
    nj = D_FF // FFN_CB

    def body(dy_ref, wd_ref, gc_ref, gp_ref, uc_ref, up_ref, dwg_ref, dwu_ref, dbg_ref, dbu_ref,
             dg_ref, du_ref, gbg_ref, gbu_ref, sg, su):
        i = pl.program_id(1)
        first = (i > 0).astype(F32)
        d_act = _dot_nt(dy_ref[...].astype(BF16), wd_ref[...])
        ups = []
        for c_ref, p_ref, dw_ref, db_ref, scr in ((gc_ref, gp_ref, dwg_ref, dbg_ref, sg), (uc_ref, up_ref, dwu_ref, dbu_ref, su)):
            scr[0:FHALO, :] = p_ref[...] * first
            scr[FHALO:FHALO + tt, :] = c_ref[...]
            ups.append(_ffn_conv(scr, c_ref[...], dw_ref, db_ref, tt))
        g, u = ups
        s = _sig(g)
        d_u = d_act * (g * s)
        d_g = d_act * u * (s * (1.0 + g * (1.0 - s)))

        @pl.when(i == 0)
        def _():
            for r in (gbg_ref, gbu_ref):
                r[...] = jnp.zeros_like(r)

        for d, o_ref, gb_ref in ((d_g, dg_ref, gbg_ref), (d_u, du_ref, gbu_ref)):
            o_ref[...] = d.astype(BF16)
            gb_ref[...] += jnp.sum(d, axis=0, keepdims=True)

    fb = tt // FHALO
    tile = pl.BlockSpec((tt, FFN_CB), lambda j, i: (i, j))
    prev = pl.BlockSpec((FHALO, FFN_CB), lambda j, i: (jnp.maximum(i * fb - 1, 0), j))
    colg = lambda r: pl.BlockSpec((r, FFN_CB), lambda j, i: (0, j))
    colu = lambda r: pl.BlockSpec((r, FFN_CB), lambda j, i: (0, j + nj))
    acc = lambda r: pl.BlockSpec((r, FFN_CB), lambda j, i: (0, j))
    return pl.pallas_call(
        body, name="ffn_bwd_act", grid=(nj, t // tt),
        in_specs=[pl.BlockSpec((tt, D_MODEL), lambda j, i: (i, 0)), pl.BlockSpec((FFN_CB, D_MODEL), lambda j, i: (j, 0)),
                  tile, prev, tile, prev, colg(3), colu(3), colg(1), colu(1)],
        out_specs=[tile, tile, acc(1), acc(1)],
        out_shape=[jax.ShapeDtypeStruct((t, D_FF), BF16), jax.ShapeDtypeStruct((t, D_FF), BF16),
                   jax.ShapeDtypeStruct((1, D_FF), F32), jax.ShapeDtypeStruct((1, D_FF), F32)],
        scratch_shapes=[pltpu.VMEM((tt + FHALO, FFN_CB), F32), pltpu.VMEM((tt + FHALO, FFN_CB), F32)],
        compiler_params=_params(("parallel", "arbitrary")),
    )(dy, w_down, hg, hg, hu, hu, dw, dw, db, db)


def _ffn_bwd_conv(dg, du, hg, hu, dw):
    t = dg.shape[0]
    tt = _token_tile(t)
    nj = D_FF // FFN_CB
    ni = t // tt

    def body(gc_ref, gn_ref, uc_ref, un_ref, hg_ref, hu_ref, dwg_ref, dwu_ref, og_ref, ou_ref, gwg_ref, gwu_ref, scr):
        i = pl.program_id(1)
        last = (i < ni - 1).astype(F32)

        @pl.when(i == 0)
        def _():
            gwg_ref[...] = jnp.zeros_like(gwg_ref)
            gwu_ref[...] = jnp.zeros_like(gwu_ref)

        for c_ref, n_ref, h_ref, dw_ref, o_ref, gw_ref in ((gc_ref, gn_ref, hg_ref, dwg_ref, og_ref, gwg_ref),
                                                           (uc_ref, un_ref, hu_ref, dwu_ref, ou_ref, gwu_ref)):
            cur = c_ref[...].astype(F32)
            scr[0:tt, :] = cur
            scr[tt:tt + FHALO, :] = n_ref[...].astype(F32) * last
            nx1 = scr[1:1 + tt, :]
            nx2 = scr[2:2 + tt, :]
            o_ref[...] = (dw_ref[2:3, :] * cur + dw_ref[1:2, :] * nx1 + dw_ref[0:1, :] * nx2).astype(BF16)
            hw = h_ref[...]
            rows = [jnp.sum(hw * d, axis=0, keepdims=True) for d in (nx2, nx1, cur)]
            gw_ref[...] += _rows_to_tile(rows, 8)

    fb = tt // FHALO
    tile = pl.BlockSpec((tt, FFN_CB), lambda j, i: (i, j))
    nxt = pl.BlockSpec((FHALO, FFN_CB), lambda j, i: (jnp.minimum((i + 1) * fb, t // FHALO - 1), j))
    acc = pl.BlockSpec((8, FFN_CB), lambda j, i: (0, j))
    return pl.pallas_call(
        body, name="ffn_bwd_conv", grid=(nj, ni),
        in_specs=[tile, nxt, tile, nxt, tile, tile, pl.BlockSpec((3, FFN_CB), lambda j, i: (0, j)),
                  pl.BlockSpec((3, FFN_CB), lambda j, i: (0, j + nj))],
        out_specs=[tile, tile, acc, acc],
        out_shape=[jax.ShapeDtypeStruct((t, D_FF), BF16), jax.ShapeDtypeStruct((t, D_FF), BF16),
                   jax.ShapeDtypeStruct((8, D_FF), F32), jax.ShapeDtypeStruct((8, D_FF), F32)],
        scratch_shapes=[pltpu.VMEM((tt + FHALO, FFN_CB), F32)],
        compiler_params=_params(("parallel", "arbitrary")),
    )(dg, dg, du, du, hg, hu, dw, dw)


def _ffn_bwd_in(dhg, dhu, w_up, x1, dy, g_ffn, riders=()):
    t = x1.shape[0]
    tt = _token_tile(t)

    def body(dg_ref, du_ref, w_ref, x1_ref, dy_ref, g_ref, dx_ref, gg_ref):
        d_h2 = _dot_nt(dg_ref[...], w_ref[:, 0:D_FF]) + _dot_nt(du_ref[...], w_ref[:, D_FF:])
        x1 = x1_ref[...]
        r = lax.rsqrt(jnp.mean(x1 * x1, axis=-1, keepdims=True) + EPS)
        xh = x1 * r
        gd = d_h2 * g_ref[...]
        dx_ref[...] = dy_ref[...] + r * (gd - xh * jnp.mean(gd * xh, axis=-1, keepdims=True))

        @pl.when(pl.program_id(0) == 0)
        def _():
            gg_ref[...] = jnp.zeros_like(gg_ref)

        gg_ref[...] += jnp.sum(d_h2 * xh, axis=0, keepdims=True)

    row = lambda w: pl.BlockSpec((tt, w), lambda i: (i, 0))
    return _call(
        body, (dhg, dhu, w_up, x1, dy, g_ffn), name="ffn_bwd_in", grid=(t // tt,), riders=riders,
        in_specs=[row(D_FF), row(D_FF), _full((D_MODEL, 2 * D_FF)), row(D_MODEL), row(D_MODEL), _full((1, D_MODEL))],
        out_specs=[row(D_MODEL), _full((1, D_MODEL))],
        out_shape=[jax.ShapeDtypeStruct((t, D_MODEL), F32), jax.ShapeDtypeStruct((1, D_MODEL), F32)])


def _grad_weight(a, b, nj, mb, name):
    t, m = a.shape
    n = b.shape[1]
    nb_ = n // nj
    tt = GRAD_TILE if t % GRAD_TILE == 0 else _token_tile(t)

    def body(a_ref, b_ref, o_ref):
        @pl.when(pl.program_id(2) == 0)
        def _():
            o_ref[...] = jnp.zeros_like(o_ref)

        o_ref[0] += _dot_tn(a_ref[...].astype(BF16), b_ref[...].astype(BF16))

    return pl.pallas_call(
        body, name=name, grid=(nj, m // mb, t // tt),
        in_specs=[pl.BlockSpec((tt, mb), lambda j, mi, i: (i, mi)), pl.BlockSpec((tt, nb_), lambda j, mi, i: (i, j))],
        out_specs=pl.BlockSpec((1, mb, nb_), lambda j, mi, i: (j, mi, 0)),
        out_shape=jax.ShapeDtypeStruct((nj, m, nb_), F32),
        compiler_params=_params(("parallel", "parallel", "arbitrary")),
    )(a, b)


def _out_proj_bwd(dx1, w_out):
    t = dx1.shape[0]
    tt = _token_tile(t)

    def body(d_ref, w_ref, dm_ref, gb_ref):
        d = d_ref[...]
        dm_ref[...] = _dot_nt(d.astype(BF16), w_ref[...])

        @pl.when(pl.program_id(0) == 0)
        def _():
            gb_ref[...] = jnp.zeros_like(gb_ref)

        gb_ref[...] += jnp.sum(d, axis=0, keepdims=True)

    row = pl.BlockSpec((tt, D_MODEL), lambda i: (i, 0))
    return pl.pallas_call(
        body, name="out_proj_bwd", grid=(t // tt,),
        in_specs=[row, _full((D_MODEL, D_MODEL))],
        out_specs=[row, _full((1, D_MODEL))],
        out_shape=[jax.ShapeDtypeStruct((t, D_MODEL), F32), jax.ShapeDtypeStruct((1, D_MODEL), F32)],
        compiler_params=_params(("arbitrary",)),
    )(dx1, w_out)


def _conv_bwd(dmix, y, pc, cw, ln_g, ln_b, riders=()):
    t = y.shape[0]
    tt = _token_tile(t)
    ni = t // tt
    ncb = CONV_W // LANES

    def body(dc_ref, dcn_ref, y_ref, yn_ref, pc_ref, pcp_ref, w_ref, g_ref, bb_ref,
             dp_ref, gw_ref, gb_ref, gg_ref, gbb_ref, gbin_ref, scr_d, scr_c, scr_o, shf_d, shf_c):
        i = pl.program_id(0)
        lo = _lo_mask()

        @pl.when(i == 0)
        def _():
            for r in (gw_ref, gb_ref, gg_ref, gbb_ref, gbin_ref):
                r[...] = jnp.zeros_like(r)

        def norm_bwd(dc, yv, cs):
            yh, rstd = _group_norm_stats(yv, lo)
            z = yh * g_ref[:, cs] + bb_ref[:, cs]
            s = _sig(z)
            dz = dc * (s * (1.0 + z * (1.0 - s)))
            dyh = dz * g_ref[:, cs]
            d_y = rstd * (dyh - _half_mean(dyh, lo) - yh * _half_mean(dyh * yh, lo))
            return d_y, dz, yh

        for p in range(ncb):
            cs = slice(p * LANES, (p + 1) * LANES)
            d_y, dz, yh = norm_bwd(dc_ref[:, cs], y_ref[:, cs], cs)
            scr_d[0:tt, cs] = d_y
            gg_ref[:, cs] += jnp.sum(dz * yh, axis=0, keepdims=True)
            gbb_ref[:, cs] += jnp.sum(dz, axis=0, keepdims=True)
            gb_ref[:, cs] += jnp.sum(d_y, axis=0, keepdims=True)
            d_yn, _, _ = norm_bwd(dcn_ref[:, cs], yn_ref[:, cs], cs)
            scr_d[tt:tt + HALO, cs] = d_yn * (i < ni - 1).astype(F32)
        scr_c[0:HALO, :] = _glu(pcp_ref[...]) * (i > 0).astype(F32)
        scr_c[HALO:HALO + tt, :] = _glu(pc_ref[...])
        _shifted_copies(scr_d, shf_d, tt)
        _shifted_copies(scr_c, shf_c, tt)

        rid = lax.broadcasted_iota(jnp.int32, (HALO, LANES), 0)
        for cbk in range(ncb):
            cs = slice(cbk * LANES, (cbk + 1) * LANES)
            for rb in range(tt // ROW_CHUNK):
                r0 = rb * ROW_CHUNK
                acc = jnp.zeros((ROW_CHUNK, LANES), F32)
                for k in range(CONV_K):
                    acc = acc + w_ref[k:k + 1, cs] * _tap_rows(scr_d, shf_d, 30 - k, r0, cs)
                scr_o[r0:r0 + ROW_CHUNK, cs] = acc
            gwt = jnp.zeros((HALO, LANES), F32)
            for k in range(CONV_K):
                acc = jnp.zeros((ROW_CHUNK, LANES), F32)
                for rb in range(tt // ROW_CHUNK):
                    r0 = rb * ROW_CHUNK
                    acc = acc + scr_d[r0:r0 + ROW_CHUNK, cs] * _tap_rows(scr_c, shf_c, 2 + k, r0, cs)
                gwt = jnp.where(rid == k, jnp.sum(acc, axis=0, keepdims=True), gwt)
            gw_ref[:, cs] += gwt
        d_c0 = scr_o[...]
        a = pc_ref[:, 0:CONV_W]
        s = _sig(pc_ref[:, CONV_W:])
        d_a = d_c0 * s
        d_gate = d_c0 * a * s * (1.0 - s)
        dp_ref[:, 0:CONV_W] = d_a.astype(BF16)
        dp_ref[:, CONV_W:] = d_gate.astype(BF16)
        gbin_ref[:, 0:CONV_W] += jnp.sum(d_a, axis=0, keepdims=True)
        gbin_ref[:, CONV_W:] += jnp.sum(d_gate, axis=0, keepdims=True)

    hb = tt // HALO
    nxt = lambda col: pl.BlockSpec((HALO, CONV_W), lambda i: (jnp.minimum((i + 1) * hb, t // HALO - 1), col))
    return _call(
        body, (dmix, dmix, y, y, pc, pc, cw, ln_g, ln_b), name="conv_bwd", grid=(ni,), riders=riders,
        in_specs=[pl.BlockSpec((tt, CONV_W), lambda i: (i, 1)), nxt(1),
                  pl.BlockSpec((tt, CONV_W), lambda i: (i, 0)), nxt(0),
                  pl.BlockSpec((tt, 2 * CONV_W), lambda i: (i, 0)),
                  pl.BlockSpec((HALO, 2 * CONV_W), lambda i: (jnp.maximum(i * hb - 1, 0), 0)),
                  _full((CONV_K, CONV_W)), _full((1, CONV_W)), _full((1, CONV_W))],
        out_specs=[pl.BlockSpec((tt, 2 * CONV_W), lambda i: (i, 0)), _full((HALO, CONV_W)), _full((1, CONV_W)),
                   _full((1, CONV_W)), _full((1, CONV_W)), _full((1, 2 * CONV_W))],
        out_shape=[jax.ShapeDtypeStruct((t, 2 * CONV_W), BF16), jax.ShapeDtypeStruct((HALO, CONV_W), F32),
                   jax.ShapeDtypeStruct((1, CONV_W), F32), jax.ShapeDtypeStruct((1, CONV_W), F32),
                   jax.ShapeDtypeStruct((1, CONV_W), F32), jax.ShapeDtypeStruct((1, 2 * CONV_W), F32)],
        scratch=[pltpu.VMEM((tt + HALO, CONV_W), F32), pltpu.VMEM((tt + HALO, CONV_W), F32),
                 pltpu.VMEM((tt, CONV_W), F32), _shift_scratch(tt), _shift_scratch(tt)])


def _attn_bwd(qkvn, dmix, lse, sinks, riders=()):
    t = qkvn.shape[0]
    nb = t // BLOCK

    def body(sink_ref, q_ref, kvc_ref, kvp_ref, do_ref, lse_ref, dq_ref, dcur_ref, dprev_ref, ds_ref):
        i = pl.program_id(0)
        kp, vp, kr, vr, relf, valid = _attn_keys(i, kvp_ref, kvc_ref)
        lo = _lo_mask()
        lane1 = lax.broadcasted_iota(jnp.int32, (1, LANES), 1)
        lane = lax.broadcasted_iota(jnp.int32, (BLOCK, LANES), 1)
        lse_t = lse_ref[...]
        zkv = lambda: jnp.zeros((2 * BLOCK, LANES), F32)
        dk_in, dk_roll, dv_in, dv_roll = zkv(), zkv(), zkv(), zkv()
        dqs = [jnp.zeros((BLOCK, LANES), F32) for _ in range(4)]
        dsink = jnp.zeros((1, LANES), F32)
        for h in range(N_HEADS):
            s, qm, hm, same = _head_scores(q_ref, h, lo, kp, kr, relf, valid)
            lse_h = jnp.sum(jnp.where(lane == h, lse_t, 0.0), axis=-1, keepdims=True)
            prob = jnp.exp(s - lse_h)
            do_m = jnp.where(hm, do_ref[:, (h // 2) * LANES:(h // 2 + 1) * LANES], 0.0).astype(BF16)
            dp = _dot_nt(do_m, vp if same else vr)
            dsum = jnp.sum(prob * dp, axis=-1, keepdims=True)
            dsc = prob * (dp - dsum)
            psink = jnp.exp(sink_ref[h] - lse_h)
            dsink = dsink + jnp.where(lane1 == h, jnp.sum(-psink * dsum, axis=0, keepdims=True), 0.0)
            dsb = (dsc * 0.125).astype(BF16)
            dqs[h // 2] = dqs[h // 2] + jnp.where(hm, _dot(dsb, kp if same else kr), 0.0)
            dk_c = _dot_tn(dsb, qm)
            dv_c = _dot_tn(prob.astype(BF16), do_m)
            if same:
                dk_in, dv_in = dk_in + dk_c, dv_in + dv_c
            else:
                dk_roll, dv_roll = dk_roll + dk_c, dv_roll + dv_c
        for p in range(4):
            dq_ref[:, p * LANES:(p + 1) * LANES] = dqs[p]
        dk = dk_in + pltpu.roll(dk_roll, HEAD_DIM, 1)
        dv = dv_in + pltpu.roll(dv_roll, HEAD_DIM, 1)
        dprev_ref[:, 0:LANES] = dk[0:BLOCK]
        dprev_ref[:, LANES:] = dv[0:BLOCK]
        dcur_ref[:, 0:LANES] = dk[BLOCK:]
        dcur_ref[:, LANES:] = dv[BLOCK:]

        @pl.when(i == 0)
        def _():
            ds_ref[...] = jnp.zeros_like(ds_ref)

        ds_ref[...] += dsink

    blk = lambda w: pl.BlockSpec((BLOCK, w), lambda i: (i, 0))
    return _call(
        body, (sinks, qkvn, qkvn, qkvn, dmix, lse), name="attn_bwd", grid=(nb,), riders=riders,
        in_specs=[pl.BlockSpec(memory_space=pltpu.SMEM), blk(Q_COLS),
                  pl.BlockSpec((BLOCK, 2 * LANES), lambda i: (i, 2)),
                  pl.BlockSpec((BLOCK, 2 * LANES), lambda i: (jnp.maximum(i - 1, 0), 2)),
                  blk(Q_COLS), blk(LANES)],
        out_specs=[blk(Q_COLS), blk(2 * LANES), blk(2 * LANES), _full((1, LANES))],
        out_shape=[jax.ShapeDtypeStruct((t, Q_COLS), F32), jax.ShapeDtypeStruct((t, 2 * LANES), F32),
                   jax.ShapeDtypeStruct((t, 2 * LANES), F32), jax.ShapeDtypeStruct((1, LANES), F32)])


def _qk_norm_bwd(dqn, dcur, dprev, pq, gq2, gk2):
    t = dqn.shape[0]
    nb = t // BLOCK

    def body(dq_ref, dc_ref, dn_ref, pq_ref, gq_ref, gk_ref, dp_ref, gbin_ref, gg_ref):
        i = pl.program_id(0)
        lo = _lo_mask()
        dkv = dc_ref[...] + dn_ref[...] * (i < nb - 1).astype(F32)

        @pl.when(i == 0)
        def _():
            gbin_ref[...] = jnp.zeros_like(gbin_ref)
            gg_ref[...] = jnp.zeros_like(gg_ref)

        for p in range(5):
            cs = slice(p * LANES, (p + 1) * LANES)
            seg = pq_ref[:, cs]
            dn = dq_ref[:, cs] if p < 4 else dkv[:, 0:LANES]
            gain = gq_ref[...] if p < 4 else gk_ref[...]
            rr = lax.rsqrt(_half_mean(seg * seg, lo) + EPS)
            xh = seg * rr
            gd = dn * gain
            d = rr * (gd - xh * _half_mean(gd * xh, lo))
            dp_ref[:, cs] = d.astype(BF16)
            gbin_ref[:, cs] += jnp.sum(d, axis=0, keepdims=True)
            gg_ref[:, cs] += jnp.sum(dn * xh, axis=0, keepdims=True)
        dv = dkv[:, LANES:]
        dp_ref[:, 640:768] = dv.astype(BF16)
        gbin_ref[:, 640:768] += jnp.sum(dv, axis=0, keepdims=True)

    blk = lambda w: pl.BlockSpec((BLOCK, w), lambda i: (i, 0))
    return pl.pallas_call(
        body, name="qk_norm_bwd", grid=(nb,),
        in_specs=[blk(Q_COLS), blk(2 * LANES), pl.BlockSpec((BLOCK, 2 * LANES), lambda i: (jnp.minimum(i + 1, nb - 1), 0)),
                  blk(QKV_COLS), _full((1, LANES)), _full((1, LANES))],
        out_specs=[blk(QKV_COLS), _full((1, QKV_COLS)), _full((1, 5 * LANES))],
        out_shape=[jax.ShapeDtypeStruct((t, QKV_COLS), BF16), jax.ShapeDtypeStruct((1, QKV_COLS), F32),
                   jax.ShapeDtypeStruct((1, 5 * LANES), F32)],
        compiler_params=_params(("arbitrary",)),
    )(dqn, dcur, dprev, pq, gq2, gk2)


def _in_proj_bwd(dpq, dpc, w_in, x, dx1, g_mix):
    t = x.shape[0]
    tt = _token_tile(t)

    def body(dq_ref, dc_ref, w_ref, x_ref, d1_ref, g_ref, gx_ref, gg_ref):
        d_h = _dot_nt(dq_ref[...], w_ref[:, 0:QKV_COLS]) + _dot_nt(dc_ref[...], w_ref[:, QKV_COLS:])
        xv = x_ref[...]
        r = lax.rsqrt(jnp.mean(xv * xv, axis=-1, keepdims=True) + EPS)
        xh = xv * r
        gd = d_h * g_ref[...]
        gx_ref[...] = d1_ref[...] + r * (gd - xh * jnp.mean(gd * xh, axis=-1, keepdims=True))

        @pl.when(pl.program_id(0) == 0)
        def _():
            gg_ref[...] = jnp.zeros_like(gg_ref)

        gg_ref[...] += jnp.sum(d_h * xh, axis=0, keepdims=True)

    row = lambda w: pl.BlockSpec((tt, w), lambda i: (i, 0))
    return pl.pallas_call(
        body, name="in_proj_bwd", grid=(t // tt,),
        in_specs=[row(QKV_COLS), row(2 * CONV_W), _full((D_MODEL, IN_COLS)), row(D_MODEL), row(D_MODEL), _full((1, D_MODEL))],
        out_specs=[row(D_MODEL), _full((1, D_MODEL))],
        out_shape=[jax.ShapeDtypeStruct((t, D_MODEL), F32), jax.ShapeDtypeStruct((1, D_MODEL), F32)],
        compiler_params=_params(("arbitrary",)),
    )(dpq, dpc, w_in, x, dx1, g_mix)


def _row_block(r):
    if r <= 256:
        return r
    return max(b for b in range(8, 257, 8) if r % b == 0)


def _adamw(w, g, m, v, name):
    r, c = w.shape
    rb = _row_block(r)

    def body(w_ref, g_ref, m_ref, v_ref, d_ref, nm_ref, nv_ref):
        gv = g_ref[...]
        nm = ADAM_B1 * m_ref[...] + (1.0 - ADAM_B1) * gv
        nv = ADAM_B2 * v_ref[...] + (1.0 - ADAM_B2) * (gv * gv)
        m_hat = nm / (1.0 - ADAM_B1 ** ADAM_STEP)
        v_hat = nv / (1.0 - ADAM_B2 ** ADAM_STEP)
        d_ref[...] = -ADAM_LR * (m_hat / (jnp.sqrt(v_hat) + ADAM_EPS) + ADAM_WD * w_ref[...])
        nm_ref[...] = nm
        nv_ref[...] = nv

    blk = pl.BlockSpec((rb, c), lambda i: (i, 0))
    shp = jax.ShapeDtypeStruct((r, c), F32)
    return pl.pallas_call(
        body, name=name, grid=(r // rb,), in_specs=[blk] * 4, out_specs=[blk] * 3, out_shape=[shp] * 3,
        compiler_params=_params(("parallel",)),
    )(w, g, m, v)


def _place():
    x, y, c = lax.axis_index("x"), lax.axis_index("y"), lax.axis_index("c")
    chips = [(1 - x, y), (x, 1 - y), (1 - x, 1 - y)]
    return x, y, c, chips


def _gather_all(v):
    r = v.shape[0]

    def body(v_ref, all_ref, sum_ref, send_sems, recv_sems):
        x, y, c, _ = _place()
        me = 4 * x + 2 * y + c
        all_ref[me] = v_ref[...]
        copies = []
        for k in range(1, 8):
            kx, ky, kc = (k >> 2) & 1, (k >> 1) & 1, k & 1
            peer = (x ^ kx, y ^ ky, c ^ kc)
            cp = pltpu.make_async_remote_copy(src_ref=v_ref, dst_ref=all_ref.at[me], send_sem=send_sems.at[k - 1],
                                              recv_sem=recv_sems.at[k - 1], device_id=peer, device_id_type=MESH)
            cp.start()
            copies.append((cp, 4 * peer[0] + 2 * peer[1] + peer[2]))
        for k, (cp, src_idx) in enumerate(copies):
            pltpu.make_async_remote_copy(src_ref=v_ref, dst_ref=all_ref.at[src_idx], send_sem=send_sems.at[k],
                                         recv_sem=recv_sems.at[k], device_id=(x, y, c), device_id_type=MESH).wait_recv()
        for cp, _ in copies:
            cp.wait_send()
        tot = all_ref[0]
        for d in range(1, 8):
            tot = tot + all_ref[d]
        sum_ref[...] = tot

    vm = pl.BlockSpec(memory_space=pltpu.VMEM)
    return pl.pallas_call(
        body, name="gather_all", in_specs=[vm], out_specs=[vm, vm],
        out_shape=[jax.ShapeDtypeStruct((8, r, LANES), v.dtype), jax.ShapeDtypeStruct((r, LANES), v.dtype)],
        scratch_shapes=[pltpu.SemaphoreType.DMA((7,)), pltpu.SemaphoreType.DMA((7,))],
        compiler_params=pltpu.CompilerParams(vmem_limit_bytes=VMEM_LIMIT),
    )(v)


def _remote(src, dst, send_sem, recv_sem, to):
    return pltpu.make_async_remote_copy(src_ref=src, dst_ref=dst, send_sem=send_sem, recv_sem=recv_sem,
                                        device_id=to, device_id_type=MESH)


def _dma_sems(*shape):
    return pltpu.SemaphoreType.DMA(shape)


def _gather_first(shards):
    n = len(shards)

    def copies(ins, outs, sems):
        x, y, c, chips = _place()
        me = 2 * x + y
        local = [pltpu.make_async_copy(ins[a], outs[a].at[me], sems[2].at[a]) for a in range(n)]
        sends = [_remote(ins[a].at[c], outs[a].at[me, c], sems[0].at[a, j], sems[1].at[a, j], (*chip, c))
                 for a in range(n) for j, chip in enumerate(chips)]
        lands = [_remote(ins[a].at[c], outs[a].at[2 * chip[0] + chip[1], c], sems[0].at[a, j], sems[1].at[a, j], (x, y, c))
                 for a in range(n) for j, chip in enumerate(chips)]
        return local, sends, lands

    def start(ins, outs, sems):
        local, sends, _ = copies(ins, outs, sems)
        for cp in local + sends:
            cp.start()

    def finish(ins, outs, sems):
        local, sends, lands = copies(ins, outs, sems)
        for cp in lands:
            cp.wait_recv()
        for cp in sends:
            cp.wait_send()
        for cp in local:
            cp.wait()

    return _Rider(shards, [jax.ShapeDtypeStruct((4,) + s.shape, s.dtype) for s in shards],
                  [_dma_sems(n, 3), _dma_sems(n, 3), _dma_sems(n)], start, finish)


def _gather_second(partials):
    n = len(partials)

    def copies(outs, sems):
        x, y, c, chips = _place()
        sends, lands = [], []
        for a in range(n):
            for j, chip in enumerate(chips):
                mine = outs[a].at[2 * chip[0] + chip[1], c]
                theirs = outs[a].at[2 * chip[0] + chip[1], 1 - c]
                sends.append(_remote(mine, mine, sems[0].at[a, j], sems[1].at[a, j], (x, y, 1 - c)))
                lands.append(_remote(theirs, theirs, sems[0].at[a, j], sems[1].at[a, j], (x, y, c)))
        return sends, lands

    def start(ins, outs, sems):
        for cp in copies(outs, sems)[0]:
            cp.start()

    def finish(ins, outs, sems):
        sends, lands = copies(outs, sems)
        for cp in lands:
            cp.wait_recv()
        for cp in sends:
            cp.wait_send()

    return _Rider(partials, [jax.ShapeDtypeStruct(p.shape, p.dtype) for p in partials],
                  [_dma_sems(n, 3), _dma_sems(n, 3)], start, finish, aliases={a: a for a in range(n)})


def _swap_halves(grads):
    n = len(grads)

    def copies(ins, outs, sems):
        x, y, c, _ = _place()
        return [_remote(ins[a].at[j, 1 - c], outs[a].at[j], sems[0].at[a, j], sems[1].at[a, j], (x, y, 1 - c))
                for a in range(n) for j in range(4)]

    def start(ins, outs, sems):
        for cp in copies(ins, outs, sems):
            cp.start()

    def finish(ins, outs, sems):
        for cp in copies(ins, outs, sems):
            cp.wait()

    return _Rider(grads, [jax.ShapeDtypeStruct((4,) + g.shape[2:], g.dtype) for g in grads],
                  [_dma_sems(n, 4), _dma_sems(n, 4)], start, finish)


def _add_sibling(g, got, c_idx, name):
    _, _, h, c = g.shape

    def body(s_ref, a_ref, b_ref, o_ref):
        o_ref[...] = (a_ref[...] + b_ref[...]).astype(BF16)

    return pl.pallas_call(
        body, name=name,
        grid_spec=pltpu.PrefetchScalarGridSpec(
            num_scalar_prefetch=1, grid=(4,),
            in_specs=[pl.BlockSpec((None, None, h, c), lambda j, s: (j, s[0], 0, 0)),
                      pl.BlockSpec((None, h, c), lambda j, s: (j, 0, 0))],
            out_specs=pl.BlockSpec((None, h, c), lambda j, s: (j, 0, 0))),
        out_shape=jax.ShapeDtypeStruct((4, h, c), BF16),
        compiler_params=_params(("parallel",)),
    )(c_idx, g, got)


def _exchange_chips(parts):
    n = len(parts)

    def copies(ins, outs, sems):
        x, y, c, chips = _place()
        return [_remote(ins[a].at[2 * chip[0] + chip[1]], outs[a].at[j], sems[0].at[a, j], sems[1].at[a, j], (*chip, c))
                for a in range(n) for j, chip in enumerate(chips)]

    def start(ins, outs, sems):
        for cp in copies(ins, outs, sems):
            cp.start()

    def finish(ins, outs, sems):
        for cp in copies(ins, outs, sems):
            cp.wait()

    return _Rider(parts, [jax.ShapeDtypeStruct((3,) + p.shape[1:], p.dtype) for p in parts],
                  [_dma_sems(n, 3), _dma_sems(n, 3)], start, finish)


def _add_chips(part, got, chip_idx, name):
    _, h, c = part.shape

    def body(s_ref, a_ref, b_ref, o_ref):
        o_ref[...] = ((a_ref[...].astype(F32) + b_ref[0].astype(F32)) + b_ref[1].astype(F32)) + b_ref[2].astype(F32)

    return pl.pallas_call(
        body, name=name,
        grid_spec=pltpu.PrefetchScalarGridSpec(
            num_scalar_prefetch=1, grid=(1,),
            in_specs=[pl.BlockSpec((None, h, c), lambda i, s: (s[0], 0, 0)),
                      pl.BlockSpec((3, h, c), lambda i, s: (0, 0, 0))],
            out_specs=pl.BlockSpec((h, c), lambda i, s: (0, 0))),
        out_shape=jax.ShapeDtypeStruct((h, c), F32),
        compiler_params=_params(("arbitrary",)),
    )(chip_idx, part, got)


def _join_halves(halves):
    n = len(halves)

    def copies(ins, outs, sems):
        x, y, c, _ = _place()
        local = [pltpu.make_async_copy(ins[a], outs[a].at[c], sems[2].at[a]) for a in range(n)]
        sends = [_remote(ins[a], outs[a].at[c], sems[0].at[a], sems[1].at[a], (x, y, 1 - c)) for a in range(n)]
        lands = [_remote(ins[a], outs[a].at[1 - c], sems[0].at[a], sems[1].at[a], (x, y, c)) for a in range(n)]
        return local, sends, lands

    def start(ins, outs, sems):
        local, sends, _ = copies(ins, outs, sems)
        for cp in local + sends:
            cp.start()

    def finish(ins, outs, sems):
        local, sends, lands = copies(ins, outs, sems)
        for cp in lands:
            cp.wait_recv()
        for cp in sends:
            cp.wait_send()
        for cp in local:
            cp.wait()

    return _Rider(halves, [jax.ShapeDtypeStruct((2,) + h.shape, h.dtype) for h in halves],
                  [_dma_sems(n), _dma_sems(n), _dma_sems(n)], start, finish)


def _pack(parts):
    flat = []
    for p in parts:
        p = p.reshape(-1).astype(F32)
        flat.append(jnp.pad(p, (0, (-p.shape[0]) % LANES)))
    v = jnp.concatenate(flat)
    v = jnp.pad(v, (0, (-v.shape[0]) % (8 * LANES)))
    return v.reshape(-1, LANES)


def _unpack(v, shapes):
    flat = v.reshape(-1)
    out, off = [], 0
    for s in shapes:
        n = 1
        for d in s:
            n *= d
        out.append(flat[off:off + n].reshape(s))
        off += n + (-n) % LANES
    return out


def kernel(x, mix_norm_gain, w_in, b_in, q_norm_gain, k_norm_gain, attn_sinks, conv_dw_w, conv_dw_b, conv_norm_gain, conv_norm_bias, w_out, b_out, ffn_norm_gain, w_up, ffn_dw_w, ffn_dw_b, w_down, loss_target, m_mix_norm_gain, m_w_in, m_b_in, m_q_norm_gain, m_k_norm_gain, m_attn_sinks, m_conv_dw_w, m_conv_dw_b, m_conv_norm_gain, m_conv_norm_bias, m_w_out, m_b_out, m_ffn_norm_gain, m_w_up, m_ffn_dw_w, m_ffn_dw_b, m_w_down, v_mix_norm_gain, v_w_in, v_b_in, v_q_norm_gain, v_k_norm_gain, v_attn_sinks, v_conv_dw_w, v_conv_dw_b, v_conv_norm_gain, v_conv_norm_bias, v_w_out, v_b_out, v_ffn_norm_gain, v_w_up, v_ffn_dw_w, v_ffn_dw_b, v_w_down):
    t = x.shape[1]
    xi, yi, ci = lax.axis_index("x"), lax.axis_index("y"), lax.axis_index("c")
    chip = 2 * xi + yi
    c_idx = jnp.reshape(ci, (1,)).astype(jnp.int32)
    chip_idx = jnp.reshape(chip, (1,)).astype(jnp.int32)
    x2 = x.reshape(t, D_MODEL)
    tgt = loss_target.reshape(t, D_MODEL)

    big = [w_in, w_out, w_up, w_down]
    halves = [w.astype(BF16).reshape(2, w.shape[0] // 2, w.shape[1]) for w in big]
    h_wi, h_wo, h_wu, h_wd = halves
    (p_wi,) = _run_riders([_gather_first([h_wi])], "gather_w_in_first")
    (g_wi,) = _run_riders([_gather_second([p_wi])], "gather_w_in_second")
    wi = jnp.concatenate([g_wi[j].reshape(D_MODEL, IN_COLS // 4) for j in range(4)], axis=1)
    small_w, _ = _gather_all(_pack([conv_dw_w, ffn_dw_w]))
    per_chip = [_unpack(small_w[4 * (j // 2) + 2 * (j % 2)], [conv_dw_w.shape, ffn_dw_w.shape]) for j in range(4)]
    cw = jnp.concatenate([p[0] for p in per_chip], axis=1)
    fw = jnp.concatenate([p[1] for p in per_chip], axis=1)

    row = lambda a: a.reshape(1, -1)
    gq2 = row(jnp.concatenate([q_norm_gain, q_norm_gain]))
    gk2 = row(jnp.concatenate([k_norm_gain, k_norm_gain]))

    (h1, pq, pc, qkvn), (p_wo,) = _fwd_in(x2, row(mix_norm_gain), wi, row(b_in), gq2, gk2, riders=[_gather_first([h_wo])])
    (attn, lse), (g_wo, p_wu) = _attn_fwd(qkvn, attn_sinks, riders=[_gather_second([p_wo]), _gather_first([h_wu])])
    (y_conv, c_act), (g_wu, p_wd) = _conv_fwd(pc, cw, row(conv_dw_b), row(conv_norm_gain), row(conv_norm_bias),
                                              riders=[_gather_second([p_wu]), _gather_first([h_wd])])
    wo = g_wo.reshape(D_MODEL, D_MODEL)
    (x1, h2), (g_wd,) = _out_proj(x2, attn, c_act, wo, row(b_out), row(ffn_norm_gain), riders=[_gather_second([p_wd])])
    wu = jnp.concatenate([g_wu[j].reshape(D_MODEL, 2 * D_FF // 4) for j in range(4)], axis=1)
    wd = g_wd.reshape(D_FF, D_MODEL)
    hg, hu, act = _ffn_up(h2, wu, fw, row(ffn_dw_b))
    dy, loss_cols = _ffn_down(act, wd, x1, tgt)

    split = lambda g: g.reshape(4, 2, g.shape[1] // 2, g.shape[2])
    dg, du, gfb_g, gfb_u = _ffn_bwd_act(dy, wd, hg, hu, fw, row(ffn_dw_b))
    dhg, dhu, gfw_g, gfw_u = _ffn_bwd_conv(dg, du, hg, hu, fw)
    gw_down = _grad_weight(act, dy, 1, FFN_CB, "grad_w_down")
    gw_up = jnp.concatenate([_grad_weight(h2, dhg, 2, D_MODEL, "grad_w_up_gate"),
                             _grad_weight(h2, dhu, 2, D_MODEL, "grad_w_up_lin")], axis=0)
    early = [split(gw_up), split(gw_down.reshape(4, D_FF // 4, D_MODEL))]
    early_names = ["w_up", "w_down"]
    (dx1, g_ffn_gain), got = _ffn_bwd_in(dhg, dhu, wu, x1, dy, row(ffn_norm_gain), riders=[_swap_halves(early)])
    early_part = [_add_sibling(g, r, c_idx, "add_sibling_" + nm_) for g, r, nm_ in zip(early, got, early_names)]
    dmix, g_b_out = _out_proj_bwd(dx1, wo)
    gw_out = jnp.concatenate([_grad_weight(attn, dx1, 1, Q_COLS, "grad_w_out_attn")[0],
                              _grad_weight(c_act, dx1, 1, CONV_W, "grad_w_out_conv")[0]], axis=0)
    (dpc, g_cw, g_cb, g_lng, g_lnb, gbin_c), got2 = _conv_bwd(dmix, y_conv, pc, cw, row(conv_norm_gain), row(conv_norm_bias),
                                                              riders=[_exchange_chips(early_part)])
    early_red = [_add_chips(p, r, chip_idx, "add_chips_" + nm_) for p, r, nm_ in zip(early_part, got2, early_names)]
    (dqn, dcur, dprev, g_sink), early_g = _attn_bwd(qkvn, dmix, lse, attn_sinks, riders=[_join_halves(early_red)])
    dpq, gbin_q, g_qk = _qk_norm_bwd(dqn, dcur, dprev, pq, gq2, gk2)
    grad_x, g_mix_gain = _in_proj_bwd(dpq, dpc, wi, x2, dx1, row(mix_norm_gain))
    gw_in = jnp.concatenate([_grad_weight(h1, dpq, 1, D_MODEL, "grad_w_in_qkv")[0],
                             _grad_weight(h1, dpc, 1, D_MODEL, "grad_w_in_conv")[0]], axis=1)

    g_qk = g_qk.reshape(5, 2, HEAD_DIM)
    small = [g_mix_gain, jnp.concatenate([gbin_q, gbin_c], axis=1), g_qk[:4].sum(axis=(0, 1)), g_qk[4].sum(axis=0),
             g_sink[0, :N_HEADS], g_cb, g_lng, g_lnb, g_b_out, g_ffn_gain, jnp.concatenate([gfb_g, gfb_u], axis=1),
             loss_cols, g_cw[:CONV_K], jnp.concatenate([gfw_g[:3], gfw_u[:3]], axis=1)]
    _, tot = _gather_all(_pack(small))
    rep_names = ["mix_norm_gain", "b_in", "q_norm_gain", "k_norm_gain", "attn_sinks", "conv_dw_b", "conv_norm_gain",
                 "conv_norm_bias", "b_out", "ffn_norm_gain", "ffn_dw_b"]
    rep_w = [mix_norm_gain, b_in, q_norm_gain, k_norm_gain, attn_sinks, conv_dw_b, conv_norm_gain, conv_norm_bias,
             b_out, ffn_norm_gain, ffn_dw_b]
    rep_m = [m_mix_norm_gain, m_b_in, m_q_norm_gain, m_k_norm_gain, m_attn_sinks, m_conv_dw_b, m_conv_norm_gain,
             m_conv_norm_bias, m_b_out, m_ffn_norm_gain, m_ffn_dw_b]
    rep_v = [v_mix_norm_gain, v_b_in, v_q_norm_gain, v_k_norm_gain, v_attn_sinks, v_conv_dw_b, v_conv_norm_gain,
             v_conv_norm_bias, v_b_out, v_ffn_norm_gain, v_ffn_dw_b]
    shapes = [w.shape for w in rep_w] + [(D_MODEL,), (CONV_K, CONV_W), (3, 2 * D_FF)]
    tot_parts = _unpack(tot, shapes)
    loss = (0.5 / D_MODEL) * jnp.sum(tot_parts[len(rep_w)])
    g_cw_full, g_fw_full = tot_parts[len(rep_w) + 1], tot_parts[len(rep_w) + 2]
    n_rep_rows = _pack(rep_w).shape[0]
    rep_d, rep_nm, rep_nv = _adamw(_pack(rep_w), tot[:n_rep_rows], _pack(rep_m), _pack(rep_v), "adamw_small")
    rep_shapes = [w.shape for w in rep_w]
    res = {}
    for nm_, g_, d_, m_, v_ in zip(rep_names, tot_parts, _unpack(rep_d, rep_shapes), _unpack(rep_nm, rep_shapes),
                                   _unpack(rep_nv, rep_shapes)):
        res[nm_] = (g_, d_, m_, v_)

    g_cw_mine = lax.dynamic_slice_in_dim(g_cw_full, chip * (CONV_W // 4), CONV_W // 4, axis=1)
    g_fw_mine = lax.dynamic_slice_in_dim(g_fw_full, chip * (2 * D_FF // 4), 2 * D_FF // 4, axis=1)
    res["conv_dw_w"] = (g_cw_mine, *_adamw(conv_dw_w, g_cw_mine, m_conv_dw_w, v_conv_dw_w, "adamw_conv_dw_w"))
    res["ffn_dw_w"] = (g_fw_mine, *_adamw(ffn_dw_w, g_fw_mine, m_ffn_dw_w, v_ffn_dw_w, "adamw_ffn_dw_w"))

    gw_in4 = gw_in.reshape(D_MODEL, 4, IN_COLS // 4).transpose(1, 0, 2)
    late = [split(gw_in4), split(gw_out.reshape(4, D_MODEL // 4, D_MODEL))]
    late_names = ["w_in", "w_out"]
    got = _run_riders([_swap_halves(late)], "swap_halves_late")
    late_part = [_add_sibling(g, r, c_idx, "add_sibling_" + nm_) for g, r, nm_ in zip(late, got, late_names)]
    got2 = _run_riders([_exchange_chips(late_part)], "exchange_chips_late")
    late_red = [_add_chips(p, r, chip_idx, "add_chips_" + nm_) for p, r, nm_ in zip(late_part, got2, late_names)]
    late_g = _run_riders([_join_halves(late_red)], "join_halves_late")
    names = ["w_in", "w_out", "w_up", "w_down"]
    shard_g = list(late_g) + list(early_g)
    for nm_, w_, g_, m_, v_ in zip(names, big, shard_g, [m_w_in, m_w_out, m_w_up, m_w_down], [v_w_in, v_w_out, v_w_up, v_w_down]):
        g_ = g_.reshape(w_.shape)
        res[nm_] = (g_, *_adamw(w_, g_, m_, v_, "adamw_" + nm_))

    order = ["mix_norm_gain", "w_in", "b_in", "q_norm_gain", "k_norm_gain", "attn_sinks", "conv_dw_w", "conv_dw_b",
             "conv_norm_gain", "conv_norm_bias", "w_out", "b_out", "ffn_norm_gain", "w_up", "ffn_dw_w", "ffn_dw_b", "w_down"]
    return (loss, grad_x.reshape(x.shape), *[res[n][0] for n in order], *[res[n][1] for n in order],
            *[res[n][2] for n in order], *[res[n][3] for n in order])
```

```python
import functools

import jax
import jax.numpy as jnp
from jax import lax
from jax.experimental import pallas as pl
from jax.experimental.pallas import tpu as pltpu

F32 = jnp.float32
BF16 = jnp.bfloat16
MESH = pl.DeviceIdType.MESH

D_MODEL = 1024
HEAD_DIM = 64
N_HEADS = 8
Q_COLS = 512
QKV_COLS = 768
CONV_W = 512
CONV_K = 31
IN_COLS = 1792
D_FF = 2816
FFN_CB = 1408
BLOCK = 128
LANES = 128
EPS = 1e-6
NEG_INF = -1e30
SLOPES = tuple(float(2.0 ** (-(h + 1.0))) for h in range(N_HEADS))
HALO = 32
FHALO = 16
ROW_CHUNK = 64
GRAD_TILE = 2048
VMEM_LIMIT = 56 * 1024 * 1024

ADAM_LR = 0.001
ADAM_B1 = 0.9
ADAM_B2 = 0.999
ADAM_EPS = 1e-08
ADAM_WD = 0.01
ADAM_STEP = 10


def _params(sem=None):
    kw = dict(vmem_limit_bytes=VMEM_LIMIT)
    if sem is not None:
        kw["dimension_semantics"] = sem
    return pltpu.CompilerParams(**kw)


def _token_tile(t):
    return 512 if t % 512 == 0 and t >= 2048 else 128


def _sig(v):
    return 1.0 / (1.0 + jnp.exp(-v))


def _lo_mask():
    return lax.broadcasted_iota(jnp.int32, (1, LANES), 1) < HEAD_DIM


def _half_mean(v, lo):
    s_lo = jnp.sum(jnp.where(lo, v, 0.0), axis=-1, keepdims=True)
    s_hi = jnp.sum(jnp.where(lo, 0.0, v), axis=-1, keepdims=True)
    return jnp.where(lo, s_lo, s_hi) * (1.0 / HEAD_DIM)


def _dot(a, b):
    return jnp.dot(a, b, preferred_element_type=F32)


def _dot_nt(a, b):
    return lax.dot_general(a, b, (((1,), (1,)), ((), ())), preferred_element_type=F32)


def _dot_tn(a, b):
    return lax.dot_general(a, b, (((0,), (0,)), ((), ())), preferred_element_type=F32)


def _full(shape):
    nd = len(shape)
    return pl.BlockSpec(shape, lambda *_: (0,) * nd)


def _rows_to_tile(rows, n_rows):
    c = rows[0].shape[-1]
    rid = lax.broadcasted_iota(jnp.int32, (n_rows, c), 0)
    out = jnp.zeros((n_rows, c), F32)
    for k, r in enumerate(rows):
        out = jnp.where(rid == k, r, out)
    return out


ANY = pl.BlockSpec(memory_space=pl.ANY)


class _Rider:
    def __init__(self, ins, outs, sems, start, finish, aliases=None):
        self.ins, self.outs, self.sems = list(ins), list(outs), list(sems)
        self.start, self.finish, self.aliases = start, finish, dict(aliases or {})


def _join_riders(riders):
    ins, outs, sems, aliases, spans = [], [], [], {}, []
    for r in riders:
        spans.append((len(ins), len(outs), len(sems), r))
        for a, b in r.aliases.items():
            aliases[len(ins) + a] = len(outs) + b
        ins += r.ins
        outs += r.outs
        sems += r.sems

    def each(which):
        def run(i_refs, o_refs, s_refs):
            for i0, o0, s0, r in spans:
                getattr(r, which)(i_refs[i0:i0 + len(r.ins)], o_refs[o0:o0 + len(r.outs)], s_refs[s0:s0 + len(r.sems)])
        return run

    return _Rider(ins, outs, sems, each("start"), each("finish"), aliases)


def _call(body, args, *, name, grid, in_specs, out_specs, out_shape, scratch=(), riders=()):
    in_specs, out_specs, out_shape, scratch = list(in_specs), list(out_specs), list(out_shape), list(scratch)
    sem = ("arbitrary",) * len(grid)
    if not riders:
        outs = pl.pallas_call(body, name=name, grid=grid, in_specs=in_specs, out_specs=out_specs, out_shape=out_shape,
                              scratch_shapes=scratch, compiler_params=_params(sem))(*args)
        return list(outs), []
    r = _join_riders(riders)
    n_in, n_out, n_scr = len(in_specs), len(out_specs), len(scratch)
    nri, nro = len(r.ins), len(r.outs)

    def full(*refs):
        ins, rin = refs[:n_in], refs[n_in:n_in + nri]
        o0 = n_in + nri
        outs, rout = refs[o0:o0 + n_out], refs[o0 + n_out:o0 + n_out + nro]
        s0 = o0 + n_out + nro
        scr, rsem = refs[s0:s0 + n_scr], refs[s0 + n_scr:]
        first = functools.reduce(jnp.logical_and, [pl.program_id(k) == 0 for k in range(len(grid))])
        last = functools.reduce(jnp.logical_and, [pl.program_id(k) == grid[k] - 1 for k in range(len(grid))])

        @pl.when(first)
        def _():
            r.start(rin, rout, rsem)

        body(*ins, *outs, *scr)

        @pl.when(last)
        def _():
            r.finish(rin, rout, rsem)

    outs = pl.pallas_call(
        full, name=name, grid=grid, in_specs=in_specs + [ANY] * nri, out_specs=out_specs + [ANY] * nro,
        out_shape=out_shape + r.outs, scratch_shapes=scratch + r.sems,
        input_output_aliases={n_in + a: n_out + b for a, b in r.aliases.items()},
        compiler_params=_params(sem))(*args, *r.ins)
    return list(outs[:n_out]), list(outs[n_out:])


def _run_riders(riders, name):
    r = _join_riders(riders)
    nri, nro = len(r.ins), len(r.outs)

    def body(*refs):
        rin, rout, rsem = refs[:nri], refs[nri:nri + nro], refs[nri + nro:]
        r.start(rin, rout, rsem)
        r.finish(rin, rout, rsem)

    outs = pl.pallas_call(body, name=name, in_specs=[ANY] * nri, out_specs=[ANY] * nro, out_shape=r.outs,
                          scratch_shapes=r.sems, input_output_aliases=r.aliases)(*r.ins)
    return list(outs)


def _fwd_in(x, g_mix, w_in, b_in, gq2, gk2, riders=()):
    t = x.shape[0]
    tt = _token_tile(t)

    def body(x_ref, g_ref, w_ref, b_ref, gq_ref, gk_ref, h1_ref, pq_ref, pc_ref, qkvn_ref):
        xv = x_ref[...]
        r = lax.rsqrt(jnp.mean(xv * xv, axis=-1, keepdims=True) + EPS)
        h = (xv * r * g_ref[...]).astype(BF16)
        h1_ref[...] = h
        proj = _dot(h, w_ref[...]) + b_ref[...]
        pq_ref[...] = proj[:, :QKV_COLS]
        pc_ref[...] = proj[:, QKV_COLS:]
        lo = _lo_mask()
        for p in range(5):
            seg = proj[:, p * LANES:(p + 1) * LANES]
            rr = lax.rsqrt(_half_mean(seg * seg, lo) + EPS)
            gain = gq_ref[...] if p < 4 else gk_ref[...]
            qkvn_ref[:, p * LANES:(p + 1) * LANES] = (seg * rr * gain).astype(BF16)
        qkvn_ref[:, 640:768] = proj[:, 640:768].astype(BF16)

    return _call(
        body, (x, g_mix, w_in, b_in, gq2, gk2), name="fwd_in", grid=(t // tt,), riders=riders,
        in_specs=[pl.BlockSpec((tt, D_MODEL), lambda i: (i, 0)), _full((1, D_MODEL)), _full((D_MODEL, IN_COLS)),
                  _full((1, IN_COLS)), _full((1, LANES)), _full((1, LANES))],
        out_specs=[pl.BlockSpec((tt, D_MODEL), lambda i: (i, 0)), pl.BlockSpec((tt, QKV_COLS), lambda i: (i, 0)),
                   pl.BlockSpec((tt, 2 * CONV_W), lambda i: (i, 0)), pl.BlockSpec((tt, QKV_COLS), lambda i: (i, 0))],
        out_shape=[jax.ShapeDtypeStruct((t, D_MODEL), BF16), jax.ShapeDtypeStruct((t, QKV_COLS), F32),
                   jax.ShapeDtypeStruct((t, 2 * CONV_W), F32), jax.ShapeDtypeStruct((t, QKV_COLS), BF16)])


GROUP = 4
GROUP_ROWS = GROUP * BLOCK


def _attn_keys(i, kvp_ref, kvc_ref):
    kv = jnp.concatenate([kvp_ref[...], kvc_ref[...]], axis=0).astype(F32)
    lo = _lo_mask()

    def both_halves(pair):
        rolled = pltpu.roll(pair, HEAD_DIM, 1)
        return [jnp.where(lo, pair, rolled).astype(BF16), jnp.where(lo, rolled, pair).astype(BF16)]

    k_dup, v_dup = both_halves(kv[:, :LANES]), both_halves(kv[:, LANES:])
    qi = lax.broadcasted_iota(jnp.int32, (GROUP_ROWS, 2 * BLOCK), 0) & (BLOCK - 1)
    kj = lax.broadcasted_iota(jnp.int32, (GROUP_ROWS, 2 * BLOCK), 1)
    rel = qi + BLOCK - kj
    valid = (rel >= 0) & (rel < BLOCK) & ((kj >= BLOCK) | (i > 0))
    return k_dup, v_dup, rel.astype(F32), valid


def _per_head_column(values):
    seg = lax.broadcasted_iota(jnp.int32, (GROUP_ROWS, 1), 0) // BLOCK
    col = jnp.zeros((GROUP_ROWS, 1), F32) + values[GROUP - 1]
    for a in range(GROUP - 2, -1, -1):
        col = jnp.where(seg == a, values[a], col)
    return col


def _stack_heads(ref, kk, lo, dtype):
    parts = []
    for a in range(GROUP):
        h = GROUP * kk + a
        pair = ref[:, (h // 2) * LANES:(h // 2 + 1) * LANES]
        hm = lo if h % 2 == 0 else jnp.logical_not(lo)
        parts.append(jnp.where(hm, pair, jnp.zeros_like(pair)).astype(dtype))
    return jnp.concatenate(parts, axis=0)


def _group_scores(q_ref, kk, lo, k_dup, relf, valid):
    qs = _stack_heads(q_ref, kk, lo, BF16)
    slopes = _per_head_column([SLOPES[GROUP * kk + a] for a in range(GROUP)])
    s = _dot_nt(qs, k_dup[kk]) * 0.125 - slopes * relf
    return jnp.where(valid, s, NEG_INF), qs


def _attn_fwd(qkvn, sinks, riders=()):
    t = qkvn.shape[0]
    nb = t // BLOCK

    def body(sink_ref, q_ref, kvc_ref, kvp_ref, o_ref, lse_ref):
        i = pl.program_id(0)
        k_dup, v_dup, relf, valid = _attn_keys(i, kvp_ref, kvc_ref)
        lo = _lo_mask()
        lane = lax.broadcasted_iota(jnp.int32, (BLOCK, LANES), 1)
        lse_t = jnp.zeros((BLOCK, LANES), F32)
        outs = [jnp.zeros((BLOCK, LANES), F32) for _ in range(4)]
        for kk in range(N_HEADS // GROUP):
            s, _ = _group_scores(q_ref, kk, lo, k_dup, relf, valid)
            sink = _per_head_column([sink_ref[GROUP * kk + a] for a in range(GROUP)])
            m = jnp.maximum(jnp.max(s, axis=-1, keepdims=True), sink)
            pe = jnp.exp(s - m)
            l = jnp.sum(pe, axis=-1, keepdims=True) + jnp.exp(sink - m)
            lse = m + jnp.log(l)
            o = _dot((pe / l).astype(BF16), v_dup[kk])
            for a in range(GROUP):
                h = GROUP * kk + a
                rows = slice(a * BLOCK, (a + 1) * BLOCK)
                hm = lo if h % 2 == 0 else jnp.logical_not(lo)
                outs[h // 2] = outs[h // 2] + jnp.where(hm, o[rows], 0.0)
                lse_t = jnp.where(lane == h, lse[rows], lse_t)
        for p in range(4):
            o_ref[:, p * LANES:(p + 1) * LANES] = outs[p].astype(BF16)
        lse_ref[...] = lse_t

    return _call(
        body, (sinks, qkvn, qkvn, qkvn), name="attn_fwd", grid=(nb,), riders=riders,
        in_specs=[pl.BlockSpec(memory_space=pltpu.SMEM),
                  pl.BlockSpec((BLOCK, Q_COLS), lambda i: (i, 0)),
                  pl.BlockSpec((BLOCK, 2 * LANES), lambda i: (i, 2)),
                  pl.BlockSpec((BLOCK, 2 * LANES), lambda i: (jnp.maximum(i - 1, 0), 2))],
        out_specs=[pl.BlockSpec((BLOCK, Q_COLS), lambda i: (i, 0)), pl.BlockSpec((BLOCK, LANES), lambda i: (i, 0))],
        out_shape=[jax.ShapeDtypeStruct((t, Q_COLS), BF16), jax.ShapeDtypeStruct((t, LANES), F32)])


def _glu(pc):
    return pc[:, :CONV_W] * _sig(pc[:, CONV_W:])


def _group_norm_stats(seg, lo):
    mu = _half_mean(seg, lo)
    d = seg - mu
    rstd = lax.rsqrt(_half_mean(d * d, lo) + EPS)
    return d * rstd, rstd


def _shifted_copies(src, dst, tt):
    n = tt + HALO - 8
    for s in range(1, 8):
        dst[s - 1, 0:n, :] = src[s:s + n, :]


def _tap_rows(src, shifted, off, r0, cs):
    q, s = divmod(off, 8)
    if s == 0:
        return src[r0 + off:r0 + off + ROW_CHUNK, cs]
    return shifted[s - 1, r0 + 8 * q:r0 + 8 * q + ROW_CHUNK, cs]


def _shift_scratch(tt):
    return pltpu.VMEM((7, tt + HALO - 8, CONV_W), F32)


def _conv_fwd(pc, cw, cb, ln_g, ln_b, riders=()):
    t = pc.shape[0]
    tt = _token_tile(t)

    def body(cur_ref, prev_ref, w_ref, b_ref, g_ref, bb_ref, y_ref, c_ref, scr, shf):
        i = pl.program_id(0)
        scr[0:HALO, :] = _glu(prev_ref[...]) * (i > 0).astype(F32)
        scr[HALO:HALO + tt, :] = _glu(cur_ref[...])
        _shifted_copies(scr, shf, tt)
        for cbk in range(CONV_W // LANES):
            cs = slice(cbk * LANES, (cbk + 1) * LANES)
            for rb in range(tt // ROW_CHUNK):
                r0 = rb * ROW_CHUNK
                acc = jnp.zeros((ROW_CHUNK, LANES), F32) + b_ref[:, cs]
                for k in range(CONV_K):
                    acc = acc + w_ref[k:k + 1, cs] * _tap_rows(scr, shf, 2 + k, r0, cs)
                y_ref[r0:r0 + ROW_CHUNK, cs] = acc
        lo = _lo_mask()
        for p in range(CONV_W // LANES):
            cs = slice(p * LANES, (p + 1) * LANES)
            yh, _ = _group_norm_stats(y_ref[:, cs], lo)
            z = yh * g_ref[:, cs] + bb_ref[:, cs]
            c_ref[:, cs] = (z * _sig(z)).astype(BF16)

    hb = tt // HALO
    return _call(
        body, (pc, pc, cw, cb, ln_g, ln_b), name="conv_fwd", grid=(t // tt,), riders=riders,
        in_specs=[pl.BlockSpec((tt, 2 * CONV_W), lambda i: (i, 0)),
                  pl.BlockSpec((HALO, 2 * CONV_W), lambda i: (jnp.maximum(i * hb - 1, 0), 0)),
                  _full((CONV_K, CONV_W)), _full((1, CONV_W)), _full((1, CONV_W)), _full((1, CONV_W))],
        out_specs=[pl.BlockSpec((tt, CONV_W), lambda i: (i, 0)), pl.BlockSpec((tt, CONV_W), lambda i: (i, 0))],
        out_shape=[jax.ShapeDtypeStruct((t, CONV_W), F32), jax.ShapeDtypeStruct((t, CONV_W), BF16)],
        scratch=[pltpu.VMEM((tt + HALO, CONV_W), F32), _shift_scratch(tt)])


def _out_proj(x, attn, c, w_out, b_out, g_ffn, riders=()):
    t = x.shape[0]
    tt = _token_tile(t)

    def body(x_ref, a_ref, c_ref, w_ref, b_ref, g_ref, x1_ref, h2_ref):
        x1 = x_ref[...] + _dot(a_ref[...], w_ref[0:Q_COLS, :]) + _dot(c_ref[...], w_ref[Q_COLS:, :]) + b_ref[...]
        x1_ref[...] = x1
        r = lax.rsqrt(jnp.mean(x1 * x1, axis=-1, keepdims=True) + EPS)
        h2_ref[...] = (x1 * r * g_ref[...]).astype(BF16)

    row = lambda w: pl.BlockSpec((tt, w), lambda i: (i, 0))
    return _call(
        body, (x, attn, c, w_out, b_out, g_ffn), name="out_proj", grid=(t // tt,), riders=riders,
        in_specs=[row(D_MODEL), row(Q_COLS), row(CONV_W), _full((D_MODEL, D_MODEL)), _full((1, D_MODEL)), _full((1, D_MODEL))],
        out_specs=[row(D_MODEL), row(D_MODEL)],
        out_shape=[jax.ShapeDtypeStruct((t, D_MODEL), F32), jax.ShapeDtypeStruct((t, D_MODEL), BF16)])


def _ffn_conv(scr, cur, dw_ref, db_ref, tt):
    return (dw_ref[0:1, :] * scr[FHALO - 2:FHALO - 2 + tt, :] + dw_ref[1:2, :] * scr[FHALO - 1:FHALO - 1 + tt, :]
            + dw_ref[2:3, :] * cur + db_ref[...])


def _ffn_up(h2, w_up, dw, db):
    t = h2.shape[0]
    tt = _token_tile(t)
    nj = D_FF // FFN_CB

    def body(hc_ref, hp_ref, wg_ref, wu_ref, dwg_ref, dwu_ref, dbg_ref, dbu_ref,
             hg_ref, hu_ref, upg_ref, upu_ref, act_ref, sg, su):
        i = pl.program_id(1)
        hc = hc_ref[...]
        hp = hp_ref[...] * (i > 0).astype(BF16)
        ups = []
        for w_ref, dw_ref, db_ref, h_ref, up_ref, scr in ((wg_ref, dwg_ref, dbg_ref, hg_ref, upg_ref, sg),
                                                          (wu_ref, dwu_ref, dbu_ref, hu_ref, upu_ref, su)):
            cur = _dot(hc, w_ref[...])
            h_ref[...] = cur.astype(BF16)
            scr[0:FHALO, :] = _dot(hp, w_ref[...])
            scr[FHALO:FHALO + tt, :] = cur
            up = _ffn_conv(scr, cur, dw_ref, db_ref, tt)
            up_ref[...] = up
            ups.append(up)
        g, u = ups
        act_ref[...] = (g * _sig(g) * u).astype(BF16)

    fb = tt // FHALO
    colg = lambda r: pl.BlockSpec((r, FFN_CB), lambda j, i: (0, j))
    colu = lambda r: pl.BlockSpec((r, FFN_CB), lambda j, i: (0, j + nj))
    tile = pl.BlockSpec((tt, FFN_CB), lambda j, i: (i, j))
    return pl.pallas_call(
        body, name="ffn_up", grid=(nj, t // tt),
        in_specs=[pl.BlockSpec((tt, D_MODEL), lambda j, i: (i, 0)),
                  pl.BlockSpec((FHALO, D_MODEL), lambda j, i: (jnp.maximum(i * fb - 1, 0), 0)),
                  colg(D_MODEL), colu(D_MODEL), colg(3), colu(3), colg(1), colu(1)],
        out_specs=[tile] * 5,
        out_shape=[jax.ShapeDtypeStruct((t, D_FF), BF16), jax.ShapeDtypeStruct((t, D_FF), BF16),
                   jax.ShapeDtypeStruct((t, D_FF), F32), jax.ShapeDtypeStruct((t, D_FF), F32),
                   jax.ShapeDtypeStruct((t, D_FF), BF16)],
        scratch_shapes=[pltpu.VMEM((tt + FHALO, FFN_CB), F32), pltpu.VMEM((tt + FHALO, FFN_CB), F32)],
        compiler_params=_params(("parallel", "parallel")),
    )(h2, h2, w_up, w_up, dw, dw, db, db)


def _ffn_down(act, w_down, x1, target):
    t = act.shape[0]
    tt = _token_tile(t)

    def body(a_ref, w_ref, x1_ref, t_ref, dy_ref, loss_ref):
        err = x1_ref[...] + _dot(a_ref[...], w_ref[...]) - t_ref[...]
        dy_ref[...] = err * (1.0 / D_MODEL)

        @pl.when(pl.program_id(0) == 0)
        def _():
            loss_ref[...] = jnp.zeros_like(loss_ref)

        loss_ref[...] += jnp.sum(err * err, axis=0, keepdims=True)

    row = lambda w: pl.BlockSpec((tt, w), lambda i: (i, 0))
    return pl.pallas_call(
        body, name="ffn_down", grid=(t // tt,),
        in_specs=[row(D_FF), _full((D_FF, D_MODEL)), row(D_MODEL), row(D_MODEL)],
        out_specs=[row(D_MODEL), _full((1, D_MODEL))],
        out_shape=[jax.ShapeDtypeStruct((t, D_MODEL), F32), jax.ShapeDtypeStruct((1, D_MODEL), F32)],
        compiler_params=_params(("arbitrary",)),
    )(act, w_down, x1, target)


def _ffn_bwd_act(dy, w_down, up_g, up_u):
    t = dy.shape[0]
    tt = _token_tile(t)
    nj = D_FF // FFN_CB

    def body(dy_ref, wd_ref, g_ref, u_ref, dg_ref, du_ref, gbg_ref, gbu_ref):
        i = pl.program_id(1)
        d_act = _dot_nt(dy_ref[...].astype(BF16), wd_ref[...])
        g, u = g_ref[...], u_ref[...]
        s = _sig(g)
        d_u = d_act * (g * s)
        d_g = d_act * u * (s * (1.0 + g * (1.0 - s)))

        @pl.when(i == 0)
        def _():
            for r in (gbg_ref, gbu_ref):
                r[...] = jnp.zeros_like(r)

        for d, o_ref, gb_ref in ((d_g, dg_ref, gbg_ref), (d_u, du_ref, gbu_ref)):
            o_ref[...] = d.astype(BF16)
            gb_ref[...] += jnp.sum(d, axis=0, keepdims=True)

    tile = pl.BlockSpec((tt, FFN_CB), lambda j, i: (i, j))
    acc = pl.BlockSpec((1, FFN_CB), lambda j, i: (0, j))
    return pl.pallas_call(
        body, name="ffn_bwd_act", grid=(nj, t // tt),
        in_specs=[pl.BlockSpec((tt, D_MODEL), lambda j, i: (i, 0)), pl.BlockSpec((FFN_CB, D_MODEL), lambda j, i: (j, 0)),
                  tile, tile],
        out_specs=[tile, tile, acc, acc],
        out_shape=[jax.ShapeDtypeStruct((t, D_FF), BF16), jax.ShapeDtypeStruct((t, D_FF), BF16),
                   jax.ShapeDtypeStruct((1, D_FF), F32), jax.ShapeDtypeStruct((1, D_FF), F32)],
        compiler_params=_params(("parallel", "arbitrary")),
    )(dy, w_down, up_g, up_u)


def _ffn_bwd_conv(dg, du, hg, hu, dw):
    t = dg.shape[0]
    tt = _token_tile(t)
    nj = D_FF // FFN_CB
    ni = t // tt

    def body(gc_ref, gn_ref, uc_ref, un_ref, hg_ref, hu_ref, dwg_ref, dwu_ref, og_ref, ou_ref, gwg_ref, gwu_ref, scr):
        i = pl.program_id(1)
        last = (i < ni - 1).astype(F32)

        @pl.when(i == 0)
        def _():
            gwg_ref[...] = jnp.zeros_like(gwg_ref)
            gwu_ref[...] = jnp.zeros_like(gwu_ref)

        for c_ref, n_ref, h_ref, dw_ref, o_ref, gw_ref in ((gc_ref, gn_ref, hg_ref, dwg_ref, og_ref, gwg_ref),
                                                           (uc_ref, un_ref, hu_ref, dwu_ref, ou_ref, gwu_ref)):
            cur = c_ref[...].astype(F32)
            scr[0:tt, :] = cur
            scr[tt:tt + FHALO, :] = n_ref[...].astype(F32) * last
            nx1 = scr[1:1 + tt, :]
            nx2 = scr[2:2 + tt, :]
            o_ref[...] = (dw_ref[2:3, :] * cur + dw_ref[1:2, :] * nx1 + dw_ref[0:1, :] * nx2).astype(BF16)
            hw = h_ref[...].astype(F32)
            rows = [jnp.sum(hw * d, axis=0, keepdims=True) for d in (nx2, nx1, cur)]
            gw_ref[...] += _rows_to_tile(rows, 8)

    fb = tt // FHALO
    tile = pl.BlockSpec((tt, FFN_CB), lambda j, i: (i, j))
    nxt = pl.BlockSpec((FHALO, FFN_CB), lambda j, i: (jnp.minimum((i + 1) * fb, t // FHALO - 1), j))
    acc = pl.BlockSpec((8, FFN_CB), lambda j, i: (0, j))
    return pl.pallas_call(
        body, name="ffn_bwd_conv", grid=(nj, ni),
        in_specs=[tile, nxt, tile, nxt, tile, tile, pl.BlockSpec((3, FFN_CB), lambda j, i: (0, j)),
                  pl.BlockSpec((3, FFN_CB), lambda j, i: (0, j + nj))],
        out_specs=[tile, tile, acc, acc],
        out_shape=[jax.ShapeDtypeStruct((t, D_FF), BF16), jax.ShapeDtypeStruct((t, D_FF), BF16),
                   jax.ShapeDtypeStruct((8, D_FF), F32), jax.ShapeDtypeStruct((8, D_FF), F32)],
        scratch_shapes=[pltpu.VMEM((tt + FHALO, FFN_CB), F32)],
        compiler_params=_params(("parallel", "arbitrary")),
    )(dg, dg, du, du, hg, hu, dw, dw)


def _ffn_bwd_in(dhg, dhu, w_up, x1, dy, g_ffn, riders=()):
    t = x1.shape[0]
    tt = _token_tile(t)

    def body(dg_ref, du_ref, w_ref, x1_ref, dy_ref, g_ref, dx_ref, gg_ref):
        d_h2 = _dot_nt(dg_ref[...], w_ref[:, 0:D_FF]) + _dot_nt(du_ref[...], w_ref[:, D_FF:])
        x1 = x1_ref[...]
        r = lax.rsqrt(jnp.mean(x1 * x1, axis=-1, keepdims=True) + EPS)
        xh = x1 * r
        gd = d_h2 * g_ref[...]
        dx_ref[...] = dy_ref[...] + r * (gd - xh * jnp.mean(gd * xh, axis=-1, keepdims=True))

        @pl.when(pl.program_id(0) == 0)
        def _():
            gg_ref[...] = jnp.zeros_like(gg_ref)

        gg_ref[...] += jnp.sum(d_h2 * xh, axis=0, keepdims=True)

    row = lambda w: pl.BlockSpec((tt, w), lambda i: (i, 0))
    return _call(
        body, (dhg, dhu, w_up, x1, dy, g_ffn), name="ffn_bwd_in", grid=(t // tt,), riders=riders,
        in_specs=[row(D_FF), row(D_FF), _full((D_MODEL, 2 * D_FF)), row(D_MODEL), row(D_MODEL), _full((1, D_MODEL))],
        out_specs=[row(D_MODEL), _full((1, D_MODEL))],
        out_shape=[jax.ShapeDtypeStruct((t, D_MODEL), F32), jax.ShapeDtypeStruct((1, D_MODEL), F32)])


def _grad_weight(a, b, nj, mb, name):
    t, m = a.shape
    n = b.shape[1]
    nb_ = n // nj
    tt = GRAD_TILE if t % GRAD_TILE == 0 else _token_tile(t)

    def body(a_ref, b_ref, o_ref):
        @pl.when(pl.program_id(2) == 0)
        def _():
            o_ref[...] = jnp.zeros_like(o_ref)

        o_ref[0] += _dot_tn(a_ref[...].astype(BF16), b_ref[...].astype(BF16))

    return pl.pallas_call(
        body, name=name, grid=(nj, m // mb, t // tt),
        in_specs=[pl.BlockSpec((tt, mb), lambda j, mi, i: (i, mi)), pl.BlockSpec((tt, nb_), lambda j, mi, i: (i, j))],
        out_specs=pl.BlockSpec((1, mb, nb_), lambda j, mi, i: (j, mi, 0)),
        out_shape=jax.ShapeDtypeStruct((nj, m, nb_), F32),
        compiler_params=_params(("parallel", "parallel", "arbitrary")),
    )(a, b)


def _out_proj_bwd(dx1, w_out):
    t = dx1.shape[0]
    tt = _token_tile(t)

    def body(d_ref, w_ref, dm_ref, gb_ref):
        d = d_ref[...]
        dm_ref[...] = _dot_nt(d.astype(BF16), w_ref[...])

        @pl.when(pl.program_id(0) == 0)
        def _():
            gb_ref[...] = jnp.zeros_like(gb_ref)

        gb_ref[...] += jnp.sum(d, axis=0, keepdims=True)

    row = pl.BlockSpec((tt, D_MODEL), lambda i: (i, 0))
    return pl.pallas_call(
        body, name="out_proj_bwd", grid=(t // tt,),
        in_specs=[row, _full((D_MODEL, D_MODEL))],
        out_specs=[row, _full((1, D_MODEL))],
        out_shape=[jax.ShapeDtypeStruct((t, D_MODEL), F32), jax.ShapeDtypeStruct((1, D_MODEL), F32)],
        compiler_params=_params(("arbitrary",)),
    )(dx1, w_out)


def _conv_bwd(dmix, y, pc, cw, ln_g, ln_b, riders=()):
    t = y.shape[0]
    tt = _token_tile(t)
    ni = t // tt
    ncb = CONV_W // LANES

    def body(dc_ref, dcn_ref, y_ref, yn_ref, pc_ref, pcp_ref, w_ref, g_ref, bb_ref,
             dp_ref, gw_ref, gb_ref, gg_ref, gbb_ref, gbin_ref, scr_d, scr_c, scr_o, shf_d, shf_c):
        i = pl.program_id(0)
        lo = _lo_mask()

        @pl.when(i == 0)
        def _():
            for r in (gw_ref, gb_ref, gg_ref, gbb_ref, gbin_ref):
                r[...] = jnp.zeros_like(r)

        def norm_bwd(dc, yv, cs):
            yh, rstd = _group_norm_stats(yv, lo)
            z = yh * g_ref[:, cs] + bb_ref[:, cs]
            s = _sig(z)
            dz = dc * (s * (1.0 + z * (1.0 - s)))
            dyh = dz * g_ref[:, cs]
            d_y = rstd * (dyh - _half_mean(dyh, lo) - yh * _half_mean(dyh * yh, lo))
            return d_y, dz, yh

        for p in range(ncb):
            cs = slice(p * LANES, (p + 1) * LANES)
            d_y, dz, yh = norm_bwd(dc_ref[:, cs], y_ref[:, cs], cs)
            scr_d[0:tt, cs] = d_y
            gg_ref[:, cs] += jnp.sum(dz * yh, axis=0, keepdims=True)
            gbb_ref[:, cs] += jnp.sum(dz, axis=0, keepdims=True)
            gb_ref[:, cs] += jnp.sum(d_y, axis=0, keepdims=True)
            d_yn, _, _ = norm_bwd(dcn_ref[:, cs], yn_ref[:, cs], cs)
            scr_d[tt:tt + HALO, cs] = d_yn * (i < ni - 1).astype(F32)
        scr_c[0:HALO, :] = _glu(pcp_ref[...]) * (i > 0).astype(F32)
        scr_c[HALO:HALO + tt, :] = _glu(pc_ref[...])
        _shifted_copies(scr_d, shf_d, tt)
        _shifted_copies(scr_c, shf_c, tt)

        rid = lax.broadcasted_iota(jnp.int32, (HALO, LANES), 0)
        for cbk in range(ncb):
            cs = slice(cbk * LANES, (cbk + 1) * LANES)
            for rb in range(tt // ROW_CHUNK):
                r0 = rb * ROW_CHUNK
                acc = jnp.zeros((ROW_CHUNK, LANES), F32)
                for k in range(CONV_K):
                    acc = acc + w_ref[k:k + 1, cs] * _tap_rows(scr_d, shf_d, 30 - k, r0, cs)
                scr_o[r0:r0 + ROW_CHUNK, cs] = acc
            gwt = jnp.zeros((HALO, LANES), F32)
            for k in range(CONV_K):
                acc = jnp.zeros((ROW_CHUNK, LANES), F32)
                for rb in range(tt // ROW_CHUNK):
                    r0 = rb * ROW_CHUNK
                    acc = acc + scr_d[r0:r0 + ROW_CHUNK, cs] * _tap_rows(scr_c, shf_c, 2 + k, r0, cs)
                gwt = jnp.where(rid == k, jnp.sum(acc, axis=0, keepdims=True), gwt)
            gw_ref[:, cs] += gwt
        d_c0 = scr_o[...]
        a = pc_ref[:, 0:CONV_W]
        s = _sig(pc_ref[:, CONV_W:])
        d_a = d_c0 * s
        d_gate = d_c0 * a * s * (1.0 - s)
        dp_ref[:, 0:CONV_W] = d_a.astype(BF16)
        dp_ref[:, CONV_W:] = d_gate.astype(BF16)
        gbin_ref[:, 0:CONV_W] += jnp.sum(d_a, axis=0, keepdims=True)
        gbin_ref[:, CONV_W:] += jnp.sum(d_gate, axis=0, keepdims=True)

    hb = tt // HALO
    nxt = lambda col: pl.BlockSpec((HALO, CONV_W), lambda i: (jnp.minimum((i + 1) * hb, t // HALO - 1), col))
    return _call(
        body, (dmix, dmix, y, y, pc, pc, cw, ln_g, ln_b), name="conv_bwd", grid=(ni,), riders=riders,
        in_specs=[pl.BlockSpec((tt, CONV_W), lambda i: (i, 1)), nxt(1),
                  pl.BlockSpec((tt, CONV_W), lambda i: (i, 0)), nxt(0),
                  pl.BlockSpec((tt, 2 * CONV_W), lambda i: (i, 0)),
                  pl.BlockSpec((HALO, 2 * CONV_W), lambda i: (jnp.maximum(i * hb - 1, 0), 0)),
                  _full((CONV_K, CONV_W)), _full((1, CONV_W)), _full((1, CONV_W))],
        out_specs=[pl.BlockSpec((tt, 2 * CONV_W), lambda i: (i, 0)), _full((HALO, CONV_W)), _full((1, CONV_W)),
                   _full((1, CONV_W)), _full((1, CONV_W)), _full((1, 2 * CONV_W))],
        out_shape=[jax.ShapeDtypeStruct((t, 2 * CONV_W), BF16), jax.ShapeDtypeStruct((HALO, CONV_W), F32),
                   jax.ShapeDtypeStruct((1, CONV_W), F32), jax.ShapeDtypeStruct((1, CONV_W), F32),
                   jax.ShapeDtypeStruct((1, CONV_W), F32), jax.ShapeDtypeStruct((1, 2 * CONV_W), F32)],
        scratch=[pltpu.VMEM((tt + HALO, CONV_W), F32), pltpu.VMEM((tt + HALO, CONV_W), F32),
                 pltpu.VMEM((tt, CONV_W), F32), _shift_scratch(tt), _shift_scratch(tt)])


def _attn_bwd(qkvn, dmix, lse, sinks, riders=()):
    t = qkvn.shape[0]
    nb = t // BLOCK

    def body(sink_ref, q_ref, kvc_ref, kvp_ref, do_ref, lse_ref, dq_ref, dcur_ref, dprev_ref, ds_ref):
        i = pl.program_id(0)
        k_dup, v_dup, relf, valid = _attn_keys(i, kvp_ref, kvc_ref)
        lo = _lo_mask()
        lane1 = lax.broadcasted_iota(jnp.int32, (1, LANES), 1)
        lane = lax.broadcasted_iota(jnp.int32, (BLOCK, LANES), 1)
        lse_t = lse_ref[...]
        dqs = [jnp.zeros((BLOCK, LANES), F32) for _ in range(4)]
        dsink = jnp.zeros((1, LANES), F32)
        dkv = []
        for kk in range(N_HEADS // GROUP):
            s, qs = _group_scores(q_ref, kk, lo, k_dup, relf, valid)
            lse = jnp.concatenate([jnp.sum(jnp.where(lane == GROUP * kk + a, lse_t, 0.0), axis=-1, keepdims=True)
                                   for a in range(GROUP)], axis=0)
            prob = jnp.exp(s - lse)
            dos = _stack_heads(do_ref, kk, lo, BF16)
            dp = _dot_nt(dos, v_dup[kk])
            dsum = jnp.sum(prob * dp, axis=-1, keepdims=True)
            dsb = (prob * (dp - dsum) * 0.125).astype(BF16)
            sink = _per_head_column([sink_ref[GROUP * kk + a] for a in range(GROUP)])
            dsk = -jnp.exp(sink - lse) * dsum
            dq = _dot(dsb, k_dup[kk])
            for a in range(GROUP):
                h = GROUP * kk + a
                rows = slice(a * BLOCK, (a + 1) * BLOCK)
                hm = lo if h % 2 == 0 else jnp.logical_not(lo)
                dqs[h // 2] = dqs[h // 2] + jnp.where(hm, dq[rows], 0.0)
                dsink = dsink + jnp.where(lane1 == h, jnp.sum(dsk[rows], axis=0, keepdims=True), 0.0)
            dk_x = _dot_tn(dsb, qs)
            dv_x = _dot_tn(prob.astype(BF16), dos)
            dkv.append((dk_x + pltpu.roll(dk_x, HEAD_DIM, 1), dv_x + pltpu.roll(dv_x, HEAD_DIM, 1)))
        for p in range(4):
            dq_ref[:, p * LANES:(p + 1) * LANES] = dqs[p]
        dk = jnp.where(lo, dkv[0][0], dkv[1][0])
        dv = jnp.where(lo, dkv[0][1], dkv[1][1])
        dprev_ref[:, 0:LANES] = dk[0:BLOCK]
        dprev_ref[:, LANES:] = dv[0:BLOCK]
        dcur_ref[:, 0:LANES] = dk[BLOCK:]
        dcur_ref[:, LANES:] = dv[BLOCK:]

        @pl.when(i == 0)
        def _():
            ds_ref[...] = jnp.zeros_like(ds_ref)

        ds_ref[...] += dsink

    blk = lambda w: pl.BlockSpec((BLOCK, w), lambda i: (i, 0))
    return _call(
        body, (sinks, qkvn, qkvn, qkvn, dmix, lse), name="attn_bwd", grid=(nb,), riders=riders,
        in_specs=[pl.BlockSpec(memory_space=pltpu.SMEM), blk(Q_COLS),
                  pl.BlockSpec((BLOCK, 2 * LANES), lambda i: (i, 2)),
                  pl.BlockSpec((BLOCK, 2 * LANES), lambda i: (jnp.maximum(i - 1, 0), 2)),
                  blk(Q_COLS), blk(LANES)],
        out_specs=[blk(Q_COLS), blk(2 * LANES), blk(2 * LANES), _full((1, LANES))],
        out_shape=[jax.ShapeDtypeStruct((t, Q_COLS), F32), jax.ShapeDtypeStruct((t, 2 * LANES), F32),
                   jax.ShapeDtypeStruct((t, 2 * LANES), F32), jax.ShapeDtypeStruct((1, LANES), F32)])


def _qk_norm_bwd(dqn, dcur, dprev, pq, gq2, gk2):
    t = dqn.shape[0]
    nb = t // BLOCK

    def body(dq_ref, dc_ref, dn_ref, pq_ref, gq_ref, gk_ref, dp_ref, gbin_ref, gg_ref):
        i = pl.program_id(0)
        lo = _lo_mask()
        dkv = dc_ref[...] + dn_ref[...] * (i < nb - 1).astype(F32)

        @pl.when(i == 0)
        def _():
            gbin_ref[...] = jnp.zeros_like(gbin_ref)
            gg_ref[...] = jnp.zeros_like(gg_ref)

        for p in range(5):
            cs = slice(p * LANES, (p + 1) * LANES)
            seg = pq_ref[:, cs]
            dn = dq_ref[:, cs] if p < 4 else dkv[:, 0:LANES]
            gain = gq_ref[...] if p < 4 else gk_ref[...]
            rr = lax.rsqrt(_half_mean(seg * seg, lo) + EPS)
            xh = seg * rr
            gd = dn * gain
            d = rr * (gd - xh * _half_mean(gd * xh, lo))
            dp_ref[:, cs] = d.astype(BF16)
            gbin_ref[:, cs] += jnp.sum(d, axis=0, keepdims=True)
            gg_ref[:, cs] += jnp.sum(dn * xh, axis=0, keepdims=True)
        dv = dkv[:, LANES:]
        dp_ref[:, 640:768] = dv.astype(BF16)
        gbin_ref[:, 640:768] += jnp.sum(dv, axis=0, keepdims=True)

    blk = lambda w: pl.BlockSpec((BLOCK, w), lambda i: (i, 0))
    return pl.pallas_call(
        body, name="qk_norm_bwd", grid=(nb,),
        in_specs=[blk(Q_COLS), blk(2 * LANES), pl.BlockSpec((BLOCK, 2 * LANES), lambda i: (jnp.minimum(i + 1, nb - 1), 0)),
                  blk(QKV_COLS), _full((1, LANES)), _full((1, LANES))],
        out_specs=[blk(QKV_COLS), _full((1, QKV_COLS)), _full((1, 5 * LANES))],
        out_shape=[jax.ShapeDtypeStruct((t, QKV_COLS), BF16), jax.ShapeDtypeStruct((1, QKV_COLS), F32),
                   jax.ShapeDtypeStruct((1, 5 * LANES), F32)],
        compiler_params=_params(("arbitrary",)),
    )(dqn, dcur, dprev, pq, gq2, gk2)


def _in_proj_bwd(dpq, dpc, w_in, x, dx1, g_mix):
    t = x.shape[0]
    tt = _token_tile(t)

    def body(dq_ref, dc_ref, w_ref, x_ref, d1_ref, g_ref, gx_ref, gg_ref):
        d_h = _dot_nt(dq_ref[...], w_ref[:, 0:QKV_COLS]) + _dot_nt(dc_ref[...], w_ref[:, QKV_COLS:])
        xv = x_ref[...]
        r = lax.rsqrt(jnp.mean(xv * xv, axis=-1, keepdims=True) + EPS)
        xh = xv * r
        gd = d_h * g_ref[...]
        gx_ref[...] = d1_ref[...] + r * (gd - xh * jnp.mean(gd * xh, axis=-1, keepdims=True))

        @pl.when(pl.program_id(0) == 0)
        def _():
            gg_ref[...] = jnp.zeros_like(gg_ref)

        gg_ref[...] += jnp.sum(d_h * xh, axis=0, keepdims=True)

    row = lambda w: pl.BlockSpec((tt, w), lambda i: (i, 0))
    return pl.pallas_call(
        body, name="in_proj_bwd", grid=(t // tt,),
        in_specs=[row(QKV_COLS), row(2 * CONV_W), _full((D_MODEL, IN_COLS)), row(D_MODEL), row(D_MODEL), _full((1, D_MODEL))],
        out_specs=[row(D_MODEL), _full((1, D_MODEL))],
        out_shape=[jax.ShapeDtypeStruct((t, D_MODEL), F32), jax.ShapeDtypeStruct((1, D_MODEL), F32)],
        compiler_params=_params(("arbitrary",)),
    )(dpq, dpc, w_in, x, dx1, g_mix)


def _row_block(r):
    if r <= 256:
        return r
    return max(b for b in range(8, 257, 8) if r % b == 0)


def _adamw(w, g, m, v, name):
    r, c = w.shape
    rb = _row_block(r)

    def body(w_ref, g_ref, m_ref, v_ref, d_ref, nm_ref, nv_ref):
        gv = g_ref[...]
        nm = ADAM_B1 * m_ref[...] + (1.0 - ADAM_B1) * gv
        nv = ADAM_B2 * v_ref[...] + (1.0 - ADAM_B2) * (gv * gv)
        m_hat = nm / (1.0 - ADAM_B1 ** ADAM_STEP)
        v_hat = nv / (1.0 - ADAM_B2 ** ADAM_STEP)
        d_ref[...] = -ADAM_LR * (m_hat / (jnp.sqrt(v_hat) + ADAM_EPS) + ADAM_WD * w_ref[...])
        nm_ref[...] = nm
        nv_ref[...] = nv

    blk = pl.BlockSpec((rb, c), lambda i: (i, 0))
    shp = jax.ShapeDtypeStruct((r, c), F32)
    return pl.pallas_call(
        body, name=name, grid=(r // rb,), in_specs=[blk] * 4, out_specs=[blk] * 3, out_shape=[shp] * 3,
        compiler_params=_params(("parallel",)),
    )(w, g, m, v)


def _place():
    x, y, c = lax.axis_index("x"), lax.axis_index("y"), lax.axis_index("c")
    chips = [(1 - x, y), (x, 1 - y), (1 - x, 1 - y)]
    return x, y, c, chips


def _gather_all(v):
    r = v.shape[0]

    def body(v_ref, all_ref, sum_ref, send_sems, recv_sems):
        x, y, c, _ = _place()
        me = 4 * x + 2 * y + c
        all_ref[me] = v_ref[...]
        copies = []
        for k in range(1, 8):
            kx, ky, kc = (k >> 2) & 1, (k >> 1) & 1, k & 1
            peer = (x ^ kx, y ^ ky, c ^ kc)
            cp = pltpu.make_async_remote_copy(src_ref=v_ref, dst_ref=all_ref.at[me], send_sem=send_sems.at[k - 1],
                                              recv_sem=recv_sems.at[k - 1], device_id=peer, device_id_type=MESH)
            cp.start()
            copies.append((cp, 4 * peer[0] + 2 * peer[1] + peer[2]))
        for k, (cp, src_idx) in enumerate(copies):
            pltpu.make_async_remote_copy(src_ref=v_ref, dst_ref=all_ref.at[src_idx], send_sem=send_sems.at[k],
                                         recv_sem=recv_sems.at[k], device_id=(x, y, c), device_id_type=MESH).wait_recv()
        for cp, _ in copies:
            cp.wait_send()
        tot = all_ref[0]
        for d in range(1, 8):
            tot = tot + all_ref[d]
        sum_ref[...] = tot

    vm = pl.BlockSpec(memory_space=pltpu.VMEM)
    return pl.pallas_call(
        body, name="gather_all", in_specs=[vm], out_specs=[vm, vm],
        out_shape=[jax.ShapeDtypeStruct((8, r, LANES), v.dtype), jax.ShapeDtypeStruct((r, LANES), v.dtype)],
        scratch_shapes=[pltpu.SemaphoreType.DMA((7,)), pltpu.SemaphoreType.DMA((7,))],
        compiler_params=pltpu.CompilerParams(vmem_limit_bytes=VMEM_LIMIT),
    )(v)


def _remote(src, dst, send_sem, recv_sem, to):
    return pltpu.make_async_remote_copy(src_ref=src, dst_ref=dst, send_sem=send_sem, recv_sem=recv_sem,
                                        device_id=to, device_id_type=MESH)


def _dma_sems(*shape):
    return pltpu.SemaphoreType.DMA(shape)


def _gather_first(shards):
    n = len(shards)

    def copies(ins, outs, sems):
        x, y, c, chips = _place()
        me = 2 * x + y
        local = [pltpu.make_async_copy(ins[a], outs[a].at[me], sems[2].at[a]) for a in range(n)]
        sends = [_remote(ins[a].at[c], outs[a].at[me, c], sems[0].at[a, j], sems[1].at[a, j], (*chip, c))
                 for a in range(n) for j, chip in enumerate(chips)]
        lands = [_remote(ins[a].at[c], outs[a].at[2 * chip[0] + chip[1], c], sems[0].at[a, j], sems[1].at[a, j], (x, y, c))
                 for a in range(n) for j, chip in enumerate(chips)]
        return local, sends, lands

    def start(ins, outs, sems):
        local, sends, _ = copies(ins, outs, sems)
        for cp in local + sends:
            cp.start()

    def finish(ins, outs, sems):
        local, sends, lands = copies(ins, outs, sems)
        for cp in lands:
            cp.wait_recv()
        for cp in sends:
            cp.wait_send()
        for cp in local:
            cp.wait()

    return _Rider(shards, [jax.ShapeDtypeStruct((4,) + s.shape, s.dtype) for s in shards],
                  [_dma_sems(n, 3), _dma_sems(n, 3), _dma_sems(n)], start, finish)


def _gather_second(partials):
    n = len(partials)

    def copies(outs, sems):
        x, y, c, chips = _place()
        sends, lands = [], []
        for a in range(n):
            for j, chip in enumerate(chips):
                mine = outs[a].at[2 * chip[0] + chip[1], c]
                theirs = outs[a].at[2 * chip[0] + chip[1], 1 - c]
                sends.append(_remote(mine, mine, sems[0].at[a, j], sems[1].at[a, j], (x, y, 1 - c)))
                lands.append(_remote(theirs, theirs, sems[0].at[a, j], sems[1].at[a, j], (x, y, c)))
        return sends, lands

    def start(ins, outs, sems):
        for cp in copies(outs, sems)[0]:
            cp.start()

    def finish(ins, outs, sems):
        sends, lands = copies(outs, sems)
        for cp in lands:
            cp.wait_recv()
        for cp in sends:
            cp.wait_send()

    return _Rider(partials, [jax.ShapeDtypeStruct(p.shape, p.dtype) for p in partials],
                  [_dma_sems(n, 3), _dma_sems(n, 3)], start, finish, aliases={a: a for a in range(n)})


def _swap_halves(grads):
    n = len(grads)

    def copies(ins, outs, sems):
        x, y, c, _ = _place()
        return [_remote(ins[a].at[j, 1 - c], outs[a].at[j], sems[0].at[a, j], sems[1].at[a, j], (x, y, 1 - c))
                for a in range(n) for j in range(4)]

    def start(ins, outs, sems):
        for cp in copies(ins, outs, sems):
            cp.start()

    def finish(ins, outs, sems):
        for cp in copies(ins, outs, sems):
            cp.wait()

    return _Rider(grads, [jax.ShapeDtypeStruct((4,) + g.shape[2:], g.dtype) for g in grads],
                  [_dma_sems(n, 4), _dma_sems(n, 4)], start, finish)


def _add_sibling(g, got, c_idx, name):
    _, _, h, c = g.shape

    def body(s_ref, a_ref, b_ref, o_ref):
        o_ref[...] = (a_ref[...] + b_ref[...]).astype(BF16)

    return pl.pallas_call(
        body, name=name,
        grid_spec=pltpu.PrefetchScalarGridSpec(
            num_scalar_prefetch=1, grid=(4,),
            in_specs=[pl.BlockSpec((None, None, h, c), lambda j, s: (j, s[0], 0, 0)),
                      pl.BlockSpec((None, h, c), lambda j, s: (j, 0, 0))],
            out_specs=pl.BlockSpec((None, h, c), lambda j, s: (j, 0, 0))),
        out_shape=jax.ShapeDtypeStruct((4, h, c), BF16),
        compiler_params=_params(("parallel",)),
    )(c_idx, g, got)


def _exchange_chips(parts):
    n = len(parts)

    def copies(ins, outs, sems):
        x, y, c, chips = _place()
        return [_remote(ins[a].at[2 * chip[0] + chip[1]], outs[a].at[j], sems[0].at[a, j], sems[1].at[a, j], (*chip, c))
                for a in range(n) for j, chip in enumerate(chips)]

    def start(ins, outs, sems):
        for cp in copies(ins, outs, sems):
            cp.start()

    def finish(ins, outs, sems):
        for cp in copies(ins, outs, sems):
            cp.wait()

    return _Rider(parts, [jax.ShapeDtypeStruct((3,) + p.shape[1:], p.dtype) for p in parts],
                  [_dma_sems(n, 3), _dma_sems(n, 3)], start, finish)


def _add_chips(part, got, chip_idx, name):
    _, h, c = part.shape

    def body(s_ref, a_ref, b_ref, o_ref):
        o_ref[...] = ((a_ref[...].astype(F32) + b_ref[0].astype(F32)) + b_ref[1].astype(F32)) + b_ref[2].astype(F32)

    return pl.pallas_call(
        body, name=name,
        grid_spec=pltpu.PrefetchScalarGridSpec(
            num_scalar_prefetch=1, grid=(1,),
            in_specs=[pl.BlockSpec((None, h, c), lambda i, s: (s[0], 0, 0)),
                      pl.BlockSpec((3, h, c), lambda i, s: (0, 0, 0))],
            out_specs=pl.BlockSpec((h, c), lambda i, s: (0, 0))),
        out_shape=jax.ShapeDtypeStruct((h, c), F32),
        compiler_params=_params(("arbitrary",)),
    )(chip_idx, part, got)


def _join_halves(halves):
    n = len(halves)

    def copies(ins, outs, sems):
        x, y, c, _ = _place()
        local = [pltpu.make_async_copy(ins[a], outs[a].at[c], sems[2].at[a]) for a in range(n)]
        sends = [_remote(ins[a], outs[a].at[c], sems[0].at[a], sems[1].at[a], (x, y, 1 - c)) for a in range(n)]
        lands = [_remote(ins[a], outs[a].at[1 - c], sems[0].at[a], sems[1].at[a], (x, y, c)) for a in range(n)]
        return local, sends, lands

    def start(ins, outs, sems):
        local, sends, _ = copies(ins, outs, sems)
        for cp in local + sends:
            cp.start()

    def finish(ins, outs, sems):
        local, sends, lands = copies(ins, outs, sems)
        for cp in lands:
            cp.wait_recv()
        for cp in sends:
            cp.wait_send()
        for cp in local:
            cp.wait()

    return _Rider(halves, [jax.ShapeDtypeStruct((2,) + h.shape, h.dtype) for h in halves],
                  [_dma_sems(n), _dma_sems(n), _dma_sems(n)], start, finish)


def _pack(parts):
    flat = []
    for p in parts:
        p = p.reshape(-1).astype(F32)
        flat.append(jnp.pad(p, (0, (-p.shape[0]) % LANES)))
    v = jnp.concatenate(flat)
    v = jnp.pad(v, (0, (-v.shape[0]) % (8 * LANES)))
    return v.reshape(-1, LANES)


def _unpack(v, shapes):
    flat = v.reshape(-1)
    out, off = [], 0
    for s in shapes:
        n = 1
        for d in s:
            n *= d
        out.append(flat[off:off + n].reshape(s))
        off += n + (-n) % LANES
    return out


def kernel(x, mix_norm_gain, w_in, b_in, q_norm_gain, k_norm_gain, attn_sinks, conv_dw_w, conv_dw_b, conv_norm_gain, conv_norm_bias, w_out, b_out, ffn_norm_gain, w_up, ffn_dw_w, ffn_dw_b, w_down, loss_target, m_mix_norm_gain, m_w_in, m_b_in, m_q_norm_gain, m_k_norm_gain, m_attn_sinks, m_conv_dw_w, m_conv_dw_b, m_conv_norm_gain, m_conv_norm_bias, m_w_out, m_b_out, m_ffn_norm_gain, m_w_up, m_ffn_dw_w, m_ffn_dw_b, m_w_down, v_mix_norm_gain, v_w_in, v_b_in, v_q_norm_gain, v_k_norm_gain, v_attn_sinks, v_conv_dw_w, v_conv_dw_b, v_conv_norm_gain, v_conv_norm_bias, v_w_out, v_b_out, v_ffn_norm_gain, v_w_up, v_ffn_dw_w, v_ffn_dw_b, v_w_down):
    t = x.shape[1]
    xi, yi, ci = lax.axis_index("x"), lax.axis_index("y"), lax.axis_index("c")
    chip = 2 * xi + yi
    c_idx = jnp.reshape(ci, (1,)).astype(jnp.int32)
    chip_idx = jnp.reshape(chip, (1,)).astype(jnp.int32)
    x2 = x.reshape(t, D_MODEL)
    tgt = loss_target.reshape(t, D_MODEL)

    big = [w_in, w_out, w_up, w_down]
    halves = [w.astype(BF16).reshape(2, w.shape[0] // 2, w.shape[1]) for w in big]
    h_wi, h_wo, h_wu, h_wd = halves
    (p_wi,) = _run_riders([_gather_first([h_wi])], "gather_w_in_first")
    (g_wi,) = _run_riders([_gather_second([p_wi])], "gather_w_in_second")
    wi = jnp.concatenate([g_wi[j].reshape(D_MODEL, IN_COLS // 4) for j in range(4)], axis=1)
    small_w, _ = _gather_all(_pack([conv_dw_w, ffn_dw_w]))
    per_chip = [_unpack(small_w[4 * (j // 2) + 2 * (j % 2)], [conv_dw_w.shape, ffn_dw_w.shape]) for j in range(4)]
    cw = jnp.concatenate([p[0] for p in per_chip], axis=1)
    fw = jnp.concatenate([p[1] for p in per_chip], axis=1)

    row = lambda a: a.reshape(1, -1)
    gq2 = row(jnp.concatenate([q_norm_gain, q_norm_gain]))
    gk2 = row(jnp.concatenate([k_norm_gain, k_norm_gain]))

    (h1, pq, pc, qkvn), (p_wo,) = _fwd_in(x2, row(mix_norm_gain), wi, row(b_in), gq2, gk2, riders=[_gather_first([h_wo])])
    (attn, lse), (g_wo, p_wu) = _attn_fwd(qkvn, attn_sinks, riders=[_gather_second([p_wo]), _gather_first([h_wu])])
    (y_conv, c_act), (g_wu, p_wd) = _conv_fwd(pc, cw, row(conv_dw_b), row(conv_norm_gain), row(conv_norm_bias),
                                              riders=[_gather_second([p_wu]), _gather_first([h_wd])])
    wo = g_wo.reshape(D_MODEL, D_MODEL)
    (x1, h2), (g_wd,) = _out_proj(x2, attn, c_act, wo, row(b_out), row(ffn_norm_gain), riders=[_gather_second([p_wd])])
    wu = jnp.concatenate([g_wu[j].reshape(D_MODEL, 2 * D_FF // 4) for j in range(4)], axis=1)
    wd = g_wd.reshape(D_FF, D_MODEL)
    hg, hu, up_g, up_u, act = _ffn_up(h2, wu, fw, row(ffn_dw_b))
    dy, loss_cols = _ffn_down(act, wd, x1, tgt)

    split = lambda g: g.reshape(4, 2, g.shape[1] // 2, g.shape[2])
    dg, du, gfb_g, gfb_u = _ffn_bwd_act(dy, wd, up_g, up_u)
    dhg, dhu, gfw_g, gfw_u = _ffn_bwd_conv(dg, du, hg, hu, fw)
    gw_down = _grad_weight(act, dy, 1, FFN_CB, "grad_w_down")
    gw_up = jnp.concatenate([_grad_weight(h2, dhg, 2, D_MODEL, "grad_w_up_gate"),
                             _grad_weight(h2, dhu, 2, D_MODEL, "grad_w_up_lin")], axis=0)
    early = [split(gw_up), split(gw_down.reshape(4, D_FF // 4, D_MODEL))]
    early_names = ["w_up", "w_down"]
    (dx1, g_ffn_gain), got = _ffn_bwd_in(dhg, dhu, wu, x1, dy, row(ffn_norm_gain), riders=[_swap_halves(early)])
    early_part = [_add_sibling(g, r, c_idx, "add_sibling_" + nm_) for g, r, nm_ in zip(early, got, early_names)]
    dmix, g_b_out = _out_proj_bwd(dx1, wo)
    gw_out = jnp.concatenate([_grad_weight(attn, dx1, 1, Q_COLS, "grad_w_out_attn")[0],
                              _grad_weight(c_act, dx1, 1, CONV_W, "grad_w_out_conv")[0]], axis=0)
    (dpc, g_cw, g_cb, g_lng, g_lnb, gbin_c), got2 = _conv_bwd(dmix, y_conv, pc, cw, row(conv_norm_gain), row(conv_norm_bias),
                                                              riders=[_exchange_chips(early_part)])
    early_red = [_add_chips(p, r, chip_idx, "add_chips_" + nm_) for p, r, nm_ in zip(early_part, got2, early_names)]
    (dqn, dcur, dprev, g_sink), early_g = _attn_bwd(qkvn, dmix, lse, attn_sinks, riders=[_join_halves(early_red)])
    dpq, gbin_q, g_qk = _qk_norm_bwd(dqn, dcur, dprev, pq, gq2, gk2)
    grad_x, g_mix_gain = _in_proj_bwd(dpq, dpc, wi, x2, dx1, row(mix_norm_gain))
    gw_in = jnp.concatenate([_grad_weight(h1, dpq, 1, D_MODEL, "grad_w_in_qkv")[0],
                             _grad_weight(h1, dpc, 1, D_MODEL, "grad_w_in_conv")[0]], axis=1)

    g_qk = g_qk.reshape(5, 2, HEAD_DIM)
    small = [g_mix_gain, jnp.concatenate([gbin_q, gbin_c], axis=1), g_qk[:4].sum(axis=(0, 1)), g_qk[4].sum(axis=0),
             g_sink[0, :N_HEADS], g_cb, g_lng, g_lnb, g_b_out, g_ffn_gain, jnp.concatenate([gfb_g, gfb_u], axis=1),
             loss_cols, g_cw[:CONV_K], jnp.concatenate([gfw_g[:3], gfw_u[:3]], axis=1)]
    _, tot = _gather_all(_pack(small))
    rep_names = ["mix_norm_gain", "b_in", "q_norm_gain", "k_norm_gain", "attn_sinks", "conv_dw_b", "conv_norm_gain",
                 "conv_norm_bias", "b_out", "ffn_norm_gain", "ffn_dw_b"]
    rep_w = [mix_norm_gain, b_in, q_norm_gain, k_norm_gain, attn_sinks, conv_dw_b, conv_norm_gain, conv_norm_bias,
             b_out, ffn_norm_gain, ffn_dw_b]
    rep_m = [m_mix_norm_gain, m_b_in, m_q_norm_gain, m_k_norm_gain, m_attn_sinks, m_conv_dw_b, m_conv_norm_gain,
             m_conv_norm_bias, m_b_out, m_ffn_norm_gain, m_ffn_dw_b]
    rep_v = [v_mix_norm_gain, v_b_in, v_q_norm_gain, v_k_norm_gain, v_attn_sinks, v_conv_dw_b, v_conv_norm_gain,
             v_conv_norm_bias, v_b_out, v_ffn_norm_gain, v_ffn_dw_b]
    shapes = [w.shape for w in rep_w] + [(D_MODEL,), (CONV_K, CONV_W), (3, 2 * D_FF)]
    tot_parts = _unpack(tot, shapes)
    loss = (0.5 / D_MODEL) * jnp.sum(tot_parts[len(rep_w)])
    g_cw_full, g_fw_full = tot_parts[len(rep_w) + 1], tot_parts[len(rep_w) + 2]
    n_rep_rows = _pack(rep_w).shape[0]
    rep_d, rep_nm, rep_nv = _adamw(_pack(rep_w), tot[:n_rep_rows], _pack(rep_m), _pack(rep_v), "adamw_small")
    rep_shapes = [w.shape for w in rep_w]
    res = {}
    for nm_, g_, d_, m_, v_ in zip(rep_names, tot_parts, _unpack(rep_d, rep_shapes), _unpack(rep_nm, rep_shapes),
                                   _unpack(rep_nv, rep_shapes)):
        res[nm_] = (g_, d_, m_, v_)

    g_cw_mine = lax.dynamic_slice_in_dim(g_cw_full, chip * (CONV_W // 4), CONV_W // 4, axis=1)
    g_fw_mine = lax.dynamic_slice_in_dim(g_fw_full, chip * (2 * D_FF // 4), 2 * D_FF // 4, axis=1)
    res["conv_dw_w"] = (g_cw_mine, *_adamw(conv_dw_w, g_cw_mine, m_conv_dw_w, v_conv_dw_w, "adamw_conv_dw_w"))
    res["ffn_dw_w"] = (g_fw_mine, *_adamw(ffn_dw_w, g_fw_mine, m_ffn_dw_w, v_ffn_dw_w, "adamw_ffn_dw_w"))

    gw_in4 = gw_in.reshape(D_MODEL, 4, IN_COLS // 4).transpose(1, 0, 2)
    late = [split(gw_in4), split(gw_out.reshape(4, D_MODEL // 4, D_MODEL))]
    late_names = ["w_in", "w_out"]
    got = _run_riders([_swap_halves(late)], "swap_halves_late")
    late_part = [_add_sibling(g, r, c_idx, "add_sibling_" + nm_) for g, r, nm_ in zip(late, got, late_names)]
    got2 = _run_riders([_exchange_chips(late_part)], "exchange_chips_late")
    late_red = [_add_chips(p, r, chip_idx, "add_chips_" + nm_) for p, r, nm_ in zip(late_part, got2, late_names)]
    late_g = _run_riders([_join_halves(late_red)], "join_halves_late")
    names = ["w_in", "w_out", "w_up", "w_down"]
    shard_g = list(late_g) + list(early_g)
    for nm_, w_, g_, m_, v_ in zip(names, big, shard_g, [m_w_in, m_w_out, m_w_up, m_w_down], [v_w_in, v_w_out, v_w_up, v_w_down]):
        g_ = g_.reshape(w_.shape)
        res[nm_] = (g_, *_adamw(w_, g_, m_, v_, "adamw_" + nm_))

    order = ["mix_norm_gain", "w_in", "b_in", "q_norm_gain", "k_norm_gain", "attn_sinks", "conv_dw_w", "conv_dw_b",
             "conv_norm_gain", "conv_norm_bias", "w_out", "b_out", "ffn_norm_gain", "w_up", "ffn_dw_w", "ffn_dw_b", "w_down"]
    return (loss, grad_x.reshape(x.shape), *[res[n][0] for n in order], *[res[n][1] for n in order],
            *[res[n][2] for n in order], *[res[n][3] for n in order])
```

```python
import functools

import jax
import jax.numpy as jnp
from jax import lax
from jax.experimental import pallas as pl
from jax.experimental.pallas import tpu as pltpu

F32 = jnp.float32
BF16 = jnp.bfloat16
MESH = pl.DeviceIdType.MESH

D_MODEL = 1024
HEAD_DIM = 64
N_HEADS = 8
Q_COLS = 512
QKV_COLS = 768
CONV_W = 512
CONV_K = 31
IN_COLS = 1792
D_FF = 2816
FFN_CB = 1408
BLOCK = 128
LANES = 128
EPS = 1e-6
NEG_INF = -1e30
SLOPES = tuple(float(2.0 ** (-(h + 1.0))) for h in range(N_HEADS))
HALO = 32
FHALO = 16
ROW_CHUNK = 64
GRAD_TILE = 2048
VMEM_LIMIT = 56 * 1024 * 1024

ADAM_LR = 0.001
ADAM_B1 = 0.9
ADAM_B2 = 0.999
ADAM_EPS = 1e-08
ADAM_WD = 0.01
ADAM_STEP = 10


def _params(sem=None):
    kw = dict(vmem_limit_bytes=VMEM_LIMIT)
    if sem is not None:
        kw["dimension_semantics"] = sem
    return pltpu.CompilerParams(**kw)


def _token_tile(t):
    return 512 if t % 512 == 0 and t >= 2048 else 128


def _sig(v):
    return 1.0 / (1.0 + jnp.exp(-v))


def _lo_mask():
    return lax.broadcasted_iota(jnp.int32, (1, LANES), 1) < HEAD_DIM


def _half_mean(v, lo):
    s_lo = jnp.sum(jnp.where(lo, v, 0.0), axis=-1, keepdims=True)
    s_hi = jnp.sum(jnp.where(lo, 0.0, v), axis=-1, keepdims=True)
    return jnp.where(lo, s_lo, s_hi) * (1.0 / HEAD_DIM)


def _dot(a, b):
    return jnp.dot(a, b, preferred_element_type=F32)


def _dot_nt(a, b):
    return lax.dot_general(a, b, (((1,), (1,)), ((), ())), preferred_element_type=F32)


def _dot_tn(a, b):
    return lax.dot_general(a, b, (((0,), (0,)), ((), ())), preferred_element_type=F32)


def _full(shape):
    nd = len(shape)
    return pl.BlockSpec(shape, lambda *_: (0,) * nd)


def _rows_to_tile(rows, n_rows):
    c = rows[0].shape[-1]
    rid = lax.broadcasted_iota(jnp.int32, (n_rows, c), 0)
    out = jnp.zeros((n_rows, c), F32)
    for k, r in enumerate(rows):
        out = jnp.where(rid == k, r, out)
    return out


ANY = pl.BlockSpec(memory_space=pl.ANY)


class _Rider:
    def __init__(self, ins, outs, sems, start, finish, aliases=None):
        self.ins, self.outs, self.sems = list(ins), list(outs), list(sems)
        self.start, self.finish, self.aliases = start, finish, dict(aliases or {})


def _join_riders(riders):
    ins, outs, sems, aliases, spans = [], [], [], {}, []
    for r in riders:
        spans.append((len(ins), len(outs), len(sems), r))
        for a, b in r.aliases.items():
            aliases[len(ins) + a] = len(outs) + b
        ins += r.ins
        outs += r.outs
        sems += r.sems

    def each(which):
        def run(i_refs, o_refs, s_refs):
            for i0, o0, s0, r in spans:
                getattr(r, which)(i_refs[i0:i0 + len(r.ins)], o_refs[o0:o0 + len(r.outs)], s_refs[s0:s0 + len(r.sems)])
        return run

    return _Rider(ins, outs, sems, each("start"), each("finish"), aliases)


def _call(body, args, *, name, grid, in_specs, out_specs, out_shape, scratch=(), riders=()):
    in_specs, out_specs, out_shape, scratch = list(in_specs), list(out_specs), list(out_shape), list(scratch)
    sem = ("arbitrary",) * len(grid)
    if not riders:
        outs = pl.pallas_call(body, name=name, grid=grid, in_specs=in_specs, out_specs=out_specs, out_shape=out_shape,
                              scratch_shapes=scratch, compiler_params=_params(sem))(*args)
        return list(outs), []
    r = _join_riders(riders)
    n_in, n_out, n_scr = len(in_specs), len(out_specs), len(scratch)
    nri, nro = len(r.ins), len(r.outs)

    def full(*refs):
        ins, rin = refs[:n_in], refs[n_in:n_in + nri]
        o0 = n_in + nri
        outs, rout = refs[o0:o0 + n_out], refs[o0 + n_out:o0 + n_out + nro]
        s0 = o0 + n_out + nro
        scr, rsem = refs[s0:s0 + n_scr], refs[s0 + n_scr:]
        first = functools.reduce(jnp.logical_and, [pl.program_id(k) == 0 for k in range(len(grid))])
        last = functools.reduce(jnp.logical_and, [pl.program_id(k) == grid[k] - 1 for k in range(len(grid))])

        @pl.when(first)
        def _():
            r.start(rin, rout, rsem)

        body(*ins, *outs, *scr)

        @pl.when(last)
        def _():
            r.finish(rin, rout, rsem)

    outs = pl.pallas_call(
        full, name=name, grid=grid, in_specs=in_specs + [ANY] * nri, out_specs=out_specs + [ANY] * nro,
        out_shape=out_shape + r.outs, scratch_shapes=scratch + r.sems,
        input_output_aliases={n_in + a: n_out + b for a, b in r.aliases.items()},
        compiler_params=_params(sem))(*args, *r.ins)
    return list(outs[:n_out]), list(outs[n_out:])


def _run_riders(riders, name):
    r = _join_riders(riders)
    nri, nro = len(r.ins), len(r.outs)

    def body(*refs):
        rin, rout, rsem = refs[:nri], refs[nri:nri + nro], refs[nri + nro:]
        r.start(rin, rout, rsem)
        r.finish(rin, rout, rsem)

    outs = pl.pallas_call(body, name=name, in_specs=[ANY] * nri, out_specs=[ANY] * nro, out_shape=r.outs,
                          scratch_shapes=r.sems, input_output_aliases=r.aliases)(*r.ins)
    return list(outs)


def _fwd_in(x, g_mix, w_in, b_in, gq2, gk2, riders=()):
    t = x.shape[0]
    tt = _token_tile(t)

    def body(x_ref, g_ref, w_ref, b_ref, gq_ref, gk_ref, h1_ref, pq_ref, pc_ref, qkvn_ref):
        xv = x_ref[...]
        r = lax.rsqrt(jnp.mean(xv * xv, axis=-1, keepdims=True) + EPS)
        h = (xv * r * g_ref[...]).astype(BF16)
        h1_ref[...] = h
        proj = _dot(h, w_ref[...]) + b_ref[...]
        pq_ref[...] = proj[:, :QKV_COLS]
        pc_ref[...] = proj[:, QKV_COLS:]
        lo = _lo_mask()
        for p in range(5):
            seg = proj[:, p * LANES:(p + 1) * LANES]
            rr = lax.rsqrt(_half_mean(seg * seg, lo) + EPS)
            gain = gq_ref[...] if p < 4 else gk_ref[...]
            qkvn_ref[:, p * LANES:(p + 1) * LANES] = (seg * rr * gain).astype(BF16)
        qkvn_ref[:, 640:768] = proj[:, 640:768].astype(BF16)

    return _call(
        body, (x, g_mix, w_in, b_in, gq2, gk2), name="fwd_in", grid=(t // tt,), riders=riders,
        in_specs=[pl.BlockSpec((tt, D_MODEL), lambda i: (i, 0)), _full((1, D_MODEL)), _full((D_MODEL, IN_COLS)),
                  _full((1, IN_COLS)), _full((1, LANES)), _full((1, LANES))],
        out_specs=[pl.BlockSpec((tt, D_MODEL), lambda i: (i, 0)), pl.BlockSpec((tt, QKV_COLS), lambda i: (i, 0)),
                   pl.BlockSpec((tt, 2 * CONV_W), lambda i: (i, 0)), pl.BlockSpec((tt, QKV_COLS), lambda i: (i, 0))],
        out_shape=[jax.ShapeDtypeStruct((t, D_MODEL), BF16), jax.ShapeDtypeStruct((t, QKV_COLS), F32),
                   jax.ShapeDtypeStruct((t, 2 * CONV_W), F32), jax.ShapeDtypeStruct((t, QKV_COLS), BF16)])


GROUP = 4
GROUP_ROWS = GROUP * BLOCK


def _attn_bias():
    qi = jnp.arange(GROUP_ROWS)[:, None] % BLOCK
    kj = jnp.arange(2 * BLOCK)[None, :]
    rel = qi + BLOCK - kj
    band = (rel >= 0) & (rel < BLOCK)
    slope_rows = jnp.repeat(jnp.asarray(SLOPES, F32).reshape(N_HEADS // GROUP, GROUP), BLOCK, axis=1)
    penalty = -(slope_rows[:, :, None] * rel.astype(F32)[None])
    later = jnp.where(band[None], penalty, NEG_INF)
    first = jnp.where((band & (kj >= BLOCK))[None], penalty, NEG_INF)
    return jnp.stack([first, later])


def _attn_bias_spec():
    return pl.BlockSpec((None, N_HEADS // GROUP, GROUP_ROWS, 2 * BLOCK), lambda i: (jnp.minimum(i, 1), 0, 0, 0))


def _attn_keys(kvp_ref, kvc_ref):
    kv = jnp.concatenate([kvp_ref[...], kvc_ref[...]], axis=0).astype(F32)
    lo = _lo_mask()

    def both_halves(pair):
        rolled = pltpu.roll(pair, HEAD_DIM, 1)
        return [jnp.where(lo, pair, rolled).astype(BF16), jnp.where(lo, rolled, pair).astype(BF16)]

    return both_halves(kv[:, :LANES]), both_halves(kv[:, LANES:])


def _per_head_column(values):
    seg = lax.broadcasted_iota(jnp.int32, (GROUP_ROWS, 1), 0) // BLOCK
    col = jnp.zeros((GROUP_ROWS, 1), F32) + values[GROUP - 1]
    for a in range(GROUP - 2, -1, -1):
        col = jnp.where(seg == a, values[a], col)
    return col


def _stack_heads(ref, kk, lo, dtype):
    parts = []
    for a in range(GROUP):
        h = GROUP * kk + a
        pair = ref[:, (h // 2) * LANES:(h // 2 + 1) * LANES]
        hm = lo if h % 2 == 0 else jnp.logical_not(lo)
        parts.append(jnp.where(hm, pair, jnp.zeros_like(pair)).astype(dtype))
    return jnp.concatenate(parts, axis=0)


def _group_scores(q_ref, kk, lo, k_dup, bias_ref):
    qs = _stack_heads(q_ref, kk, lo, BF16)
    return _dot_nt(qs, k_dup[kk]) * 0.125 + bias_ref[kk], qs


def _attn_fwd(qkvn, sinks, bias, riders=()):
    t = qkvn.shape[0]
    nb = t // BLOCK

    def body(sink_ref, q_ref, kvc_ref, kvp_ref, bias_ref, o_ref, lse_ref):
        k_dup, v_dup = _attn_keys(kvp_ref, kvc_ref)
        lo = _lo_mask()
        lane = lax.broadcasted_iota(jnp.int32, (BLOCK, LANES), 1)
        lse_t = jnp.zeros((BLOCK, LANES), F32)
        outs = [jnp.zeros((BLOCK, LANES), F32) for _ in range(4)]
        for kk in range(N_HEADS // GROUP):
            s, _ = _group_scores(q_ref, kk, lo, k_dup, bias_ref)
            sink = _per_head_column([sink_ref[GROUP * kk + a] for a in range(GROUP)])
            m = jnp.maximum(jnp.max(s, axis=-1, keepdims=True), sink)
            pe = jnp.exp(s - m)
            l = jnp.sum(pe, axis=-1, keepdims=True) + jnp.exp(sink - m)
            lse = m + jnp.log(l)
            o = _dot((pe / l).astype(BF16), v_dup[kk])
            for a in range(GROUP):
                h = GROUP * kk + a
                rows = slice(a * BLOCK, (a + 1) * BLOCK)
                hm = lo if h % 2 == 0 else jnp.logical_not(lo)
                outs[h // 2] = outs[h // 2] + jnp.where(hm, o[rows], 0.0)
                lse_t = jnp.where(lane == h, lse[rows], lse_t)
        for p in range(4):
            o_ref[:, p * LANES:(p + 1) * LANES] = outs[p].astype(BF16)
        lse_ref[...] = lse_t

    return _call(
        body, (sinks, qkvn, qkvn, qkvn, bias), name="attn_fwd", grid=(nb,), riders=riders,
        in_specs=[pl.BlockSpec(memory_space=pltpu.SMEM),
                  pl.BlockSpec((BLOCK, Q_COLS), lambda i: (i, 0)),
                  pl.BlockSpec((BLOCK, 2 * LANES), lambda i: (i, 2)),
                  pl.BlockSpec((BLOCK, 2 * LANES), lambda i: (jnp.maximum(i - 1, 0), 2)),
                  _attn_bias_spec()],
        out_specs=[pl.BlockSpec((BLOCK, Q_COLS), lambda i: (i, 0)), pl.BlockSpec((BLOCK, LANES), lambda i: (i, 0))],
        out_shape=[jax.ShapeDtypeStruct((t, Q_COLS), BF16), jax.ShapeDtypeStruct((t, LANES), F32)])


def _glu(pc):
    return pc[:, :CONV_W] * _sig(pc[:, CONV_W:])


def _glu_chunks(pc_ref, scr, tt):
    for r0, cs in _chunks(tt, CONV_W):
        rows = slice(r0, r0 + ROW_CHUNK)
        gate = slice(cs.start + CONV_W, cs.stop + CONV_W)
        scr[HALO + r0:HALO + r0 + ROW_CHUNK, cs] = pc_ref[rows, cs] * _sig(pc_ref[rows, gate])


def _group_norm_stats(seg, lo):
    mu = _half_mean(seg, lo)
    d = seg - mu
    rstd = lax.rsqrt(_half_mean(d * d, lo) + EPS)
    return d * rstd, rstd


def _shifted_copies(src, dst, tt):
    n = tt + HALO - 8
    for s in range(1, 8):
        dst[s - 1, 0:n, :] = src[s:s + n, :]


def _tap_rows(src, shifted, off, r0, cs):
    q, s = divmod(off, 8)
    if s == 0:
        return src[r0 + off:r0 + off + ROW_CHUNK, cs]
    return shifted[s - 1, r0 + 8 * q:r0 + 8 * q + ROW_CHUNK, cs]


def _shift_scratch(tt):
    return pltpu.VMEM((7, tt + HALO - 8, CONV_W), F32)


def _conv_fwd(pc, cw, cb, ln_g, ln_b, riders=()):
    t = pc.shape[0]
    tt = _token_tile(t)

    def body(cur_ref, prev_ref, w_ref, b_ref, g_ref, bb_ref, y_ref, c_ref, scr, shf):
        i = pl.program_id(0)
        scr[0:HALO, :] = _glu(prev_ref[...]) * (i > 0).astype(F32)
        _glu_chunks(cur_ref, scr, tt)
        _shifted_copies(scr, shf, tt)
        lo = _lo_mask()
        for r0, cs in _chunks(tt, CONV_W):
            acc = jnp.zeros((ROW_CHUNK, LANES), F32) + b_ref[:, cs]
            for k in range(CONV_K):
                acc = acc + w_ref[k:k + 1, cs] * _tap_rows(scr, shf, 2 + k, r0, cs)
            y_ref[r0:r0 + ROW_CHUNK, cs] = acc
            yh, _ = _group_norm_stats(acc, lo)
            z = yh * g_ref[:, cs] + bb_ref[:, cs]
            c_ref[r0:r0 + ROW_CHUNK, cs] = (z * _sig(z)).astype(BF16)

    hb = tt // HALO
    return _call(
        body, (pc, pc, cw, cb, ln_g, ln_b), name="conv_fwd", grid=(t // tt,), riders=riders,
        in_specs=[pl.BlockSpec((tt, 2 * CONV_W), lambda i: (i, 0)),
                  pl.BlockSpec((HALO, 2 * CONV_W), lambda i: (jnp.maximum(i * hb - 1, 0), 0)),
                  _full((CONV_K, CONV_W)), _full((1, CONV_W)), _full((1, CONV_W)), _full((1, CONV_W))],
        out_specs=[pl.BlockSpec((tt, CONV_W), lambda i: (i, 0)), pl.BlockSpec((tt, CONV_W), lambda i: (i, 0))],
        out_shape=[jax.ShapeDtypeStruct((t, CONV_W), F32), jax.ShapeDtypeStruct((t, CONV_W), BF16)],
        scratch=[pltpu.VMEM((tt + HALO, CONV_W), F32), _shift_scratch(tt)])


def _out_proj(x, attn, c, w_out, b_out, g_ffn, riders=()):
    t = x.shape[0]
    tt = _token_tile(t)

    def body(x_ref, a_ref, c_ref, w_ref, b_ref, g_ref, x1_ref, h2_ref):
        x1 = x_ref[...] + _dot(a_ref[...], w_ref[0:Q_COLS, :]) + _dot(c_ref[...], w_ref[Q_COLS:, :]) + b_ref[...]
        x1_ref[...] = x1
        r = lax.rsqrt(jnp.mean(x1 * x1, axis=-1, keepdims=True) + EPS)
        h2_ref[...] = (x1 * r * g_ref[...]).astype(BF16)

    row = lambda w: pl.BlockSpec((tt, w), lambda i: (i, 0))
    return _call(
        body, (x, attn, c, w_out, b_out, g_ffn), name="out_proj", grid=(t // tt,), riders=riders,
        in_specs=[row(D_MODEL), row(Q_COLS), row(CONV_W), _full((D_MODEL, D_MODEL)), _full((1, D_MODEL)), _full((1, D_MODEL))],
        out_specs=[row(D_MODEL), row(D_MODEL)],
        out_shape=[jax.ShapeDtypeStruct((t, D_MODEL), F32), jax.ShapeDtypeStruct((t, D_MODEL), BF16)])


def _chunks(rows, cols):
    return [(r0, slice(c0, c0 + LANES)) for c0 in range(0, cols, LANES) for r0 in range(0, rows, ROW_CHUNK)]


def _ffn_up(h2, w_up, dw, db):
    t = h2.shape[0]
    tt = _token_tile(t)
    nj = D_FF // FFN_CB

    def body(hc_ref, hp_ref, wg_ref, wu_ref, dwg_ref, dwu_ref, dbg_ref, dbu_ref,
             hg_ref, hu_ref, upg_ref, upu_ref, act_ref, sg, su):
        i = pl.program_id(1)
        hc = hc_ref[...]
        hp = hp_ref[...] * (i > 0).astype(BF16)
        ups = []
        for w_ref, dw_ref, db_ref, h_ref, up_ref, scr in ((wg_ref, dwg_ref, dbg_ref, hg_ref, upg_ref, sg),
                                                          (wu_ref, dwu_ref, dbu_ref, hu_ref, upu_ref, su)):
            cur = _dot(hc, w_ref[...])
            h_ref[...] = cur.astype(BF16)
            scr[0:FHALO, :] = _dot(hp, w_ref[...])
            scr[FHALO:FHALO + tt, :] = cur
            up = (dw_ref[0:1, :] * scr[FHALO - 2:FHALO - 2 + tt, :] + dw_ref[1:2, :] * scr[FHALO - 1:FHALO - 1 + tt, :]
                  + dw_ref[2:3, :] * cur + db_ref[...])
            up_ref[...] = up
            ups.append(up)
        g, u = ups
        act_ref[...] = (g * _sig(g) * u).astype(BF16)

    fb = tt // FHALO
    colg = lambda r: pl.BlockSpec((r, FFN_CB), lambda j, i: (0, j))
    colu = lambda r: pl.BlockSpec((r, FFN_CB), lambda j, i: (0, j + nj))
    tile = pl.BlockSpec((tt, FFN_CB), lambda j, i: (i, j))
    return pl.pallas_call(
        body, name="ffn_up", grid=(nj, t // tt),
        in_specs=[pl.BlockSpec((tt, D_MODEL), lambda j, i: (i, 0)),
                  pl.BlockSpec((FHALO, D_MODEL), lambda j, i: (jnp.maximum(i * fb - 1, 0), 0)),
                  pl.BlockSpec((None, D_MODEL, FFN_CB), lambda j, i: (j, 0, 0)),
                  pl.BlockSpec((None, D_MODEL, FFN_CB), lambda j, i: (j + nj, 0, 0)),
                  colg(3), colu(3), colg(1), colu(1)],
        out_specs=[tile] * 5,
        out_shape=[jax.ShapeDtypeStruct((t, D_FF), BF16), jax.ShapeDtypeStruct((t, D_FF), BF16),
                   jax.ShapeDtypeStruct((t, D_FF), F32), jax.ShapeDtypeStruct((t, D_FF), F32),
                   jax.ShapeDtypeStruct((t, D_FF), BF16)],
        scratch_shapes=[pltpu.VMEM((tt + FHALO, FFN_CB), F32), pltpu.VMEM((tt + FHALO, FFN_CB), F32)],
        compiler_params=_params(("parallel", "parallel")),
    )(h2, h2, w_up, w_up, dw, dw, db, db)


def _ffn_down(act, w_down, x1, target):
    t = act.shape[0]
    tt = _token_tile(t)

    def body(a_ref, w_ref, x1_ref, t_ref, dy_ref, loss_ref):
        err = x1_ref[...] + _dot(a_ref[...], w_ref[...]) - t_ref[...]
        dy_ref[...] = err * (1.0 / D_MODEL)

        @pl.when(pl.program_id(0) == 0)
        def _():
            loss_ref[...] = jnp.zeros_like(loss_ref)

        loss_ref[...] += jnp.sum(err * err, axis=0, keepdims=True)

    row = lambda w: pl.BlockSpec((tt, w), lambda i: (i, 0))
    return pl.pallas_call(
        body, name="ffn_down", grid=(t // tt,),
        in_specs=[row(D_FF), _full((D_FF, D_MODEL)), row(D_MODEL), row(D_MODEL)],
        out_specs=[row(D_MODEL), _full((1, D_MODEL))],
        out_shape=[jax.ShapeDtypeStruct((t, D_MODEL), F32), jax.ShapeDtypeStruct((1, D_MODEL), F32)],
        compiler_params=_params(("arbitrary",)),
    )(act, w_down, x1, target)


def _ffn_bwd_act(dy, w_down, up_g, up_u):
    t = dy.shape[0]
    tt = _token_tile(t)
    nj = D_FF // FFN_CB

    def body(dy_ref, wd_ref, g_ref, u_ref, dg_ref, du_ref, gbg_ref, gbu_ref):
        i = pl.program_id(1)
        d_act = _dot_nt(dy_ref[...].astype(BF16), wd_ref[...])
        g, u = g_ref[...], u_ref[...]
        s = _sig(g)
        d_u = d_act * (g * s)
        d_g = d_act * u * (s * (1.0 + g * (1.0 - s)))

        @pl.when(i == 0)
        def _():
            for r in (gbg_ref, gbu_ref):
                r[...] = jnp.zeros_like(r)

        for d, o_ref, gb_ref in ((d_g, dg_ref, gbg_ref), (d_u, du_ref, gbu_ref)):
            o_ref[...] = d.astype(BF16)
            gb_ref[...] += jnp.sum(d, axis=0, keepdims=True)

    tile = pl.BlockSpec((tt, FFN_CB), lambda j, i: (i, j))
    acc = pl.BlockSpec((1, FFN_CB), lambda j, i: (0, j))
    return pl.pallas_call(
        body, name="ffn_bwd_act", grid=(nj, t // tt),
        in_specs=[pl.BlockSpec((tt, D_MODEL), lambda j, i: (i, 0)), pl.BlockSpec((FFN_CB, D_MODEL), lambda j, i: (j, 0)),
                  tile, tile],
        out_specs=[tile, tile, acc, acc],
        out_shape=[jax.ShapeDtypeStruct((t, D_FF), BF16), jax.ShapeDtypeStruct((t, D_FF), BF16),
                   jax.ShapeDtypeStruct((1, D_FF), F32), jax.ShapeDtypeStruct((1, D_FF), F32)],
        compiler_params=_params(("parallel", "arbitrary")),
    )(dy, w_down, up_g, up_u)


def _ffn_bwd_conv(dg, du, hg, hu, dw):
    t = dg.shape[0]
    tt = _token_tile(t)
    nj = D_FF // FFN_CB
    ni = t // tt

    def body(gc_ref, gn_ref, uc_ref, un_ref, hg_ref, hu_ref, dwg_ref, dwu_ref, og_ref, ou_ref, gwg_ref, gwu_ref, scr):
        i = pl.program_id(1)
        last = (i < ni - 1).astype(F32)

        @pl.when(i == 0)
        def _():
            gwg_ref[...] = jnp.zeros_like(gwg_ref)
            gwu_ref[...] = jnp.zeros_like(gwu_ref)

        for c_ref, n_ref, h_ref, dw_ref, o_ref, gw_ref in ((gc_ref, gn_ref, hg_ref, dwg_ref, og_ref, gwg_ref),
                                                           (uc_ref, un_ref, hu_ref, dwu_ref, ou_ref, gwu_ref)):
            scr[0:tt, :] = c_ref[...].astype(F32)
            scr[tt:tt + FHALO, :] = n_ref[...].astype(F32) * last
            sums = None
            for r0, cs in _chunks(tt, FFN_CB):
                shifted = [scr[r0 + d:r0 + d + ROW_CHUNK, cs] for d in (2, 1, 0)]
                o_ref[r0:r0 + ROW_CHUNK, cs] = (dw_ref[0:1, cs] * shifted[0] + dw_ref[1:2, cs] * shifted[1]
                                                + dw_ref[2:3, cs] * shifted[2]).astype(BF16)
                hw = h_ref[r0:r0 + ROW_CHUNK, cs].astype(F32)
                prods = [hw * d for d in shifted]
                sums = prods if r0 == 0 else [a + b for a, b in zip(sums, prods)]
                if r0 == tt - ROW_CHUNK:
                    gw_ref[:, cs] += _rows_to_tile([jnp.sum(a, axis=0, keepdims=True) for a in sums], 8)

    fb = tt // FHALO
    tile = pl.BlockSpec((tt, FFN_CB), lambda j, i: (i, j))
    nxt = pl.BlockSpec((FHALO, FFN_CB), lambda j, i: (jnp.minimum((i + 1) * fb, t // FHALO - 1), j))
    acc = pl.BlockSpec((8, FFN_CB), lambda j, i: (0, j))
    return pl.pallas_call(
        body, name="ffn_bwd_conv", grid=(nj, ni),
        in_specs=[tile, nxt, tile, nxt, tile, tile, pl.BlockSpec((3, FFN_CB), lambda j, i: (0, j)),
                  pl.BlockSpec((3, FFN_CB), lambda j, i: (0, j + nj))],
        out_specs=[tile, tile, acc, acc],
        out_shape=[jax.ShapeDtypeStruct((t, D_FF), BF16), jax.ShapeDtypeStruct((t, D_FF), BF16),
                   jax.ShapeDtypeStruct((8, D_FF), F32), jax.ShapeDtypeStruct((8, D_FF), F32)],
        scratch_shapes=[pltpu.VMEM((tt + FHALO, FFN_CB), F32)],
        compiler_params=_params(("parallel", "arbitrary")),
    )(dg, dg, du, du, hg, hu, dw, dw)


def _ffn_bwd_in(dhg, dhu, w_up, x1, dy, g_ffn, riders=()):
    t = x1.shape[0]
    tt = _token_tile(t)

    def body(dg_ref, du_ref, w_ref, x1_ref, dy_ref, g_ref, dx_ref, gg_ref):
        d_h2 = (_dot_nt(dg_ref[:, 0:FFN_CB], w_ref[0]) + _dot_nt(dg_ref[:, FFN_CB:], w_ref[1])
                + _dot_nt(du_ref[:, 0:FFN_CB], w_ref[2]) + _dot_nt(du_ref[:, FFN_CB:], w_ref[3]))
        x1 = x1_ref[...]
        r = lax.rsqrt(jnp.mean(x1 * x1, axis=-1, keepdims=True) + EPS)
        xh = x1 * r
        gd = d_h2 * g_ref[...]
        dx_ref[...] = dy_ref[...] + r * (gd - xh * jnp.mean(gd * xh, axis=-1, keepdims=True))

        @pl.when(pl.program_id(0) == 0)
        def _():
            gg_ref[...] = jnp.zeros_like(gg_ref)

        gg_ref[...] += jnp.sum(d_h2 * xh, axis=0, keepdims=True)

    row = lambda w: pl.BlockSpec((tt, w), lambda i: (i, 0))
    return _call(
        body, (dhg, dhu, w_up, x1, dy, g_ffn), name="ffn_bwd_in", grid=(t // tt,), riders=riders,
        in_specs=[row(D_FF), row(D_FF), _full((4, D_MODEL, FFN_CB)), row(D_MODEL), row(D_MODEL), _full((1, D_MODEL))],
        out_specs=[row(D_MODEL), _full((1, D_MODEL))],
        out_shape=[jax.ShapeDtypeStruct((t, D_MODEL), F32), jax.ShapeDtypeStruct((1, D_MODEL), F32)])


def _grad_weight(a, b, nj, mb, name, lead=None, into=None, offset=0):
    t, m = a.shape
    n = b.shape[1]
    nb_ = n // nj
    tt = GRAD_TILE if t % GRAD_TILE == 0 else _token_tile(t)

    def body(*refs):
        a_ref, b_ref, o_ref = refs[0], refs[1], refs[-1]

        @pl.when(pl.program_id(2) == 0)
        def _():
            o_ref[...] = jnp.zeros_like(o_ref)

        o_ref[0] += _dot_tn(a_ref[...].astype(BF16), b_ref[...].astype(BF16))

    in_specs = [pl.BlockSpec((tt, mb), lambda j, mi, i: (i, mi)), pl.BlockSpec((tt, nb_), lambda j, mi, i: (i, j))]
    out_shape = jax.ShapeDtypeStruct((lead or nj, m, nb_) if into is None else into.shape, F32)
    return pl.pallas_call(
        body, name=name, grid=(nj, m // mb, t // tt),
        in_specs=in_specs if into is None else in_specs + [pl.BlockSpec(memory_space=pl.ANY)],
        out_specs=pl.BlockSpec((1, mb, nb_), lambda j, mi, i: (j + offset, mi, 0)),
        out_shape=out_shape,
        input_output_aliases={} if into is None else {2: 0},
        compiler_params=_params(("parallel", "parallel", "arbitrary")),
    )(*((a, b) if into is None else (a, b, into)))


def _out_proj_bwd(dx1, w_out):
    t = dx1.shape[0]
    tt = _token_tile(t)

    def body(d_ref, w_ref, dm_ref, gb_ref):
        d = d_ref[...]
        dm_ref[...] = _dot_nt(d.astype(BF16), w_ref[...])

        @pl.when(pl.program_id(0) == 0)
        def _():
            gb_ref[...] = jnp.zeros_like(gb_ref)

        gb_ref[...] += jnp.sum(d, axis=0, keepdims=True)

    row = pl.BlockSpec((tt, D_MODEL), lambda i: (i, 0))
    return pl.pallas_call(
        body, name="out_proj_bwd", grid=(t // tt,),
        in_specs=[row, _full((D_MODEL, D_MODEL))],
        out_specs=[row, _full((1, D_MODEL))],
        out_shape=[jax.ShapeDtypeStruct((t, D_MODEL), F32), jax.ShapeDtypeStruct((1, D_MODEL), F32)],
        compiler_params=_params(("arbitrary",)),
    )(dx1, w_out)


def _conv_bwd(dmix, y, pc, cw, ln_g, ln_b, riders=()):
    t = y.shape[0]
    tt = _token_tile(t)
    ni = t // tt
    ncb = CONV_W // LANES

    def body(dc_ref, dcn_ref, y_ref, yn_ref, pc_ref, pcp_ref, w_ref, g_ref, bb_ref,
             dp_ref, gw_ref, gb_ref, gg_ref, gbb_ref, gbin_ref, scr_d, scr_c, scr_o, shf_d, shf_c):
        i = pl.program_id(0)
        lo = _lo_mask()

        @pl.when(i == 0)
        def _():
            for r in (gw_ref, gb_ref, gg_ref, gbb_ref, gbin_ref):
                r[...] = jnp.zeros_like(r)

        def norm_bwd(dc, yv, cs):
            yh, rstd = _group_norm_stats(yv, lo)
            z = yh * g_ref[:, cs] + bb_ref[:, cs]
            s = _sig(z)
            dz = dc * (s * (1.0 + z * (1.0 - s)))
            dyh = dz * g_ref[:, cs]
            d_y = rstd * (dyh - _half_mean(dyh, lo) - yh * _half_mean(dyh * yh, lo))
            return d_y, dz, yh

        for p in range(ncb):
            cs = slice(p * LANES, (p + 1) * LANES)
            d_y, dz, yh = norm_bwd(dc_ref[:, cs], y_ref[:, cs], cs)
            scr_d[0:tt, cs] = d_y
            gg_ref[:, cs] += jnp.sum(dz * yh, axis=0, keepdims=True)
            gbb_ref[:, cs] += jnp.sum(dz, axis=0, keepdims=True)
            gb_ref[:, cs] += jnp.sum(d_y, axis=0, keepdims=True)
            d_yn, _, _ = norm_bwd(dcn_ref[:, cs], yn_ref[:, cs], cs)
            scr_d[tt:tt + HALO, cs] = d_yn * (i < ni - 1).astype(F32)
        scr_c[0:HALO, :] = _glu(pcp_ref[...]) * (i > 0).astype(F32)
        scr_c[HALO:HALO + tt, :] = _glu(pc_ref[...])
        _shifted_copies(scr_d, shf_d, tt)
        _shifted_copies(scr_c, shf_c, tt)

        rid = lax.broadcasted_iota(jnp.int32, (HALO, LANES), 0)
        for cbk in range(ncb):
            cs = slice(cbk * LANES, (cbk + 1) * LANES)
            for rb in range(tt // ROW_CHUNK):
                r0 = rb * ROW_CHUNK
                acc = jnp.zeros((ROW_CHUNK, LANES), F32)
                for k in range(CONV_K):
                    acc = acc + w_ref[k:k + 1, cs] * _tap_rows(scr_d, shf_d, 30 - k, r0, cs)
                scr_o[r0:r0 + ROW_CHUNK, cs] = acc
            gwt = jnp.zeros((HALO, LANES), F32)
            for k in range(CONV_K):
                acc = jnp.zeros((ROW_CHUNK, LANES), F32)
                for rb in range(tt // ROW_CHUNK):
                    r0 = rb * ROW_CHUNK
                    acc = acc + scr_d[r0:r0 + ROW_CHUNK, cs] * _tap_rows(scr_c, shf_c, 2 + k, r0, cs)
                gwt = jnp.where(rid == k, jnp.sum(acc, axis=0, keepdims=True), gwt)
            gw_ref[:, cs] += gwt
        d_c0 = scr_o[...]
        a = pc_ref[:, 0:CONV_W]
        s = _sig(pc_ref[:, CONV_W:])
        d_a = d_c0 * s
        d_gate = d_c0 * a * s * (1.0 - s)
        dp_ref[:, 0:CONV_W] = d_a.astype(BF16)
        dp_ref[:, CONV_W:] = d_gate.astype(BF16)
        gbin_ref[:, 0:CONV_W] += jnp.sum(d_a, axis=0, keepdims=True)
        gbin_ref[:, CONV_W:] += jnp.sum(d_gate, axis=0, keepdims=True)

    hb = tt // HALO
    nxt = lambda col: pl.BlockSpec((HALO, CONV_W), lambda i: (jnp.minimum((i + 1) * hb, t // HALO - 1), col))
    return _call(
        body, (dmix, dmix, y, y, pc, pc, cw, ln_g, ln_b), name="conv_bwd", grid=(ni,), riders=riders,
        in_specs=[pl.BlockSpec((tt, CONV_W), lambda i: (i, 1)), nxt(1),
                  pl.BlockSpec((tt, CONV_W), lambda i: (i, 0)), nxt(0),
                  pl.BlockSpec((tt, 2 * CONV_W), lambda i: (i, 0)),
                  pl.BlockSpec((HALO, 2 * CONV_W), lambda i: (jnp.maximum(i * hb - 1, 0), 0)),
                  _full((CONV_K, CONV_W)), _full((1, CONV_W)), _full((1, CONV_W))],
        out_specs=[pl.BlockSpec((tt, 2 * CONV_W), lambda i: (i, 0)), _full((HALO, CONV_W)), _full((1, CONV_W)),
                   _full((1, CONV_W)), _full((1, CONV_W)), _full((1, 2 * CONV_W))],
        out_shape=[jax.ShapeDtypeStruct((t, 2 * CONV_W), BF16), jax.ShapeDtypeStruct((HALO, CONV_W), F32),
                   jax.ShapeDtypeStruct((1, CONV_W), F32), jax.ShapeDtypeStruct((1, CONV_W), F32),
                   jax.ShapeDtypeStruct((1, CONV_W), F32), jax.ShapeDtypeStruct((1, 2 * CONV_W), F32)],
        scratch=[pltpu.VMEM((tt + HALO, CONV_W), F32), pltpu.VMEM((tt + HALO, CONV_W), F32),
                 pltpu.VMEM((tt, CONV_W), F32), _shift_scratch(tt), _shift_scratch(tt)])


def _attn_bwd(qkvn, dmix, lse, sinks, bias, riders=()):
    t = qkvn.shape[0]
    nb = t // BLOCK

    def body(sink_ref, q_ref, kvc_ref, kvp_ref, do_ref, lse_ref, bias_ref, dq_ref, dcur_ref, dprev_ref, ds_ref):
        i = pl.program_id(0)
        k_dup, v_dup = _attn_keys(kvp_ref, kvc_ref)
        lo = _lo_mask()
        lane1 = lax.broadcasted_iota(jnp.int32, (1, LANES), 1)
        lane = lax.broadcasted_iota(jnp.int32, (BLOCK, LANES), 1)
        lse_t = lse_ref[...]
        dqs = [jnp.zeros((BLOCK, LANES), F32) for _ in range(4)]
        dsink = jnp.zeros((1, LANES), F32)
        dkv = []
        for kk in range(N_HEADS // GROUP):
            s, qs = _group_scores(q_ref, kk, lo, k_dup, bias_ref)
            lse = jnp.concatenate([jnp.sum(jnp.where(lane == GROUP * kk + a, lse_t, 0.0), axis=-1, keepdims=True)
                                   for a in range(GROUP)], axis=0)
            prob = jnp.exp(s - lse)
            dos = _stack_heads(do_ref, kk, lo, BF16)
            dp = _dot_nt(dos, v_dup[kk])
            dsum = jnp.sum(prob * dp, axis=-1, keepdims=True)
            dsb = (prob * (dp - dsum) * 0.125).astype(BF16)
            sink = _per_head_column([sink_ref[GROUP * kk + a] for a in range(GROUP)])
            dsk = -jnp.exp(sink - lse) * dsum
            dq = _dot(dsb, k_dup[kk])
            for a in range(GROUP):
                h = GROUP * kk + a
                rows = slice(a * BLOCK, (a + 1) * BLOCK)
                hm = lo if h % 2 == 0 else jnp.logical_not(lo)
                dqs[h // 2] = dqs[h // 2] + jnp.where(hm, dq[rows], 0.0)
                dsink = dsink + jnp.where(lane1 == h, jnp.sum(dsk[rows], axis=0, keepdims=True), 0.0)
            dk_x = _dot_tn(dsb, qs)
            dv_x = _dot_tn(prob.astype(BF16), dos)
            dkv.append((dk_x + pltpu.roll(dk_x, HEAD_DIM, 1), dv_x + pltpu.roll(dv_x, HEAD_DIM, 1)))
        for p in range(4):
            dq_ref[:, p * LANES:(p + 1) * LANES] = dqs[p]
        dk = jnp.where(lo, dkv[0][0], dkv[1][0])
        dv = jnp.where(lo, dkv[0][1], dkv[1][1])
        dprev_ref[:, 0:LANES] = dk[0:BLOCK]
        dprev_ref[:, LANES:] = dv[0:BLOCK]
        dcur_ref[:, 0:LANES] = dk[BLOCK:]
        dcur_ref[:, LANES:] = dv[BLOCK:]

        @pl.when(i == 0)
        def _():
            ds_ref[...] = jnp.zeros_like(ds_ref)

        ds_ref[...] += dsink

    blk = lambda w: pl.BlockSpec((BLOCK, w), lambda i: (i, 0))
    return _call(
        body, (sinks, qkvn, qkvn, qkvn, dmix, lse, bias), name="attn_bwd", grid=(nb,), riders=riders,
        in_specs=[pl.BlockSpec(memory_space=pltpu.SMEM), blk(Q_COLS),
                  pl.BlockSpec((BLOCK, 2 * LANES), lambda i: (i, 2)),
                  pl.BlockSpec((BLOCK, 2 * LANES), lambda i: (jnp.maximum(i - 1, 0), 2)),
                  blk(Q_COLS), blk(LANES), _attn_bias_spec()],
        out_specs=[blk(Q_COLS), blk(2 * LANES), blk(2 * LANES), _full((1, LANES))],
        out_shape=[jax.ShapeDtypeStruct((t, Q_COLS), F32), jax.ShapeDtypeStruct((t, 2 * LANES), F32),
                   jax.ShapeDtypeStruct((t, 2 * LANES), F32), jax.ShapeDtypeStruct((1, LANES), F32)])


def _qk_norm_bwd(dqn, dcur, dprev, pq, gq2, gk2):
    t = dqn.shape[0]
    nb = t // BLOCK

    def body(dq_ref, dc_ref, dn_ref, pq_ref, gq_ref, gk_ref, dp_ref, gbin_ref, gg_ref):
        i = pl.program_id(0)
        lo = _lo_mask()
        dkv = dc_ref[...] + dn_ref[...] * (i < nb - 1).astype(F32)

        @pl.when(i == 0)
        def _():
            gbin_ref[...] = jnp.zeros_like(gbin_ref)
            gg_ref[...] = jnp.zeros_like(gg_ref)

        for p in range(5):
            cs = slice(p * LANES, (p + 1) * LANES)
            seg = pq_ref[:, cs]
            dn = dq_ref[:, cs] if p < 4 else dkv[:, 0:LANES]
            gain = gq_ref[...] if p < 4 else gk_ref[...]
            rr = lax.rsqrt(_half_mean(seg * seg, lo) + EPS)
            xh = seg * rr
            gd = dn * gain
            d = rr * (gd - xh * _half_mean(gd * xh, lo))
            dp_ref[:, cs] = d.astype(BF16)
            gbin_ref[:, cs] += jnp.sum(d, axis=0, keepdims=True)
            gg_ref[:, cs] += jnp.sum(dn * xh, axis=0, keepdims=True)
        dv = dkv[:, LANES:]
        dp_ref[:, 640:768] = dv.astype(BF16)
        gbin_ref[:, 640:768] += jnp.sum(dv, axis=0, keepdims=True)

    blk = lambda w: pl.BlockSpec((BLOCK, w), lambda i: (i, 0))
    return pl.pallas_call(
        body, name="qk_norm_bwd", grid=(nb,),
        in_specs=[blk(Q_COLS), blk(2 * LANES), pl.BlockSpec((BLOCK, 2 * LANES), lambda i: (jnp.minimum(i + 1, nb - 1), 0)),
                  blk(QKV_COLS), _full((1, LANES)), _full((1, LANES))],
        out_specs=[blk(QKV_COLS), _full((1, QKV_COLS)), _full((1, 5 * LANES))],
        out_shape=[jax.ShapeDtypeStruct((t, QKV_COLS), BF16), jax.ShapeDtypeStruct((1, QKV_COLS), F32),
                   jax.ShapeDtypeStruct((1, 5 * LANES), F32)],
        compiler_params=_params(("arbitrary",)),
    )(dqn, dcur, dprev, pq, gq2, gk2)


def _in_proj_bwd(dpq, dpc, w_in, x, dx1, g_mix):
    t = x.shape[0]
    tt = _token_tile(t)

    def body(dq_ref, dc_ref, w_ref, x_ref, d1_ref, g_ref, gx_ref, gg_ref):
        d_h = _dot_nt(dq_ref[...], w_ref[:, 0:QKV_COLS]) + _dot_nt(dc_ref[...], w_ref[:, QKV_COLS:])
        xv = x_ref[...]
        r = lax.rsqrt(jnp.mean(xv * xv, axis=-1, keepdims=True) + EPS)
        xh = xv * r
        gd = d_h * g_ref[...]
        gx_ref[...] = d1_ref[...] + r * (gd - xh * jnp.mean(gd * xh, axis=-1, keepdims=True))

        @pl.when(pl.program_id(0) == 0)
        def _():
            gg_ref[...] = jnp.zeros_like(gg_ref)

        gg_ref[...] += jnp.sum(d_h * xh, axis=0, keepdims=True)

    row = lambda w: pl.BlockSpec((tt, w), lambda i: (i, 0))
    return pl.pallas_call(
        body, name="in_proj_bwd", grid=(t // tt,),
        in_specs=[row(QKV_COLS), row(2 * CONV_W), _full((D_MODEL, IN_COLS)), row(D_MODEL), row(D_MODEL), _full((1, D_MODEL))],
        out_specs=[row(D_MODEL), _full((1, D_MODEL))],
        out_shape=[jax.ShapeDtypeStruct((t, D_MODEL), F32), jax.ShapeDtypeStruct((1, D_MODEL), F32)],
        compiler_params=_params(("arbitrary",)),
    )(dpq, dpc, w_in, x, dx1, g_mix)


def _row_block(r):
    if r <= 256:
        return r
    return max(b for b in range(8, 257, 8) if r % b == 0)


def _adamw(w, g, m, v, name):
    r, c = w.shape
    rb = _row_block(r)

    def body(w_ref, g_ref, m_ref, v_ref, d_ref, nm_ref, nv_ref):
        gv = g_ref[...]
        nm = ADAM_B1 * m_ref[...] + (1.0 - ADAM_B1) * gv
        nv = ADAM_B2 * v_ref[...] + (1.0 - ADAM_B2) * (gv * gv)
        m_hat = nm / (1.0 - ADAM_B1 ** ADAM_STEP)
        v_hat = nv / (1.0 - ADAM_B2 ** ADAM_STEP)
        d_ref[...] = -ADAM_LR * (m_hat / (jnp.sqrt(v_hat) + ADAM_EPS) + ADAM_WD * w_ref[...])
        nm_ref[...] = nm
        nv_ref[...] = nv

    blk = pl.BlockSpec((rb, c), lambda i: (i, 0))
    shp = jax.ShapeDtypeStruct((r, c), F32)
    return pl.pallas_call(
        body, name=name, grid=(r // rb,), in_specs=[blk] * 4, out_specs=[blk] * 3, out_shape=[shp] * 3,
        compiler_params=_params(("parallel",)),
    )(w, g, m, v)


def _place():
    x, y, c = lax.axis_index("x"), lax.axis_index("y"), lax.axis_index("c")
    chips = [(1 - x, y), (x, 1 - y), (1 - x, 1 - y)]
    return x, y, c, chips


def _gather_all(v):
    r = v.shape[0]

    def body(v_ref, all_ref, sum_ref, send_sems, recv_sems):
        x, y, c, _ = _place()
        me = 4 * x + 2 * y + c
        all_ref[me] = v_ref[...]
        copies = []
        for k in range(1, 8):
            kx, ky, kc = (k >> 2) & 1, (k >> 1) & 1, k & 1
            peer = (x ^ kx, y ^ ky, c ^ kc)
            cp = pltpu.make_async_remote_copy(src_ref=v_ref, dst_ref=all_ref.at[me], send_sem=send_sems.at[k - 1],
                                              recv_sem=recv_sems.at[k - 1], device_id=peer, device_id_type=MESH)
            cp.start()
            copies.append((cp, 4 * peer[0] + 2 * peer[1] + peer[2]))
        for k, (cp, src_idx) in enumerate(copies):
            pltpu.make_async_remote_copy(src_ref=v_ref, dst_ref=all_ref.at[src_idx], send_sem=send_sems.at[k],
                                         recv_sem=recv_sems.at[k], device_id=(x, y, c), device_id_type=MESH).wait_recv()
        for cp, _ in copies:
            cp.wait_send()
        tot = all_ref[0]
        for d in range(1, 8):
            tot = tot + all_ref[d]
        sum_ref[...] = tot

    vm = pl.BlockSpec(memory_space=pltpu.VMEM)
    return pl.pallas_call(
        body, name="gather_all", in_specs=[vm], out_specs=[vm, vm],
        out_shape=[jax.ShapeDtypeStruct((8, r, LANES), v.dtype), jax.ShapeDtypeStruct((r, LANES), v.dtype)],
        scratch_shapes=[pltpu.SemaphoreType.DMA((7,)), pltpu.SemaphoreType.DMA((7,))],
        compiler_params=pltpu.CompilerParams(vmem_limit_bytes=VMEM_LIMIT),
    )(v)


def _remote(src, dst, send_sem, recv_sem, to):
    return pltpu.make_async_remote_copy(src_ref=src, dst_ref=dst, send_sem=send_sem, recv_sem=recv_sem,
                                        device_id=to, device_id_type=MESH)


def _dma_sems(*shape):
    return pltpu.SemaphoreType.DMA(shape)


def _gather_first(shards):
    n = len(shards)

    def copies(ins, outs, sems):
        x, y, c, chips = _place()
        me = 2 * x + y
        local = [pltpu.make_async_copy(ins[a], outs[a].at[me], sems[2].at[a]) for a in range(n)]
        sends = [_remote(ins[a].at[c], outs[a].at[me, c], sems[0].at[a, j], sems[1].at[a, j], (*chip, c))
                 for a in range(n) for j, chip in enumerate(chips)]
        lands = [_remote(ins[a].at[c], outs[a].at[2 * chip[0] + chip[1], c], sems[0].at[a, j], sems[1].at[a, j], (x, y, c))
                 for a in range(n) for j, chip in enumerate(chips)]
        return local, sends, lands

    def start(ins, outs, sems):
        local, sends, _ = copies(ins, outs, sems)
        for cp in local + sends:
            cp.start()

    def finish(ins, outs, sems):
        local, sends, lands = copies(ins, outs, sems)
        for cp in lands:
            cp.wait_recv()
        for cp in sends:
            cp.wait_send()
        for cp in local:
            cp.wait()

    return _Rider(shards, [jax.ShapeDtypeStruct((4,) + s.shape, s.dtype) for s in shards],
                  [_dma_sems(n, 3), _dma_sems(n, 3), _dma_sems(n)], start, finish)


def _gather_second(partials):
    n = len(partials)

    def copies(outs, sems):
        x, y, c, chips = _place()
        sends, lands = [], []
        for a in range(n):
            for j, chip in enumerate(chips):
                mine = outs[a].at[2 * chip[0] + chip[1], c]
                theirs = outs[a].at[2 * chip[0] + chip[1], 1 - c]
                sends.append(_remote(mine, mine, sems[0].at[a, j], sems[1].at[a, j], (x, y, 1 - c)))
                lands.append(_remote(theirs, theirs, sems[0].at[a, j], sems[1].at[a, j], (x, y, c)))
        return sends, lands

    def start(ins, outs, sems):
        for cp in copies(outs, sems)[0]:
            cp.start()

    def finish(ins, outs, sems):
        sends, lands = copies(outs, sems)
        for cp in lands:
            cp.wait_recv()
        for cp in sends:
            cp.wait_send()

    return _Rider(partials, [jax.ShapeDtypeStruct(p.shape, p.dtype) for p in partials],
                  [_dma_sems(n, 3), _dma_sems(n, 3)], start, finish, aliases={a: a for a in range(n)})


def _swap_halves(grads):
    n = len(grads)

    def copies(ins, outs, sems):
        x, y, c, _ = _place()
        return [_remote(ins[a].at[j, 1 - c], outs[a].at[j], sems[0].at[a, j], sems[1].at[a, j], (x, y, 1 - c))
                for a in range(n) for j in range(4)]

    def start(ins, outs, sems):
        for cp in copies(ins, outs, sems):
            cp.start()

    def finish(ins, outs, sems):
        for cp in copies(ins, outs, sems):
            cp.wait()

    return _Rider(grads, [jax.ShapeDtypeStruct((4,) + g.shape[2:], g.dtype) for g in grads],
                  [_dma_sems(n, 4), _dma_sems(n, 4)], start, finish)


def _add_sibling(g, got, c_idx, name):
    _, _, h, c = g.shape

    def body(s_ref, a_ref, b_ref, o_ref):
        o_ref[...] = (a_ref[...] + b_ref[...]).astype(BF16)

    return pl.pallas_call(
        body, name=name,
        grid_spec=pltpu.PrefetchScalarGridSpec(
            num_scalar_prefetch=1, grid=(4,),
            in_specs=[pl.BlockSpec((None, None, h, c), lambda j, s: (j, s[0], 0, 0)),
                      pl.BlockSpec((None, h, c), lambda j, s: (j, 0, 0))],
            out_specs=pl.BlockSpec((None, h, c), lambda j, s: (j, 0, 0))),
        out_shape=jax.ShapeDtypeStruct((4, h, c), BF16),
        compiler_params=_params(("parallel",)),
    )(c_idx, g, got)


def _exchange_chips(parts):
    n = len(parts)

    def copies(ins, outs, sems):
        x, y, c, chips = _place()
        return [_remote(ins[a].at[2 * chip[0] + chip[1]], outs[a].at[j], sems[0].at[a, j], sems[1].at[a, j], (*chip, c))
                for a in range(n) for j, chip in enumerate(chips)]

    def start(ins, outs, sems):
        for cp in copies(ins, outs, sems):
            cp.start()

    def finish(ins, outs, sems):
        for cp in copies(ins, outs, sems):
            cp.wait()

    return _Rider(parts, [jax.ShapeDtypeStruct((3,) + p.shape[1:], p.dtype) for p in parts],
                  [_dma_sems(n, 3), _dma_sems(n, 3)], start, finish)


def _add_chips(part, got, chip_idx, name):
    _, h, c = part.shape

    def body(s_ref, a_ref, b_ref, o_ref):
        o_ref[...] = ((a_ref[...].astype(F32) + b_ref[0].astype(F32)) + b_ref[1].astype(F32)) + b_ref[2].astype(F32)

    return pl.pallas_call(
        body, name=name,
        grid_spec=pltpu.PrefetchScalarGridSpec(
            num_scalar_prefetch=1, grid=(1,),
            in_specs=[pl.BlockSpec((None, h, c), lambda i, s: (s[0], 0, 0)),
                      pl.BlockSpec((3, h, c), lambda i, s: (0, 0, 0))],
            out_specs=pl.BlockSpec((h, c), lambda i, s: (0, 0))),
        out_shape=jax.ShapeDtypeStruct((h, c), F32),
        compiler_params=_params(("arbitrary",)),
    )(chip_idx, part, got)


def _join_halves(halves):
    n = len(halves)

    def copies(ins, outs, sems):
        x, y, c, _ = _place()
        local = [pltpu.make_async_copy(ins[a], outs[a].at[c], sems[2].at[a]) for a in range(n)]
        sends = [_remote(ins[a], outs[a].at[c], sems[0].at[a], sems[1].at[a], (x, y, 1 - c)) for a in range(n)]
        lands = [_remote(ins[a], outs[a].at[1 - c], sems[0].at[a], sems[1].at[a], (x, y, c)) for a in range(n)]
        return local, sends, lands

    def start(ins, outs, sems):
        local, sends, _ = copies(ins, outs, sems)
        for cp in local + sends:
            cp.start()

    def finish(ins, outs, sems):
        local, sends, lands = copies(ins, outs, sems)
        for cp in lands:
            cp.wait_recv()
        for cp in sends:
            cp.wait_send()
        for cp in local:
            cp.wait()

    return _Rider(halves, [jax.ShapeDtypeStruct((2,) + h.shape, h.dtype) for h in halves],
                  [_dma_sems(n), _dma_sems(n), _dma_sems(n)], start, finish)


def _pack(parts):
    flat = []
    for p in parts:
        p = p.reshape(-1).astype(F32)
        flat.append(jnp.pad(p, (0, (-p.shape[0]) % LANES)))
    v = jnp.concatenate(flat)
    v = jnp.pad(v, (0, (-v.shape[0]) % (8 * LANES)))
    return v.reshape(-1, LANES)


def _unpack(v, shapes):
    flat = v.reshape(-1)
    out, off = [], 0
    for s in shapes:
        n = 1
        for d in s:
            n *= d
        out.append(flat[off:off + n].reshape(s))
        off += n + (-n) % LANES
    return out


def kernel(x, mix_norm_gain, w_in, b_in, q_norm_gain, k_norm_gain, attn_sinks, conv_dw_w, conv_dw_b, conv_norm_gain, conv_norm_bias, w_out, b_out, ffn_norm_gain, w_up, ffn_dw_w, ffn_dw_b, w_down, loss_target, m_mix_norm_gain, m_w_in, m_b_in, m_q_norm_gain, m_k_norm_gain, m_attn_sinks, m_conv_dw_w, m_conv_dw_b, m_conv_norm_gain, m_conv_norm_bias, m_w_out, m_b_out, m_ffn_norm_gain, m_w_up, m_ffn_dw_w, m_ffn_dw_b, m_w_down, v_mix_norm_gain, v_w_in, v_b_in, v_q_norm_gain, v_k_norm_gain, v_attn_sinks, v_conv_dw_w, v_conv_dw_b, v_conv_norm_gain, v_conv_norm_bias, v_w_out, v_b_out, v_ffn_norm_gain, v_w_up, v_ffn_dw_w, v_ffn_dw_b, v_w_down):
    t = x.shape[1]
    xi, yi, ci = lax.axis_index("x"), lax.axis_index("y"), lax.axis_index("c")
    chip = 2 * xi + yi
    c_idx = jnp.reshape(ci, (1,)).astype(jnp.int32)
    chip_idx = jnp.reshape(chip, (1,)).astype(jnp.int32)
    x2 = x.reshape(t, D_MODEL)
    tgt = loss_target.reshape(t, D_MODEL)

    big = [w_in, w_out, w_up, w_down]
    halves = [w.astype(BF16).reshape(2, w.shape[0] // 2, w.shape[1]) for w in big]
    h_wi, h_wo, h_wu, h_wd = halves
    (p_wi,) = _run_riders([_gather_first([h_wi])], "gather_w_in_first")
    (g_wi,) = _run_riders([_gather_second([p_wi])], "gather_w_in_second")
    wi = jnp.concatenate([g_wi[j].reshape(D_MODEL, IN_COLS // 4) for j in range(4)], axis=1)
    small_w, _ = _gather_all(_pack([conv_dw_w, ffn_dw_w]))
    per_chip = [_unpack(small_w[4 * (j // 2) + 2 * (j % 2)], [conv_dw_w.shape, ffn_dw_w.shape]) for j in range(4)]
    cw = jnp.concatenate([p[0] for p in per_chip], axis=1)
    fw = jnp.concatenate([p[1] for p in per_chip], axis=1)

    row = lambda a: a.reshape(1, -1)
    gq2 = row(jnp.concatenate([q_norm_gain, q_norm_gain]))
    gk2 = row(jnp.concatenate([k_norm_gain, k_norm_gain]))

    (h1, pq, pc, qkvn), (p_wo,) = _fwd_in(x2, row(mix_norm_gain), wi, row(b_in), gq2, gk2, riders=[_gather_first([h_wo])])
    bias = _attn_bias()
    (attn, lse), (g_wo, p_wu) = _attn_fwd(qkvn, attn_sinks, bias, riders=[_gather_second([p_wo]), _gather_first([h_wu])])
    (y_conv, c_act), (g_wu, p_wd) = _conv_fwd(pc, cw, row(conv_dw_b), row(conv_norm_gain), row(conv_norm_bias),
                                              riders=[_gather_second([p_wu]), _gather_first([h_wd])])
    wo = g_wo.reshape(D_MODEL, D_MODEL)
    (x1, h2), (g_wd,) = _out_proj(x2, attn, c_act, wo, row(b_out), row(ffn_norm_gain), riders=[_gather_second([p_wd])])
    wu = g_wu.reshape(4, D_MODEL, FFN_CB)
    wd = g_wd.reshape(D_FF, D_MODEL)
    hg, hu, up_g, up_u, act = _ffn_up(h2, wu, fw, row(ffn_dw_b))
    dy, loss_cols = _ffn_down(act, wd, x1, tgt)

    split = lambda g: g.reshape(4, 2, g.shape[1] // 2, g.shape[2])
    dg, du, gfb_g, gfb_u = _ffn_bwd_act(dy, wd, up_g, up_u)
    dhg, dhu, gfw_g, gfw_u = _ffn_bwd_conv(dg, du, hg, hu, fw)
    gw_down = _grad_weight(act, dy, 1, FFN_CB, "grad_w_down")
    gw_up = _grad_weight(h2, dhg, 2, D_MODEL, "grad_w_up_gate", lead=4)
    gw_up = _grad_weight(h2, dhu, 2, D_MODEL, "grad_w_up_lin", into=gw_up, offset=2)
    early = [split(gw_up), split(gw_down.reshape(4, D_FF // 4, D_MODEL))]
    early_names = ["w_up", "w_down"]
    (dx1, g_ffn_gain), got = _ffn_bwd_in(dhg, dhu, wu, x1, dy, row(ffn_norm_gain), riders=[_swap_halves(early)])
    early_part = [_add_sibling(g, r, c_idx, "add_sibling_" + nm_) for g, r, nm_ in zip(early, got, early_names)]
    dmix, g_b_out = _out_proj_bwd(dx1, wo)
    gw_out = jnp.concatenate([_grad_weight(attn, dx1, 1, Q_COLS, "grad_w_out_attn")[0],
                              _grad_weight(c_act, dx1, 1, CONV_W, "grad_w_out_conv")[0]], axis=0)
    (dpc, g_cw, g_cb, g_lng, g_lnb, gbin_c), got2 = _conv_bwd(dmix, y_conv, pc, cw, row(conv_norm_gain), row(conv_norm_bias),
                                                              riders=[_exchange_chips(early_part)])
    early_red = [_add_chips(p, r, chip_idx, "add_chips_" + nm_) for p, r, nm_ in zip(early_part, got2, early_names)]
    (dqn, dcur, dprev, g_sink), early_g = _attn_bwd(qkvn, dmix, lse, attn_sinks, bias, riders=[_join_halves(early_red)])
    dpq, gbin_q, g_qk = _qk_norm_bwd(dqn, dcur, dprev, pq, gq2, gk2)
    grad_x, g_mix_gain = _in_proj_bwd(dpq, dpc, wi, x2, dx1, row(mix_norm_gain))
    gw_in = jnp.concatenate([_grad_weight(h1, dpq, 1, D_MODEL, "grad_w_in_qkv")[0],
                             _grad_weight(h1, dpc, 1, D_MODEL, "grad_w_in_conv")[0]], axis=1)

    g_qk = g_qk.reshape(5, 2, HEAD_DIM)
    small = [g_mix_gain, jnp.concatenate([gbin_q, gbin_c], axis=1), g_qk[:4].sum(axis=(0, 1)), g_qk[4].sum(axis=0),
             g_sink[0, :N_HEADS], g_cb, g_lng, g_lnb, g_b_out, g_ffn_gain, jnp.concatenate([gfb_g, gfb_u], axis=1),
             loss_cols, g_cw[:CONV_K], jnp.concatenate([gfw_g[:3], gfw_u[:3]], axis=1)]
    _, tot = _gather_all(_pack(small))
    rep_names = ["mix_norm_gain", "b_in", "q_norm_gain", "k_norm_gain", "attn_sinks", "conv_dw_b", "conv_norm_gain",
                 "conv_norm_bias", "b_out", "ffn_norm_gain", "ffn_dw_b"]
    rep_w = [mix_norm_gain, b_in, q_norm_gain, k_norm_gain, attn_sinks, conv_dw_b, conv_norm_gain, conv_norm_bias,
             b_out, ffn_norm_gain, ffn_dw_b]
    rep_m = [m_mix_norm_gain, m_b_in, m_q_norm_gain, m_k_norm_gain, m_attn_sinks, m_conv_dw_b, m_conv_norm_gain,
             m_conv_norm_bias, m_b_out, m_ffn_norm_gain, m_ffn_dw_b]
    rep_v = [v_mix_norm_gain, v_b_in, v_q_norm_gain, v_k_norm_gain, v_attn_sinks, v_conv_dw_b, v_conv_norm_gain,
             v_conv_norm_bias, v_b_out, v_ffn_norm_gain, v_ffn_dw_b]
    shapes = [w.shape for w in rep_w] + [(D_MODEL,), (CONV_K, CONV_W), (3, 2 * D_FF)]
    tot_parts = _unpack(tot, shapes)
    loss = (0.5 / D_MODEL) * jnp.sum(tot_parts[len(rep_w)])
    g_cw_full, g_fw_full = tot_parts[len(rep_w) + 1], tot_parts[len(rep_w) + 2]
    n_rep_rows = _pack(rep_w).shape[0]
    rep_d, rep_nm, rep_nv = _adamw(_pack(rep_w), tot[:n_rep_rows], _pack(rep_m), _pack(rep_v), "adamw_small")
    rep_shapes = [w.shape for w in rep_w]
    res = {}
    for nm_, g_, d_, m_, v_ in zip(rep_names, tot_parts, _unpack(rep_d, rep_shapes), _unpack(rep_nm, rep_shapes),
                                   _unpack(rep_nv, rep_shapes)):
        res[nm_] = (g_, d_, m_, v_)

    g_cw_mine = lax.dynamic_slice_in_dim(g_cw_full, chip * (CONV_W // 4), CONV_W // 4, axis=1)
    g_fw_mine = lax.dynamic_slice_in_dim(g_fw_full, chip * (2 * D_FF // 4), 2 * D_FF // 4, axis=1)
    res["conv_dw_w"] = (g_cw_mine, *_adamw(conv_dw_w, g_cw_mine, m_conv_dw_w, v_conv_dw_w, "adamw_conv_dw_w"))
    res["ffn_dw_w"] = (g_fw_mine, *_adamw(ffn_dw_w, g_fw_mine, m_ffn_dw_w, v_ffn_dw_w, "adamw_ffn_dw_w"))

    gw_in4 = gw_in.reshape(D_MODEL, 4, IN_COLS // 4).transpose(1, 0, 2)
    late = [split(gw_in4), split(gw_out.reshape(4, D_MODEL // 4, D_MODEL))]
    late_names = ["w_in", "w_out"]
    got = _run_riders([_swap_halves(late)], "swap_halves_late")
    late_part = [_add_sibling(g, r, c_idx, "add_sibling_" + nm_) for g, r, nm_ in zip(late, got, late_names)]
    got2 = _run_riders([_exchange_chips(late_part)], "exchange_chips_late")
    late_red = [_add_chips(p, r, chip_idx, "add_chips_" + nm_) for p, r, nm_ in zip(late_part, got2, late_names)]
    late_g = _run_riders([_join_halves(late_red)], "join_halves_late")
    names = ["w_in", "w_out", "w_up", "w_down"]
    shard_g = list(late_g) + list(early_g)
    for nm_, w_, g_, m_, v_ in zip(names, big, shard_g, [m_w_in, m_w_out, m_w_up, m_w_down], [v_w_in, v_w_out, v_w_up, v_w_down]):
        g_ = g_.reshape(w_.shape)
        res[nm_] = (g_, *_adamw(w_, g_, m_, v_, "adamw_" + nm_))

    order = ["mix_norm_gain", "w_in", "b_in", "q_norm_gain", "k_norm_gain", "attn_sinks", "conv_dw_w", "conv_dw_b",
             "conv_norm_gain", "conv_norm_bias", "w_out", "b_out", "ffn_norm_gain", "w_up", "ffn_dw_w", "ffn_dw_b", "w_down"]
    return (loss, grad_x.reshape(x.shape), *[res[n][0] for n in order], *[res[n][1] for n in order],
            *[res[n][2] for n in order], *[res[n][3] for n in order])
```

```python
import functools

import jax
import jax.numpy as jnp
from jax import lax
from jax.experimental import pallas as pl
from jax.experimental.pallas import tpu as pltpu

F32 = jnp.float32
BF16 = jnp.bfloat16
MESH = pl.DeviceIdType.MESH

D_MODEL = 1024
HEAD_DIM = 64
N_HEADS = 8
Q_COLS = 512
QKV_COLS = 768
CONV_W = 512
CONV_K = 31
IN_COLS = 1792
D_FF = 2816
FFN_CB = 1408
BLOCK = 128
LANES = 128
EPS = 1e-6
NEG_INF = -1e30
SLOPES = tuple(float(2.0 ** (-(h + 1.0))) for h in range(N_HEADS))
HALO = 32
FHALO = 16
ROW_CHUNK = 64
GRAD_TILE = 2048
VMEM_LIMIT = 56 * 1024 * 1024

ADAM_LR = 0.001
ADAM_B1 = 0.9
ADAM_B2 = 0.999
ADAM_EPS = 1e-08
ADAM_WD = 0.01
ADAM_STEP = 10


def _params(sem=None):
    kw = dict(vmem_limit_bytes=VMEM_LIMIT)
    if sem is not None:
        kw["dimension_semantics"] = sem
    return pltpu.CompilerParams(**kw)


def _token_tile(t):
    return 512 if t % 512 == 0 and t >= 2048 else 128


def _sig(v):
    return 1.0 / (1.0 + jnp.exp(-v))


def _lo_mask():
    return lax.broadcasted_iota(jnp.int32, (1, LANES), 1) < HEAD_DIM


def _half_mean(v, lo):
    s_lo = jnp.sum(jnp.where(lo, v, 0.0), axis=-1, keepdims=True)
    s_hi = jnp.sum(jnp.where(lo, 0.0, v), axis=-1, keepdims=True)
    return jnp.where(lo, s_lo, s_hi) * (1.0 / HEAD_DIM)


def _dot(a, b):
    return jnp.dot(a, b, preferred_element_type=F32)


def _dot_nt(a, b):
    return lax.dot_general(a, b, (((1,), (1,)), ((), ())), preferred_element_type=F32)


def _dot_tn(a, b):
    return lax.dot_general(a, b, (((0,), (0,)), ((), ())), preferred_element_type=F32)


def _full(shape):
    nd = len(shape)
    return pl.BlockSpec(shape, lambda *_: (0,) * nd)


def _rows_to_tile(rows, n_rows):
    c = rows[0].shape[-1]
    rid = lax.broadcasted_iota(jnp.int32, (n_rows, c), 0)
    out = jnp.zeros((n_rows, c), F32)
    for k, r in enumerate(rows):
        out = jnp.where(rid == k, r, out)
    return out


ANY = pl.BlockSpec(memory_space=pl.ANY)


class _Rider:
    def __init__(self, ins, outs, sems, start, finish, aliases=None):
        self.ins, self.outs, self.sems = list(ins), list(outs), list(sems)
        self.start, self.finish, self.aliases = start, finish, dict(aliases or {})


def _join_riders(riders):
    ins, outs, sems, aliases, spans = [], [], [], {}, []
    for r in riders:
        spans.append((len(ins), len(outs), len(sems), r))
        for a, b in r.aliases.items():
            aliases[len(ins) + a] = len(outs) + b
        ins += r.ins
        outs += r.outs
        sems += r.sems

    def each(which):
        def run(i_refs, o_refs, s_refs):
            for i0, o0, s0, r in spans:
                getattr(r, which)(i_refs[i0:i0 + len(r.ins)], o_refs[o0:o0 + len(r.outs)], s_refs[s0:s0 + len(r.sems)])
        return run

    return _Rider(ins, outs, sems, each("start"), each("finish"), aliases)


def _call(body, args, *, name, grid, in_specs, out_specs, out_shape, scratch=(), riders=()):
    in_specs, out_specs, out_shape, scratch = list(in_specs), list(out_specs), list(out_shape), list(scratch)
    sem = ("arbitrary",) * len(grid)
    if not riders:
        outs = pl.pallas_call(body, name=name, grid=grid, in_specs=in_specs, out_specs=out_specs, out_shape=out_shape,
                              scratch_shapes=scratch, compiler_params=_params(sem))(*args)
        return list(outs), []
    r = _join_riders(riders)
    n_in, n_out, n_scr = len(in_specs), len(out_specs), len(scratch)
    nri, nro = len(r.ins), len(r.outs)

    def full(*refs):
        ins, rin = refs[:n_in], refs[n_in:n_in + nri]
        o0 = n_in + nri
        outs, rout = refs[o0:o0 + n_out], refs[o0 + n_out:o0 + n_out + nro]
        s0 = o0 + n_out + nro
        scr, rsem = refs[s0:s0 + n_scr], refs[s0 + n_scr:]
        first = functools.reduce(jnp.logical_and, [pl.program_id(k) == 0 for k in range(len(grid))])
        last = functools.reduce(jnp.logical_and, [pl.program_id(k) == grid[k] - 1 for k in range(len(grid))])

        @pl.when(first)
        def _():
            r.start(rin, rout, rsem)

        body(*ins, *outs, *scr)

        @pl.when(last)
        def _():
            r.finish(rin, rout, rsem)

    outs = pl.pallas_call(
        full, name=name, grid=grid, in_specs=in_specs + [ANY] * nri, out_specs=out_specs + [ANY] * nro,
        out_shape=out_shape + r.outs, scratch_shapes=scratch + r.sems,
        input_output_aliases={n_in + a: n_out + b for a, b in r.aliases.items()},
        compiler_params=_params(sem))(*args, *r.ins)
    return list(outs[:n_out]), list(outs[n_out:])


def _run_riders(riders, name):
    r = _join_riders(riders)
    nri, nro = len(r.ins), len(r.outs)

    def body(*refs):
        rin, rout, rsem = refs[:nri], refs[nri:nri + nro], refs[nri + nro:]
        r.start(rin, rout, rsem)
        r.finish(rin, rout, rsem)

    outs = pl.pallas_call(body, name=name, in_specs=[ANY] * nri, out_specs=[ANY] * nro, out_shape=r.outs,
                          scratch_shapes=r.sems, input_output_aliases=r.aliases)(*r.ins)
    return list(outs)


def _fwd_in(x, g_mix, w_in, b_in, gq2, gk2, riders=()):
    t = x.shape[0]
    tt = _token_tile(t)

    def body(x_ref, g_ref, w_ref, b_ref, gq_ref, gk_ref, h1_ref, pq_ref, pc_ref, qkvn_ref):
        xv = x_ref[...]
        r = lax.rsqrt(jnp.mean(xv * xv, axis=-1, keepdims=True) + EPS)
        h = (xv * r * g_ref[...]).astype(BF16)
        h1_ref[...] = h
        proj = _dot(h, w_ref[...]) + b_ref[...]
        pq_ref[...] = proj[:, :QKV_COLS]
        pc_ref[...] = proj[:, QKV_COLS:]
        lo = _lo_mask()
        for p in range(5):
            seg = proj[:, p * LANES:(p + 1) * LANES]
            rr = lax.rsqrt(_half_mean(seg * seg, lo) + EPS)
            gain = gq_ref[...] if p < 4 else gk_ref[...]
            qkvn_ref[:, p * LANES:(p + 1) * LANES] = (seg * rr * gain).astype(BF16)
        qkvn_ref[:, 640:768] = proj[:, 640:768].astype(BF16)

    return _call(
        body, (x, g_mix, w_in, b_in, gq2, gk2), name="fwd_in", grid=(t // tt,), riders=riders,
        in_specs=[pl.BlockSpec((tt, D_MODEL), lambda i: (i, 0)), _full((1, D_MODEL)), _full((D_MODEL, IN_COLS)),
                  _full((1, IN_COLS)), _full((1, LANES)), _full((1, LANES))],
        out_specs=[pl.BlockSpec((tt, D_MODEL), lambda i: (i, 0)), pl.BlockSpec((tt, QKV_COLS), lambda i: (i, 0)),
                   pl.BlockSpec((tt, 2 * CONV_W), lambda i: (i, 0)), pl.BlockSpec((tt, QKV_COLS), lambda i: (i, 0))],
        out_shape=[jax.ShapeDtypeStruct((t, D_MODEL), BF16), jax.ShapeDtypeStruct((t, QKV_COLS), F32),
                   jax.ShapeDtypeStruct((t, 2 * CONV_W), F32), jax.ShapeDtypeStruct((t, QKV_COLS), BF16)])


GROUP = 4
GROUP_ROWS = GROUP * BLOCK


def _attn_bias():
    qi = jnp.arange(GROUP_ROWS)[:, None] % BLOCK
    kj = jnp.arange(2 * BLOCK)[None, :]
    rel = qi + BLOCK - kj
    band = (rel >= 0) & (rel < BLOCK)
    slope_rows = jnp.repeat(jnp.asarray(SLOPES, F32).reshape(N_HEADS // GROUP, GROUP), BLOCK, axis=1)
    penalty = -(slope_rows[:, :, None] * rel.astype(F32)[None])
    later = jnp.where(band[None], penalty, NEG_INF)
    first = jnp.where((band & (kj >= BLOCK))[None], penalty, NEG_INF)
    return jnp.stack([first, later])


def _attn_blocks_per_step(nb):
    return 2 if nb % 2 == 0 else 1


def _attn_specs(bps):
    return [pl.BlockSpec((bps * BLOCK, Q_COLS), lambda i: (i, 0)),
            pl.BlockSpec((bps * BLOCK, 2 * LANES), lambda i: (i, 2)),
            pl.BlockSpec((BLOCK, 2 * LANES), lambda i: (jnp.maximum(bps * i - 1, 0), 2)),
            _full((2, N_HEADS // GROUP, GROUP_ROWS, 2 * BLOCK))]


def _attn_window(b, kvc_ref, kvp_ref):
    own = kvc_ref[b * BLOCK:(b + 1) * BLOCK, :]
    before = kvp_ref[...] if b == 0 else kvc_ref[(b - 1) * BLOCK:b * BLOCK, :]
    return jnp.concatenate([before, own], axis=0)


def _attn_bias_of(bias_ref, b, kk):
    if b > 0:
        return bias_ref[1, kk]
    return bias_ref[jnp.minimum(pl.program_id(0), 1), kk]


def _attn_keys(kv):
    kv = kv.astype(F32)
    lo = _lo_mask()

    def both_halves(pair):
        rolled = pltpu.roll(pair, HEAD_DIM, 1)
        return [jnp.where(lo, pair, rolled).astype(BF16), jnp.where(lo, rolled, pair).astype(BF16)]

    return both_halves(kv[:, :LANES]), both_halves(kv[:, LANES:])


def _per_head_column(values):
    seg = lax.broadcasted_iota(jnp.int32, (GROUP_ROWS, 1), 0) // BLOCK
    col = jnp.zeros((GROUP_ROWS, 1), F32) + values[GROUP - 1]
    for a in range(GROUP - 2, -1, -1):
        col = jnp.where(seg == a, values[a], col)
    return col


def _stack_heads(ref, rows, kk, lo, dtype):
    parts = []
    for a in range(GROUP):
        h = GROUP * kk + a
        pair = ref[rows, (h // 2) * LANES:(h // 2 + 1) * LANES]
        hm = lo if h % 2 == 0 else jnp.logical_not(lo)
        parts.append(jnp.where(hm, pair, jnp.zeros_like(pair)).astype(dtype))
    return jnp.concatenate(parts, axis=0)


def _group_scores(q_ref, rows, kk, lo, k_dup, bias):
    qs = _stack_heads(q_ref, rows, kk, lo, BF16)
    return _dot_nt(qs, k_dup[kk]) * 0.125 + bias, qs


def _attn_fwd(qkvn, sinks, bias, riders=()):
    t = qkvn.shape[0]
    nb = t // BLOCK
    bps = _attn_blocks_per_step(nb)

    def body(sink_ref, q_ref, kvc_ref, kvp_ref, bias_ref, o_ref, lse_ref):
        lo = _lo_mask()
        lane = lax.broadcasted_iota(jnp.int32, (BLOCK, LANES), 1)
        for b in range(bps):
            rows_b = slice(b * BLOCK, (b + 1) * BLOCK)
            k_dup, v_dup = _attn_keys(_attn_window(b, kvc_ref, kvp_ref))
            lse_t = jnp.zeros((BLOCK, LANES), F32)
            outs = [jnp.zeros((BLOCK, LANES), F32) for _ in range(4)]
            for kk in range(N_HEADS // GROUP):
                s, _ = _group_scores(q_ref, rows_b, kk, lo, k_dup, _attn_bias_of(bias_ref, b, kk))
                sink = _per_head_column([sink_ref[GROUP * kk + a] for a in range(GROUP)])
                m = jnp.maximum(jnp.max(s, axis=-1, keepdims=True), sink)
                pe = jnp.exp(s - m)
                l = jnp.sum(pe, axis=-1, keepdims=True) + jnp.exp(sink - m)
                lse = m + jnp.log(l)
                o = _dot((pe / l).astype(BF16), v_dup[kk])
                for a in range(GROUP):
                    h = GROUP * kk + a
                    rows = slice(a * BLOCK, (a + 1) * BLOCK)
                    hm = lo if h % 2 == 0 else jnp.logical_not(lo)
                    outs[h // 2] = outs[h // 2] + jnp.where(hm, o[rows], 0.0)
                    lse_t = jnp.where(lane == h, lse[rows], lse_t)
            for p in range(4):
                o_ref[rows_b, p * LANES:(p + 1) * LANES] = outs[p].astype(BF16)
            lse_ref[rows_b, :] = lse_t

    blk = lambda w: pl.BlockSpec((bps * BLOCK, w), lambda i: (i, 0))
    return _call(
        body, (sinks, qkvn, qkvn, qkvn, bias), name="attn_fwd", grid=(nb // bps,), riders=riders,
        in_specs=[pl.BlockSpec(memory_space=pltpu.SMEM)] + _attn_specs(bps),
        out_specs=[blk(Q_COLS), blk(LANES)],
        out_shape=[jax.ShapeDtypeStruct((t, Q_COLS), BF16), jax.ShapeDtypeStruct((t, LANES), F32)])


def _glu(pc):
    return pc[:, :CONV_W] * _sig(pc[:, CONV_W:])


def _glu_chunks(pc_ref, scr, tt):
    for r0, cs in _chunks(tt, CONV_W):
        rows = slice(r0, r0 + ROW_CHUNK)
        gate = slice(cs.start + CONV_W, cs.stop + CONV_W)
        scr[HALO + r0:HALO + r0 + ROW_CHUNK, cs] = pc_ref[rows, cs] * _sig(pc_ref[rows, gate])


def _group_norm_stats(seg, lo):
    mu = _half_mean(seg, lo)
    d = seg - mu
    rstd = lax.rsqrt(_half_mean(d * d, lo) + EPS)
    return d * rstd, rstd


def _shifted_copies(src, dst, tt):
    n = tt + HALO - 8
    for s in range(1, 8):
        dst[s - 1, 0:n, :] = src[s:s + n, :]


def _tap_rows(src, shifted, off, r0, cs):
    q, s = divmod(off, 8)
    if s == 0:
        return src[r0 + off:r0 + off + ROW_CHUNK, cs]
    return shifted[s - 1, r0 + 8 * q:r0 + 8 * q + ROW_CHUNK, cs]


def _shift_scratch(tt):
    return pltpu.VMEM((7, tt + HALO - 8, CONV_W), F32)


def _conv_fwd(pc, cw, cb, ln_g, ln_b, riders=()):
    t = pc.shape[0]
    tt = _token_tile(t)

    def body(cur_ref, prev_ref, w_ref, b_ref, g_ref, bb_ref, y_ref, c_ref, scr, shf):
        i = pl.program_id(0)
        scr[0:HALO, :] = _glu(prev_ref[...]) * (i > 0).astype(F32)
        _glu_chunks(cur_ref, scr, tt)
        _shifted_copies(scr, shf, tt)
        lo = _lo_mask()
        for r0, cs in _chunks(tt, CONV_W):
            acc = jnp.zeros((ROW_CHUNK, LANES), F32) + b_ref[:, cs]
            for k in range(CONV_K):
                acc = acc + w_ref[k:k + 1, cs] * _tap_rows(scr, shf, 2 + k, r0, cs)
            y_ref[r0:r0 + ROW_CHUNK, cs] = acc
            yh, _ = _group_norm_stats(acc, lo)
            z = yh * g_ref[:, cs] + bb_ref[:, cs]
            c_ref[r0:r0 + ROW_CHUNK, cs] = (z * _sig(z)).astype(BF16)

    hb = tt // HALO
    return _call(
        body, (pc, pc, cw, cb, ln_g, ln_b), name="conv_fwd", grid=(t // tt,), riders=riders,
        in_specs=[pl.BlockSpec((tt, 2 * CONV_W), lambda i: (i, 0)),
                  pl.BlockSpec((HALO, 2 * CONV_W), lambda i: (jnp.maximum(i * hb - 1, 0), 0)),
                  _full((CONV_K, CONV_W)), _full((1, CONV_W)), _full((1, CONV_W)), _full((1, CONV_W))],
        out_specs=[pl.BlockSpec((tt, CONV_W), lambda i: (i, 0)), pl.BlockSpec((tt, CONV_W), lambda i: (i, 0))],
        out_shape=[jax.ShapeDtypeStruct((t, CONV_W), F32), jax.ShapeDtypeStruct((t, CONV_W), BF16)],
        scratch=[pltpu.VMEM((tt + HALO, CONV_W), F32), _shift_scratch(tt)])


def _out_proj(x, attn, c, w_out, b_out, g_ffn, riders=()):
    t = x.shape[0]
    tt = _token_tile(t)

    def body(x_ref, a_ref, c_ref, w_ref, b_ref, g_ref, x1_ref, h2_ref):
        x1 = x_ref[...] + _dot(a_ref[...], w_ref[0:Q_COLS, :]) + _dot(c_ref[...], w_ref[Q_COLS:, :]) + b_ref[...]
        x1_ref[...] = x1
        r = lax.rsqrt(jnp.mean(x1 * x1, axis=-1, keepdims=True) + EPS)
        h2_ref[...] = (x1 * r * g_ref[...]).astype(BF16)

    row = lambda w: pl.BlockSpec((tt, w), lambda i: (i, 0))
    return _call(
        body, (x, attn, c, w_out, b_out, g_ffn), name="out_proj", grid=(t // tt,), riders=riders,
        in_specs=[row(D_MODEL), row(Q_COLS), row(CONV_W), _full((D_MODEL, D_MODEL)), _full((1, D_MODEL)), _full((1, D_MODEL))],
        out_specs=[row(D_MODEL), row(D_MODEL)],
        out_shape=[jax.ShapeDtypeStruct((t, D_MODEL), F32), jax.ShapeDtypeStruct((t, D_MODEL), BF16)])


def _chunks(rows, cols):
    return [(r0, slice(c0, c0 + LANES)) for c0 in range(0, cols, LANES) for r0 in range(0, rows, ROW_CHUNK)]


def _ffn_up(h2, w_up, dw, db):
    t = h2.shape[0]
    tt = _token_tile(t)
    nj = D_FF // FFN_CB

    def body(hc_ref, hp_ref, wg_ref, wu_ref, dwg_ref, dwu_ref, dbg_ref, dbu_ref,
             hg_ref, hu_ref, upg_ref, upu_ref, act_ref, sg, su):
        i = pl.program_id(1)
        hc = hc_ref[...]
        hp = hp_ref[...] * (i > 0).astype(BF16)
        ups = []
        for w_ref, dw_ref, db_ref, h_ref, up_ref, scr in ((wg_ref, dwg_ref, dbg_ref, hg_ref, upg_ref, sg),
                                                          (wu_ref, dwu_ref, dbu_ref, hu_ref, upu_ref, su)):
            cur = _dot(hc, w_ref[...])
            h_ref[...] = cur.astype(BF16)
            scr[0:FHALO, :] = _dot(hp, w_ref[...])
            scr[FHALO:FHALO + tt, :] = cur
            up = (dw_ref[0:1, :] * scr[FHALO - 2:FHALO - 2 + tt, :] + dw_ref[1:2, :] * scr[FHALO - 1:FHALO - 1 + tt, :]
                  + dw_ref[2:3, :] * cur + db_ref[...])
            up_ref[...] = up
            ups.append(up)
        g, u = ups
        act_ref[...] = (g * _sig(g) * u).astype(BF16)

    fb = tt // FHALO
    colg = lambda r: pl.BlockSpec((r, FFN_CB), lambda j, i: (0, j))
    colu = lambda r: pl.BlockSpec((r, FFN_CB), lambda j, i: (0, j + nj))
    tile = pl.BlockSpec((tt, FFN_CB), lambda j, i: (i, j))
    return pl.pallas_call(
        body, name="ffn_up", grid=(nj, t // tt),
        in_specs=[pl.BlockSpec((tt, D_MODEL), lambda j, i: (i, 0)),
                  pl.BlockSpec((FHALO, D_MODEL), lambda j, i: (jnp.maximum(i * fb - 1, 0), 0)),
                  pl.BlockSpec((None, D_MODEL, FFN_CB), lambda j, i: (j, 0, 0)),
                  pl.BlockSpec((None, D_MODEL, FFN_CB), lambda j, i: (j + nj, 0, 0)),
                  colg(3), colu(3), colg(1), colu(1)],
        out_specs=[tile] * 5,
        out_shape=[jax.ShapeDtypeStruct((t, D_FF), BF16), jax.ShapeDtypeStruct((t, D_FF), BF16),
                   jax.ShapeDtypeStruct((t, D_FF), F32), jax.ShapeDtypeStruct((t, D_FF), F32),
                   jax.ShapeDtypeStruct((t, D_FF), BF16)],
        scratch_shapes=[pltpu.VMEM((tt + FHALO, FFN_CB), F32), pltpu.VMEM((tt + FHALO, FFN_CB), F32)],
        compiler_params=_params(("parallel", "parallel")),
    )(h2, h2, w_up, w_up, dw, dw, db, db)


def _ffn_down(act, w_down, x1, target):
    t = act.shape[0]
    tt = _token_tile(t)

    def body(a_ref, w_ref, x1_ref, t_ref, dy_ref, loss_ref):
        err = x1_ref[...] + _dot(a_ref[...], w_ref[...]) - t_ref[...]
        dy_ref[...] = err * (1.0 / D_MODEL)

        @pl.when(pl.program_id(0) == 0)
        def _():
            loss_ref[...] = jnp.zeros_like(loss_ref)

        loss_ref[...] += jnp.sum(err * err, axis=0, keepdims=True)

    row = lambda w: pl.BlockSpec((tt, w), lambda i: (i, 0))
    return pl.pallas_call(
        body, name="ffn_down", grid=(t // tt,),
        in_specs=[row(D_FF), _full((D_FF, D_MODEL)), row(D_MODEL), row(D_MODEL)],
        out_specs=[row(D_MODEL), _full((1, D_MODEL))],
        out_shape=[jax.ShapeDtypeStruct((t, D_MODEL), F32), jax.ShapeDtypeStruct((1, D_MODEL), F32)],
        compiler_params=_params(("arbitrary",)),
    )(act, w_down, x1, target)


def _ffn_bwd_act(dy, w_down, up_g, up_u):
    t = dy.shape[0]
    tt = _token_tile(t)
    nj = D_FF // FFN_CB

    def body(dy_ref, wd_ref, g_ref, u_ref, dg_ref, du_ref, gbg_ref, gbu_ref):
        i = pl.program_id(1)
        d_act = _dot_nt(dy_ref[...].astype(BF16), wd_ref[...])
        g, u = g_ref[...], u_ref[...]
        s = _sig(g)
        d_u = d_act * (g * s)
        d_g = d_act * u * (s * (1.0 + g * (1.0 - s)))

        @pl.when(i == 0)
        def _():
            for r in (gbg_ref, gbu_ref):
                r[...] = jnp.zeros_like(r)

        for d, o_ref, gb_ref in ((d_g, dg_ref, gbg_ref), (d_u, du_ref, gbu_ref)):
            o_ref[...] = d.astype(BF16)
            gb_ref[...] += jnp.sum(d, axis=0, keepdims=True)

    tile = pl.BlockSpec((tt, FFN_CB), lambda j, i: (i, j))
    acc = pl.BlockSpec((1, FFN_CB), lambda j, i: (0, j))
    return pl.pallas_call(
        body, name="ffn_bwd_act", grid=(nj, t // tt),
        in_specs=[pl.BlockSpec((tt, D_MODEL), lambda j, i: (i, 0)), pl.BlockSpec((FFN_CB, D_MODEL), lambda j, i: (j, 0)),
                  tile, tile],
        out_specs=[tile, tile, acc, acc],
        out_shape=[jax.ShapeDtypeStruct((t, D_FF), BF16), jax.ShapeDtypeStruct((t, D_FF), BF16),
                   jax.ShapeDtypeStruct((1, D_FF), F32), jax.ShapeDtypeStruct((1, D_FF), F32)],
        compiler_params=_params(("parallel", "arbitrary")),
    )(dy, w_down, up_g, up_u)


def _ffn_bwd_conv(dg, du, hg, hu, dw):
    t = dg.shape[0]
    tt = _token_tile(t)
    nj = D_FF // FFN_CB
    ni = t // tt

    def body(gc_ref, gn_ref, uc_ref, un_ref, hg_ref, hu_ref, dwg_ref, dwu_ref, og_ref, ou_ref, gwg_ref, gwu_ref, scr):
        i = pl.program_id(1)
        last = (i < ni - 1).astype(F32)

        @pl.when(i == 0)
        def _():
            gwg_ref[...] = jnp.zeros_like(gwg_ref)
            gwu_ref[...] = jnp.zeros_like(gwu_ref)

        for c_ref, n_ref, h_ref, dw_ref, o_ref, gw_ref in ((gc_ref, gn_ref, hg_ref, dwg_ref, og_ref, gwg_ref),
                                                           (uc_ref, un_ref, hu_ref, dwu_ref, ou_ref, gwu_ref)):
            scr[0:tt, :] = c_ref[...].astype(F32)
            scr[tt:tt + FHALO, :] = n_ref[...].astype(F32) * last
            sums = None
            for r0, cs in _chunks(tt, FFN_CB):
                shifted = [scr[r0 + d:r0 + d + ROW_CHUNK, cs] for d in (2, 1, 0)]
                o_ref[r0:r0 + ROW_CHUNK, cs] = (dw_ref[0:1, cs] * shifted[0] + dw_ref[1:2, cs] * shifted[1]
                                                + dw_ref[2:3, cs] * shifted[2]).astype(BF16)
                hw = h_ref[r0:r0 + ROW_CHUNK, cs].astype(F32)
                prods = [hw * d for d in shifted]
                sums = prods if r0 == 0 else [a + b for a, b in zip(sums, prods)]
                if r0 == tt - ROW_CHUNK:
                    gw_ref[:, cs] += _rows_to_tile([jnp.sum(a, axis=0, keepdims=True) for a in sums], 8)

    fb = tt // FHALO
    tile = pl.BlockSpec((tt, FFN_CB), lambda j, i: (i, j))
    nxt = pl.BlockSpec((FHALO, FFN_CB), lambda j, i: (jnp.minimum((i + 1) * fb, t // FHALO - 1), j))
    acc = pl.BlockSpec((8, FFN_CB), lambda j, i: (0, j))
    return pl.pallas_call(
        body, name="ffn_bwd_conv", grid=(nj, ni),
        in_specs=[tile, nxt, tile, nxt, tile, tile, pl.BlockSpec((3, FFN_CB), lambda j, i: (0, j)),
                  pl.BlockSpec((3, FFN_CB), lambda j, i: (0, j + nj))],
        out_specs=[tile, tile, acc, acc],
        out_shape=[jax.ShapeDtypeStruct((t, D_FF), BF16), jax.ShapeDtypeStruct((t, D_FF), BF16),
                   jax.ShapeDtypeStruct((8, D_FF), F32), jax.ShapeDtypeStruct((8, D_FF), F32)],
        scratch_shapes=[pltpu.VMEM((tt + FHALO, FFN_CB), F32)],
        compiler_params=_params(("parallel", "arbitrary")),
    )(dg, dg, du, du, hg, hu, dw, dw)


def _ffn_bwd_in(dhg, dhu, w_up, x1, dy, g_ffn, riders=()):
    t = x1.shape[0]
    tt = _token_tile(t)

    def body(dg_ref, du_ref, w_ref, x1_ref, dy_ref, g_ref, dx_ref, gg_ref):
        d_h2 = (_dot_nt(dg_ref[:, 0:FFN_CB], w_ref[0]) + _dot_nt(dg_ref[:, FFN_CB:], w_ref[1])
                + _dot_nt(du_ref[:, 0:FFN_CB], w_ref[2]) + _dot_nt(du_ref[:, FFN_CB:], w_ref[3]))
        x1 = x1_ref[...]
        r = lax.rsqrt(jnp.mean(x1 * x1, axis=-1, keepdims=True) + EPS)
        xh = x1 * r
        gd = d_h2 * g_ref[...]
        dx_ref[...] = dy_ref[...] + r * (gd - xh * jnp.mean(gd * xh, axis=-1, keepdims=True))

        @pl.when(pl.program_id(0) == 0)
        def _():
            gg_ref[...] = jnp.zeros_like(gg_ref)

        gg_ref[...] += jnp.sum(d_h2 * xh, axis=0, keepdims=True)

    row = lambda w: pl.BlockSpec((tt, w), lambda i: (i, 0))
    return _call(
        body, (dhg, dhu, w_up, x1, dy, g_ffn), name="ffn_bwd_in", grid=(t // tt,), riders=riders,
        in_specs=[row(D_FF), row(D_FF), _full((4, D_MODEL, FFN_CB)), row(D_MODEL), row(D_MODEL), _full((1, D_MODEL))],
        out_specs=[row(D_MODEL), _full((1, D_MODEL))],
        out_shape=[jax.ShapeDtypeStruct((t, D_MODEL), F32), jax.ShapeDtypeStruct((1, D_MODEL), F32)])


def _grad_weight(a, b, nj, mb, name, lead=None, into=None, offset=0):
    t, m = a.shape
    n = b.shape[1]
    nb_ = n // nj
    tt = GRAD_TILE if t % GRAD_TILE == 0 else _token_tile(t)

    def body(*refs):
        a_ref, b_ref, o_ref = refs[0], refs[1], refs[-1]

        @pl.when(pl.program_id(2) == 0)
        def _():
            o_ref[...] = jnp.zeros_like(o_ref)

        o_ref[0] += _dot_tn(a_ref[...].astype(BF16), b_ref[...].astype(BF16))

    in_specs = [pl.BlockSpec((tt, mb), lambda j, mi, i: (i, mi)), pl.BlockSpec((tt, nb_), lambda j, mi, i: (i, j))]
    out_shape = jax.ShapeDtypeStruct((lead or nj, m, nb_) if into is None else into.shape, F32)
    return pl.pallas_call(
        body, name=name, grid=(nj, m // mb, t // tt),
        in_specs=in_specs if into is None else in_specs + [pl.BlockSpec(memory_space=pl.ANY)],
        out_specs=pl.BlockSpec((1, mb, nb_), lambda j, mi, i: (j + offset, mi, 0)),
        out_shape=out_shape,
        input_output_aliases={} if into is None else {2: 0},
        compiler_params=_params(("parallel", "parallel", "arbitrary")),
    )(*((a, b) if into is None else (a, b, into)))


def _out_proj_bwd(dx1, w_out):
    t = dx1.shape[0]
    tt = _token_tile(t)

    def body(d_ref, w_ref, dm_ref, gb_ref):
        d = d_ref[...]
        dm_ref[...] = _dot_nt(d.astype(BF16), w_ref[...])

        @pl.when(pl.program_id(0) == 0)
        def _():
            gb_ref[...] = jnp.zeros_like(gb_ref)

        gb_ref[...] += jnp.sum(d, axis=0, keepdims=True)

    row = pl.BlockSpec((tt, D_MODEL), lambda i: (i, 0))
    return pl.pallas_call(
        body, name="out_proj_bwd", grid=(t // tt,),
        in_specs=[row, _full((D_MODEL, D_MODEL))],
        out_specs=[row, _full((1, D_MODEL))],
        out_shape=[jax.ShapeDtypeStruct((t, D_MODEL), F32), jax.ShapeDtypeStruct((1, D_MODEL), F32)],
        compiler_params=_params(("arbitrary",)),
    )(dx1, w_out)


def _conv_bwd(dmix, y, pc, cw, ln_g, ln_b, riders=()):
    t = y.shape[0]
    tt = _token_tile(t)
    ni = t // tt
    ncb = CONV_W // LANES

    def body(dc_ref, dcn_ref, y_ref, yn_ref, pc_ref, pcp_ref, w_ref, g_ref, bb_ref,
             dp_ref, gw_ref, gb_ref, gg_ref, gbb_ref, gbin_ref, scr_d, scr_c, scr_o, shf_d, shf_c):
        i = pl.program_id(0)
        lo = _lo_mask()

        @pl.when(i == 0)
        def _():
            for r in (gw_ref, gb_ref, gg_ref, gbb_ref, gbin_ref):
                r[...] = jnp.zeros_like(r)

        def norm_bwd(dc, yv, cs):
            yh, rstd = _group_norm_stats(yv, lo)
            z = yh * g_ref[:, cs] + bb_ref[:, cs]
            s = _sig(z)
            dz = dc * (s * (1.0 + z * (1.0 - s)))
            dyh = dz * g_ref[:, cs]
            d_y = rstd * (dyh - _half_mean(dyh, lo) - yh * _half_mean(dyh * yh, lo))
            return d_y, dz, yh

        for p in range(ncb):
            cs = slice(p * LANES, (p + 1) * LANES)
            d_y, dz, yh = norm_bwd(dc_ref[:, cs], y_ref[:, cs], cs)
            scr_d[0:tt, cs] = d_y
            gg_ref[:, cs] += jnp.sum(dz * yh, axis=0, keepdims=True)
            gbb_ref[:, cs] += jnp.sum(dz, axis=0, keepdims=True)
            gb_ref[:, cs] += jnp.sum(d_y, axis=0, keepdims=True)
            d_yn, _, _ = norm_bwd(dcn_ref[:, cs], yn_ref[:, cs], cs)
            scr_d[tt:tt + HALO, cs] = d_yn * (i < ni - 1).astype(F32)
        scr_c[0:HALO, :] = _glu(pcp_ref[...]) * (i > 0).astype(F32)
        scr_c[HALO:HALO + tt, :] = _glu(pc_ref[...])
        _shifted_copies(scr_d, shf_d, tt)
        _shifted_copies(scr_c, shf_c, tt)

        rid = lax.broadcasted_iota(jnp.int32, (HALO, LANES), 0)
        for cbk in range(ncb):
            cs = slice(cbk * LANES, (cbk + 1) * LANES)
            for rb in range(tt // ROW_CHUNK):
                r0 = rb * ROW_CHUNK
                acc = jnp.zeros((ROW_CHUNK, LANES), F32)
                for k in range(CONV_K):
                    acc = acc + w_ref[k:k + 1, cs] * _tap_rows(scr_d, shf_d, 30 - k, r0, cs)
                scr_o[r0:r0 + ROW_CHUNK, cs] = acc
            gwt = jnp.zeros((HALO, LANES), F32)
            for k in range(CONV_K):
                acc = jnp.zeros((ROW_CHUNK, LANES), F32)
                for rb in range(tt // ROW_CHUNK):
                    r0 = rb * ROW_CHUNK
                    acc = acc + scr_d[r0:r0 + ROW_CHUNK, cs] * _tap_rows(scr_c, shf_c, 2 + k, r0, cs)
                gwt = jnp.where(rid == k, jnp.sum(acc, axis=0, keepdims=True), gwt)
            gw_ref[:, cs] += gwt
        d_c0 = scr_o[...]
        a = pc_ref[:, 0:CONV_W]
        s = _sig(pc_ref[:, CONV_W:])
        d_a = d_c0 * s
        d_gate = d_c0 * a * s * (1.0 - s)
        dp_ref[:, 0:CONV_W] = d_a.astype(BF16)
        dp_ref[:, CONV_W:] = d_gate.astype(BF16)
        gbin_ref[:, 0:CONV_W] += jnp.sum(d_a, axis=0, keepdims=True)
        gbin_ref[:, CONV_W:] += jnp.sum(d_gate, axis=0, keepdims=True)

    hb = tt // HALO
    nxt = lambda col: pl.BlockSpec((HALO, CONV_W), lambda i: (jnp.minimum((i + 1) * hb, t // HALO - 1), col))
    return _call(
        body, (dmix, dmix, y, y, pc, pc, cw, ln_g, ln_b), name="conv_bwd", grid=(ni,), riders=riders,
        in_specs=[pl.BlockSpec((tt, CONV_W), lambda i: (i, 1)), nxt(1),
                  pl.BlockSpec((tt, CONV_W), lambda i: (i, 0)), nxt(0),
                  pl.BlockSpec((tt, 2 * CONV_W), lambda i: (i, 0)),
                  pl.BlockSpec((HALO, 2 * CONV_W), lambda i: (jnp.maximum(i * hb - 1, 0), 0)),
                  _full((CONV_K, CONV_W)), _full((1, CONV_W)), _full((1, CONV_W))],
        out_specs=[pl.BlockSpec((tt, 2 * CONV_W), lambda i: (i, 0)), _full((HALO, CONV_W)), _full((1, CONV_W)),
                   _full((1, CONV_W)), _full((1, CONV_W)), _full((1, 2 * CONV_W))],
        out_shape=[jax.ShapeDtypeStruct((t, 2 * CONV_W), BF16), jax.ShapeDtypeStruct((HALO, CONV_W), F32),
                   jax.ShapeDtypeStruct((1, CONV_W), F32), jax.ShapeDtypeStruct((1, CONV_W), F32),
                   jax.ShapeDtypeStruct((1, CONV_W), F32), jax.ShapeDtypeStruct((1, 2 * CONV_W), F32)],
        scratch=[pltpu.VMEM((tt + HALO, CONV_W), F32), pltpu.VMEM((tt + HALO, CONV_W), F32),
                 pltpu.VMEM((tt, CONV_W), F32), _shift_scratch(tt), _shift_scratch(tt)])


def _attn_bwd(qkvn, dmix, lse, sinks, bias, riders=()):
    t = qkvn.shape[0]
    nb = t // BLOCK
    bps = _attn_blocks_per_step(nb)

    def body(sink_ref, q_ref, kvc_ref, kvp_ref, bias_ref, do_ref, lse_ref, dq_ref, dcur_ref, dprev_ref, ds_ref):
        i = pl.program_id(0)
        lo = _lo_mask()
        lane1 = lax.broadcasted_iota(jnp.int32, (1, LANES), 1)
        lane = lax.broadcasted_iota(jnp.int32, (BLOCK, LANES), 1)
        dsink = jnp.zeros((1, LANES), F32)
        for b in range(bps):
            rows_b = slice(b * BLOCK, (b + 1) * BLOCK)
            k_dup, v_dup = _attn_keys(_attn_window(b, kvc_ref, kvp_ref))
            lse_t = lse_ref[rows_b, :]
            dqs = [jnp.zeros((BLOCK, LANES), F32) for _ in range(4)]
            dkv = []
            for kk in range(N_HEADS // GROUP):
                s, qs = _group_scores(q_ref, rows_b, kk, lo, k_dup, _attn_bias_of(bias_ref, b, kk))
                lse = jnp.concatenate([jnp.sum(jnp.where(lane == GROUP * kk + a, lse_t, 0.0), axis=-1, keepdims=True)
                                       for a in range(GROUP)], axis=0)
                prob = jnp.exp(s - lse)
                dos = _stack_heads(do_ref, rows_b, kk, lo, BF16)
                dp = _dot_nt(dos, v_dup[kk])
                dsum = jnp.sum(prob * dp, axis=-1, keepdims=True)
                dsb = (prob * (dp - dsum) * 0.125).astype(BF16)
                sink = _per_head_column([sink_ref[GROUP * kk + a] for a in range(GROUP)])
                dsk = -jnp.exp(sink - lse) * dsum
                dq = _dot(dsb, k_dup[kk])
                for a in range(GROUP):
                    h = GROUP * kk + a
                    rows = slice(a * BLOCK, (a + 1) * BLOCK)
                    hm = lo if h % 2 == 0 else jnp.logical_not(lo)
                    dqs[h // 2] = dqs[h // 2] + jnp.where(hm, dq[rows], 0.0)
                    dsink = dsink + jnp.where(lane1 == h, jnp.sum(dsk[rows], axis=0, keepdims=True), 0.0)
                dk_x = _dot_tn(dsb, qs)
                dv_x = _dot_tn(prob.astype(BF16), dos)
                dkv.append((dk_x + pltpu.roll(dk_x, HEAD_DIM, 1), dv_x + pltpu.roll(dv_x, HEAD_DIM, 1)))
            for p in range(4):
                dq_ref[rows_b, p * LANES:(p + 1) * LANES] = dqs[p]
            dk = jnp.where(lo, dkv[0][0], dkv[1][0])
            dv = jnp.where(lo, dkv[0][1], dkv[1][1])
            dprev_ref[rows_b, 0:LANES] = dk[0:BLOCK]
            dprev_ref[rows_b, LANES:] = dv[0:BLOCK]
            dcur_ref[rows_b, 0:LANES] = dk[BLOCK:]
            dcur_ref[rows_b, LANES:] = dv[BLOCK:]

        @pl.when(i == 0)
        def _():
            ds_ref[...] = jnp.zeros_like(ds_ref)

        ds_ref[...] += dsink

    blk = lambda w: pl.BlockSpec((bps * BLOCK, w), lambda i: (i, 0))
    return _call(
        body, (sinks, qkvn, qkvn, qkvn, bias, dmix, lse), name="attn_bwd", grid=(nb // bps,), riders=riders,
        in_specs=[pl.BlockSpec(memory_space=pltpu.SMEM)] + _attn_specs(bps) + [blk(Q_COLS), blk(LANES)],
        out_specs=[blk(Q_COLS), blk(2 * LANES), blk(2 * LANES), _full((1, LANES))],
        out_shape=[jax.ShapeDtypeStruct((t, Q_COLS), F32), jax.ShapeDtypeStruct((t, 2 * LANES), F32),
                   jax.ShapeDtypeStruct((t, 2 * LANES), F32), jax.ShapeDtypeStruct((1, LANES), F32)])


def _qk_norm_bwd(dqn, dcur, dprev, pq, gq2, gk2):
    t = dqn.shape[0]
    nb = t // BLOCK
    tt = _token_tile(t)
    ni = t // tt

    def body(dq_ref, dc_ref, dt_ref, dn_ref, pq_ref, gq_ref, gk_ref, dp_ref, gbin_ref, gg_ref):
        i = pl.program_id(0)
        lo = _lo_mask()
        nxt = dn_ref[...] * (i < ni - 1).astype(F32)
        from_next = nxt if tt == BLOCK else jnp.concatenate([dt_ref[BLOCK:, :], nxt], axis=0)
        dkv = dc_ref[...] + from_next

        @pl.when(i == 0)
        def _():
            gbin_ref[...] = jnp.zeros_like(gbin_ref)
            gg_ref[...] = jnp.zeros_like(gg_ref)

        for p in range(5):
            cs = slice(p * LANES, (p + 1) * LANES)
            seg = pq_ref[:, cs]
            dn = dq_ref[:, cs] if p < 4 else dkv[:, 0:LANES]
            gain = gq_ref[...] if p < 4 else gk_ref[...]
            rr = lax.rsqrt(_half_mean(seg * seg, lo) + EPS)
            xh = seg * rr
            gd = dn * gain
            d = rr * (gd - xh * _half_mean(gd * xh, lo))
            dp_ref[:, cs] = d.astype(BF16)
            gbin_ref[:, cs] += jnp.sum(d, axis=0, keepdims=True)
            gg_ref[:, cs] += jnp.sum(dn * xh, axis=0, keepdims=True)
        dv = dkv[:, LANES:]
        dp_ref[:, 640:768] = dv.astype(BF16)
        gbin_ref[:, 640:768] += jnp.sum(dv, axis=0, keepdims=True)

    blk = lambda w: pl.BlockSpec((tt, w), lambda i: (i, 0))
    per = tt // BLOCK
    return pl.pallas_call(
        body, name="qk_norm_bwd", grid=(ni,),
        in_specs=[blk(Q_COLS), blk(2 * LANES), blk(2 * LANES),
                  pl.BlockSpec((BLOCK, 2 * LANES), lambda i: (jnp.minimum((i + 1) * per, nb - 1), 0)),
                  blk(QKV_COLS), _full((1, LANES)), _full((1, LANES))],
        out_specs=[blk(QKV_COLS), _full((1, QKV_COLS)), _full((1, 5 * LANES))],
        out_shape=[jax.ShapeDtypeStruct((t, QKV_COLS), BF16), jax.ShapeDtypeStruct((1, QKV_COLS), F32),
                   jax.ShapeDtypeStruct((1, 5 * LANES), F32)],
        compiler_params=_params(("arbitrary",)),
    )(dqn, dcur, dprev, dprev, pq, gq2, gk2)


def _in_proj_bwd(dpq, dpc, w_in, x, dx1, g_mix):
    t = x.shape[0]
    tt = _token_tile(t)

    def body(dq_ref, dc_ref, w_ref, x_ref, d1_ref, g_ref, gx_ref, gg_ref):
        d_h = _dot_nt(dq_ref[...], w_ref[:, 0:QKV_COLS]) + _dot_nt(dc_ref[...], w_ref[:, QKV_COLS:])
        xv = x_ref[...]
        r = lax.rsqrt(jnp.mean(xv * xv, axis=-1, keepdims=True) + EPS)
        xh = xv * r
        gd = d_h * g_ref[...]
        gx_ref[...] = d1_ref[...] + r * (gd - xh * jnp.mean(gd * xh, axis=-1, keepdims=True))

        @pl.when(pl.program_id(0) == 0)
        def _():
            gg_ref[...] = jnp.zeros_like(gg_ref)

        gg_ref[...] += jnp.sum(d_h * xh, axis=0, keepdims=True)

    row = lambda w: pl.BlockSpec((tt, w), lambda i: (i, 0))
    return pl.pallas_call(
        body, name="in_proj_bwd", grid=(t // tt,),
        in_specs=[row(QKV_COLS), row(2 * CONV_W), _full((D_MODEL, IN_COLS)), row(D_MODEL), row(D_MODEL), _full((1, D_MODEL))],
        out_specs=[row(D_MODEL), _full((1, D_MODEL))],
        out_shape=[jax.ShapeDtypeStruct((t, D_MODEL), F32), jax.ShapeDtypeStruct((1, D_MODEL), F32)],
        compiler_params=_params(("arbitrary",)),
    )(dpq, dpc, w_in, x, dx1, g_mix)


def _row_block(r):
    if r <= 256:
        return r
    return max(b for b in range(8, 257, 8) if r % b == 0)


def _adamw(w, g, m, v, name):
    r, c = w.shape
    rb = _row_block(r)

    def body(w_ref, g_ref, m_ref, v_ref, d_ref, nm_ref, nv_ref):
        gv = g_ref[...]
        nm = ADAM_B1 * m_ref[...] + (1.0 - ADAM_B1) * gv
        nv = ADAM_B2 * v_ref[...] + (1.0 - ADAM_B2) * (gv * gv)
        m_hat = nm / (1.0 - ADAM_B1 ** ADAM_STEP)
        v_hat = nv / (1.0 - ADAM_B2 ** ADAM_STEP)
        d_ref[...] = -ADAM_LR * (m_hat / (jnp.sqrt(v_hat) + ADAM_EPS) + ADAM_WD * w_ref[...])
        nm_ref[...] = nm
        nv_ref[...] = nv

    blk = pl.BlockSpec((rb, c), lambda i: (i, 0))
    shp = jax.ShapeDtypeStruct((r, c), F32)
    return pl.pallas_call(
        body, name=name, grid=(r // rb,), in_specs=[blk] * 4, out_specs=[blk] * 3, out_shape=[shp] * 3,
        compiler_params=_params(("parallel",)),
    )(w, g, m, v)


def _place():
    x, y, c = lax.axis_index("x"), lax.axis_index("y"), lax.axis_index("c")
    chips = [(1 - x, y), (x, 1 - y), (1 - x, 1 - y)]
    return x, y, c, chips


def _gather_all(v):
    r = v.shape[0]

    def body(v_ref, all_ref, sum_ref, send_sems, recv_sems):
        x, y, c, _ = _place()
        me = 4 * x + 2 * y + c
        all_ref[me] = v_ref[...]
        copies = []
        for k in range(1, 8):
            kx, ky, kc = (k >> 2) & 1, (k >> 1) & 1, k & 1
            peer = (x ^ kx, y ^ ky, c ^ kc)
            cp = pltpu.make_async_remote_copy(src_ref=v_ref, dst_ref=all_ref.at[me], send_sem=send_sems.at[k - 1],
                                              recv_sem=recv_sems.at[k - 1], device_id=peer, device_id_type=MESH)
            cp.start()
            copies.append((cp, 4 * peer[0] + 2 * peer[1] + peer[2]))
        for k, (cp, src_idx) in enumerate(copies):
            pltpu.make_async_remote_copy(src_ref=v_ref, dst_ref=all_ref.at[src_idx], send_sem=send_sems.at[k],
                                         recv_sem=recv_sems.at[k], device_id=(x, y, c), device_id_type=MESH).wait_recv()
        for cp, _ in copies:
            cp.wait_send()
        tot = all_ref[0]
        for d in range(1, 8):
            tot = tot + all_ref[d]
        sum_ref[...] = tot

    vm = pl.BlockSpec(memory_space=pltpu.VMEM)
    return pl.pallas_call(
        body, name="gather_all", in_specs=[vm], out_specs=[vm, vm],
        out_shape=[jax.ShapeDtypeStruct((8, r, LANES), v.dtype), jax.ShapeDtypeStruct((r, LANES), v.dtype)],
        scratch_shapes=[pltpu.SemaphoreType.DMA((7,)), pltpu.SemaphoreType.DMA((7,))],
        compiler_params=pltpu.CompilerParams(vmem_limit_bytes=VMEM_LIMIT),
    )(v)


def _remote(src, dst, send_sem, recv_sem, to):
    return pltpu.make_async_remote_copy(src_ref=src, dst_ref=dst, send_sem=send_sem, recv_sem=recv_sem,
                                        device_id=to, device_id_type=MESH)


def _dma_sems(*shape):
    return pltpu.SemaphoreType.DMA(shape)


def _gather_first(shards):
    n = len(shards)

    def copies(ins, outs, sems):
        x, y, c, chips = _place()
        me = 2 * x + y
        local = [pltpu.make_async_copy(ins[a], outs[a].at[me], sems[2].at[a]) for a in range(n)]
        sends = [_remote(ins[a].at[c], outs[a].at[me, c], sems[0].at[a, j], sems[1].at[a, j], (*chip, c))
                 for a in range(n) for j, chip in enumerate(chips)]
        lands = [_remote(ins[a].at[c], outs[a].at[2 * chip[0] + chip[1], c], sems[0].at[a, j], sems[1].at[a, j], (x, y, c))
                 for a in range(n) for j, chip in enumerate(chips)]
        return local, sends, lands

    def start(ins, outs, sems):
        local, sends, _ = copies(ins, outs, sems)
        for cp in local + sends:
            cp.start()

    def finish(ins, outs, sems):
        local, sends, lands = copies(ins, outs, sems)
        for cp in lands:
            cp.wait_recv()
        for cp in sends:
            cp.wait_send()
        for cp in local:
            cp.wait()

    return _Rider(shards, [jax.ShapeDtypeStruct((4,) + s.shape, s.dtype) for s in shards],
                  [_dma_sems(n, 3), _dma_sems(n, 3), _dma_sems(n)], start, finish)


def _gather_second(partials):
    n = len(partials)

    def copies(outs, sems):
        x, y, c, chips = _place()
        sends, lands = [], []
        for a in range(n):
            for j, chip in enumerate(chips):
                mine = outs[a].at[2 * chip[0] + chip[1], c]
                theirs = outs[a].at[2 * chip[0] + chip[1], 1 - c]
                sends.append(_remote(mine, mine, sems[0].at[a, j], sems[1].at[a, j], (x, y, 1 - c)))
                lands.append(_remote(theirs, theirs, sems[0].at[a, j], sems[1].at[a, j], (x, y, c)))
        return sends, lands

    def start(ins, outs, sems):
        for cp in copies(outs, sems)[0]:
            cp.start()

    def finish(ins, outs, sems):
        sends, lands = copies(outs, sems)
        for cp in lands:
            cp.wait_recv()
        for cp in sends:
            cp.wait_send()

    return _Rider(partials, [jax.ShapeDtypeStruct(p.shape, p.dtype) for p in partials],
                  [_dma_sems(n, 3), _dma_sems(n, 3)], start, finish, aliases={a: a for a in range(n)})


def _swap_halves(grads):
    n = len(grads)

    def copies(ins, outs, sems):
        x, y, c, _ = _place()
        return [_remote(ins[a].at[j, 1 - c], outs[a].at[j], sems[0].at[a, j], sems[1].at[a, j], (x, y, 1 - c))
                for a in range(n) for j in range(4)]

    def start(ins, outs, sems):
        for cp in copies(ins, outs, sems):
            cp.start()

    def finish(ins, outs, sems):
        for cp in copies(ins, outs, sems):
            cp.wait()

    return _Rider(grads, [jax.ShapeDtypeStruct((4,) + g.shape[2:], g.dtype) for g in grads],
                  [_dma_sems(n, 4), _dma_sems(n, 4)], start, finish)


def _add_sibling(g, got, c_idx, name):
    _, _, h, c = g.shape

    def body(s_ref, a_ref, b_ref, o_ref):
        o_ref[...] = (a_ref[...] + b_ref[...]).astype(BF16)

    return pl.pallas_call(
        body, name=name,
        grid_spec=pltpu.PrefetchScalarGridSpec(
            num_scalar_prefetch=1, grid=(4,),
            in_specs=[pl.BlockSpec((None, None, h, c), lambda j, s: (j, s[0], 0, 0)),
                      pl.BlockSpec((None, h, c), lambda j, s: (j, 0, 0))],
            out_specs=pl.BlockSpec((None, h, c), lambda j, s: (j, 0, 0))),
        out_shape=jax.ShapeDtypeStruct((4, h, c), BF16),
        compiler_params=_params(("parallel",)),
    )(c_idx, g, got)


def _exchange_chips(parts):
    n = len(parts)

    def copies(ins, outs, sems):
        x, y, c, chips = _place()
        return [_remote(ins[a].at[2 * chip[0] + chip[1]], outs[a].at[j], sems[0].at[a, j], sems[1].at[a, j], (*chip, c))
                for a in range(n) for j, chip in enumerate(chips)]

    def start(ins, outs, sems):
        for cp in copies(ins, outs, sems):
            cp.start()

    def finish(ins, outs, sems):
        for cp in copies(ins, outs, sems):
            cp.wait()

    return _Rider(parts, [jax.ShapeDtypeStruct((3,) + p.shape[1:], p.dtype) for p in parts],
                  [_dma_sems(n, 3), _dma_sems(n, 3)], start, finish)


def _add_chips(part, got, chip_idx, name):
    _, h, c = part.shape

    def body(s_ref, a_ref, b_ref, o_ref):
        o_ref[...] = ((a_ref[...].astype(F32) + b_ref[0].astype(F32)) + b_ref[1].astype(F32)) + b_ref[2].astype(F32)

    return pl.pallas_call(
        body, name=name,
        grid_spec=pltpu.PrefetchScalarGridSpec(
            num_scalar_prefetch=1, grid=(1,),
            in_specs=[pl.BlockSpec((None, h, c), lambda i, s: (s[0], 0, 0)),
                      pl.BlockSpec((3, h, c), lambda i, s: (0, 0, 0))],
            out_specs=pl.BlockSpec((h, c), lambda i, s: (0, 0))),
        out_shape=jax.ShapeDtypeStruct((h, c), F32),
        compiler_params=_params(("arbitrary",)),
    )(chip_idx, part, got)


def _join_halves(halves):
    n = len(halves)

    def copies(ins, outs, sems):
        x, y, c, _ = _place()
        local = [pltpu.make_async_copy(ins[a], outs[a].at[c], sems[2].at[a]) for a in range(n)]
        sends = [_remote(ins[a], outs[a].at[c], sems[0].at[a], sems[1].at[a], (x, y, 1 - c)) for a in range(n)]
        lands = [_remote(ins[a], outs[a].at[1 - c], sems[0].at[a], sems[1].at[a], (x, y, c)) for a in range(n)]
        return local, sends, lands

    def start(ins, outs, sems):
        local, sends, _ = copies(ins, outs, sems)
        for cp in local + sends:
            cp.start()

    def finish(ins, outs, sems):
        local, sends, lands = copies(ins, outs, sems)
        for cp in lands:
            cp.wait_recv()
        for cp in sends:
            cp.wait_send()
        for cp in local:
            cp.wait()

    return _Rider(halves, [jax.ShapeDtypeStruct((2,) + h.shape, h.dtype) for h in halves],
                  [_dma_sems(n), _dma_sems(n), _dma_sems(n)], start, finish)


def _pack(parts):
    flat = []
    for p in parts:
        p = p.reshape(-1).astype(F32)
        flat.append(jnp.pad(p, (0, (-p.shape[0]) % LANES)))
    v = jnp.concatenate(flat)
    v = jnp.pad(v, (0, (-v.shape[0]) % (8 * LANES)))
    return v.reshape(-1, LANES)


def _unpack(v, shapes):
    flat = v.reshape(-1)
    out, off = [], 0
    for s in shapes:
        n = 1
        for d in s:
            n *= d
        out.append(flat[off:off + n].reshape(s))
        off += n + (-n) % LANES
    return out


def kernel(x, mix_norm_gain, w_in, b_in, q_norm_gain, k_norm_gain, attn_sinks, conv_dw_w, conv_dw_b, conv_norm_gain, conv_norm_bias, w_out, b_out, ffn_norm_gain, w_up, ffn_dw_w, ffn_dw_b, w_down, loss_target, m_mix_norm_gain, m_w_in, m_b_in, m_q_norm_gain, m_k_norm_gain, m_attn_sinks, m_conv_dw_w, m_conv_dw_b, m_conv_norm_gain, m_conv_norm_bias, m_w_out, m_b_out, m_ffn_norm_gain, m_w_up, m_ffn_dw_w, m_ffn_dw_b, m_w_down, v_mix_norm_gain, v_w_in, v_b_in, v_q_norm_gain, v_k_norm_gain, v_attn_sinks, v_conv_dw_w, v_conv_dw_b, v_conv_norm_gain, v_conv_norm_bias, v_w_out, v_b_out, v_ffn_norm_gain, v_w_up, v_ffn_dw_w, v_ffn_dw_b, v_w_down):
    t = x.shape[1]
    xi, yi, ci = lax.axis_index("x"), lax.axis_index("y"), lax.axis_index("c")
    chip = 2 * xi + yi
    c_idx = jnp.reshape(ci, (1,)).astype(jnp.int32)
    chip_idx = jnp.reshape(chip, (1,)).astype(jnp.int32)
    x2 = x.reshape(t, D_MODEL)
    tgt = loss_target.reshape(t, D_MODEL)

    big = [w_in, w_out, w_up, w_down]
    halves = [w.astype(BF16).reshape(2, w.shape[0] // 2, w.shape[1]) for w in big]
    h_wi, h_wo, h_wu, h_wd = halves
    (p_wi,) = _run_riders([_gather_first([h_wi])], "gather_w_in_first")
    (g_wi,) = _run_riders([_gather_second([p_wi])], "gather_w_in_second")
    wi = jnp.concatenate([g_wi[j].reshape(D_MODEL, IN_COLS // 4) for j in range(4)], axis=1)
    small_w, _ = _gather_all(_pack([conv_dw_w, ffn_dw_w]))
    per_chip = [_unpack(small_w[4 * (j // 2) + 2 * (j % 2)], [conv_dw_w.shape, ffn_dw_w.shape]) for j in range(4)]
    cw = jnp.concatenate([p[0] for p in per_chip], axis=1)
    fw = jnp.concatenate([p[1] for p in per_chip], axis=1)

    row = lambda a: a.reshape(1, -1)
    gq2 = row(jnp.concatenate([q_norm_gain, q_norm_gain]))
    gk2 = row(jnp.concatenate([k_norm_gain, k_norm_gain]))

    (h1, pq, pc, qkvn), (p_wo,) = _fwd_in(x2, row(mix_norm_gain), wi, row(b_in), gq2, gk2, riders=[_gather_first([h_wo])])
    bias = _attn_bias()
    (attn, lse), (g_wo, p_wu) = _attn_fwd(qkvn, attn_sinks, bias, riders=[_gather_second([p_wo]), _gather_first([h_wu])])
    (y_conv, c_act), (g_wu, p_wd) = _conv_fwd(pc, cw, row(conv_dw_b), row(conv_norm_gain), row(conv_norm_bias),
                                              riders=[_gather_second([p_wu]), _gather_first([h_wd])])
    wo = g_wo.reshape(D_MODEL, D_MODEL)
    (x1, h2), (g_wd,) = _out_proj(x2, attn, c_act, wo, row(b_out), row(ffn_norm_gain), riders=[_gather_second([p_wd])])
    wu = g_wu.reshape(4, D_MODEL, FFN_CB)
    wd = g_wd.reshape(D_FF, D_MODEL)
    hg, hu, up_g, up_u, act = _ffn_up(h2, wu, fw, row(ffn_dw_b))
    dy, loss_cols = _ffn_down(act, wd, x1, tgt)

    split = lambda g: g.reshape(4, 2, g.shape[1] // 2, g.shape[2])
    dg, du, gfb_g, gfb_u = _ffn_bwd_act(dy, wd, up_g, up_u)
    dhg, dhu, gfw_g, gfw_u = _ffn_bwd_conv(dg, du, hg, hu, fw)
    gw_down = _grad_weight(act, dy, 1, FFN_CB, "grad_w_down")
    gw_up = _grad_weight(h2, dhg, 2, D_MODEL, "grad_w_up_gate", lead=4)
    gw_up = _grad_weight(h2, dhu, 2, D_MODEL, "grad_w_up_lin", into=gw_up, offset=2)
    early = [split(gw_up), split(gw_down.reshape(4, D_FF // 4, D_MODEL))]
    early_names = ["w_up", "w_down"]
    (dx1, g_ffn_gain), got = _ffn_bwd_in(dhg, dhu, wu, x1, dy, row(ffn_norm_gain), riders=[_swap_halves(early)])
    early_part = [_add_sibling(g, r, c_idx, "add_sibling_" + nm_) for g, r, nm_ in zip(early, got, early_names)]
    dmix, g_b_out = _out_proj_bwd(dx1, wo)
    gw_out = jnp.concatenate([_grad_weight(attn, dx1, 1, Q_COLS, "grad_w_out_attn")[0],
                              _grad_weight(c_act, dx1, 1, CONV_W, "grad_w_out_conv")[0]], axis=0)
    (dpc, g_cw, g_cb, g_lng, g_lnb, gbin_c), got2 = _conv_bwd(dmix, y_conv, pc, cw, row(conv_norm_gain), row(conv_norm_bias),
                                                              riders=[_exchange_chips(early_part)])
    early_red = [_add_chips(p, r, chip_idx, "add_chips_" + nm_) for p, r, nm_ in zip(early_part, got2, early_names)]
    (dqn, dcur, dprev, g_sink), early_g = _attn_bwd(qkvn, dmix, lse, attn_sinks, bias, riders=[_join_halves(early_red)])
    dpq, gbin_q, g_qk = _qk_norm_bwd(dqn, dcur, dprev, pq, gq2, gk2)
    grad_x, g_mix_gain = _in_proj_bwd(dpq, dpc, wi, x2, dx1, row(mix_norm_gain))
    gw_in = jnp.concatenate([_grad_weight(h1, dpq, 1, D_MODEL, "grad_w_in_qkv")[0],
                             _grad_weight(h1, dpc, 1, D_MODEL, "grad_w_in_conv")[0]], axis=1)

    g_qk = g_qk.reshape(5, 2, HEAD_DIM)
    small = [g_mix_gain, jnp.concatenate([gbin_q, gbin_c], axis=1), g_qk[:4].sum(axis=(0, 1)), g_qk[4].sum(axis=0),
             g_sink[0, :N_HEADS], g_cb, g_lng, g_lnb, g_b_out, g_ffn_gain, jnp.concatenate([gfb_g, gfb_u], axis=1),
             loss_cols, g_cw[:CONV_K], jnp.concatenate([gfw_g[:3], gfw_u[:3]], axis=1)]
    _, tot = _gather_all(_pack(small))
    rep_names = ["mix_norm_gain", "b_in", "q_norm_gain", "k_norm_gain", "attn_sinks", "conv_dw_b", "conv_norm_gain",
                 "conv_norm_bias", "b_out", "ffn_norm_gain", "ffn_dw_b"]
    rep_w = [mix_norm_gain, b_in, q_norm_gain, k_norm_gain, attn_sinks, conv_dw_b, conv_norm_gain, conv_norm_bias,
             b_out, ffn_norm_gain, ffn_dw_b]
    rep_m = [m_mix_norm_gain, m_b_in, m_q_norm_gain, m_k_norm_gain, m_attn_sinks, m_conv_dw_b, m_conv_norm_gain,
             m_conv_norm_bias, m_b_out, m_ffn_norm_gain, m_ffn_dw_b]
    rep_v = [v_mix_norm_gain, v_b_in, v_q_norm_gain, v_k_norm_gain, v_attn_sinks, v_conv_dw_b, v_conv_norm_gain,
             v_conv_norm_bias, v_b_out, v_ffn_norm_gain, v_ffn_dw_b]
    shapes = [w.shape for w in rep_w] + [(D_MODEL,), (CONV_K, CONV_W), (3, 2 * D_FF)]
    tot_parts = _unpack(tot, shapes)
    loss = (0.5 / D_MODEL) * jnp.sum(tot_parts[len(rep_w)])
    g_cw_full, g_fw_full = tot_parts[len(rep_w) + 1], tot_parts[len(rep_w) + 2]
    n_rep_rows = _pack(rep_w).shape[0]
    rep_d, rep_nm, rep_nv = _adamw(_pack(rep_w), tot[:n_rep_rows], _pack(rep_m), _pack(rep_v), "adamw_small")
    rep_shapes = [w.shape for w in rep_w]
    res = {}
    for nm_, g_, d_, m_, v_ in zip(rep_names, tot_parts, _unpack(rep_d, rep_shapes), _unpack(rep_nm, rep_shapes),
                                   _unpack(rep_nv, rep_shapes)):
        res[nm_] = (g_, d_, m_, v_)

    g_cw_mine = lax.dynamic_slice_in_dim(g_cw_full, chip * (CONV_W // 4), CONV_W // 4, axis=1)
    g_fw_mine = lax.dynamic_slice_in_dim(g_fw_full, chip * (2 * D_FF // 4), 2 * D_FF // 4, axis=1)
    res["conv_dw_w"] = (g_cw_mine, *_adamw(conv_dw_w, g_cw_mine, m_conv_dw_w, v_conv_dw_w, "adamw_conv_dw_w"))
    res["ffn_dw_w"] = (g_fw_mine, *_adamw(ffn_dw_w, g_fw_mine, m_ffn_dw_w, v_ffn_dw_w, "adamw_ffn_dw_w"))

    gw_in4 = gw_in.reshape(D_MODEL, 4, IN_COLS // 4).transpose(1, 0, 2)
    late = [split(gw_in4), split(gw_out.reshape(4, D_MODEL // 4, D_MODEL))]
    late_names = ["w_in", "w_out"]
    got = _run_riders([_swap_halves(late)], "swap_halves_late")
    late_part = [_add_sibling(g, r, c_idx, "add_sibling_" + nm_) for g, r, nm_ in zip(late, got, late_names)]
    got2 = _run_riders([_exchange_chips(late_part)], "exchange_chips_late")
    late_red = [_add_chips(p, r, chip_idx, "add_chips_" + nm_) for p, r, nm_ in zip(late_part, got2, late_names)]
    late_g = _run_riders([_join_halves(late_red)], "join_halves_late")
    names = ["w_in", "w_out", "w_up", "w_down"]
    shard_g = list(late_g) + list(early_g)
    for nm_, w_, g_, m_, v_ in zip(names, big, shard_g, [m_w_in, m_w_out, m_w_up, m_w_down], [v_w_in, v_w_out, v_w_up, v_w_down]):
        g_ = g_.reshape(w_.shape)
        res[nm_] = (g_, *_adamw(w_, g_, m_, v_, "adamw_" + nm_))

    order = ["mix_norm_gain", "w_in", "b_in", "q_norm_gain", "k_norm_gain", "attn_sinks", "conv_dw_w", "conv_dw_b",
             "conv_norm_gain", "conv_norm_bias", "w_out", "b_out", "ffn_norm_gain", "w_up", "ffn_dw_w", "ffn_dw_b", "w_down"]
    return (loss, grad_x.reshape(x.shape), *[res[n][0] for n in order], *[res[n][1] for n in order],
            *[res[n][2] for n in order], *[res[n][3] for n in order])
```

```python
import functools

import jax
import jax.numpy as jnp
from jax import lax
from jax.experimental import pallas as pl
from jax.experimental.pallas import tpu as pltpu

F32 = jnp.float32
BF16 = jnp.bfloat16
MESH = pl.DeviceIdType.MESH

D_MODEL = 1024
HEAD_DIM = 64
N_HEADS = 8
Q_COLS = 512
QKV_COLS = 768
CONV_W = 512
CONV_K = 31
IN_COLS = 1792
D_FF = 2816
FFN_CB = 1408
BLOCK = 128
LANES = 128
EPS = 1e-6
NEG_INF = -1e30
SLOPES = tuple(float(2.0 ** (-(h + 1.0))) for h in range(N_HEADS))
HALO = 32
FHALO = 16
ROW_CHUNK = 64
GRAD_TILE = 2048
VMEM_LIMIT = 56 * 1024 * 1024

ADAM_LR = 0.001
ADAM_B1 = 0.9
ADAM_B2 = 0.999
ADAM_EPS = 1e-08
ADAM_WD = 0.01
ADAM_STEP = 10


def _params(sem=None):
    kw = dict(vmem_limit_bytes=VMEM_LIMIT)
    if sem is not None:
        kw["dimension_semantics"] = sem
    return pltpu.CompilerParams(**kw)


def _token_tile(t):
    return 512 if t % 512 == 0 and t >= 2048 else 128


def _sig(v):
    return 1.0 / (1.0 + jnp.exp(-v))


def _lo_mask():
    return lax.broadcasted_iota(jnp.int32, (1, LANES), 1) < HEAD_DIM


def _half_mean(v, lo):
    s_lo = jnp.sum(jnp.where(lo, v, 0.0), axis=-1, keepdims=True)
    s_hi = jnp.sum(jnp.where(lo, 0.0, v), axis=-1, keepdims=True)
    return jnp.where(lo, s_lo, s_hi) * (1.0 / HEAD_DIM)


def _dot(a, b):
    return jnp.dot(a, b, preferred_element_type=F32)


def _dot_nt(a, b):
    return lax.dot_general(a, b, (((1,), (1,)), ((), ())), preferred_element_type=F32)


def _dot_tn(a, b):
    return lax.dot_general(a, b, (((0,), (0,)), ((), ())), preferred_element_type=F32)


def _full(shape):
    nd = len(shape)
    return pl.BlockSpec(shape, lambda *_: (0,) * nd)


def _rows_to_tile(rows, n_rows):
    c = rows[0].shape[-1]
    rid = lax.broadcasted_iota(jnp.int32, (n_rows, c), 0)
    out = jnp.zeros((n_rows, c), F32)
    for k, r in enumerate(rows):
        out = jnp.where(rid == k, r, out)
    return out


ANY = pl.BlockSpec(memory_space=pl.ANY)


class _Rider:
    def __init__(self, ins, outs, sems, start, finish, aliases=None):
        self.ins, self.outs, self.sems = list(ins), list(outs), list(sems)
        self.start, self.finish, self.aliases = start, finish, dict(aliases or {})


def _join_riders(riders):
    ins, outs, sems, aliases, spans = [], [], [], {}, []
    for r in riders:
        spans.append((len(ins), len(outs), len(sems), r))
        for a, b in r.aliases.items():
            aliases[len(ins) + a] = len(outs) + b
        ins += r.ins
        outs += r.outs
        sems += r.sems

    def each(which):
        def run(i_refs, o_refs, s_refs):
            for i0, o0, s0, r in spans:
                getattr(r, which)(i_refs[i0:i0 + len(r.ins)], o_refs[o0:o0 + len(r.outs)], s_refs[s0:s0 + len(r.sems)])
        return run

    return _Rider(ins, outs, sems, each("start"), each("finish"), aliases)


def _call(body, args, *, name, grid, in_specs, out_specs, out_shape, scratch=(), riders=()):
    in_specs, out_specs, out_shape, scratch = list(in_specs), list(out_specs), list(out_shape), list(scratch)
    sem = ("arbitrary",) * len(grid)
    if not riders:
        outs = pl.pallas_call(body, name=name, grid=grid, in_specs=in_specs, out_specs=out_specs, out_shape=out_shape,
                              scratch_shapes=scratch, compiler_params=_params(sem))(*args)
        return list(outs), []
    r = _join_riders(riders)
    n_in, n_out, n_scr = len(in_specs), len(out_specs), len(scratch)
    nri, nro = len(r.ins), len(r.outs)

    def full(*refs):
        ins, rin = refs[:n_in], refs[n_in:n_in + nri]
        o0 = n_in + nri
        outs, rout = refs[o0:o0 + n_out], refs[o0 + n_out:o0 + n_out + nro]
        s0 = o0 + n_out + nro
        scr, rsem = refs[s0:s0 + n_scr], refs[s0 + n_scr:]
        first = functools.reduce(jnp.logical_and, [pl.program_id(k) == 0 for k in range(len(grid))])
        last = functools.reduce(jnp.logical_and, [pl.program_id(k) == grid[k] - 1 for k in range(len(grid))])

        @pl.when(first)
        def _():
            r.start(rin, rout, rsem)

        body(*ins, *outs, *scr)

        @pl.when(last)
        def _():
            r.finish(rin, rout, rsem)

    outs = pl.pallas_call(
        full, name=name, grid=grid, in_specs=in_specs + [ANY] * nri, out_specs=out_specs + [ANY] * nro,
        out_shape=out_shape + r.outs, scratch_shapes=scratch + r.sems,
        input_output_aliases={n_in + a: n_out + b for a, b in r.aliases.items()},
        compiler_params=_params(sem))(*args, *r.ins)
    return list(outs[:n_out]), list(outs[n_out:])


def _run_riders(riders, name):
    r = _join_riders(riders)
    nri, nro = len(r.ins), len(r.outs)

    def body(*refs):
        rin, rout, rsem = refs[:nri], refs[nri:nri + nro], refs[nri + nro:]
        r.start(rin, rout, rsem)
        r.finish(rin, rout, rsem)

    outs = pl.pallas_call(body, name=name, in_specs=[ANY] * nri, out_specs=[ANY] * nro, out_shape=r.outs,
                          scratch_shapes=r.sems, input_output_aliases=r.aliases)(*r.ins)
    return list(outs)


def _fwd_in(x, g_mix, w_in, b_in, gq2, gk2, riders=()):
    t = x.shape[0]
    tt = _token_tile(t)

    def body(x_ref, g_ref, w_ref, b_ref, gq_ref, gk_ref, h1_ref, pq_ref, pc_ref, qkvn_ref):
        xv = x_ref[...]
        r = lax.rsqrt(jnp.mean(xv * xv, axis=-1, keepdims=True) + EPS)
        h = (xv * r * g_ref[...]).astype(BF16)
        h1_ref[...] = h
        proj = _dot(h, w_ref[...]) + b_ref[...]
        pq_ref[...] = proj[:, :QKV_COLS]
        pc_ref[...] = proj[:, QKV_COLS:]
        lo = _lo_mask()
        for p in range(5):
            seg = proj[:, p * LANES:(p + 1) * LANES]
            rr = lax.rsqrt(_half_mean(seg * seg, lo) + EPS)
            gain = gq_ref[...] if p < 4 else gk_ref[...]
            qkvn_ref[:, p * LANES:(p + 1) * LANES] = (seg * rr * gain).astype(BF16)
        qkvn_ref[:, 640:768] = proj[:, 640:768].astype(BF16)

    return _call(
        body, (x, g_mix, w_in, b_in, gq2, gk2), name="fwd_in", grid=(t // tt,), riders=riders,
        in_specs=[pl.BlockSpec((tt, D_MODEL), lambda i: (i, 0)), _full((1, D_MODEL)), _full((D_MODEL, IN_COLS)),
                  _full((1, IN_COLS)), _full((1, LANES)), _full((1, LANES))],
        out_specs=[pl.BlockSpec((tt, D_MODEL), lambda i: (i, 0)), pl.BlockSpec((tt, QKV_COLS), lambda i: (i, 0)),
                   pl.BlockSpec((tt, 2 * CONV_W), lambda i: (i, 0)), pl.BlockSpec((tt, QKV_COLS), lambda i: (i, 0))],
        out_shape=[jax.ShapeDtypeStruct((t, D_MODEL), BF16), jax.ShapeDtypeStruct((t, QKV_COLS), F32),
                   jax.ShapeDtypeStruct((t, 2 * CONV_W), F32), jax.ShapeDtypeStruct((t, QKV_COLS), BF16)])


GROUP = 4
GROUP_ROWS = GROUP * BLOCK


def _attn_bias():
    qi = jnp.arange(GROUP_ROWS)[:, None] % BLOCK
    kj = jnp.arange(2 * BLOCK)[None, :]
    rel = qi + BLOCK - kj
    band = (rel >= 0) & (rel < BLOCK)
    slope_rows = jnp.repeat(jnp.asarray(SLOPES, F32).reshape(N_HEADS // GROUP, GROUP), BLOCK, axis=1)
    penalty = -(slope_rows[:, :, None] * rel.astype(F32)[None])
    later = jnp.where(band[None], penalty, NEG_INF)
    first = jnp.where((band & (kj >= BLOCK))[None], penalty, NEG_INF)
    return jnp.stack([first, later])


def _attn_blocks_per_step(nb):
    return 2 if nb % 2 == 0 else 1


def _attn_specs(bps):
    return [pl.BlockSpec((bps * BLOCK, Q_COLS), lambda i: (i, 0)),
            pl.BlockSpec((bps * BLOCK, 2 * LANES), lambda i: (i, 2)),
            pl.BlockSpec((BLOCK, 2 * LANES), lambda i: (jnp.maximum(bps * i - 1, 0), 2)),
            _full((2, N_HEADS // GROUP, GROUP_ROWS, 2 * BLOCK))]


def _attn_window(b, kvc_ref, kvp_ref):
    own = kvc_ref[b * BLOCK:(b + 1) * BLOCK, :]
    before = kvp_ref[...] if b == 0 else kvc_ref[(b - 1) * BLOCK:b * BLOCK, :]
    return jnp.concatenate([before, own], axis=0)


def _attn_bias_of(bias_ref, b, kk):
    if b > 0:
        return bias_ref[1, kk]
    return bias_ref[jnp.minimum(pl.program_id(0), 1), kk]


def _attn_keys(kv):
    kv = kv.astype(F32)
    lo = _lo_mask()

    def both_halves(pair):
        rolled = pltpu.roll(pair, HEAD_DIM, 1)
        return [jnp.where(lo, pair, rolled).astype(BF16), jnp.where(lo, rolled, pair).astype(BF16)]

    return both_halves(kv[:, :LANES]), both_halves(kv[:, LANES:])


def _per_head_column(values):
    seg = lax.broadcasted_iota(jnp.int32, (GROUP_ROWS, 1), 0) // BLOCK
    col = jnp.zeros((GROUP_ROWS, 1), F32) + values[GROUP - 1]
    for a in range(GROUP - 2, -1, -1):
        col = jnp.where(seg == a, values[a], col)
    return col


def _stack_heads(ref, rows, kk, lo, dtype):
    parts = []
    for a in range(GROUP):
        h = GROUP * kk + a
        pair = ref[rows, (h // 2) * LANES:(h // 2 + 1) * LANES]
        hm = lo if h % 2 == 0 else jnp.logical_not(lo)
        parts.append(jnp.where(hm, pair, jnp.zeros_like(pair)).astype(dtype))
    return jnp.concatenate(parts, axis=0)


def _group_scores(q_ref, rows, kk, lo, k_dup, bias):
    qs = _stack_heads(q_ref, rows, kk, lo, BF16)
    return _dot_nt(qs, k_dup[kk]) * 0.125 + bias, qs


def _attn_fwd(qkvn, sinks, bias, riders=()):
    t = qkvn.shape[0]
    nb = t // BLOCK
    bps = _attn_blocks_per_step(nb)

    def body(sink_ref, q_ref, kvc_ref, kvp_ref, bias_ref, o_ref, lse_ref):
        lo = _lo_mask()
        lane = lax.broadcasted_iota(jnp.int32, (BLOCK, LANES), 1)
        for b in range(bps):
            rows_b = slice(b * BLOCK, (b + 1) * BLOCK)
            k_dup, v_dup = _attn_keys(_attn_window(b, kvc_ref, kvp_ref))
            lse_t = jnp.zeros((BLOCK, LANES), F32)
            outs = [jnp.zeros((BLOCK, LANES), F32) for _ in range(4)]
            for kk in range(N_HEADS // GROUP):
                s, _ = _group_scores(q_ref, rows_b, kk, lo, k_dup, _attn_bias_of(bias_ref, b, kk))
                sink = _per_head_column([sink_ref[GROUP * kk + a] for a in range(GROUP)])
                m = jnp.maximum(jnp.max(s, axis=-1, keepdims=True), sink)
                pe = jnp.exp(s - m)
                l = jnp.sum(pe, axis=-1, keepdims=True) + jnp.exp(sink - m)
                lse = m + jnp.log(l)
                o = _dot((pe / l).astype(BF16), v_dup[kk])
                for a in range(GROUP):
                    h = GROUP * kk + a
                    rows = slice(a * BLOCK, (a + 1) * BLOCK)
                    hm = lo if h % 2 == 0 else jnp.logical_not(lo)
                    outs[h // 2] = outs[h // 2] + jnp.where(hm, o[rows], 0.0)
                    lse_t = jnp.where(lane == h, lse[rows], lse_t)
            for p in range(4):
                o_ref[rows_b, p * LANES:(p + 1) * LANES] = outs[p].astype(BF16)
            lse_ref[rows_b, :] = lse_t

    blk = lambda w: pl.BlockSpec((bps * BLOCK, w), lambda i: (i, 0))
    return _call(
        body, (sinks, qkvn, qkvn, qkvn, bias), name="attn_fwd", grid=(nb // bps,), riders=riders,
        in_specs=[pl.BlockSpec(memory_space=pltpu.SMEM)] + _attn_specs(bps),
        out_specs=[blk(Q_COLS), blk(LANES)],
        out_shape=[jax.ShapeDtypeStruct((t, Q_COLS), BF16), jax.ShapeDtypeStruct((t, LANES), F32)])


def _glu(pc):
    return pc[:, :CONV_W] * _sig(pc[:, CONV_W:])


def _glu_chunks(pc_ref, scr, tt):
    for r0, cs in _chunks(tt, CONV_W):
        rows = slice(r0, r0 + ROW_CHUNK)
        gate = slice(cs.start + CONV_W, cs.stop + CONV_W)
        scr[HALO + r0:HALO + r0 + ROW_CHUNK, cs] = pc_ref[rows, cs] * _sig(pc_ref[rows, gate])


def _group_norm_stats(seg, lo):
    mu = _half_mean(seg, lo)
    d = seg - mu
    rstd = lax.rsqrt(_half_mean(d * d, lo) + EPS)
    return d * rstd, rstd


def _shifted_copies(src, dst, tt):
    n = tt + HALO - 8
    for s in range(1, 8):
        dst[s - 1, 0:n, :] = src[s:s + n, :]


def _tap_rows(src, shifted, off, r0, cs):
    q, s = divmod(off, 8)
    if s == 0:
        return src[r0 + off:r0 + off + ROW_CHUNK, cs]
    return shifted[s - 1, r0 + 8 * q:r0 + 8 * q + ROW_CHUNK, cs]


def _shift_scratch(tt):
    return pltpu.VMEM((7, tt + HALO - 8, CONV_W), F32)


def _conv_fwd(pc, cw, cb, ln_g, ln_b, riders=()):
    t = pc.shape[0]
    tt = _token_tile(t)

    def body(cur_ref, prev_ref, w_ref, b_ref, g_ref, bb_ref, y_ref, c_ref, scr, shf):
        i = pl.program_id(0)
        scr[0:HALO, :] = _glu(prev_ref[...]) * (i > 0).astype(F32)
        _glu_chunks(cur_ref, scr, tt)
        _shifted_copies(scr, shf, tt)
        lo = _lo_mask()
        for r0, cs in _chunks(tt, CONV_W):
            acc = jnp.zeros((ROW_CHUNK, LANES), F32) + b_ref[:, cs]
            for k in range(CONV_K):
                acc = acc + w_ref[k:k + 1, cs] * _tap_rows(scr, shf, 2 + k, r0, cs)
            y_ref[r0:r0 + ROW_CHUNK, cs] = acc
            yh, _ = _group_norm_stats(acc, lo)
            z = yh * g_ref[:, cs] + bb_ref[:, cs]
            c_ref[r0:r0 + ROW_CHUNK, cs] = (z * _sig(z)).astype(BF16)

    hb = tt // HALO
    return _call(
        body, (pc, pc, cw, cb, ln_g, ln_b), name="conv_fwd", grid=(t // tt,), riders=riders,
        in_specs=[pl.BlockSpec((tt, 2 * CONV_W), lambda i: (i, 0)),
                  pl.BlockSpec((HALO, 2 * CONV_W), lambda i: (jnp.maximum(i * hb - 1, 0), 0)),
                  _full((CONV_K, CONV_W)), _full((1, CONV_W)), _full((1, CONV_W)), _full((1, CONV_W))],
        out_specs=[pl.BlockSpec((tt, CONV_W), lambda i: (i, 0)), pl.BlockSpec((tt, CONV_W), lambda i: (i, 0))],
        out_shape=[jax.ShapeDtypeStruct((t, CONV_W), F32), jax.ShapeDtypeStruct((t, CONV_W), BF16)],
        scratch=[pltpu.VMEM((tt + HALO, CONV_W), F32), _shift_scratch(tt)])


def _out_proj(x, attn, c, w_out, b_out, g_ffn, riders=()):
    t = x.shape[0]
    tt = _token_tile(t)

    def body(x_ref, a_ref, c_ref, w_ref, b_ref, g_ref, x1_ref, h2_ref):
        x1 = x_ref[...] + _dot(a_ref[...], w_ref[0:Q_COLS, :]) + _dot(c_ref[...], w_ref[Q_COLS:, :]) + b_ref[...]
        x1_ref[...] = x1
        r = lax.rsqrt(jnp.mean(x1 * x1, axis=-1, keepdims=True) + EPS)
        h2_ref[...] = (x1 * r * g_ref[...]).astype(BF16)

    row = lambda w: pl.BlockSpec((tt, w), lambda i: (i, 0))
    return _call(
        body, (x, attn, c, w_out, b_out, g_ffn), name="out_proj", grid=(t // tt,), riders=riders,
        in_specs=[row(D_MODEL), row(Q_COLS), row(CONV_W), _full((D_MODEL, D_MODEL)), _full((1, D_MODEL)), _full((1, D_MODEL))],
        out_specs=[row(D_MODEL), row(D_MODEL)],
        out_shape=[jax.ShapeDtypeStruct((t, D_MODEL), F32), jax.ShapeDtypeStruct((t, D_MODEL), BF16)])


def _chunks(rows, cols):
    return [(r0, slice(c0, c0 + LANES)) for c0 in range(0, cols, LANES) for r0 in range(0, rows, ROW_CHUNK)]


def _ffn_up(h2, w_up, dw, db):
    t = h2.shape[0]
    tt = _token_tile(t)
    nj = D_FF // FFN_CB

    def body(hc_ref, hp_ref, wg_ref, wu_ref, dwg_ref, dwu_ref, dbg_ref, dbu_ref,
             hg_ref, hu_ref, upg_ref, upu_ref, act_ref, sg, su):
        i = pl.program_id(1)
        hc = hc_ref[...]
        hp = hp_ref[...] * (i > 0).astype(BF16)
        ups = []
        for w_ref, dw_ref, db_ref, h_ref, up_ref, scr in ((wg_ref, dwg_ref, dbg_ref, hg_ref, upg_ref, sg),
                                                          (wu_ref, dwu_ref, dbu_ref, hu_ref, upu_ref, su)):
            cur = _dot(hc, w_ref[...])
            h_ref[...] = cur.astype(BF16)
            scr[0:FHALO, :] = _dot(hp, w_ref[...])
            scr[FHALO:FHALO + tt, :] = cur
            up = (dw_ref[0:1, :] * scr[FHALO - 2:FHALO - 2 + tt, :] + dw_ref[1:2, :] * scr[FHALO - 1:FHALO - 1 + tt, :]
                  + dw_ref[2:3, :] * cur + db_ref[...])
            up_ref[...] = up
            ups.append(up)
        g, u = ups
        act_ref[...] = (g * _sig(g) * u).astype(BF16)

    fb = tt // FHALO
    colg = lambda r: pl.BlockSpec((r, FFN_CB), lambda j, i: (0, j))
    colu = lambda r: pl.BlockSpec((r, FFN_CB), lambda j, i: (0, j + nj))
    tile = pl.BlockSpec((tt, FFN_CB), lambda j, i: (i, j))
    return pl.pallas_call(
        body, name="ffn_up", grid=(nj, t // tt),
        in_specs=[pl.BlockSpec((tt, D_MODEL), lambda j, i: (i, 0)),
                  pl.BlockSpec((FHALO, D_MODEL), lambda j, i: (jnp.maximum(i * fb - 1, 0), 0)),
                  pl.BlockSpec((None, D_MODEL, FFN_CB), lambda j, i: (j, 0, 0)),
                  pl.BlockSpec((None, D_MODEL, FFN_CB), lambda j, i: (j + nj, 0, 0)),
                  colg(3), colu(3), colg(1), colu(1)],
        out_specs=[tile] * 5,
        out_shape=[jax.ShapeDtypeStruct((t, D_FF), BF16), jax.ShapeDtypeStruct((t, D_FF), BF16),
                   jax.ShapeDtypeStruct((t, D_FF), F32), jax.ShapeDtypeStruct((t, D_FF), F32),
                   jax.ShapeDtypeStruct((t, D_FF), BF16)],
        scratch_shapes=[pltpu.VMEM((tt + FHALO, FFN_CB), F32), pltpu.VMEM((tt + FHALO, FFN_CB), F32)],
        compiler_params=_params(("parallel", "parallel")),
    )(h2, h2, w_up, w_up, dw, dw, db, db)


def _ffn_down(act, w_down, x1, target):
    t = act.shape[0]
    tt = _token_tile(t)

    def body(a_ref, w_ref, x1_ref, t_ref, dy_ref, loss_ref):
        err = x1_ref[...] + _dot(a_ref[...], w_ref[...]) - t_ref[...]
        dy_ref[...] = err * (1.0 / D_MODEL)

        @pl.when(pl.program_id(0) == 0)
        def _():
            loss_ref[...] = jnp.zeros_like(loss_ref)

        loss_ref[...] += jnp.sum(err * err, axis=0, keepdims=True)

    row = lambda w: pl.BlockSpec((tt, w), lambda i: (i, 0))
    return pl.pallas_call(
        body, name="ffn_down", grid=(t // tt,),
        in_specs=[row(D_FF), _full((D_FF, D_MODEL)), row(D_MODEL), row(D_MODEL)],
        out_specs=[row(D_MODEL), _full((1, D_MODEL))],
        out_shape=[jax.ShapeDtypeStruct((t, D_MODEL), F32), jax.ShapeDtypeStruct((1, D_MODEL), F32)],
        compiler_params=_params(("arbitrary",)),
    )(act, w_down, x1, target)


def _ffn_bwd_act(dy, w_down, up_g, up_u):
    t = dy.shape[0]
    tt = _token_tile(t)
    nj = D_FF // FFN_CB

    def body(dy_ref, wd_ref, g_ref, u_ref, dg_ref, du_ref, gbg_ref, gbu_ref):
        i = pl.program_id(1)
        d_act = _dot_nt(dy_ref[...].astype(BF16), wd_ref[...])
        g, u = g_ref[...], u_ref[...]
        s = _sig(g)
        d_u = d_act * (g * s)
        d_g = d_act * u * (s * (1.0 + g * (1.0 - s)))

        @pl.when(i == 0)
        def _():
            for r in (gbg_ref, gbu_ref):
                r[...] = jnp.zeros_like(r)

        for d, o_ref, gb_ref in ((d_g, dg_ref, gbg_ref), (d_u, du_ref, gbu_ref)):
            o_ref[...] = d.astype(BF16)
            gb_ref[...] += jnp.sum(d, axis=0, keepdims=True)

    tile = pl.BlockSpec((tt, FFN_CB), lambda j, i: (i, j))
    acc = pl.BlockSpec((1, FFN_CB), lambda j, i: (0, j))
    return pl.pallas_call(
        body, name="ffn_bwd_act", grid=(nj, t // tt),
        in_specs=[pl.BlockSpec((tt, D_MODEL), lambda j, i: (i, 0)), pl.BlockSpec((FFN_CB, D_MODEL), lambda j, i: (j, 0)),
                  tile, tile],
        out_specs=[tile, tile, acc, acc],
        out_shape=[jax.ShapeDtypeStruct((t, D_FF), BF16), jax.ShapeDtypeStruct((t, D_FF), BF16),
                   jax.ShapeDtypeStruct((1, D_FF), F32), jax.ShapeDtypeStruct((1, D_FF), F32)],
        compiler_params=_params(("parallel", "arbitrary")),
    )(dy, w_down, up_g, up_u)


def _ffn_bwd_conv(dg, du, hg, hu, dw):
    t = dg.shape[0]
    tt = _token_tile(t)
    nj = D_FF // FFN_CB
    ni = t // tt

    def body(gc_ref, gn_ref, uc_ref, un_ref, hg_ref, hu_ref, dwg_ref, dwu_ref, og_ref, ou_ref, gwg_ref, gwu_ref, scr):
        i = pl.program_id(1)
        last = (i < ni - 1).astype(F32)

        @pl.when(i == 0)
        def _():
            gwg_ref[...] = jnp.zeros_like(gwg_ref)
            gwu_ref[...] = jnp.zeros_like(gwu_ref)

        for c_ref, n_ref, h_ref, dw_ref, o_ref, gw_ref in ((gc_ref, gn_ref, hg_ref, dwg_ref, og_ref, gwg_ref),
                                                           (uc_ref, un_ref, hu_ref, dwu_ref, ou_ref, gwu_ref)):
            scr[0:tt, :] = c_ref[...].astype(F32)
            scr[tt:tt + FHALO, :] = n_ref[...].astype(F32) * last
            sums = None
            for r0, cs in _chunks(tt, FFN_CB):
                shifted = [scr[r0 + d:r0 + d + ROW_CHUNK, cs] for d in (2, 1, 0)]
                o_ref[r0:r0 + ROW_CHUNK, cs] = (dw_ref[0:1, cs] * shifted[0] + dw_ref[1:2, cs] * shifted[1]
                                                + dw_ref[2:3, cs] * shifted[2]).astype(BF16)
                hw = h_ref[r0:r0 + ROW_CHUNK, cs].astype(F32)
                prods = [hw * d for d in shifted]
                sums = prods if r0 == 0 else [a + b for a, b in zip(sums, prods)]
                if r0 == tt - ROW_CHUNK:
                    gw_ref[:, cs] += _rows_to_tile([jnp.sum(a, axis=0, keepdims=True) for a in sums], 8)

    fb = tt // FHALO
    tile = pl.BlockSpec((tt, FFN_CB), lambda j, i: (i, j))
    nxt = pl.BlockSpec((FHALO, FFN_CB), lambda j, i: (jnp.minimum((i + 1) * fb, t // FHALO - 1), j))
    acc = pl.BlockSpec((8, FFN_CB), lambda j, i: (0, j))
    return pl.pallas_call(
        body, name="ffn_bwd_conv", grid=(nj, ni),
        in_specs=[tile, nxt, tile, nxt, tile, tile, pl.BlockSpec((3, FFN_CB), lambda j, i: (0, j)),
                  pl.BlockSpec((3, FFN_CB), lambda j, i: (0, j + nj))],
        out_specs=[tile, tile, acc, acc],
        out_shape=[jax.ShapeDtypeStruct((t, D_FF), BF16), jax.ShapeDtypeStruct((t, D_FF), BF16),
                   jax.ShapeDtypeStruct((8, D_FF), F32), jax.ShapeDtypeStruct((8, D_FF), F32)],
        scratch_shapes=[pltpu.VMEM((tt + FHALO, FFN_CB), F32)],
        compiler_params=_params(("parallel", "arbitrary")),
    )(dg, dg, du, du, hg, hu, dw, dw)


def _ffn_bwd_in(dhg, dhu, w_up, x1, dy, g_ffn, riders=()):
    t = x1.shape[0]
    tt = _token_tile(t)

    def body(dg_ref, du_ref, w_ref, x1_ref, dy_ref, g_ref, dx_ref, gg_ref):
        d_h2 = (_dot_nt(dg_ref[:, 0:FFN_CB], w_ref[0]) + _dot_nt(dg_ref[:, FFN_CB:], w_ref[1])
                + _dot_nt(du_ref[:, 0:FFN_CB], w_ref[2]) + _dot_nt(du_ref[:, FFN_CB:], w_ref[3]))
        x1 = x1_ref[...]
        r = lax.rsqrt(jnp.mean(x1 * x1, axis=-1, keepdims=True) + EPS)
        xh = x1 * r
        gd = d_h2 * g_ref[...]
        dx_ref[...] = dy_ref[...] + r * (gd - xh * jnp.mean(gd * xh, axis=-1, keepdims=True))

        @pl.when(pl.program_id(0) == 0)
        def _():
            gg_ref[...] = jnp.zeros_like(gg_ref)

        gg_ref[...] += jnp.sum(d_h2 * xh, axis=0, keepdims=True)

    row = lambda w: pl.BlockSpec((tt, w), lambda i: (i, 0))
    return _call(
        body, (dhg, dhu, w_up, x1, dy, g_ffn), name="ffn_bwd_in", grid=(t // tt,), riders=riders,
        in_specs=[row(D_FF), row(D_FF), _full((4, D_MODEL, FFN_CB)), row(D_MODEL), row(D_MODEL), _full((1, D_MODEL))],
        out_specs=[row(D_MODEL), _full((1, D_MODEL))],
        out_shape=[jax.ShapeDtypeStruct((t, D_MODEL), F32), jax.ShapeDtypeStruct((1, D_MODEL), F32)])


def _grad_weight(a, b, nj, mb, name, lead=None, into=None, offset=0):
    t, m = a.shape
    n = b.shape[1]
    nb_ = n // nj
    tt = GRAD_TILE if t % GRAD_TILE == 0 else _token_tile(t)

    def body(*refs):
        a_ref, b_ref, o_ref = refs[0], refs[1], refs[-1]

        @pl.when(pl.program_id(2) == 0)
        def _():
            o_ref[...] = jnp.zeros_like(o_ref)

        o_ref[0] += _dot_tn(a_ref[...].astype(BF16), b_ref[...].astype(BF16))

    in_specs = [pl.BlockSpec((tt, mb), lambda j, mi, i: (i, mi)), pl.BlockSpec((tt, nb_), lambda j, mi, i: (i, j))]
    out_shape = jax.ShapeDtypeStruct((lead or nj, m, nb_) if into is None else into.shape, F32)
    return pl.pallas_call(
        body, name=name, grid=(nj, m // mb, t // tt),
        in_specs=in_specs if into is None else in_specs + [pl.BlockSpec(memory_space=pl.ANY)],
        out_specs=pl.BlockSpec((1, mb, nb_), lambda j, mi, i: (j + offset, mi, 0)),
        out_shape=out_shape,
        input_output_aliases={} if into is None else {2: 0},
        compiler_params=_params(("parallel", "parallel", "arbitrary")),
    )(*((a, b) if into is None else (a, b, into)))


def _out_proj_bwd(dx1, w_out):
    t = dx1.shape[0]
    tt = _token_tile(t)

    def body(d_ref, w_ref, dm_ref, gb_ref):
        d = d_ref[...]
        dm_ref[...] = _dot_nt(d.astype(BF16), w_ref[...])

        @pl.when(pl.program_id(0) == 0)
        def _():
            gb_ref[...] = jnp.zeros_like(gb_ref)

        gb_ref[...] += jnp.sum(d, axis=0, keepdims=True)

    row = pl.BlockSpec((tt, D_MODEL), lambda i: (i, 0))
    return pl.pallas_call(
        body, name="out_proj_bwd", grid=(t // tt,),
        in_specs=[row, _full((D_MODEL, D_MODEL))],
        out_specs=[row, _full((1, D_MODEL))],
        out_shape=[jax.ShapeDtypeStruct((t, D_MODEL), F32), jax.ShapeDtypeStruct((1, D_MODEL), F32)],
        compiler_params=_params(("arbitrary",)),
    )(dx1, w_out)


def _conv_bwd(dmix, y, pc, cw, ln_g, ln_b, riders=()):
    t = y.shape[0]
    tt = _token_tile(t)
    ni = t // tt
    ncb = CONV_W // LANES

    def body(dc_ref, dcn_ref, y_ref, yn_ref, pc_ref, pcp_ref, w_ref, g_ref, bb_ref,
             dp_ref, gw_ref, gb_ref, gg_ref, gbb_ref, gbin_ref, scr_d, scr_c, scr_o, shf_d, shf_c):
        i = pl.program_id(0)
        lo = _lo_mask()

        @pl.when(i == 0)
        def _():
            for r in (gw_ref, gb_ref, gg_ref, gbb_ref, gbin_ref):
                r[...] = jnp.zeros_like(r)

        def norm_bwd(dc, yv, cs):
            yh, rstd = _group_norm_stats(yv, lo)
            z = yh * g_ref[:, cs] + bb_ref[:, cs]
            s = _sig(z)
            dz = dc * (s * (1.0 + z * (1.0 - s)))
            dyh = dz * g_ref[:, cs]
            d_y = rstd * (dyh - _half_mean(dyh, lo) - yh * _half_mean(dyh * yh, lo))
            return d_y, dz, yh

        for p in range(ncb):
            cs = slice(p * LANES, (p + 1) * LANES)
            d_y, dz, yh = norm_bwd(dc_ref[:, cs], y_ref[:, cs], cs)
            scr_d[0:tt, cs] = d_y
            gg_ref[:, cs] += jnp.sum(dz * yh, axis=0, keepdims=True)
            gbb_ref[:, cs] += jnp.sum(dz, axis=0, keepdims=True)
            gb_ref[:, cs] += jnp.sum(d_y, axis=0, keepdims=True)
            d_yn, _, _ = norm_bwd(dcn_ref[:, cs], yn_ref[:, cs], cs)
            scr_d[tt:tt + HALO, cs] = d_yn * (i < ni - 1).astype(F32)
        scr_c[0:HALO, :] = _glu(pcp_ref[...]) * (i > 0).astype(F32)
        scr_c[HALO:HALO + tt, :] = _glu(pc_ref[...])
        _shifted_copies(scr_d, shf_d, tt)
        _shifted_copies(scr_c, shf_c, tt)

        rid = lax.broadcasted_iota(jnp.int32, (HALO, LANES), 0)
        for cbk in range(ncb):
            cs = slice(cbk * LANES, (cbk + 1) * LANES)
            for rb in range(tt // ROW_CHUNK):
                r0 = rb * ROW_CHUNK
                acc = jnp.zeros((ROW_CHUNK, LANES), F32)
                for k in range(CONV_K):
                    acc = acc + w_ref[k:k + 1, cs] * _tap_rows(scr_d, shf_d, 30 - k, r0, cs)
                scr_o[r0:r0 + ROW_CHUNK, cs] = acc
            gwt = jnp.zeros((HALO, LANES), F32)
            for k in range(CONV_K):
                acc = jnp.zeros((ROW_CHUNK, LANES), F32)
                for rb in range(tt // ROW_CHUNK):
                    r0 = rb * ROW_CHUNK
                    acc = acc + scr_d[r0:r0 + ROW_CHUNK, cs] * _tap_rows(scr_c, shf_c, 2 + k, r0, cs)
                gwt = jnp.where(rid == k, jnp.sum(acc, axis=0, keepdims=True), gwt)
            gw_ref[:, cs] += gwt
        d_c0 = scr_o[...]
        a = pc_ref[:, 0:CONV_W]
        s = _sig(pc_ref[:, CONV_W:])
        d_a = d_c0 * s
        d_gate = d_c0 * a * s * (1.0 - s)
        dp_ref[:, 0:CONV_W] = d_a.astype(BF16)
        dp_ref[:, CONV_W:] = d_gate.astype(BF16)
        gbin_ref[:, 0:CONV_W] += jnp.sum(d_a, axis=0, keepdims=True)
        gbin_ref[:, CONV_W:] += jnp.sum(d_gate, axis=0, keepdims=True)

    hb = tt // HALO
    nxt = lambda col: pl.BlockSpec((HALO, CONV_W), lambda i: (jnp.minimum((i + 1) * hb, t // HALO - 1), col))
    return _call(
        body, (dmix, dmix, y, y, pc, pc, cw, ln_g, ln_b), name="conv_bwd", grid=(ni,), riders=riders,
        in_specs=[pl.BlockSpec((tt, CONV_W), lambda i: (i, 1)), nxt(1),
                  pl.BlockSpec((tt, CONV_W), lambda i: (i, 0)), nxt(0),
                  pl.BlockSpec((tt, 2 * CONV_W), lambda i: (i, 0)),
                  pl.BlockSpec((HALO, 2 * CONV_W), lambda i: (jnp.maximum(i * hb - 1, 0), 0)),
                  _full((CONV_K, CONV_W)), _full((1, CONV_W)), _full((1, CONV_W))],
        out_specs=[pl.BlockSpec((tt, 2 * CONV_W), lambda i: (i, 0)), _full((HALO, CONV_W)), _full((1, CONV_W)),
                   _full((1, CONV_W)), _full((1, CONV_W)), _full((1, 2 * CONV_W))],
        out_shape=[jax.ShapeDtypeStruct((t, 2 * CONV_W), BF16), jax.ShapeDtypeStruct((HALO, CONV_W), F32),
                   jax.ShapeDtypeStruct((1, CONV_W), F32), jax.ShapeDtypeStruct((1, CONV_W), F32),
                   jax.ShapeDtypeStruct((1, CONV_W), F32), jax.ShapeDtypeStruct((1, 2 * CONV_W), F32)],
        scratch=[pltpu.VMEM((tt + HALO, CONV_W), F32), pltpu.VMEM((tt + HALO, CONV_W), F32),
                 pltpu.VMEM((tt, CONV_W), F32), _shift_scratch(tt), _shift_scratch(tt)])


def _attn_bwd(qkvn, dmix, lse, sinks, bias, riders=()):
    t = qkvn.shape[0]
    nb = t // BLOCK
    bps = _attn_blocks_per_step(nb)

    def body(sink_ref, q_ref, kvc_ref, kvp_ref, bias_ref, do_ref, lse_ref, dq_ref, dcur_ref, dprev_ref, ds_ref):
        i = pl.program_id(0)
        lo = _lo_mask()
        lane1 = lax.broadcasted_iota(jnp.int32, (1, LANES), 1)
        lane = lax.broadcasted_iota(jnp.int32, (BLOCK, LANES), 1)
        dsink = jnp.zeros((1, LANES), F32)
        for b in range(bps):
            rows_b = slice(b * BLOCK, (b + 1) * BLOCK)
            k_dup, v_dup = _attn_keys(_attn_window(b, kvc_ref, kvp_ref))
            lse_t = lse_ref[rows_b, :]
            dqs = [jnp.zeros((BLOCK, LANES), F32) for _ in range(4)]
            dkv = []
            for kk in range(N_HEADS // GROUP):
                s, qs = _group_scores(q_ref, rows_b, kk, lo, k_dup, _attn_bias_of(bias_ref, b, kk))
                lse = jnp.concatenate([jnp.sum(jnp.where(lane == GROUP * kk + a, lse_t, 0.0), axis=-1, keepdims=True)
                                       for a in range(GROUP)], axis=0)
                prob = jnp.exp(s - lse)
                dos = _stack_heads(do_ref, rows_b, kk, lo, BF16)
                dp = _dot_nt(dos, v_dup[kk])
                dsum = jnp.sum(prob * dp, axis=-1, keepdims=True)
                dsb = (prob * (dp - dsum) * 0.125).astype(BF16)
                sink = _per_head_column([sink_ref[GROUP * kk + a] for a in range(GROUP)])
                dsk = -jnp.exp(sink - lse) * dsum
                dq = _dot(dsb, k_dup[kk])
                for a in range(GROUP):
                    h = GROUP * kk + a
                    rows = slice(a * BLOCK, (a + 1) * BLOCK)
                    hm = lo if h % 2 == 0 else jnp.logical_not(lo)
                    dqs[h // 2] = dqs[h // 2] + jnp.where(hm, dq[rows], 0.0)
                    dsink = dsink + jnp.where(lane1 == h, jnp.sum(dsk[rows], axis=0, keepdims=True), 0.0)
                dk_x = _dot_tn(dsb, qs)
                dv_x = _dot_tn(prob.astype(BF16), dos)
                dkv.append((dk_x + pltpu.roll(dk_x, HEAD_DIM, 1), dv_x + pltpu.roll(dv_x, HEAD_DIM, 1)))
            for p in range(4):
                dq_ref[rows_b, p * LANES:(p + 1) * LANES] = dqs[p]
            dk = jnp.where(lo, dkv[0][0], dkv[1][0])
            dv = jnp.where(lo, dkv[0][1], dkv[1][1])
            dprev_ref[rows_b, 0:LANES] = dk[0:BLOCK]
            dprev_ref[rows_b, LANES:] = dv[0:BLOCK]
            dcur_ref[rows_b, 0:LANES] = dk[BLOCK:]
            dcur_ref[rows_b, LANES:] = dv[BLOCK:]

        @pl.when(i == 0)
        def _():
            ds_ref[...] = jnp.zeros_like(ds_ref)

        ds_ref[...] += dsink

    blk = lambda w: pl.BlockSpec((bps * BLOCK, w), lambda i: (i, 0))
    return _call(
        body, (sinks, qkvn, qkvn, qkvn, bias, dmix, lse), name="attn_bwd", grid=(nb // bps,), riders=riders,
        in_specs=[pl.BlockSpec(memory_space=pltpu.SMEM)] + _attn_specs(bps) + [blk(Q_COLS), blk(LANES)],
        out_specs=[blk(Q_COLS), blk(2 * LANES), blk(2 * LANES), _full((1, LANES))],
        out_shape=[jax.ShapeDtypeStruct((t, Q_COLS), F32), jax.ShapeDtypeStruct((t, 2 * LANES), F32),
                   jax.ShapeDtypeStruct((t, 2 * LANES), F32), jax.ShapeDtypeStruct((1, LANES), F32)])


def _qk_norm_bwd(dqn, dcur, dprev, pq, gq2, gk2, riders=()):
    t = dqn.shape[0]
    nb = t // BLOCK
    tt = _token_tile(t)
    ni = t // tt

    def body(dq_ref, dc_ref, dt_ref, dn_ref, pq_ref, gq_ref, gk_ref, dp_ref, gbin_ref, gg_ref):
        i = pl.program_id(0)
        lo = _lo_mask()
        nxt = dn_ref[...] * (i < ni - 1).astype(F32)
        from_next = nxt if tt == BLOCK else jnp.concatenate([dt_ref[BLOCK:, :], nxt], axis=0)
        dkv = dc_ref[...] + from_next

        @pl.when(i == 0)
        def _():
            gbin_ref[...] = jnp.zeros_like(gbin_ref)
            gg_ref[...] = jnp.zeros_like(gg_ref)

        for p in range(5):
            cs = slice(p * LANES, (p + 1) * LANES)
            seg = pq_ref[:, cs]
            dn = dq_ref[:, cs] if p < 4 else dkv[:, 0:LANES]
            gain = gq_ref[...] if p < 4 else gk_ref[...]
            rr = lax.rsqrt(_half_mean(seg * seg, lo) + EPS)
            xh = seg * rr
            gd = dn * gain
            d = rr * (gd - xh * _half_mean(gd * xh, lo))
            dp_ref[:, cs] = d.astype(BF16)
            gbin_ref[:, cs] += jnp.sum(d, axis=0, keepdims=True)
            gg_ref[:, cs] += jnp.sum(dn * xh, axis=0, keepdims=True)
        dv = dkv[:, LANES:]
        dp_ref[:, 640:768] = dv.astype(BF16)
        gbin_ref[:, 640:768] += jnp.sum(dv, axis=0, keepdims=True)

    blk = lambda w: pl.BlockSpec((tt, w), lambda i: (i, 0))
    per = tt // BLOCK
    return _call(
        body, (dqn, dcur, dprev, dprev, pq, gq2, gk2), name="qk_norm_bwd", grid=(ni,), riders=riders,
        in_specs=[blk(Q_COLS), blk(2 * LANES), blk(2 * LANES),
                  pl.BlockSpec((BLOCK, 2 * LANES), lambda i: (jnp.minimum((i + 1) * per, nb - 1), 0)),
                  blk(QKV_COLS), _full((1, LANES)), _full((1, LANES))],
        out_specs=[blk(QKV_COLS), _full((1, QKV_COLS)), _full((1, 5 * LANES))],
        out_shape=[jax.ShapeDtypeStruct((t, QKV_COLS), BF16), jax.ShapeDtypeStruct((1, QKV_COLS), F32),
                   jax.ShapeDtypeStruct((1, 5 * LANES), F32)])


def _in_proj_bwd(dpq, dpc, w_in, x, dx1, g_mix):
    t = x.shape[0]
    tt = _token_tile(t)

    def body(dq_ref, dc_ref, w_ref, x_ref, d1_ref, g_ref, gx_ref, gg_ref):
        d_h = _dot_nt(dq_ref[...], w_ref[:, 0:QKV_COLS]) + _dot_nt(dc_ref[...], w_ref[:, QKV_COLS:])
        xv = x_ref[...]
        r = lax.rsqrt(jnp.mean(xv * xv, axis=-1, keepdims=True) + EPS)
        xh = xv * r
        gd = d_h * g_ref[...]
        gx_ref[...] = d1_ref[...] + r * (gd - xh * jnp.mean(gd * xh, axis=-1, keepdims=True))

        @pl.when(pl.program_id(0) == 0)
        def _():
            gg_ref[...] = jnp.zeros_like(gg_ref)

        gg_ref[...] += jnp.sum(d_h * xh, axis=0, keepdims=True)

    row = lambda w: pl.BlockSpec((tt, w), lambda i: (i, 0))
    return pl.pallas_call(
        body, name="in_proj_bwd", grid=(t // tt,),
        in_specs=[row(QKV_COLS), row(2 * CONV_W), _full((D_MODEL, IN_COLS)), row(D_MODEL), row(D_MODEL), _full((1, D_MODEL))],
        out_specs=[row(D_MODEL), _full((1, D_MODEL))],
        out_shape=[jax.ShapeDtypeStruct((t, D_MODEL), F32), jax.ShapeDtypeStruct((1, D_MODEL), F32)],
        compiler_params=_params(("arbitrary",)),
    )(dpq, dpc, w_in, x, dx1, g_mix)


def _row_block(r):
    if r <= 256:
        return r
    return max(b for b in range(8, 257, 8) if r % b == 0)


def _adamw(w, g, m, v, name):
    r, c = w.shape
    rb = _row_block(r)

    def body(w_ref, g_ref, m_ref, v_ref, d_ref, nm_ref, nv_ref):
        gv = g_ref[...]
        nm = ADAM_B1 * m_ref[...] + (1.0 - ADAM_B1) * gv
        nv = ADAM_B2 * v_ref[...] + (1.0 - ADAM_B2) * (gv * gv)
        m_hat = nm / (1.0 - ADAM_B1 ** ADAM_STEP)
        v_hat = nv / (1.0 - ADAM_B2 ** ADAM_STEP)
        d_ref[...] = -ADAM_LR * (m_hat / (jnp.sqrt(v_hat) + ADAM_EPS) + ADAM_WD * w_ref[...])
        nm_ref[...] = nm
        nv_ref[...] = nv

    blk = pl.BlockSpec((rb, c), lambda i: (i, 0))
    shp = jax.ShapeDtypeStruct((r, c), F32)
    return pl.pallas_call(
        body, name=name, grid=(r // rb,), in_specs=[blk] * 4, out_specs=[blk] * 3, out_shape=[shp] * 3,
        compiler_params=_params(("parallel",)),
    )(w, g, m, v)


def _place():
    x, y, c = lax.axis_index("x"), lax.axis_index("y"), lax.axis_index("c")
    chips = [(1 - x, y), (x, 1 - y), (1 - x, 1 - y)]
    return x, y, c, chips


def _gather_all(v):
    r = v.shape[0]

    def body(v_ref, all_ref, sum_ref, send_sems, recv_sems):
        x, y, c, _ = _place()
        me = 4 * x + 2 * y + c
        all_ref[me] = v_ref[...]
        copies = []
        for k in range(1, 8):
            kx, ky, kc = (k >> 2) & 1, (k >> 1) & 1, k & 1
            peer = (x ^ kx, y ^ ky, c ^ kc)
            cp = pltpu.make_async_remote_copy(src_ref=v_ref, dst_ref=all_ref.at[me], send_sem=send_sems.at[k - 1],
                                              recv_sem=recv_sems.at[k - 1], device_id=peer, device_id_type=MESH)
            cp.start()
            copies.append((cp, 4 * peer[0] + 2 * peer[1] + peer[2]))
        for k, (cp, src_idx) in enumerate(copies):
            pltpu.make_async_remote_copy(src_ref=v_ref, dst_ref=all_ref.at[src_idx], send_sem=send_sems.at[k],
                                         recv_sem=recv_sems.at[k], device_id=(x, y, c), device_id_type=MESH).wait_recv()
        for cp, _ in copies:
            cp.wait_send()
        tot = all_ref[0]
        for d in range(1, 8):
            tot = tot + all_ref[d]
        sum_ref[...] = tot

    vm = pl.BlockSpec(memory_space=pltpu.VMEM)
    return pl.pallas_call(
        body, name="gather_all", in_specs=[vm], out_specs=[vm, vm],
        out_shape=[jax.ShapeDtypeStruct((8, r, LANES), v.dtype), jax.ShapeDtypeStruct((r, LANES), v.dtype)],
        scratch_shapes=[pltpu.SemaphoreType.DMA((7,)), pltpu.SemaphoreType.DMA((7,))],
        compiler_params=pltpu.CompilerParams(vmem_limit_bytes=VMEM_LIMIT),
    )(v)


def _remote(src, dst, send_sem, recv_sem, to):
    return pltpu.make_async_remote_copy(src_ref=src, dst_ref=dst, send_sem=send_sem, recv_sem=recv_sem,
                                        device_id=to, device_id_type=MESH)


def _dma_sems(*shape):
    return pltpu.SemaphoreType.DMA(shape)


def _gather_first(shards):
    n = len(shards)

    def copies(ins, outs, sems):
        x, y, c, chips = _place()
        me = 2 * x + y
        local = [pltpu.make_async_copy(ins[a], outs[a].at[me], sems[2].at[a]) for a in range(n)]
        sends = [_remote(ins[a].at[c], outs[a].at[me, c], sems[0].at[a, j], sems[1].at[a, j], (*chip, c))
                 for a in range(n) for j, chip in enumerate(chips)]
        lands = [_remote(ins[a].at[c], outs[a].at[2 * chip[0] + chip[1], c], sems[0].at[a, j], sems[1].at[a, j], (x, y, c))
                 for a in range(n) for j, chip in enumerate(chips)]
        return local, sends, lands

    def start(ins, outs, sems):
        local, sends, _ = copies(ins, outs, sems)
        for cp in local + sends:
            cp.start()

    def finish(ins, outs, sems):
        local, sends, lands = copies(ins, outs, sems)
        for cp in lands:
            cp.wait_recv()
        for cp in sends:
            cp.wait_send()
        for cp in local:
            cp.wait()

    return _Rider(shards, [jax.ShapeDtypeStruct((4,) + s.shape, s.dtype) for s in shards],
                  [_dma_sems(n, 3), _dma_sems(n, 3), _dma_sems(n)], start, finish)


def _gather_second(partials):
    n = len(partials)

    def copies(outs, sems):
        x, y, c, chips = _place()
        sends, lands = [], []
        for a in range(n):
            for j, chip in enumerate(chips):
                mine = outs[a].at[2 * chip[0] + chip[1], c]
                theirs = outs[a].at[2 * chip[0] + chip[1], 1 - c]
                sends.append(_remote(mine, mine, sems[0].at[a, j], sems[1].at[a, j], (x, y, 1 - c)))
                lands.append(_remote(theirs, theirs, sems[0].at[a, j], sems[1].at[a, j], (x, y, c)))
        return sends, lands

    def start(ins, outs, sems):
        for cp in copies(outs, sems)[0]:
            cp.start()

    def finish(ins, outs, sems):
        sends, lands = copies(outs, sems)
        for cp in lands:
            cp.wait_recv()
        for cp in sends:
            cp.wait_send()

    return _Rider(partials, [jax.ShapeDtypeStruct(p.shape, p.dtype) for p in partials],
                  [_dma_sems(n, 3), _dma_sems(n, 3)], start, finish, aliases={a: a for a in range(n)})


def _swap_halves(grads):
    n = len(grads)

    def copies(ins, outs, sems):
        x, y, c, _ = _place()
        return [_remote(ins[a].at[j, 1 - c], outs[a].at[j], sems[0].at[a, j], sems[1].at[a, j], (x, y, 1 - c))
                for a in range(n) for j in range(4)]

    def start(ins, outs, sems):
        for cp in copies(ins, outs, sems):
            cp.start()

    def finish(ins, outs, sems):
        for cp in copies(ins, outs, sems):
            cp.wait()

    return _Rider(grads, [jax.ShapeDtypeStruct((4,) + g.shape[2:], g.dtype) for g in grads],
                  [_dma_sems(n, 4), _dma_sems(n, 4)], start, finish)


def _add_sibling(g, got, c_idx, name):
    _, _, h, c = g.shape

    def body(s_ref, a_ref, b_ref, o_ref):
        o_ref[...] = (a_ref[...] + b_ref[...]).astype(BF16)

    return pl.pallas_call(
        body, name=name,
        grid_spec=pltpu.PrefetchScalarGridSpec(
            num_scalar_prefetch=1, grid=(4,),
            in_specs=[pl.BlockSpec((None, None, h, c), lambda j, s: (j, s[0], 0, 0)),
                      pl.BlockSpec((None, h, c), lambda j, s: (j, 0, 0))],
            out_specs=pl.BlockSpec((None, h, c), lambda j, s: (j, 0, 0))),
        out_shape=jax.ShapeDtypeStruct((4, h, c), BF16),
        compiler_params=_params(("parallel",)),
    )(c_idx, g, got)


def _exchange_chips(parts):
    n = len(parts)

    def copies(ins, outs, sems):
        x, y, c, chips = _place()
        return [_remote(ins[a].at[2 * chip[0] + chip[1]], outs[a].at[j], sems[0].at[a, j], sems[1].at[a, j], (*chip, c))
                for a in range(n) for j, chip in enumerate(chips)]

    def start(ins, outs, sems):
        for cp in copies(ins, outs, sems):
            cp.start()

    def finish(ins, outs, sems):
        for cp in copies(ins, outs, sems):
            cp.wait()

    return _Rider(parts, [jax.ShapeDtypeStruct((3,) + p.shape[1:], p.dtype) for p in parts],
                  [_dma_sems(n, 3), _dma_sems(n, 3)], start, finish)


def _add_chips(part, got, chip_idx, name):
    _, h, c = part.shape

    def body(s_ref, a_ref, b_ref, o_ref):
        o_ref[...] = ((a_ref[...].astype(F32) + b_ref[0].astype(F32)) + b_ref[1].astype(F32)) + b_ref[2].astype(F32)

    return pl.pallas_call(
        body, name=name,
        grid_spec=pltpu.PrefetchScalarGridSpec(
            num_scalar_prefetch=1, grid=(1,),
            in_specs=[pl.BlockSpec((None, h, c), lambda i, s: (s[0], 0, 0)),
                      pl.BlockSpec((3, h, c), lambda i, s: (0, 0, 0))],
            out_specs=pl.BlockSpec((h, c), lambda i, s: (0, 0))),
        out_shape=jax.ShapeDtypeStruct((h, c), F32),
        compiler_params=_params(("arbitrary",)),
    )(chip_idx, part, got)


def _join_halves(halves):
    n = len(halves)

    def copies(ins, outs, sems):
        x, y, c, _ = _place()
        local = [pltpu.make_async_copy(ins[a], outs[a].at[c], sems[2].at[a]) for a in range(n)]
        sends = [_remote(ins[a], outs[a].at[c], sems[0].at[a], sems[1].at[a], (x, y, 1 - c)) for a in range(n)]
        lands = [_remote(ins[a], outs[a].at[1 - c], sems[0].at[a], sems[1].at[a], (x, y, c)) for a in range(n)]
        return local, sends, lands

    def start(ins, outs, sems):
        local, sends, _ = copies(ins, outs, sems)
        for cp in local + sends:
            cp.start()

    def finish(ins, outs, sems):
        local, sends, lands = copies(ins, outs, sems)
        for cp in lands:
            cp.wait_recv()
        for cp in sends:
            cp.wait_send()
        for cp in local:
            cp.wait()

    return _Rider(halves, [jax.ShapeDtypeStruct((2,) + h.shape, h.dtype) for h in halves],
                  [_dma_sems(n), _dma_sems(n), _dma_sems(n)], start, finish)


def _pack(parts):
    flat = []
    for p in parts:
        p = p.reshape(-1).astype(F32)
        flat.append(jnp.pad(p, (0, (-p.shape[0]) % LANES)))
    v = jnp.concatenate(flat)
    v = jnp.pad(v, (0, (-v.shape[0]) % (8 * LANES)))
    return v.reshape(-1, LANES)


def _unpack(v, shapes):
    flat = v.reshape(-1)
    out, off = [], 0
    for s in shapes:
        n = 1
        for d in s:
            n *= d
        out.append(flat[off:off + n].reshape(s))
        off += n + (-n) % LANES
    return out


def kernel(x, mix_norm_gain, w_in, b_in, q_norm_gain, k_norm_gain, attn_sinks, conv_dw_w, conv_dw_b, conv_norm_gain, conv_norm_bias, w_out, b_out, ffn_norm_gain, w_up, ffn_dw_w, ffn_dw_b, w_down, loss_target, m_mix_norm_gain, m_w_in, m_b_in, m_q_norm_gain, m_k_norm_gain, m_attn_sinks, m_conv_dw_w, m_conv_dw_b, m_conv_norm_gain, m_conv_norm_bias, m_w_out, m_b_out, m_ffn_norm_gain, m_w_up, m_ffn_dw_w, m_ffn_dw_b, m_w_down, v_mix_norm_gain, v_w_in, v_b_in, v_q_norm_gain, v_k_norm_gain, v_attn_sinks, v_conv_dw_w, v_conv_dw_b, v_conv_norm_gain, v_conv_norm_bias, v_w_out, v_b_out, v_ffn_norm_gain, v_w_up, v_ffn_dw_w, v_ffn_dw_b, v_w_down):
    t = x.shape[1]
    xi, yi, ci = lax.axis_index("x"), lax.axis_index("y"), lax.axis_index("c")
    chip = 2 * xi + yi
    c_idx = jnp.reshape(ci, (1,)).astype(jnp.int32)
    chip_idx = jnp.reshape(chip, (1,)).astype(jnp.int32)
    x2 = x.reshape(t, D_MODEL)
    tgt = loss_target.reshape(t, D_MODEL)

    big = [w_in, w_out, w_up, w_down]
    halves = [w.astype(BF16).reshape(2, w.shape[0] // 2, w.shape[1]) for w in big]
    h_wi, h_wo, h_wu, h_wd = halves
    (p_wi,) = _run_riders([_gather_first([h_wi])], "gather_w_in_first")
    (g_wi,) = _run_riders([_gather_second([p_wi])], "gather_w_in_second")
    wi = jnp.concatenate([g_wi[j].reshape(D_MODEL, IN_COLS // 4) for j in range(4)], axis=1)
    small_w, _ = _gather_all(_pack([conv_dw_w, ffn_dw_w]))
    per_chip = [_unpack(small_w[4 * (j // 2) + 2 * (j % 2)], [conv_dw_w.shape, ffn_dw_w.shape]) for j in range(4)]
    cw = jnp.concatenate([p[0] for p in per_chip], axis=1)
    fw = jnp.concatenate([p[1] for p in per_chip], axis=1)

    row = lambda a: a.reshape(1, -1)
    gq2 = row(jnp.concatenate([q_norm_gain, q_norm_gain]))
    gk2 = row(jnp.concatenate([k_norm_gain, k_norm_gain]))

    (h1, pq, pc, qkvn), (p_wo,) = _fwd_in(x2, row(mix_norm_gain), wi, row(b_in), gq2, gk2, riders=[_gather_first([h_wo])])
    bias = _attn_bias()
    (attn, lse), (g_wo, p_wu) = _attn_fwd(qkvn, attn_sinks, bias, riders=[_gather_second([p_wo]), _gather_first([h_wu])])
    (y_conv, c_act), (g_wu, p_wd) = _conv_fwd(pc, cw, row(conv_dw_b), row(conv_norm_gain), row(conv_norm_bias),
                                              riders=[_gather_second([p_wu]), _gather_first([h_wd])])
    wo = g_wo.reshape(D_MODEL, D_MODEL)
    (x1, h2), (g_wd,) = _out_proj(x2, attn, c_act, wo, row(b_out), row(ffn_norm_gain), riders=[_gather_second([p_wd])])
    wu = g_wu.reshape(4, D_MODEL, FFN_CB)
    wd = g_wd.reshape(D_FF, D_MODEL)
    hg, hu, up_g, up_u, act = _ffn_up(h2, wu, fw, row(ffn_dw_b))
    dy, loss_cols = _ffn_down(act, wd, x1, tgt)

    split = lambda g: g.reshape(4, 2, g.shape[1] // 2, g.shape[2])
    dg, du, gfb_g, gfb_u = _ffn_bwd_act(dy, wd, up_g, up_u)
    dhg, dhu, gfw_g, gfw_u = _ffn_bwd_conv(dg, du, hg, hu, fw)
    gw_down = _grad_weight(act, dy, 1, FFN_CB, "grad_w_down")
    gw_up = _grad_weight(h2, dhg, 2, D_MODEL, "grad_w_up_gate", lead=4)
    gw_up = _grad_weight(h2, dhu, 2, D_MODEL, "grad_w_up_lin", into=gw_up, offset=2)
    early = [split(gw_up), split(gw_down.reshape(4, D_FF // 4, D_MODEL))]
    early_names = ["w_up", "w_down"]
    (dx1, g_ffn_gain), got = _ffn_bwd_in(dhg, dhu, wu, x1, dy, row(ffn_norm_gain), riders=[_swap_halves(early)])
    early_part = [_add_sibling(g, r, c_idx, "add_sibling_" + nm_) for g, r, nm_ in zip(early, got, early_names)]
    dmix, g_b_out = _out_proj_bwd(dx1, wo)
    gw_out = jnp.concatenate([_grad_weight(attn, dx1, 1, Q_COLS, "grad_w_out_attn")[0],
                              _grad_weight(c_act, dx1, 1, CONV_W, "grad_w_out_conv")[0]], axis=0)
    mid = [split(gw_out.reshape(4, D_MODEL // 4, D_MODEL))]
    (dpc, g_cw, g_cb, g_lng, g_lnb, gbin_c), got2 = _conv_bwd(dmix, y_conv, pc, cw, row(conv_norm_gain), row(conv_norm_bias),
                                                              riders=[_exchange_chips(early_part), _swap_halves(mid)])
    early_red = [_add_chips(p, r, chip_idx, "add_chips_" + nm_) for p, r, nm_ in zip(early_part, got2[:2], early_names)]
    mid_part = [_add_sibling(mid[0], got2[2], c_idx, "add_sibling_w_out")]
    (dqn, dcur, dprev, g_sink), got3 = _attn_bwd(qkvn, dmix, lse, attn_sinks, bias,
                                                 riders=[_join_halves(early_red), _exchange_chips(mid_part)])
    early_g = got3[:2]
    mid_red = [_add_chips(mid_part[0], got3[2], chip_idx, "add_chips_w_out")]
    (dpq, gbin_q, g_qk), mid_g = _qk_norm_bwd(dqn, dcur, dprev, pq, gq2, gk2, riders=[_join_halves(mid_red)])
    grad_x, g_mix_gain = _in_proj_bwd(dpq, dpc, wi, x2, dx1, row(mix_norm_gain))
    gw_in = jnp.concatenate([_grad_weight(h1, dpq, 1, D_MODEL, "grad_w_in_qkv")[0],
                             _grad_weight(h1, dpc, 1, D_MODEL, "grad_w_in_conv")[0]], axis=1)

    g_qk = g_qk.reshape(5, 2, HEAD_DIM)
    small = [g_mix_gain, jnp.concatenate([gbin_q, gbin_c], axis=1), g_qk[:4].sum(axis=(0, 1)), g_qk[4].sum(axis=0),
             g_sink[0, :N_HEADS], g_cb, g_lng, g_lnb, g_b_out, g_ffn_gain, jnp.concatenate([gfb_g, gfb_u], axis=1),
             loss_cols, g_cw[:CONV_K], jnp.concatenate([gfw_g[:3], gfw_u[:3]], axis=1)]
    _, tot = _gather_all(_pack(small))
    rep_names = ["mix_norm_gain", "b_in", "q_norm_gain", "k_norm_gain", "attn_sinks", "conv_dw_b", "conv_norm_gain",
                 "conv_norm_bias", "b_out", "ffn_norm_gain", "ffn_dw_b"]
    rep_w = [mix_norm_gain, b_in, q_norm_gain, k_norm_gain, attn_sinks, conv_dw_b, conv_norm_gain, conv_norm_bias,
             b_out, ffn_norm_gain, ffn_dw_b]
    rep_m = [m_mix_norm_gain, m_b_in, m_q_norm_gain, m_k_norm_gain, m_attn_sinks, m_conv_dw_b, m_conv_norm_gain,
             m_conv_norm_bias, m_b_out, m_ffn_norm_gain, m_ffn_dw_b]
    rep_v = [v_mix_norm_gain, v_b_in, v_q_norm_gain, v_k_norm_gain, v_attn_sinks, v_conv_dw_b, v_conv_norm_gain,
             v_conv_norm_bias, v_b_out, v_ffn_norm_gain, v_ffn_dw_b]
    shapes = [w.shape for w in rep_w] + [(D_MODEL,), (CONV_K, CONV_W), (3, 2 * D_FF)]
    tot_parts = _unpack(tot, shapes)
    loss = (0.5 / D_MODEL) * jnp.sum(tot_parts[len(rep_w)])
    g_cw_full, g_fw_full = tot_parts[len(rep_w) + 1], tot_parts[len(rep_w) + 2]
    n_rep_rows = _pack(rep_w).shape[0]
    rep_d, rep_nm, rep_nv = _adamw(_pack(rep_w), tot[:n_rep_rows], _pack(rep_m), _pack(rep_v), "adamw_small")
    rep_shapes = [w.shape for w in rep_w]
    res = {}
    for nm_, g_, d_, m_, v_ in zip(rep_names, tot_parts, _unpack(rep_d, rep_shapes), _unpack(rep_nm, rep_shapes),
                                   _unpack(rep_nv, rep_shapes)):
        res[nm_] = (g_, d_, m_, v_)

    g_cw_mine = lax.dynamic_slice_in_dim(g_cw_full, chip * (CONV_W // 4), CONV_W // 4, axis=1)
    g_fw_mine = lax.dynamic_slice_in_dim(g_fw_full, chip * (2 * D_FF // 4), 2 * D_FF // 4, axis=1)
    res["conv_dw_w"] = (g_cw_mine, *_adamw(conv_dw_w, g_cw_mine, m_conv_dw_w, v_conv_dw_w, "adamw_conv_dw_w"))
    res["ffn_dw_w"] = (g_fw_mine, *_adamw(ffn_dw_w, g_fw_mine, m_ffn_dw_w, v_ffn_dw_w, "adamw_ffn_dw_w"))

    gw_in4 = gw_in.reshape(D_MODEL, 4, IN_COLS // 4).transpose(1, 0, 2)
    late = [split(gw_in4)]
    late_names = ["w_in"]
    got = _run_riders([_swap_halves(late)], "swap_halves_late")
    late_part = [_add_sibling(g, r, c_idx, "add_sibling_" + nm_) for g, r, nm_ in zip(late, got, late_names)]
    got2 = _run_riders([_exchange_chips(late_part)], "exchange_chips_late")
    late_red = [_add_chips(p, r, chip_idx, "add_chips_" + nm_) for p, r, nm_ in zip(late_part, got2, late_names)]
    late_g = _run_riders([_join_halves(late_red)], "join_halves_late")
    names = ["w_in", "w_out", "w_up", "w_down"]
    shard_g = list(late_g) + list(mid_g) + list(early_g)
    for nm_, w_, g_, m_, v_ in zip(names, big, shard_g, [m_w_in, m_w_out, m_w_up, m_w_down], [v_w_in, v_w_out, v_w_up, v_w_down]):
        g_ = g_.reshape(w_.shape)
        res[nm_] = (g_, *_adamw(w_, g_, m_, v_, "adamw_" + nm_))

    order = ["mix_norm_gain", "w_in", "b_in", "q_norm_gain", "k_norm_gain", "attn_sinks", "conv_dw_w", "conv_dw_b",
             "conv_norm_gain", "conv_norm_bias", "w_out", "b_out", "ffn_norm_gain", "w_up", "ffn_dw_w", "ffn_dw_b", "w_down"]
    return (loss, grad_x.reshape(x.shape), *[res[n][0] for n in order], *[res[n][1] for n in order],
            *[res[n][2] for n in order], *[res[n][3] for n in order])
```

```python
import functools

import jax
import jax.numpy as jnp
from jax import lax
from jax.experimental import pallas as pl
from jax.experimental.pallas import tpu as pltpu

F32 = jnp.float32
BF16 = jnp.bfloat16
MESH = pl.DeviceIdType.MESH

D_MODEL = 1024
HEAD_DIM = 64
N_HEADS = 8
Q_COLS = 512
QKV_COLS = 768
CONV_W = 512
CONV_K = 31
IN_COLS = 1792
D_FF = 2816
FFN_CB = 1408
BLOCK = 128
LANES = 128
EPS = 1e-6
NEG_INF = -1e30
SLOPES = tuple(float(2.0 ** (-(h + 1.0))) for h in range(N_HEADS))
HALO = 32
FHALO = 16
ROW_CHUNK = 64
GRAD_TILE = 2048
VMEM_LIMIT = 56 * 1024 * 1024

ADAM_LR = 0.001
ADAM_B1 = 0.9
ADAM_B2 = 0.999
ADAM_EPS = 1e-08
ADAM_WD = 0.01
ADAM_STEP = 10


def _params(sem=None):
    kw = dict(vmem_limit_bytes=VMEM_LIMIT)
    if sem is not None:
        kw["dimension_semantics"] = sem
    return pltpu.CompilerParams(**kw)


def _token_tile(t):
    return 512 if t % 512 == 0 and t >= 2048 else 128


def _wide_tile(t):
    return 1024 if t % 1024 == 0 and t >= 2048 else _token_tile(t)


def _sig(v):
    return 1.0 / (1.0 + jnp.exp(-v))


def _lo_mask():
    return lax.broadcasted_iota(jnp.int32, (1, LANES), 1) < HEAD_DIM


def _half_mean(v, lo):
    s_lo = jnp.sum(jnp.where(lo, v, 0.0), axis=-1, keepdims=True)
    s_hi = jnp.sum(jnp.where(lo, 0.0, v), axis=-1, keepdims=True)
    return jnp.where(lo, s_lo, s_hi) * (1.0 / HEAD_DIM)


def _dot(a, b):
    return jnp.dot(a, b, preferred_element_type=F32)


def _dot_nt(a, b):
    return lax.dot_general(a, b, (((1,), (1,)), ((), ())), preferred_element_type=F32)


def _dot_tn(a, b):
    return lax.dot_general(a, b, (((0,), (0,)), ((), ())), preferred_element_type=F32)


def _full(shape):
    nd = len(shape)
    return pl.BlockSpec(shape, lambda *_: (0,) * nd)


def _rows_to_tile(rows, n_rows):
    c = rows[0].shape[-1]
    rid = lax.broadcasted_iota(jnp.int32, (n_rows, c), 0)
    out = jnp.zeros((n_rows, c), F32)
    for k, r in enumerate(rows):
        out = jnp.where(rid == k, r, out)
    return out


ANY = pl.BlockSpec(memory_space=pl.ANY)


class _Rider:
    def __init__(self, ins, outs, sems, start, finish, aliases=None):
        self.ins, self.outs, self.sems = list(ins), list(outs), list(sems)
        self.start, self.finish, self.aliases = start, finish, dict(aliases or {})


def _join_riders(riders):
    ins, outs, sems, aliases, spans = [], [], [], {}, []
    for r in riders:
        spans.append((len(ins), len(outs), len(sems), r))
        for a, b in r.aliases.items():
            aliases[len(ins) + a] = len(outs) + b
        ins += r.ins
        outs += r.outs
        sems += r.sems

    def each(which):
        def run(i_refs, o_refs, s_refs):
            for i0, o0, s0, r in spans:
                getattr(r, which)(i_refs[i0:i0 + len(r.ins)], o_refs[o0:o0 + len(r.outs)], s_refs[s0:s0 + len(r.sems)])
        return run

    return _Rider(ins, outs, sems, each("start"), each("finish"), aliases)


def _call(body, args, *, name, grid, in_specs, out_specs, out_shape, scratch=(), riders=()):
    in_specs, out_specs, out_shape, scratch = list(in_specs), list(out_specs), list(out_shape), list(scratch)
    sem = ("arbitrary",) * len(grid)
    if not riders:
        outs = pl.pallas_call(body, name=name, grid=grid, in_specs=in_specs, out_specs=out_specs, out_shape=out_shape,
                              scratch_shapes=scratch, compiler_params=_params(sem))(*args)
        return list(outs), []
    r = _join_riders(riders)
    n_in, n_out, n_scr = len(in_specs), len(out_specs), len(scratch)
    nri, nro = len(r.ins), len(r.outs)

    def full(*refs):
        ins, rin = refs[:n_in], refs[n_in:n_in + nri]
        o0 = n_in + nri
        outs, rout = refs[o0:o0 + n_out], refs[o0 + n_out:o0 + n_out + nro]
        s0 = o0 + n_out + nro
        scr, rsem = refs[s0:s0 + n_scr], refs[s0 + n_scr:]
        first = functools.reduce(jnp.logical_and, [pl.program_id(k) == 0 for k in range(len(grid))])
        last = functools.reduce(jnp.logical_and, [pl.program_id(k) == grid[k] - 1 for k in range(len(grid))])

        @pl.when(first)
        def _():
            r.start(rin, rout, rsem)

        body(*ins, *outs, *scr)

        @pl.when(last)
        def _():
            r.finish(rin, rout, rsem)

    outs = pl.pallas_call(
        full, name=name, grid=grid, in_specs=in_specs + [ANY] * nri, out_specs=out_specs + [ANY] * nro,
        out_shape=out_shape + r.outs, scratch_shapes=scratch + r.sems,
        input_output_aliases={n_in + a: n_out + b for a, b in r.aliases.items()},
        compiler_params=_params(sem))(*args, *r.ins)
    return list(outs[:n_out]), list(outs[n_out:])


def _run_riders(riders, name):
    r = _join_riders(riders)
    nri, nro = len(r.ins), len(r.outs)

    def body(*refs):
        rin, rout, rsem = refs[:nri], refs[nri:nri + nro], refs[nri + nro:]
        r.start(rin, rout, rsem)
        r.finish(rin, rout, rsem)

    outs = pl.pallas_call(body, name=name, in_specs=[ANY] * nri, out_specs=[ANY] * nro, out_shape=r.outs,
                          scratch_shapes=r.sems, input_output_aliases=r.aliases)(*r.ins)
    return list(outs)


def _fwd_in(x, g_mix, w_in, b_in, gq2, gk2, riders=()):
    t = x.shape[0]
    tt = _wide_tile(t)

    def body(x_ref, g_ref, w_ref, b_ref, gq_ref, gk_ref, h1_ref, pq_ref, pc_ref, qkvn_ref):
        xv = x_ref[...]
        r = lax.rsqrt(jnp.mean(xv * xv, axis=-1, keepdims=True) + EPS)
        h = (xv * r * g_ref[...]).astype(BF16)
        h1_ref[...] = h
        proj = _dot(h, w_ref[...]) + b_ref[...]
        pq_ref[...] = proj[:, :QKV_COLS]
        pc_ref[...] = proj[:, QKV_COLS:]
        lo = _lo_mask()
        for p in range(5):
            seg = proj[:, p * LANES:(p + 1) * LANES]
            rr = lax.rsqrt(_half_mean(seg * seg, lo) + EPS)
            gain = gq_ref[...] if p < 4 else gk_ref[...]
            qkvn_ref[:, p * LANES:(p + 1) * LANES] = (seg * rr * gain).astype(BF16)
        qkvn_ref[:, 640:768] = proj[:, 640:768].astype(BF16)

    return _call(
        body, (x, g_mix, w_in, b_in, gq2, gk2), name="fwd_in", grid=(t // tt,), riders=riders,
        in_specs=[pl.BlockSpec((tt, D_MODEL), lambda i: (i, 0)), _full((1, D_MODEL)), _full((D_MODEL, IN_COLS)),
                  _full((1, IN_COLS)), _full((1, LANES)), _full((1, LANES))],
        out_specs=[pl.BlockSpec((tt, D_MODEL), lambda i: (i, 0)), pl.BlockSpec((tt, QKV_COLS), lambda i: (i, 0)),
                   pl.BlockSpec((tt, 2 * CONV_W), lambda i: (i, 0)), pl.BlockSpec((tt, QKV_COLS), lambda i: (i, 0))],
        out_shape=[jax.ShapeDtypeStruct((t, D_MODEL), BF16), jax.ShapeDtypeStruct((t, QKV_COLS), F32),
                   jax.ShapeDtypeStruct((t, 2 * CONV_W), F32), jax.ShapeDtypeStruct((t, QKV_COLS), BF16)])


GROUP = 4
GROUP_ROWS = GROUP * BLOCK


def _attn_bias():
    qi = jnp.arange(GROUP_ROWS)[:, None] % BLOCK
    kj = jnp.arange(2 * BLOCK)[None, :]
    rel = qi + BLOCK - kj
    band = (rel >= 0) & (rel < BLOCK)
    slope_rows = jnp.repeat(jnp.asarray(SLOPES, F32).reshape(N_HEADS // GROUP, GROUP), BLOCK, axis=1)
    penalty = -(slope_rows[:, :, None] * rel.astype(F32)[None])
    later = jnp.where(band[None], penalty, NEG_INF)
    first = jnp.where((band & (kj >= BLOCK))[None], penalty, NEG_INF)
    return jnp.stack([first, later])


def _attn_blocks_per_step(nb):
    return 2 if nb % 2 == 0 else 1


def _attn_specs(bps):
    return [pl.BlockSpec((bps * BLOCK, Q_COLS), lambda i: (i, 0)),
            pl.BlockSpec((bps * BLOCK, 2 * LANES), lambda i: (i, 2)),
            pl.BlockSpec((BLOCK, 2 * LANES), lambda i: (jnp.maximum(bps * i - 1, 0), 2)),
            _full((2, N_HEADS // GROUP, GROUP_ROWS, 2 * BLOCK))]


def _attn_window(b, kvc_ref, kvp_ref):
    own = kvc_ref[b * BLOCK:(b + 1) * BLOCK, :]
    before = kvp_ref[...] if b == 0 else kvc_ref[(b - 1) * BLOCK:b * BLOCK, :]
    return jnp.concatenate([before, own], axis=0)


def _attn_bias_of(bias_ref, b, kk):
    if b > 0:
        return bias_ref[1, kk]
    return bias_ref[jnp.minimum(pl.program_id(0), 1), kk]


def _attn_keys(kv):
    kv = kv.astype(F32)
    lo = _lo_mask()

    def both_halves(pair):
        rolled = pltpu.roll(pair, HEAD_DIM, 1)
        return [jnp.where(lo, pair, rolled).astype(BF16), jnp.where(lo, rolled, pair).astype(BF16)]

    return both_halves(kv[:, :LANES]), both_halves(kv[:, LANES:])


def _per_head_column(values):
    seg = lax.broadcasted_iota(jnp.int32, (GROUP_ROWS, 1), 0) // BLOCK
    col = jnp.zeros((GROUP_ROWS, 1), F32) + values[GROUP - 1]
    for a in range(GROUP - 2, -1, -1):
        col = jnp.where(seg == a, values[a], col)
    return col


def _stack_heads(ref, rows, kk, lo, dtype):
    parts = []
    for a in range(GROUP):
        h = GROUP * kk + a
        pair = ref[rows, (h // 2) * LANES:(h // 2 + 1) * LANES]
        hm = lo if h % 2 == 0 else jnp.logical_not(lo)
        parts.append(jnp.where(hm, pair, jnp.zeros_like(pair)).astype(dtype))
    return jnp.concatenate(parts, axis=0)


def _group_scores(q_ref, rows, kk, lo, k_dup, bias):
    qs = _stack_heads(q_ref, rows, kk, lo, BF16)
    return _dot_nt(qs, k_dup[kk]) * 0.125 + bias, qs


def _attn_fwd(qkvn, sinks, bias, riders=()):
    t = qkvn.shape[0]
    nb = t // BLOCK
    bps = _attn_blocks_per_step(nb)

    def body(sink_ref, q_ref, kvc_ref, kvp_ref, bias_ref, o_ref, lse_ref):
        lo = _lo_mask()
        lane = lax.broadcasted_iota(jnp.int32, (BLOCK, LANES), 1)
        for b in range(bps):
            rows_b = slice(b * BLOCK, (b + 1) * BLOCK)
            k_dup, v_dup = _attn_keys(_attn_window(b, kvc_ref, kvp_ref))
            lse_t = jnp.zeros((BLOCK, LANES), F32)
            outs = [jnp.zeros((BLOCK, LANES), F32) for _ in range(4)]
            for kk in range(N_HEADS // GROUP):
                s, _ = _group_scores(q_ref, rows_b, kk, lo, k_dup, _attn_bias_of(bias_ref, b, kk))
                sink = _per_head_column([sink_ref[GROUP * kk + a] for a in range(GROUP)])
                m = jnp.maximum(jnp.max(s, axis=-1, keepdims=True), sink)
                pe = jnp.exp(s - m)
                l = jnp.sum(pe, axis=-1, keepdims=True) + jnp.exp(sink - m)
                lse = m + jnp.log(l)
                o = _dot((pe / l).astype(BF16), v_dup[kk])
                for a in range(GROUP):
                    h = GROUP * kk + a
                    rows = slice(a * BLOCK, (a + 1) * BLOCK)
                    hm = lo if h % 2 == 0 else jnp.logical_not(lo)
                    outs[h // 2] = outs[h // 2] + jnp.where(hm, o[rows], 0.0)
                    lse_t = jnp.where(lane == h, lse[rows], lse_t)
            for p in range(4):
                o_ref[rows_b, p * LANES:(p + 1) * LANES] = outs[p].astype(BF16)
            lse_ref[rows_b, :] = lse_t

    blk = lambda w: pl.BlockSpec((bps * BLOCK, w), lambda i: (i, 0))
    return _call(
        body, (sinks, qkvn, qkvn, qkvn, bias), name="attn_fwd", grid=(nb // bps,), riders=riders,
        in_specs=[pl.BlockSpec(memory_space=pltpu.SMEM)] + _attn_specs(bps),
        out_specs=[blk(Q_COLS), blk(LANES)],
        out_shape=[jax.ShapeDtypeStruct((t, Q_COLS), BF16), jax.ShapeDtypeStruct((t, LANES), F32)])


def _glu(pc):
    return pc[:, :CONV_W] * _sig(pc[:, CONV_W:])


def _glu_chunks(pc_ref, scr, tt):
    for r0, cs in _chunks(tt, CONV_W):
        rows = slice(r0, r0 + ROW_CHUNK)
        gate = slice(cs.start + CONV_W, cs.stop + CONV_W)
        scr[HALO + r0:HALO + r0 + ROW_CHUNK, cs] = pc_ref[rows, cs] * _sig(pc_ref[rows, gate])


def _group_norm_stats(seg, lo):
    mu = _half_mean(seg, lo)
    d = seg - mu
    rstd = lax.rsqrt(_half_mean(d * d, lo) + EPS)
    return d * rstd, rstd


def _shifted_copies(src, dst, tt):
    n = tt + HALO - 8
    for s in range(1, 8):
        dst[s - 1, 0:n, :] = src[s:s + n, :]


def _tap_rows(src, shifted, off, r0, cs):
    q, s = divmod(off, 8)
    if s == 0:
        return src[r0 + off:r0 + off + ROW_CHUNK, cs]
    return shifted[s - 1, r0 + 8 * q:r0 + 8 * q + ROW_CHUNK, cs]


def _shift_scratch(tt):
    return pltpu.VMEM((7, tt + HALO - 8, CONV_W), F32)


def _conv_fwd(pc, cw, cb, ln_g, ln_b, riders=()):
    t = pc.shape[0]
    tt = _token_tile(t)

    def body(cur_ref, prev_ref, w_ref, b_ref, g_ref, bb_ref, y_ref, c_ref, scr, shf):
        i = pl.program_id(0)
        scr[0:HALO, :] = _glu(prev_ref[...]) * (i > 0).astype(F32)
        _glu_chunks(cur_ref, scr, tt)
        _shifted_copies(scr, shf, tt)
        lo = _lo_mask()
        for r0, cs in _chunks(tt, CONV_W):
            acc = jnp.zeros((ROW_CHUNK, LANES), F32) + b_ref[:, cs]
            for k in range(CONV_K):
                acc = acc + w_ref[k:k + 1, cs] * _tap_rows(scr, shf, 2 + k, r0, cs)
            y_ref[r0:r0 + ROW_CHUNK, cs] = acc
            yh, _ = _group_norm_stats(acc, lo)
            z = yh * g_ref[:, cs] + bb_ref[:, cs]
            c_ref[r0:r0 + ROW_CHUNK, cs] = (z * _sig(z)).astype(BF16)

    hb = tt // HALO
    return _call(
        body, (pc, pc, cw, cb, ln_g, ln_b), name="conv_fwd", grid=(t // tt,), riders=riders,
        in_specs=[pl.BlockSpec((tt, 2 * CONV_W), lambda i: (i, 0)),
                  pl.BlockSpec((HALO, 2 * CONV_W), lambda i: (jnp.maximum(i * hb - 1, 0), 0)),
                  _full((CONV_K, CONV_W)), _full((1, CONV_W)), _full((1, CONV_W)), _full((1, CONV_W))],
        out_specs=[pl.BlockSpec((tt, CONV_W), lambda i: (i, 0)), pl.BlockSpec((tt, CONV_W), lambda i: (i, 0))],
        out_shape=[jax.ShapeDtypeStruct((t, CONV_W), F32), jax.ShapeDtypeStruct((t, CONV_W), BF16)],
        scratch=[pltpu.VMEM((tt + HALO, CONV_W), F32), _shift_scratch(tt)])


def _out_proj(x, attn, c, w_out, b_out, g_ffn, riders=()):
    t = x.shape[0]
    tt = _wide_tile(t)

    def body(x_ref, a_ref, c_ref, w_ref, b_ref, g_ref, x1_ref, h2_ref):
        x1 = x_ref[...] + _dot(a_ref[...], w_ref[0:Q_COLS, :]) + _dot(c_ref[...], w_ref[Q_COLS:, :]) + b_ref[...]
        x1_ref[...] = x1
        r = lax.rsqrt(jnp.mean(x1 * x1, axis=-1, keepdims=True) + EPS)
        h2_ref[...] = (x1 * r * g_ref[...]).astype(BF16)

    row = lambda w: pl.BlockSpec((tt, w), lambda i: (i, 0))
    return _call(
        body, (x, attn, c, w_out, b_out, g_ffn), name="out_proj", grid=(t // tt,), riders=riders,
        in_specs=[row(D_MODEL), row(Q_COLS), row(CONV_W), _full((D_MODEL, D_MODEL)), _full((1, D_MODEL)), _full((1, D_MODEL))],
        out_specs=[row(D_MODEL), row(D_MODEL)],
        out_shape=[jax.ShapeDtypeStruct((t, D_MODEL), F32), jax.ShapeDtypeStruct((t, D_MODEL), BF16)])


def _chunks(rows, cols):
    return [(r0, slice(c0, c0 + LANES)) for c0 in range(0, cols, LANES) for r0 in range(0, rows, ROW_CHUNK)]


def _ffn_up(h2, w_up, dw, db):
    t = h2.shape[0]
    tt = _token_tile(t)
    nj = D_FF // FFN_CB

    def body(hc_ref, hp_ref, wg_ref, wu_ref, dwg_ref, dwu_ref, dbg_ref, dbu_ref,
             hg_ref, hu_ref, upg_ref, upu_ref, act_ref, sg, su):
        i = pl.program_id(1)
        hc = hc_ref[...]
        hp = hp_ref[...] * (i > 0).astype(BF16)
        ups = []
        for w_ref, dw_ref, db_ref, h_ref, up_ref, scr in ((wg_ref, dwg_ref, dbg_ref, hg_ref, upg_ref, sg),
                                                          (wu_ref, dwu_ref, dbu_ref, hu_ref, upu_ref, su)):
            cur = _dot(hc, w_ref[...])
            h_ref[...] = cur.astype(BF16)
            scr[0:FHALO, :] = _dot(hp, w_ref[...])
            scr[FHALO:FHALO + tt, :] = cur
            up = (dw_ref[0:1, :] * scr[FHALO - 2:FHALO - 2 + tt, :] + dw_ref[1:2, :] * scr[FHALO - 1:FHALO - 1 + tt, :]
                  + dw_ref[2:3, :] * cur + db_ref[...])
            up_ref[...] = up
            ups.append(up)
        g, u = ups
        act_ref[...] = (g * _sig(g) * u).astype(BF16)

    fb = tt // FHALO
    colg = lambda r: pl.BlockSpec((r, FFN_CB), lambda j, i: (0, j))
    colu = lambda r: pl.BlockSpec((r, FFN_CB), lambda j, i: (0, j + nj))
    tile = pl.BlockSpec((tt, FFN_CB), lambda j, i: (i, j))
    return pl.pallas_call(
        body, name="ffn_up", grid=(nj, t // tt),
        in_specs=[pl.BlockSpec((tt, D_MODEL), lambda j, i: (i, 0)),
                  pl.BlockSpec((FHALO, D_MODEL), lambda j, i: (jnp.maximum(i * fb - 1, 0), 0)),
                  pl.BlockSpec((None, D_MODEL, FFN_CB), lambda j, i: (j, 0, 0)),
                  pl.BlockSpec((None, D_MODEL, FFN_CB), lambda j, i: (j + nj, 0, 0)),
                  colg(3), colu(3), colg(1), colu(1)],
        out_specs=[tile] * 5,
        out_shape=[jax.ShapeDtypeStruct((t, D_FF), BF16), jax.ShapeDtypeStruct((t, D_FF), BF16),
                   jax.ShapeDtypeStruct((t, D_FF), F32), jax.ShapeDtypeStruct((t, D_FF), F32),
                   jax.ShapeDtypeStruct((t, D_FF), BF16)],
        scratch_shapes=[pltpu.VMEM((tt + FHALO, FFN_CB), F32), pltpu.VMEM((tt + FHALO, FFN_CB), F32)],
        compiler_params=_params(("parallel", "parallel")),
    )(h2, h2, w_up, w_up, dw, dw, db, db)


def _ffn_down(act, w_down, x1, target):
    t = act.shape[0]
    tt = _token_tile(t)

    def body(a_ref, w_ref, x1_ref, t_ref, dy_ref, loss_ref):
        err = x1_ref[...] + _dot(a_ref[...], w_ref[...]) - t_ref[...]
        dy_ref[...] = err * (1.0 / D_MODEL)

        @pl.when(pl.program_id(0) == 0)
        def _():
            loss_ref[...] = jnp.zeros_like(loss_ref)

        loss_ref[...] += jnp.sum(err * err, axis=0, keepdims=True)

    row = lambda w: pl.BlockSpec((tt, w), lambda i: (i, 0))
    return pl.pallas_call(
        body, name="ffn_down", grid=(t // tt,),
        in_specs=[row(D_FF), _full((D_FF, D_MODEL)), row(D_MODEL), row(D_MODEL)],
        out_specs=[row(D_MODEL), _full((1, D_MODEL))],
        out_shape=[jax.ShapeDtypeStruct((t, D_MODEL), F32), jax.ShapeDtypeStruct((1, D_MODEL), F32)],
        compiler_params=_params(("arbitrary",)),
    )(act, w_down, x1, target)


def _ffn_bwd_act(dy, w_down, up_g, up_u):
    t = dy.shape[0]
    tt = _token_tile(t)
    nj = D_FF // FFN_CB

    def body(dy_ref, wd_ref, g_ref, u_ref, dg_ref, du_ref, gbg_ref, gbu_ref):
        i = pl.program_id(1)
        d_act = _dot_nt(dy_ref[...].astype(BF16), wd_ref[...])
        g, u = g_ref[...], u_ref[...]
        s = _sig(g)
        d_u = d_act * (g * s)
        d_g = d_act * u * (s * (1.0 + g * (1.0 - s)))

        @pl.when(i == 0)
        def _():
            for r in (gbg_ref, gbu_ref):
                r[...] = jnp.zeros_like(r)

        for d, o_ref, gb_ref in ((d_g, dg_ref, gbg_ref), (d_u, du_ref, gbu_ref)):
            o_ref[...] = d.astype(BF16)
            gb_ref[...] += jnp.sum(d, axis=0, keepdims=True)

    tile = pl.BlockSpec((tt, FFN_CB), lambda j, i: (i, j))
    acc = pl.BlockSpec((1, FFN_CB), lambda j, i: (0, j))
    return pl.pallas_call(
        body, name="ffn_bwd_act", grid=(nj, t // tt),
        in_specs=[pl.BlockSpec((tt, D_MODEL), lambda j, i: (i, 0)), pl.BlockSpec((FFN_CB, D_MODEL), lambda j, i: (j, 0)),
                  tile, tile],
        out_specs=[tile, tile, acc, acc],
        out_shape=[jax.ShapeDtypeStruct((t, D_FF), BF16), jax.ShapeDtypeStruct((t, D_FF), BF16),
                   jax.ShapeDtypeStruct((1, D_FF), F32), jax.ShapeDtypeStruct((1, D_FF), F32)],
        compiler_params=_params(("parallel", "arbitrary")),
    )(dy, w_down, up_g, up_u)


def _ffn_bwd_conv(dg, du, hg, hu, dw):
    t = dg.shape[0]
    tt = _token_tile(t)
    nj = D_FF // FFN_CB
    ni = t // tt

    def body(gc_ref, gn_ref, uc_ref, un_ref, hg_ref, hu_ref, dwg_ref, dwu_ref, og_ref, ou_ref, gwg_ref, gwu_ref, scr):
        i = pl.program_id(1)
        last = (i < ni - 1).astype(F32)

        @pl.when(i == 0)
        def _():
            gwg_ref[...] = jnp.zeros_like(gwg_ref)
            gwu_ref[...] = jnp.zeros_like(gwu_ref)

        for c_ref, n_ref, h_ref, dw_ref, o_ref, gw_ref in ((gc_ref, gn_ref, hg_ref, dwg_ref, og_ref, gwg_ref),
                                                           (uc_ref, un_ref, hu_ref, dwu_ref, ou_ref, gwu_ref)):
            scr[0:tt, :] = c_ref[...].astype(F32)
            scr[tt:tt + FHALO, :] = n_ref[...].astype(F32) * last
            sums = None
            for r0, cs in _chunks(tt, FFN_CB):
                shifted = [scr[r0 + d:r0 + d + ROW_CHUNK, cs] for d in (2, 1, 0)]
                o_ref[r0:r0 + ROW_CHUNK, cs] = (dw_ref[0:1, cs] * shifted[0] + dw_ref[1:2, cs] * shifted[1]
                                                + dw_ref[2:3, cs] * shifted[2]).astype(BF16)
                hw = h_ref[r0:r0 + ROW_CHUNK, cs].astype(F32)
                prods = [hw * d for d in shifted]
                sums = prods if r0 == 0 else [a + b for a, b in zip(sums, prods)]
                if r0 == tt - ROW_CHUNK:
                    gw_ref[:, cs] += _rows_to_tile([jnp.sum(a, axis=0, keepdims=True) for a in sums], 8)

    fb = tt // FHALO
    tile = pl.BlockSpec((tt, FFN_CB), lambda j, i: (i, j))
    nxt = pl.BlockSpec((FHALO, FFN_CB), lambda j, i: (jnp.minimum((i + 1) * fb, t // FHALO - 1), j))
    acc = pl.BlockSpec((8, FFN_CB), lambda j, i: (0, j))
    return pl.pallas_call(
        body, name="ffn_bwd_conv", grid=(nj, ni),
        in_specs=[tile, nxt, tile, nxt, tile, tile, pl.BlockSpec((3, FFN_CB), lambda j, i: (0, j)),
                  pl.BlockSpec((3, FFN_CB), lambda j, i: (0, j + nj))],
        out_specs=[tile, tile, acc, acc],
        out_shape=[jax.ShapeDtypeStruct((t, D_FF), BF16), jax.ShapeDtypeStruct((t, D_FF), BF16),
                   jax.ShapeDtypeStruct((8, D_FF), F32), jax.ShapeDtypeStruct((8, D_FF), F32)],
        scratch_shapes=[pltpu.VMEM((tt + FHALO, FFN_CB), F32)],
        compiler_params=_params(("parallel", "arbitrary")),
    )(dg, dg, du, du, hg, hu, dw, dw)


def _ffn_bwd_in(dhg, dhu, w_up, x1, dy, g_ffn, riders=()):
    t = x1.shape[0]
    tt = _token_tile(t)

    def body(dg_ref, du_ref, w_ref, x1_ref, dy_ref, g_ref, dx_ref, gg_ref):
        d_h2 = (_dot_nt(dg_ref[:, 0:FFN_CB], w_ref[0]) + _dot_nt(dg_ref[:, FFN_CB:], w_ref[1])
                + _dot_nt(du_ref[:, 0:FFN_CB], w_ref[2]) + _dot_nt(du_ref[:, FFN_CB:], w_ref[3]))
        x1 = x1_ref[...]
        r = lax.rsqrt(jnp.mean(x1 * x1, axis=-1, keepdims=True) + EPS)
        xh = x1 * r
        gd = d_h2 * g_ref[...]
        dx_ref[...] = dy_ref[...] + r * (gd - xh * jnp.mean(gd * xh, axis=-1, keepdims=True))

        @pl.when(pl.program_id(0) == 0)
        def _():
            gg_ref[...] = jnp.zeros_like(gg_ref)

        gg_ref[...] += jnp.sum(d_h2 * xh, axis=0, keepdims=True)

    row = lambda w: pl.BlockSpec((tt, w), lambda i: (i, 0))
    return _call(
        body, (dhg, dhu, w_up, x1, dy, g_ffn), name="ffn_bwd_in", grid=(t // tt,), riders=riders,
        in_specs=[row(D_FF), row(D_FF), _full((4, D_MODEL, FFN_CB)), row(D_MODEL), row(D_MODEL), _full((1, D_MODEL))],
        out_specs=[row(D_MODEL), _full((1, D_MODEL))],
        out_shape=[jax.ShapeDtypeStruct((t, D_MODEL), F32), jax.ShapeDtypeStruct((1, D_MODEL), F32)])


def _grad_weight(a, b, nj, mb, name, lead=None, into=None, offset=0):
    t, m = a.shape
    n = b.shape[1]
    nb_ = n // nj
    tt = GRAD_TILE if t % GRAD_TILE == 0 else _token_tile(t)

    def body(*refs):
        a_ref, b_ref, o_ref = refs[0], refs[1], refs[-1]

        @pl.when(pl.program_id(2) == 0)
        def _():
            o_ref[...] = jnp.zeros_like(o_ref)

        o_ref[0] += _dot_tn(a_ref[...].astype(BF16), b_ref[...].astype(BF16))

    in_specs = [pl.BlockSpec((tt, mb), lambda j, mi, i: (i, mi)), pl.BlockSpec((tt, nb_), lambda j, mi, i: (i, j))]
    out_shape = jax.ShapeDtypeStruct((lead or nj, m, nb_) if into is None else into.shape, F32)
    return pl.pallas_call(
        body, name=name, grid=(nj, m // mb, t // tt),
        in_specs=in_specs if into is None else in_specs + [pl.BlockSpec(memory_space=pl.ANY)],
        out_specs=pl.BlockSpec((1, mb, nb_), lambda j, mi, i: (j + offset, mi, 0)),
        out_shape=out_shape,
        input_output_aliases={} if into is None else {2: 0},
        compiler_params=_params(("parallel", "parallel", "arbitrary")),
    )(*((a, b) if into is None else (a, b, into)))


def _out_proj_bwd(dx1, w_out):
    t = dx1.shape[0]
    tt = _wide_tile(t)

    def body(d_ref, w_ref, dm_ref, gb_ref):
        d = d_ref[...]
        dm_ref[...] = _dot_nt(d.astype(BF16), w_ref[...])

        @pl.when(pl.program_id(0) == 0)
        def _():
            gb_ref[...] = jnp.zeros_like(gb_ref)

        gb_ref[...] += jnp.sum(d, axis=0, keepdims=True)

    row = pl.BlockSpec((tt, D_MODEL), lambda i: (i, 0))
    return pl.pallas_call(
        body, name="out_proj_bwd", grid=(t // tt,),
        in_specs=[row, _full((D_MODEL, D_MODEL))],
        out_specs=[row, _full((1, D_MODEL))],
        out_shape=[jax.ShapeDtypeStruct((t, D_MODEL), F32), jax.ShapeDtypeStruct((1, D_MODEL), F32)],
        compiler_params=_params(("arbitrary",)),
    )(dx1, w_out)


def _conv_bwd(dmix, y, pc, cw, ln_g, ln_b, riders=()):
    t = y.shape[0]
    tt = _token_tile(t)
    ni = t // tt
    ncb = CONV_W // LANES

    def body(dc_ref, dcn_ref, y_ref, yn_ref, pc_ref, pcp_ref, w_ref, g_ref, bb_ref,
             dp_ref, gw_ref, gb_ref, gg_ref, gbb_ref, gbin_ref, scr_d, scr_c, scr_o, shf_d, shf_c):
        i = pl.program_id(0)
        lo = _lo_mask()

        @pl.when(i == 0)
        def _():
            for r in (gw_ref, gb_ref, gg_ref, gbb_ref, gbin_ref):
                r[...] = jnp.zeros_like(r)

        def norm_bwd(dc, yv, cs):
            yh, rstd = _group_norm_stats(yv, lo)
            z = yh * g_ref[:, cs] + bb_ref[:, cs]
            s = _sig(z)
            dz = dc * (s * (1.0 + z * (1.0 - s)))
            dyh = dz * g_ref[:, cs]
            d_y = rstd * (dyh - _half_mean(dyh, lo) - yh * _half_mean(dyh * yh, lo))
            return d_y, dz, yh

        for p in range(ncb):
            cs = slice(p * LANES, (p + 1) * LANES)
            d_y, dz, yh = norm_bwd(dc_ref[:, cs], y_ref[:, cs], cs)
            scr_d[0:tt, cs] = d_y
            gg_ref[:, cs] += jnp.sum(dz * yh, axis=0, keepdims=True)
            gbb_ref[:, cs] += jnp.sum(dz, axis=0, keepdims=True)
            gb_ref[:, cs] += jnp.sum(d_y, axis=0, keepdims=True)
            d_yn, _, _ = norm_bwd(dcn_ref[:, cs], yn_ref[:, cs], cs)
            scr_d[tt:tt + HALO, cs] = d_yn * (i < ni - 1).astype(F32)
        scr_c[0:HALO, :] = _glu(pcp_ref[...]) * (i > 0).astype(F32)
        scr_c[HALO:HALO + tt, :] = _glu(pc_ref[...])
        _shifted_copies(scr_d, shf_d, tt)
        _shifted_copies(scr_c, shf_c, tt)

        rid = lax.broadcasted_iota(jnp.int32, (HALO, LANES), 0)
        for cbk in range(ncb):
            cs = slice(cbk * LANES, (cbk + 1) * LANES)
            for rb in range(tt // ROW_CHUNK):
                r0 = rb * ROW_CHUNK
                acc = jnp.zeros((ROW_CHUNK, LANES), F32)
                for k in range(CONV_K):
                    acc = acc + w_ref[k:k + 1, cs] * _tap_rows(scr_d, shf_d, 30 - k, r0, cs)
                scr_o[r0:r0 + ROW_CHUNK, cs] = acc
            gwt = jnp.zeros((HALO, LANES), F32)
            for k in range(CONV_K):
                acc = jnp.zeros((ROW_CHUNK, LANES), F32)
                for rb in range(tt // ROW_CHUNK):
                    r0 = rb * ROW_CHUNK
                    acc = acc + scr_d[r0:r0 + ROW_CHUNK, cs] * _tap_rows(scr_c, shf_c, 2 + k, r0, cs)
                gwt = jnp.where(rid == k, jnp.sum(acc, axis=0, keepdims=True), gwt)
            gw_ref[:, cs] += gwt
        d_c0 = scr_o[...]
        a = pc_ref[:, 0:CONV_W]
        s = _sig(pc_ref[:, CONV_W:])
        d_a = d_c0 * s
        d_gate = d_c0 * a * s * (1.0 - s)
        dp_ref[:, 0:CONV_W] = d_a.astype(BF16)
        dp_ref[:, CONV_W:] = d_gate.astype(BF16)
        gbin_ref[:, 0:CONV_W] += jnp.sum(d_a, axis=0, keepdims=True)
        gbin_ref[:, CONV_W:] += jnp.sum(d_gate, axis=0, keepdims=True)

    hb = tt // HALO
    nxt = lambda col: pl.BlockSpec((HALO, CONV_W), lambda i: (jnp.minimum((i + 1) * hb, t // HALO - 1), col))
    return _call(
        body, (dmix, dmix, y, y, pc, pc, cw, ln_g, ln_b), name="conv_bwd", grid=(ni,), riders=riders,
        in_specs=[pl.BlockSpec((tt, CONV_W), lambda i: (i, 1)), nxt(1),
                  pl.BlockSpec((tt, CONV_W), lambda i: (i, 0)), nxt(0),
                  pl.BlockSpec((tt, 2 * CONV_W), lambda i: (i, 0)),
                  pl.BlockSpec((HALO, 2 * CONV_W), lambda i: (jnp.maximum(i * hb - 1, 0), 0)),
                  _full((CONV_K, CONV_W)), _full((1, CONV_W)), _full((1, CONV_W))],
        out_specs=[pl.BlockSpec((tt, 2 * CONV_W), lambda i: (i, 0)), _full((HALO, CONV_W)), _full((1, CONV_W)),
                   _full((1, CONV_W)), _full((1, CONV_W)), _full((1, 2 * CONV_W))],
        out_shape=[jax.ShapeDtypeStruct((t, 2 * CONV_W), BF16), jax.ShapeDtypeStruct((HALO, CONV_W), F32),
                   jax.ShapeDtypeStruct((1, CONV_W), F32), jax.ShapeDtypeStruct((1, CONV_W), F32),
                   jax.ShapeDtypeStruct((1, CONV_W), F32), jax.ShapeDtypeStruct((1, 2 * CONV_W), F32)],
        scratch=[pltpu.VMEM((tt + HALO, CONV_W), F32), pltpu.VMEM((tt + HALO, CONV_W), F32),
                 pltpu.VMEM((tt, CONV_W), F32), _shift_scratch(tt), _shift_scratch(tt)])


def _attn_bwd(qkvn, dmix, lse, sinks, bias, riders=()):
    t = qkvn.shape[0]
    nb = t // BLOCK
    bps = _attn_blocks_per_step(nb)

    def body(sink_ref, q_ref, kvc_ref, kvp_ref, bias_ref, do_ref, lse_ref, dq_ref, dcur_ref, dprev_ref, ds_ref):
        i = pl.program_id(0)
        lo = _lo_mask()
        lane1 = lax.broadcasted_iota(jnp.int32, (1, LANES), 1)
        lane = lax.broadcasted_iota(jnp.int32, (BLOCK, LANES), 1)
        dsink = jnp.zeros((1, LANES), F32)
        for b in range(bps):
            rows_b = slice(b * BLOCK, (b + 1) * BLOCK)
            k_dup, v_dup = _attn_keys(_attn_window(b, kvc_ref, kvp_ref))
            lse_t = lse_ref[rows_b, :]
            dqs = [jnp.zeros((BLOCK, LANES), F32) for _ in range(4)]
            dkv = []
            for kk in range(N_HEADS // GROUP):
                s, qs = _group_scores(q_ref, rows_b, kk, lo, k_dup, _attn_bias_of(bias_ref, b, kk))
                lse = jnp.concatenate([jnp.sum(jnp.where(lane == GROUP * kk + a, lse_t, 0.0), axis=-1, keepdims=True)
                                       for a in range(GROUP)], axis=0)
                prob = jnp.exp(s - lse)
                dos = _stack_heads(do_ref, rows_b, kk, lo, BF16)
                dp = _dot_nt(dos, v_dup[kk])
                dsum = jnp.sum(prob * dp, axis=-1, keepdims=True)
                dsb = (prob * (dp - dsum) * 0.125).astype(BF16)
                sink = _per_head_column([sink_ref[GROUP * kk + a] for a in range(GROUP)])
                dsk = -jnp.exp(sink - lse) * dsum
                dq = _dot(dsb, k_dup[kk])
                for a in range(GROUP):
                    h = GROUP * kk + a
                    rows = slice(a * BLOCK, (a + 1) * BLOCK)
                    hm = lo if h % 2 == 0 else jnp.logical_not(lo)
                    dqs[h // 2] = dqs[h // 2] + jnp.where(hm, dq[rows], 0.0)
                    dsink = dsink + jnp.where(lane1 == h, jnp.sum(dsk[rows], axis=0, keepdims=True), 0.0)
                dk_x = _dot_tn(dsb, qs)
                dv_x = _dot_tn(prob.astype(BF16), dos)
                dkv.append((dk_x + pltpu.roll(dk_x, HEAD_DIM, 1), dv_x + pltpu.roll(dv_x, HEAD_DIM, 1)))
            for p in range(4):
                dq_ref[rows_b, p * LANES:(p + 1) * LANES] = dqs[p]
            dk = jnp.where(lo, dkv[0][0], dkv[1][0])
            dv = jnp.where(lo, dkv[0][1], dkv[1][1])
            dprev_ref[rows_b, 0:LANES] = dk[0:BLOCK]
            dprev_ref[rows_b, LANES:] = dv[0:BLOCK]
            dcur_ref[rows_b, 0:LANES] = dk[BLOCK:]
            dcur_ref[rows_b, LANES:] = dv[BLOCK:]

        @pl.when(i == 0)
        def _():
            ds_ref[...] = jnp.zeros_like(ds_ref)

        ds_ref[...] += dsink

    blk = lambda w: pl.BlockSpec((bps * BLOCK, w), lambda i: (i, 0))
    return _call(
        body, (sinks, qkvn, qkvn, qkvn, bias, dmix, lse), name="attn_bwd", grid=(nb // bps,), riders=riders,
        in_specs=[pl.BlockSpec(memory_space=pltpu.SMEM)] + _attn_specs(bps) + [blk(Q_COLS), blk(LANES)],
        out_specs=[blk(Q_COLS), blk(2 * LANES), blk(2 * LANES), _full((1, LANES))],
        out_shape=[jax.ShapeDtypeStruct((t, Q_COLS), F32), jax.ShapeDtypeStruct((t, 2 * LANES), F32),
                   jax.ShapeDtypeStruct((t, 2 * LANES), F32), jax.ShapeDtypeStruct((1, LANES), F32)])


def _qk_norm_bwd(dqn, dcur, dprev, pq, gq2, gk2, riders=()):
    t = dqn.shape[0]
    nb = t // BLOCK
    tt = _token_tile(t)
    ni = t // tt

    def body(dq_ref, dc_ref, dt_ref, dn_ref, pq_ref, gq_ref, gk_ref, dp_ref, gbin_ref, gg_ref):
        i = pl.program_id(0)
        lo = _lo_mask()
        nxt = dn_ref[...] * (i < ni - 1).astype(F32)
        from_next = nxt if tt == BLOCK else jnp.concatenate([dt_ref[BLOCK:, :], nxt], axis=0)
        dkv = dc_ref[...] + from_next

        @pl.when(i == 0)
        def _():
            gbin_ref[...] = jnp.zeros_like(gbin_ref)
            gg_ref[...] = jnp.zeros_like(gg_ref)

        for p in range(5):
            cs = slice(p * LANES, (p + 1) * LANES)
            seg = pq_ref[:, cs]
            dn = dq_ref[:, cs] if p < 4 else dkv[:, 0:LANES]
            gain = gq_ref[...] if p < 4 else gk_ref[...]
            rr = lax.rsqrt(_half_mean(seg * seg, lo) + EPS)
            xh = seg * rr
            gd = dn * gain
            d = rr * (gd - xh * _half_mean(gd * xh, lo))
            dp_ref[:, cs] = d.astype(BF16)
            gbin_ref[:, cs] += jnp.sum(d, axis=0, keepdims=True)
            gg_ref[:, cs] += jnp.sum(dn * xh, axis=0, keepdims=True)
        dv = dkv[:, LANES:]
        dp_ref[:, 640:768] = dv.astype(BF16)
        gbin_ref[:, 640:768] += jnp.sum(dv, axis=0, keepdims=True)

    blk = lambda w: pl.BlockSpec((tt, w), lambda i: (i, 0))
    per = tt // BLOCK
    return _call(
        body, (dqn, dcur, dprev, dprev, pq, gq2, gk2), name="qk_norm_bwd", grid=(ni,), riders=riders,
        in_specs=[blk(Q_COLS), blk(2 * LANES), blk(2 * LANES),
                  pl.BlockSpec((BLOCK, 2 * LANES), lambda i: (jnp.minimum((i + 1) * per, nb - 1), 0)),
                  blk(QKV_COLS), _full((1, LANES)), _full((1, LANES))],
        out_specs=[blk(QKV_COLS), _full((1, QKV_COLS)), _full((1, 5 * LANES))],
        out_shape=[jax.ShapeDtypeStruct((t, QKV_COLS), BF16), jax.ShapeDtypeStruct((1, QKV_COLS), F32),
                   jax.ShapeDtypeStruct((1, 5 * LANES), F32)])


def _in_proj_bwd(dpq, dpc, w_in, x, dx1, g_mix):
    t = x.shape[0]
    tt = _wide_tile(t)

    def body(dq_ref, dc_ref, w_ref, x_ref, d1_ref, g_ref, gx_ref, gg_ref):
        d_h = _dot_nt(dq_ref[...], w_ref[:, 0:QKV_COLS]) + _dot_nt(dc_ref[...], w_ref[:, QKV_COLS:])
        xv = x_ref[...]
        r = lax.rsqrt(jnp.mean(xv * xv, axis=-1, keepdims=True) + EPS)
        xh = xv * r
        gd = d_h * g_ref[...]
        gx_ref[...] = d1_ref[...] + r * (gd - xh * jnp.mean(gd * xh, axis=-1, keepdims=True))

        @pl.when(pl.program_id(0) == 0)
        def _():
            gg_ref[...] = jnp.zeros_like(gg_ref)

        gg_ref[...] += jnp.sum(d_h * xh, axis=0, keepdims=True)

    row = lambda w: pl.BlockSpec((tt, w), lambda i: (i, 0))
    return pl.pallas_call(
        body, name="in_proj_bwd", grid=(t // tt,),
        in_specs=[row(QKV_COLS), row(2 * CONV_W), _full((D_MODEL, IN_COLS)), row(D_MODEL), row(D_MODEL), _full((1, D_MODEL))],
        out_specs=[row(D_MODEL), _full((1, D_MODEL))],
        out_shape=[jax.ShapeDtypeStruct((t, D_MODEL), F32), jax.ShapeDtypeStruct((1, D_MODEL), F32)],
        compiler_params=_params(("arbitrary",)),
    )(dpq, dpc, w_in, x, dx1, g_mix)


def _row_block(r):
    if r <= 256:
        return r
    return max(b for b in range(8, 257, 8) if r % b == 0)


def _adamw(w, g, m, v, name):
    r, c = w.shape
    rb = _row_block(r)

    def body(w_ref, g_ref, m_ref, v_ref, d_ref, nm_ref, nv_ref):
        gv = g_ref[...]
        nm = ADAM_B1 * m_ref[...] + (1.0 - ADAM_B1) * gv
        nv = ADAM_B2 * v_ref[...] + (1.0 - ADAM_B2) * (gv * gv)
        m_hat = nm / (1.0 - ADAM_B1 ** ADAM_STEP)
        v_hat = nv / (1.0 - ADAM_B2 ** ADAM_STEP)
        d_ref[...] = -ADAM_LR * (m_hat / (jnp.sqrt(v_hat) + ADAM_EPS) + ADAM_WD * w_ref[...])
        nm_ref[...] = nm
        nv_ref[...] = nv

    blk = pl.BlockSpec((rb, c), lambda i: (i, 0))
    shp = jax.ShapeDtypeStruct((r, c), F32)
    return pl.pallas_call(
        body, name=name, grid=(r // rb,), in_specs=[blk] * 4, out_specs=[blk] * 3, out_shape=[shp] * 3,
        compiler_params=_params(("parallel",)),
    )(w, g, m, v)


def _place():
    x, y, c = lax.axis_index("x"), lax.axis_index("y"), lax.axis_index("c")
    chips = [(1 - x, y), (x, 1 - y), (1 - x, 1 - y)]
    return x, y, c, chips


def _gather_all(v):
    r = v.shape[0]

    def body(v_ref, all_ref, sum_ref, send_sems, recv_sems):
        x, y, c, _ = _place()
        me = 4 * x + 2 * y + c
        all_ref[me] = v_ref[...]
        copies = []
        for k in range(1, 8):
            kx, ky, kc = (k >> 2) & 1, (k >> 1) & 1, k & 1
            peer = (x ^ kx, y ^ ky, c ^ kc)
            cp = pltpu.make_async_remote_copy(src_ref=v_ref, dst_ref=all_ref.at[me], send_sem=send_sems.at[k - 1],
                                              recv_sem=recv_sems.at[k - 1], device_id=peer, device_id_type=MESH)
            cp.start()
            copies.append((cp, 4 * peer[0] + 2 * peer[1] + peer[2]))
        for k, (cp, src_idx) in enumerate(copies):
            pltpu.make_async_remote_copy(src_ref=v_ref, dst_ref=all_ref.at[src_idx], send_sem=send_sems.at[k],
                                         recv_sem=recv_sems.at[k], device_id=(x, y, c), device_id_type=MESH).wait_recv()
        for cp, _ in copies:
            cp.wait_send()
        tot = all_ref[0]
        for d in range(1, 8):
            tot = tot + all_ref[d]
        sum_ref[...] = tot

    vm = pl.BlockSpec(memory_space=pltpu.VMEM)
    return pl.pallas_call(
        body, name="gather_all", in_specs=[vm], out_specs=[vm, vm],
        out_shape=[jax.ShapeDtypeStruct((8, r, LANES), v.dtype), jax.ShapeDtypeStruct((r, LANES), v.dtype)],
        scratch_shapes=[pltpu.SemaphoreType.DMA((7,)), pltpu.SemaphoreType.DMA((7,))],
        compiler_params=pltpu.CompilerParams(vmem_limit_bytes=VMEM_LIMIT),
    )(v)


def _remote(src, dst, send_sem, recv_sem, to):
    return pltpu.make_async_remote_copy(src_ref=src, dst_ref=dst, send_sem=send_sem, recv_sem=recv_sem,
                                        device_id=to, device_id_type=MESH)


def _dma_sems(*shape):
    return pltpu.SemaphoreType.DMA(shape)


def _gather_first(shards):
    n = len(shards)

    def copies(ins, outs, sems):
        x, y, c, chips = _place()
        me = 2 * x + y
        local = [pltpu.make_async_copy(ins[a], outs[a].at[me], sems[2].at[a]) for a in range(n)]
        sends = [_remote(ins[a].at[c], outs[a].at[me, c], sems[0].at[a, j], sems[1].at[a, j], (*chip, c))
                 for a in range(n) for j, chip in enumerate(chips)]
        lands = [_remote(ins[a].at[c], outs[a].at[2 * chip[0] + chip[1], c], sems[0].at[a, j], sems[1].at[a, j], (x, y, c))
                 for a in range(n) for j, chip in enumerate(chips)]
        return local, sends, lands

    def start(ins, outs, sems):
        local, sends, _ = copies(ins, outs, sems)
        for cp in local + sends:
            cp.start()

    def finish(ins, outs, sems):
        local, sends, lands = copies(ins, outs, sems)
        for cp in lands:
            cp.wait_recv()
        for cp in sends:
            cp.wait_send()
        for cp in local:
            cp.wait()

    return _Rider(shards, [jax.ShapeDtypeStruct((4,) + s.shape, s.dtype) for s in shards],
                  [_dma_sems(n, 3), _dma_sems(n, 3), _dma_sems(n)], start, finish)


def _gather_second(partials):
    n = len(partials)

    def copies(outs, sems):
        x, y, c, chips = _place()
        sends, lands = [], []
        for a in range(n):
            for j, chip in enumerate(chips):
                mine = outs[a].at[2 * chip[0] + chip[1], c]
                theirs = outs[a].at[2 * chip[0] + chip[1], 1 - c]
                sends.append(_remote(mine, mine, sems[0].at[a, j], sems[1].at[a, j], (x, y, 1 - c)))
                lands.append(_remote(theirs, theirs, sems[0].at[a, j], sems[1].at[a, j], (x, y, c)))
        return sends, lands

    def start(ins, outs, sems):
        for cp in copies(outs, sems)[0]:
            cp.start()

    def finish(ins, outs, sems):
        sends, lands = copies(outs, sems)
        for cp in lands:
            cp.wait_recv()
        for cp in sends:
            cp.wait_send()

    return _Rider(partials, [jax.ShapeDtypeStruct(p.shape, p.dtype) for p in partials],
                  [_dma_sems(n, 3), _dma_sems(n, 3)], start, finish, aliases={a: a for a in range(n)})


def _swap_halves(grads):
    n = len(grads)

    def copies(ins, outs, sems):
        x, y, c, _ = _place()
        return [_remote(ins[a].at[j, 1 - c], outs[a].at[j], sems[0].at[a, j], sems[1].at[a, j], (x, y, 1 - c))
                for a in range(n) for j in range(4)]

    def start(ins, outs, sems):
        for cp in copies(ins, outs, sems):
            cp.start()

    def finish(ins, outs, sems):
        for cp in copies(ins, outs, sems):
            cp.wait()

    return _Rider(grads, [jax.ShapeDtypeStruct((4,) + g.shape[2:], g.dtype) for g in grads],
                  [_dma_sems(n, 4), _dma_sems(n, 4)], start, finish)


def _add_sibling(g, got, c_idx, name):
    _, _, h, c = g.shape

    def body(s_ref, a_ref, b_ref, o_ref):
        o_ref[...] = (a_ref[...] + b_ref[...]).astype(BF16)

    return pl.pallas_call(
        body, name=name,
        grid_spec=pltpu.PrefetchScalarGridSpec(
            num_scalar_prefetch=1, grid=(4,),
            in_specs=[pl.BlockSpec((None, None, h, c), lambda j, s: (j, s[0], 0, 0)),
                      pl.BlockSpec((None, h, c), lambda j, s: (j, 0, 0))],
            out_specs=pl.BlockSpec((None, h, c), lambda j, s: (j, 0, 0))),
        out_shape=jax.ShapeDtypeStruct((4, h, c), BF16),
        compiler_params=_params(("parallel",)),
    )(c_idx, g, got)


def _exchange_chips(parts):
    n = len(parts)

    def copies(ins, outs, sems):
        x, y, c, chips = _place()
        return [_remote(ins[a].at[2 * chip[0] + chip[1]], outs[a].at[j], sems[0].at[a, j], sems[1].at[a, j], (*chip, c))
                for a in range(n) for j, chip in enumerate(chips)]

    def start(ins, outs, sems):
        for cp in copies(ins, outs, sems):
            cp.start()

    def finish(ins, outs, sems):
        for cp in copies(ins, outs, sems):
            cp.wait()

    return _Rider(parts, [jax.ShapeDtypeStruct((3,) + p.shape[1:], p.dtype) for p in parts],
                  [_dma_sems(n, 3), _dma_sems(n, 3)], start, finish)


def _add_chips(part, got, chip_idx, name):
    _, h, c = part.shape

    def body(s_ref, a_ref, b_ref, o_ref):
        o_ref[...] = ((a_ref[...].astype(F32) + b_ref[0].astype(F32)) + b_ref[1].astype(F32)) + b_ref[2].astype(F32)

    return pl.pallas_call(
        body, name=name,
        grid_spec=pltpu.PrefetchScalarGridSpec(
            num_scalar_prefetch=1, grid=(1,),
            in_specs=[pl.BlockSpec((None, h, c), lambda i, s: (s[0], 0, 0)),
                      pl.BlockSpec((3, h, c), lambda i, s: (0, 0, 0))],
            out_specs=pl.BlockSpec((h, c), lambda i, s: (0, 0))),
        out_shape=jax.ShapeDtypeStruct((h, c), F32),
        compiler_params=_params(("arbitrary",)),
    )(chip_idx, part, got)


def _join_halves(halves):
    n = len(halves)

    def copies(ins, outs, sems):
        x, y, c, _ = _place()
        local = [pltpu.make_async_copy(ins[a], outs[a].at[c], sems[2].at[a]) for a in range(n)]
        sends = [_remote(ins[a], outs[a].at[c], sems[0].at[a], sems[1].at[a], (x, y, 1 - c)) for a in range(n)]
        lands = [_remote(ins[a], outs[a].at[1 - c], sems[0].at[a], sems[1].at[a], (x, y, c)) for a in range(n)]
        return local, sends, lands

    def start(ins, outs, sems):
        local, sends, _ = copies(ins, outs, sems)
        for cp in local + sends:
            cp.start()

    def finish(ins, outs, sems):
        local, sends, lands = copies(ins, outs, sems)
        for cp in lands:
            cp.wait_recv()
        for cp in sends:
            cp.wait_send()
        for cp in local:
            cp.wait()

    return _Rider(halves, [jax.ShapeDtypeStruct((2,) + h.shape, h.dtype) for h in halves],
                  [_dma_sems(n), _dma_sems(n), _dma_sems(n)], start, finish)


def _pack(parts):
    flat = []
    for p in parts:
        p = p.reshape(-1).astype(F32)
        flat.append(jnp.pad(p, (0, (-p.shape[0]) % LANES)))
    v = jnp.concatenate(flat)
    v = jnp.pad(v, (0, (-v.shape[0]) % (8 * LANES)))
    return v.reshape(-1, LANES)


def _unpack(v, shapes):
    flat = v.reshape(-1)
    out, off = [], 0
    for s in shapes:
        n = 1
        for d in s:
            n *= d
        out.append(flat[off:off + n].reshape(s))
        off += n + (-n) % LANES
    return out


def kernel(x, mix_norm_gain, w_in, b_in, q_norm_gain, k_norm_gain, attn_sinks, conv_dw_w, conv_dw_b, conv_norm_gain, conv_norm_bias, w_out, b_out, ffn_norm_gain, w_up, ffn_dw_w, ffn_dw_b, w_down, loss_target, m_mix_norm_gain, m_w_in, m_b_in, m_q_norm_gain, m_k_norm_gain, m_attn_sinks, m_conv_dw_w, m_conv_dw_b, m_conv_norm_gain, m_conv_norm_bias, m_w_out, m_b_out, m_ffn_norm_gain, m_w_up, m_ffn_dw_w, m_ffn_dw_b, m_w_down, v_mix_norm_gain, v_w_in, v_b_in, v_q_norm_gain, v_k_norm_gain, v_attn_sinks, v_conv_dw_w, v_conv_dw_b, v_conv_norm_gain, v_conv_norm_bias, v_w_out, v_b_out, v_ffn_norm_gain, v_w_up, v_ffn_dw_w, v_ffn_dw_b, v_w_down):
    t = x.shape[1]
    xi, yi, ci = lax.axis_index("x"), lax.axis_index("y"), lax.axis_index("c")
    chip = 2 * xi + yi
    c_idx = jnp.reshape(ci, (1,)).astype(jnp.int32)
    chip_idx = jnp.reshape(chip, (1,)).astype(jnp.int32)
    x2 = x.reshape(t, D_MODEL)
    tgt = loss_target.reshape(t, D_MODEL)

    big = [w_in, w_out, w_up, w_down]
    halves = [w.astype(BF16).reshape(2, w.shape[0] // 2, w.shape[1]) for w in big]
    h_wi, h_wo, h_wu, h_wd = halves
    (p_wi,) = _run_riders([_gather_first([h_wi])], "gather_w_in_first")
    (g_wi,) = _run_riders([_gather_second([p_wi])], "gather_w_in_second")
    wi = jnp.concatenate([g_wi[j].reshape(D_MODEL, IN_COLS // 4) for j in range(4)], axis=1)
    small_w, _ = _gather_all(_pack([conv_dw_w, ffn_dw_w]))
    per_chip = [_unpack(small_w[4 * (j // 2) + 2 * (j % 2)], [conv_dw_w.shape, ffn_dw_w.shape]) for j in range(4)]
    cw = jnp.concatenate([p[0] for p in per_chip], axis=1)
    fw = jnp.concatenate([p[1] for p in per_chip], axis=1)

    row = lambda a: a.reshape(1, -1)
    gq2 = row(jnp.concatenate([q_norm_gain, q_norm_gain]))
    gk2 = row(jnp.concatenate([k_norm_gain, k_norm_gain]))

    (h1, pq, pc, qkvn), (p_wo,) = _fwd_in(x2, row(mix_norm_gain), wi, row(b_in), gq2, gk2, riders=[_gather_first([h_wo])])
    bias = _attn_bias()
    (attn, lse), (g_wo, p_wu) = _attn_fwd(qkvn, attn_sinks, bias, riders=[_gather_second([p_wo]), _gather_first([h_wu])])
    (y_conv, c_act), (g_wu, p_wd) = _conv_fwd(pc, cw, row(conv_dw_b), row(conv_norm_gain), row(conv_norm_bias),
                                              riders=[_gather_second([p_wu]), _gather_first([h_wd])])
    wo = g_wo.reshape(D_MODEL, D_MODEL)
    (x1, h2), (g_wd,) = _out_proj(x2, attn, c_act, wo, row(b_out), row(ffn_norm_gain), riders=[_gather_second([p_wd])])
    wu = g_wu.reshape(4, D_MODEL, FFN_CB)
    wd = g_wd.reshape(D_FF, D_MODEL)
    hg, hu, up_g, up_u, act = _ffn_up(h2, wu, fw, row(ffn_dw_b))
    dy, loss_cols = _ffn_down(act, wd, x1, tgt)

    split = lambda g: g.reshape(4, 2, g.shape[1] // 2, g.shape[2])
    dg, du, gfb_g, gfb_u = _ffn_bwd_act(dy, wd, up_g, up_u)
    dhg, dhu, gfw_g, gfw_u = _ffn_bwd_conv(dg, du, hg, hu, fw)
    gw_down = _grad_weight(act, dy, 1, FFN_CB, "grad_w_down")
    gw_up = _grad_weight(h2, dhg, 2, D_MODEL, "grad_w_up_gate", lead=4)
    gw_up = _grad_weight(h2, dhu, 2, D_MODEL, "grad_w_up_lin", into=gw_up, offset=2)
    early = [split(gw_up), split(gw_down.reshape(4, D_FF // 4, D_MODEL))]
    early_names = ["w_up", "w_down"]
    (dx1, g_ffn_gain), got = _ffn_bwd_in(dhg, dhu, wu, x1, dy, row(ffn_norm_gain), riders=[_swap_halves(early)])
    early_part = [_add_sibling(g, r, c_idx, "add_sibling_" + nm_) for g, r, nm_ in zip(early, got, early_names)]
    dmix, g_b_out = _out_proj_bwd(dx1, wo)
    gw_out = jnp.concatenate([_grad_weight(attn, dx1, 1, Q_COLS, "grad_w_out_attn")[0],
                              _grad_weight(c_act, dx1, 1, CONV_W, "grad_w_out_conv")[0]], axis=0)
    mid = [split(gw_out.reshape(4, D_MODEL // 4, D_MODEL))]
    (dpc, g_cw, g_cb, g_lng, g_lnb, gbin_c), got2 = _conv_bwd(dmix, y_conv, pc, cw, row(conv_norm_gain), row(conv_norm_bias),
                                                              riders=[_exchange_chips(early_part), _swap_halves(mid)])
    early_red = [_add_chips(p, r, chip_idx, "add_chips_" + nm_) for p, r, nm_ in zip(early_part, got2[:2], early_names)]
    mid_part = [_add_sibling(mid[0], got2[2], c_idx, "add_sibling_w_out")]
    (dqn, dcur, dprev, g_sink), got3 = _attn_bwd(qkvn, dmix, lse, attn_sinks, bias,
                                                 riders=[_join_halves(early_red), _exchange_chips(mid_part)])
    early_g = got3[:2]
    mid_red = [_add_chips(mid_part[0], got3[2], chip_idx, "add_chips_w_out")]
    (dpq, gbin_q, g_qk), mid_g = _qk_norm_bwd(dqn, dcur, dprev, pq, gq2, gk2, riders=[_join_halves(mid_red)])
    grad_x, g_mix_gain = _in_proj_bwd(dpq, dpc, wi, x2, dx1, row(mix_norm_gain))
    gw_in = jnp.concatenate([_grad_weight(h1, dpq, 1, D_MODEL, "grad_w_in_qkv")[0],
                             _grad_weight(h1, dpc, 1, D_MODEL, "grad_w_in_conv")[0]], axis=1)

    g_qk = g_qk.reshape(5, 2, HEAD_DIM)
    small = [g_mix_gain, jnp.concatenate([gbin_q, gbin_c], axis=1), g_qk[:4].sum(axis=(0, 1)), g_qk[4].sum(axis=0),
             g_sink[0, :N_HEADS], g_cb, g_lng, g_lnb, g_b_out, g_ffn_gain, jnp.concatenate([gfb_g, gfb_u], axis=1),
             loss_cols, g_cw[:CONV_K], jnp.concatenate([gfw_g[:3], gfw_u[:3]], axis=1)]
    _, tot = _gather_all(_pack(small))
    rep_names = ["mix_norm_gain", "b_in", "q_norm_gain", "k_norm_gain", "attn_sinks", "conv_dw_b", "conv_norm_gain",
                 "conv_norm_bias", "b_out", "ffn_norm_gain", "ffn_dw_b"]
    rep_w = [mix_norm_gain, b_in, q_norm_gain, k_norm_gain, attn_sinks, conv_dw_b, conv_norm_gain, conv_norm_bias,
             b_out, ffn_norm_gain, ffn_dw_b]
    rep_m = [m_mix_norm_gain, m_b_in, m_q_norm_gain, m_k_norm_gain, m_attn_sinks, m_conv_dw_b, m_conv_norm_gain,
             m_conv_norm_bias, m_b_out, m_ffn_norm_gain, m_ffn_dw_b]
    rep_v = [v_mix_norm_gain, v_b_in, v_q_norm_gain, v_k_norm_gain, v_attn_sinks, v_conv_dw_b, v_conv_norm_gain,
             v_conv_norm_bias, v_b_out, v_ffn_norm_gain, v_ffn_dw_b]
    shapes = [w.shape for w in rep_w] + [(D_MODEL,), (CONV_K, CONV_W), (3, 2 * D_FF)]
    tot_parts = _unpack(tot, shapes)
    loss = (0.5 / D_MODEL) * jnp.sum(tot_parts[len(rep_w)])
    g_cw_full, g_fw_full = tot_parts[len(rep_w) + 1], tot_parts[len(rep_w) + 2]
    n_rep_rows = _pack(rep_w).shape[0]
    rep_d, rep_nm, rep_nv = _adamw(_pack(rep_w), tot[:n_rep_rows], _pack(rep_m), _pack(rep_v), "adamw_small")
    rep_shapes = [w.shape for w in rep_w]
    res = {}
    for nm_, g_, d_, m_, v_ in zip(rep_names, tot_parts, _unpack(rep_d, rep_shapes), _unpack(rep_nm, rep_shapes),
                                   _unpack(rep_nv, rep_shapes)):
        res[nm_] = (g_, d_, m_, v_)

    g_cw_mine = lax.dynamic_slice_in_dim(g_cw_full, chip * (CONV_W // 4), CONV_W // 4, axis=1)
    g_fw_mine = lax.dynamic_slice_in_dim(g_fw_full, chip * (2 * D_FF // 4), 2 * D_FF // 4, axis=1)
    res["conv_dw_w"] = (g_cw_mine, *_adamw(conv_dw_w, g_cw_mine, m_conv_dw_w, v_conv_dw_w, "adamw_conv_dw_w"))
    res["ffn_dw_w"] = (g_fw_mine, *_adamw(ffn_dw_w, g_fw_mine, m_ffn_dw_w, v_ffn_dw_w, "adamw_ffn_dw_w"))

    gw_in4 = gw_in.reshape(D_MODEL, 4, IN_COLS // 4).transpose(1, 0, 2)
    late = [split(gw_in4)]
    late_names = ["w_in"]
    got = _run_riders([_swap_halves(late)], "swap_halves_late")
    late_part = [_add_sibling(g, r, c_idx, "add_sibling_" + nm_) for g, r, nm_ in zip(late, got, late_names)]
    got2 = _run_riders([_exchange_chips(late_part)], "exchange_chips_late")
    late_red = [_add_chips(p, r, chip_idx, "add_chips_" + nm_) for p, r, nm_ in zip(late_part, got2, late_names)]
    late_g = _run_riders([_join_halves(late_red)], "join_halves_late")
    names = ["w_in", "w_out", "w_up", "w_down"]
    shard_g = list(late_g) + list(mid_g) + list(early_g)
    for nm_, w_, g_, m_, v_ in zip(names, big, shard_g, [m_w_in, m_w_out, m_w_up, m_w_down], [v_w_in, v_w_out, v_w_up, v_w_down]):
        g_ = g_.reshape(w_.shape)
        res[nm_] = (g_, *_adamw(w_, g_, m_, v_, "adamw_" + nm_))

    order = ["mix_norm_gain", "w_in", "b_in", "q_norm_gain", "k_norm_gain", "attn_sinks", "conv_dw_w", "conv_dw_b",
             "conv_norm_gain", "conv_norm_bias", "w_out", "b_out", "ffn_norm_gain", "w_up", "ffn_dw_w", "ffn_dw_b", "w_down"]
    return (loss, grad_x.reshape(x.shape), *[res[n][0] for n in order], *[res[n][1] for n in order],
            *[res[n][2] for n in order], *[res[n][3] for n in order])
```

```python
import functools

import jax
import jax.numpy as jnp
from jax import lax
from jax.experimental import pallas as pl
from jax.experimental.pallas import tpu as pltpu

F32 = jnp.float32
BF16 = jnp.bfloat16
MESH = pl.DeviceIdType.MESH

D_MODEL = 1024
HEAD_DIM = 64
N_HEADS = 8
Q_COLS = 512
QKV_COLS = 768
CONV_W = 512
CONV_K = 31
IN_COLS = 1792
D_FF = 2816
FFN_CB = 1408
BLOCK = 128
LANES = 128
EPS = 1e-6
NEG_INF = -1e30
SLOPES = tuple(float(2.0 ** (-(h + 1.0))) for h in range(N_HEADS))
HALO = 32
FHALO = 16
ROW_CHUNK = 64
GRAD_TILE = 2048
VMEM_LIMIT = 56 * 1024 * 1024

ADAM_LR = 0.001
ADAM_B1 = 0.9
ADAM_B2 = 0.999
ADAM_EPS = 1e-08
ADAM_WD = 0.01
ADAM_STEP = 10


def _params(sem=None):
    kw = dict(vmem_limit_bytes=VMEM_LIMIT)
    if sem is not None:
        kw["dimension_semantics"] = sem
    return pltpu.CompilerParams(**kw)


def _token_tile(t):
    return 512 if t % 512 == 0 and t >= 2048 else 128


def _wide_tile(t):
    return 1024 if t % 1024 == 0 and t >= 2048 else _token_tile(t)


def _sig(v):
    return 1.0 / (1.0 + jnp.exp(-v))


def _lo_mask():
    return lax.broadcasted_iota(jnp.int32, (1, LANES), 1) < HEAD_DIM


def _half_mean(v, lo):
    s_lo = jnp.sum(jnp.where(lo, v, 0.0), axis=-1, keepdims=True)
    s_hi = jnp.sum(jnp.where(lo, 0.0, v), axis=-1, keepdims=True)
    return jnp.where(lo, s_lo, s_hi) * (1.0 / HEAD_DIM)


def _dot(a, b):
    return jnp.dot(a, b, preferred_element_type=F32)


def _dot_nt(a, b):
    return lax.dot_general(a, b, (((1,), (1,)), ((), ())), preferred_element_type=F32)


def _dot_tn(a, b):
    return lax.dot_general(a, b, (((0,), (0,)), ((), ())), preferred_element_type=F32)


def _full(shape):
    nd = len(shape)
    return pl.BlockSpec(shape, lambda *_: (0,) * nd)


def _rows_to_tile(rows, n_rows):
    c = rows[0].shape[-1]
    rid = lax.broadcasted_iota(jnp.int32, (n_rows, c), 0)
    out = jnp.zeros((n_rows, c), F32)
    for k, r in enumerate(rows):
        out = jnp.where(rid == k, r, out)
    return out


ANY = pl.BlockSpec(memory_space=pl.ANY)


class _Rider:
    def __init__(self, ins, outs, sems, start, finish, aliases=None):
        self.ins, self.outs, self.sems = list(ins), list(outs), list(sems)
        self.start, self.finish, self.aliases = start, finish, dict(aliases or {})


def _join_riders(riders):
    ins, outs, sems, aliases, spans = [], [], [], {}, []
    for r in riders:
        spans.append((len(ins), len(outs), len(sems), r))
        for a, b in r.aliases.items():
            aliases[len(ins) + a] = len(outs) + b
        ins += r.ins
        outs += r.outs
        sems += r.sems

    def each(which):
        def run(i_refs, o_refs, s_refs):
            for i0, o0, s0, r in spans:
                getattr(r, which)(i_refs[i0:i0 + len(r.ins)], o_refs[o0:o0 + len(r.outs)], s_refs[s0:s0 + len(r.sems)])
        return run

    return _Rider(ins, outs, sems, each("start"), each("finish"), aliases)


def _call(body, args, *, name, grid, in_specs, out_specs, out_shape, scratch=(), riders=()):
    in_specs, out_specs, out_shape, scratch = list(in_specs), list(out_specs), list(out_shape), list(scratch)
    sem = ("arbitrary",) * len(grid)
    if not riders:
        outs = pl.pallas_call(body, name=name, grid=grid, in_specs=in_specs, out_specs=out_specs, out_shape=out_shape,
                              scratch_shapes=scratch, compiler_params=_params(sem))(*args)
        return list(outs), []
    r = _join_riders(riders)
    n_in, n_out, n_scr = len(in_specs), len(out_specs), len(scratch)
    nri, nro = len(r.ins), len(r.outs)

    def full(*refs):
        ins, rin = refs[:n_in], refs[n_in:n_in + nri]
        o0 = n_in + nri
        outs, rout = refs[o0:o0 + n_out], refs[o0 + n_out:o0 + n_out + nro]
        s0 = o0 + n_out + nro
        scr, rsem = refs[s0:s0 + n_scr], refs[s0 + n_scr:]
        first = functools.reduce(jnp.logical_and, [pl.program_id(k) == 0 for k in range(len(grid))])
        last = functools.reduce(jnp.logical_and, [pl.program_id(k) == grid[k] - 1 for k in range(len(grid))])

        @pl.when(first)
        def _():
            r.start(rin, rout, rsem)

        body(*ins, *outs, *scr)

        @pl.when(last)
        def _():
            r.finish(rin, rout, rsem)

    outs = pl.pallas_call(
        full, name=name, grid=grid, in_specs=in_specs + [ANY] * nri, out_specs=out_specs + [ANY] * nro,
        out_shape=out_shape + r.outs, scratch_shapes=scratch + r.sems,
        input_output_aliases={n_in + a: n_out + b for a, b in r.aliases.items()},
        compiler_params=_params(sem))(*args, *r.ins)
    return list(outs[:n_out]), list(outs[n_out:])


def _run_riders(riders, name):
    r = _join_riders(riders)
    nri, nro = len(r.ins), len(r.outs)

    def body(*refs):
        rin, rout, rsem = refs[:nri], refs[nri:nri + nro], refs[nri + nro:]
        r.start(rin, rout, rsem)
        r.finish(rin, rout, rsem)

    outs = pl.pallas_call(body, name=name, in_specs=[ANY] * nri, out_specs=[ANY] * nro, out_shape=r.outs,
                          scratch_shapes=r.sems, input_output_aliases=r.aliases)(*r.ins)
    return list(outs)


def _fwd_in(x, g_mix, w_in, b_in, gq2, gk2, riders=()):
    t = x.shape[0]
    tt = _wide_tile(t)

    def body(x_ref, g_ref, w_ref, b_ref, gq_ref, gk_ref, h1_ref, pq_ref, pc_ref, qkvn_ref):
        xv = x_ref[...]
        r = lax.rsqrt(jnp.mean(xv * xv, axis=-1, keepdims=True) + EPS)
        h = (xv * r * g_ref[...]).astype(BF16)
        h1_ref[...] = h
        proj = _dot(h, w_ref[...]) + b_ref[...]
        pq_ref[...] = proj[:, :QKV_COLS]
        pc_ref[...] = proj[:, QKV_COLS:]
        lo = _lo_mask()
        for p in range(5):
            seg = proj[:, p * LANES:(p + 1) * LANES]
            rr = lax.rsqrt(_half_mean(seg * seg, lo) + EPS)
            gain = gq_ref[...] if p < 4 else gk_ref[...]
            qkvn_ref[:, p * LANES:(p + 1) * LANES] = (seg * rr * gain).astype(BF16)
        qkvn_ref[:, 640:768] = proj[:, 640:768].astype(BF16)

    return _call(
        body, (x, g_mix, w_in, b_in, gq2, gk2), name="fwd_in", grid=(t // tt,), riders=riders,
        in_specs=[pl.BlockSpec((tt, D_MODEL), lambda i: (i, 0)), _full((1, D_MODEL)), _full((D_MODEL, IN_COLS)),
                  _full((1, IN_COLS)), _full((1, LANES)), _full((1, LANES))],
        out_specs=[pl.BlockSpec((tt, D_MODEL), lambda i: (i, 0)), pl.BlockSpec((tt, QKV_COLS), lambda i: (i, 0)),
                   pl.BlockSpec((tt, 2 * CONV_W), lambda i: (i, 0)), pl.BlockSpec((tt, QKV_COLS), lambda i: (i, 0))],
        out_shape=[jax.ShapeDtypeStruct((t, D_MODEL), BF16), jax.ShapeDtypeStruct((t, QKV_COLS), F32),
                   jax.ShapeDtypeStruct((t, 2 * CONV_W), F32), jax.ShapeDtypeStruct((t, QKV_COLS), BF16)])


GROUP = 4
GROUP_ROWS = GROUP * BLOCK


def _attn_bias():
    qi = jnp.arange(GROUP_ROWS)[:, None] % BLOCK
    kj = jnp.arange(2 * BLOCK)[None, :]
    rel = qi + BLOCK - kj
    band = (rel >= 0) & (rel < BLOCK)
    slope_rows = jnp.repeat(jnp.asarray(SLOPES, F32).reshape(N_HEADS // GROUP, GROUP), BLOCK, axis=1)
    penalty = -(slope_rows[:, :, None] * rel.astype(F32)[None])
    later = jnp.where(band[None], penalty, NEG_INF)
    first = jnp.where((band & (kj >= BLOCK))[None], penalty, NEG_INF)
    return jnp.stack([first, later])


def _attn_blocks_per_step(nb):
    return 2 if nb % 2 == 0 else 1


def _attn_specs(bps):
    return [pl.BlockSpec((bps * BLOCK, Q_COLS), lambda i: (i, 0)),
            pl.BlockSpec((bps * BLOCK, 2 * LANES), lambda i: (i, 2)),
            pl.BlockSpec((BLOCK, 2 * LANES), lambda i: (jnp.maximum(bps * i - 1, 0), 2)),
            _full((2, N_HEADS // GROUP, GROUP_ROWS, 2 * BLOCK))]


def _attn_window(b, kvc_ref, kvp_ref):
    own = kvc_ref[b * BLOCK:(b + 1) * BLOCK, :]
    before = kvp_ref[...] if b == 0 else kvc_ref[(b - 1) * BLOCK:b * BLOCK, :]
    return jnp.concatenate([before, own], axis=0)


def _attn_bias_of(bias_ref, b, kk):
    if b > 0:
        return bias_ref[1, kk]
    return bias_ref[jnp.minimum(pl.program_id(0), 1), kk]


def _attn_keys(kv):
    kv = kv.astype(F32)
    lo = _lo_mask()

    def both_halves(pair):
        rolled = pltpu.roll(pair, HEAD_DIM, 1)
        return [jnp.where(lo, pair, rolled).astype(BF16), jnp.where(lo, rolled, pair).astype(BF16)]

    return both_halves(kv[:, :LANES]), both_halves(kv[:, LANES:])


def _per_head_column(values):
    seg = lax.broadcasted_iota(jnp.int32, (GROUP_ROWS, 1), 0) // BLOCK
    col = jnp.zeros((GROUP_ROWS, 1), F32) + values[GROUP - 1]
    for a in range(GROUP - 2, -1, -1):
        col = jnp.where(seg == a, values[a], col)
    return col


def _stack_heads(ref, rows, kk, lo, dtype):
    parts = []
    for a in range(GROUP):
        h = GROUP * kk + a
        pair = ref[rows, (h // 2) * LANES:(h // 2 + 1) * LANES]
        hm = lo if h % 2 == 0 else jnp.logical_not(lo)
        parts.append(jnp.where(hm, pair, jnp.zeros_like(pair)).astype(dtype))
    return jnp.concatenate(parts, axis=0)


def _group_scores(q_ref, rows, kk, lo, k_dup, bias):
    qs = _stack_heads(q_ref, rows, kk, lo, BF16)
    return _dot_nt(qs, k_dup[kk]) * 0.125 + bias, qs


def _attn_fwd(qkvn, sinks, bias, riders=()):
    t = qkvn.shape[0]
    nb = t // BLOCK
    bps = _attn_blocks_per_step(nb)

    def body(sink_ref, q_ref, kvc_ref, kvp_ref, bias_ref, o_ref, lse_ref):
        lo = _lo_mask()
        lane = lax.broadcasted_iota(jnp.int32, (BLOCK, LANES), 1)
        for b in range(bps):
            rows_b = slice(b * BLOCK, (b + 1) * BLOCK)
            k_dup, v_dup = _attn_keys(_attn_window(b, kvc_ref, kvp_ref))
            lse_t = jnp.zeros((BLOCK, LANES), F32)
            outs = [jnp.zeros((BLOCK, LANES), F32) for _ in range(4)]
            for kk in range(N_HEADS // GROUP):
                s, _ = _group_scores(q_ref, rows_b, kk, lo, k_dup, _attn_bias_of(bias_ref, b, kk))
                sink = _per_head_column([sink_ref[GROUP * kk + a] for a in range(GROUP)])
                m = jnp.maximum(jnp.max(s, axis=-1, keepdims=True), sink)
                pe = jnp.exp(s - m)
                l = jnp.sum(pe, axis=-1, keepdims=True) + jnp.exp(sink - m)
                lse = m + jnp.log(l)
                o = _dot((pe / l).astype(BF16), v_dup[kk])
                for a in range(GROUP):
                    h = GROUP * kk + a
                    rows = slice(a * BLOCK, (a + 1) * BLOCK)
                    hm = lo if h % 2 == 0 else jnp.logical_not(lo)
                    outs[h // 2] = outs[h // 2] + jnp.where(hm, o[rows], 0.0)
                    lse_t = jnp.where(lane == h, lse[rows], lse_t)
            for p in range(4):
                o_ref[rows_b, p * LANES:(p + 1) * LANES] = outs[p].astype(BF16)
            lse_ref[rows_b, :] = lse_t

    blk = lambda w: pl.BlockSpec((bps * BLOCK, w), lambda i: (i, 0))
    return _call(
        body, (sinks, qkvn, qkvn, qkvn, bias), name="attn_fwd", grid=(nb // bps,), riders=riders,
        in_specs=[pl.BlockSpec(memory_space=pltpu.SMEM)] + _attn_specs(bps),
        out_specs=[blk(Q_COLS), blk(LANES)],
        out_shape=[jax.ShapeDtypeStruct((t, Q_COLS), BF16), jax.ShapeDtypeStruct((t, LANES), F32)])


def _glu(pc):
    return pc[:, :CONV_W] * _sig(pc[:, CONV_W:])


def _glu_chunks(pc_ref, scr, tt):
    for r0, cs in _chunks(tt, CONV_W):
        rows = slice(r0, r0 + ROW_CHUNK)
        gate = slice(cs.start + CONV_W, cs.stop + CONV_W)
        scr[HALO + r0:HALO + r0 + ROW_CHUNK, cs] = pc_ref[rows, cs] * _sig(pc_ref[rows, gate])


def _group_norm_stats(seg, lo):
    mu = _half_mean(seg, lo)
    d = seg - mu
    rstd = lax.rsqrt(_half_mean(d * d, lo) + EPS)
    return d * rstd, rstd


def _shifted_copies(src, dst, tt):
    n = tt + HALO - 8
    for s in range(1, 8):
        dst[s - 1, 0:n, :] = src[s:s + n, :]


def _tap_rows(src, shifted, off, r0, cs):
    q, s = divmod(off, 8)
    if s == 0:
        return src[r0 + off:r0 + off + ROW_CHUNK, cs]
    return shifted[s - 1, r0 + 8 * q:r0 + 8 * q + ROW_CHUNK, cs]


def _shift_scratch(tt):
    return pltpu.VMEM((7, tt + HALO - 8, CONV_W), F32)


def _conv_fwd(pc, cw, cb, ln_g, ln_b, riders=()):
    t = pc.shape[0]
    tt = _token_tile(t)

    def body(cur_ref, prev_ref, w_ref, b_ref, g_ref, bb_ref, y_ref, c_ref, scr, shf):
        i = pl.program_id(0)
        scr[0:HALO, :] = _glu(prev_ref[...]) * (i > 0).astype(F32)
        _glu_chunks(cur_ref, scr, tt)
        _shifted_copies(scr, shf, tt)
        lo = _lo_mask()
        for r0, cs in _chunks(tt, CONV_W):
            acc = jnp.zeros((ROW_CHUNK, LANES), F32) + b_ref[:, cs]
            for k in range(CONV_K):
                acc = acc + w_ref[k:k + 1, cs] * _tap_rows(scr, shf, 2 + k, r0, cs)
            y_ref[r0:r0 + ROW_CHUNK, cs] = acc
            yh, _ = _group_norm_stats(acc, lo)
            z = yh * g_ref[:, cs] + bb_ref[:, cs]
            c_ref[r0:r0 + ROW_CHUNK, cs] = (z * _sig(z)).astype(BF16)

    hb = tt // HALO
    return _call(
        body, (pc, pc, cw, cb, ln_g, ln_b), name="conv_fwd", grid=(t // tt,), riders=riders,
        in_specs=[pl.BlockSpec((tt, 2 * CONV_W), lambda i: (i, 0)),
                  pl.BlockSpec((HALO, 2 * CONV_W), lambda i: (jnp.maximum(i * hb - 1, 0), 0)),
                  _full((CONV_K, CONV_W)), _full((1, CONV_W)), _full((1, CONV_W)), _full((1, CONV_W))],
        out_specs=[pl.BlockSpec((tt, CONV_W), lambda i: (i, 0)), pl.BlockSpec((tt, CONV_W), lambda i: (i, 0))],
        out_shape=[jax.ShapeDtypeStruct((t, CONV_W), F32), jax.ShapeDtypeStruct((t, CONV_W), BF16)],
        scratch=[pltpu.VMEM((tt + HALO, CONV_W), F32), _shift_scratch(tt)])


def _out_proj(x, attn, c, w_out, b_out, g_ffn, riders=()):
    t = x.shape[0]
    tt = _wide_tile(t)

    def body(x_ref, a_ref, c_ref, w_ref, b_ref, g_ref, x1_ref, h2_ref):
        x1 = x_ref[...] + _dot(a_ref[...], w_ref[0:Q_COLS, :]) + _dot(c_ref[...], w_ref[Q_COLS:, :]) + b_ref[...]
        x1_ref[...] = x1
        r = lax.rsqrt(jnp.mean(x1 * x1, axis=-1, keepdims=True) + EPS)
        h2_ref[...] = (x1 * r * g_ref[...]).astype(BF16)

    row = lambda w: pl.BlockSpec((tt, w), lambda i: (i, 0))
    return _call(
        body, (x, attn, c, w_out, b_out, g_ffn), name="out_proj", grid=(t // tt,), riders=riders,
        in_specs=[row(D_MODEL), row(Q_COLS), row(CONV_W), _full((D_MODEL, D_MODEL)), _full((1, D_MODEL)), _full((1, D_MODEL))],
        out_specs=[row(D_MODEL), row(D_MODEL)],
        out_shape=[jax.ShapeDtypeStruct((t, D_MODEL), F32), jax.ShapeDtypeStruct((t, D_MODEL), BF16)])


def _chunks(rows, cols):
    return [(r0, slice(c0, c0 + LANES)) for c0 in range(0, cols, LANES) for r0 in range(0, rows, ROW_CHUNK)]


def _ffn_up(h2, w_up, dw, db):
    t = h2.shape[0]
    tt = _token_tile(t)
    nj = D_FF // FFN_CB

    def body(hc_ref, hp_ref, wg_ref, wu_ref, dwg_ref, dwu_ref, dbg_ref, dbu_ref,
             hg_ref, hu_ref, upg_ref, upu_ref, act_ref, sg, su):
        i = pl.program_id(1)
        hc = hc_ref[...]
        hp = hp_ref[...] * (i > 0).astype(BF16)
        ups = []
        for w_ref, dw_ref, db_ref, h_ref, up_ref, scr in ((wg_ref, dwg_ref, dbg_ref, hg_ref, upg_ref, sg),
                                                          (wu_ref, dwu_ref, dbu_ref, hu_ref, upu_ref, su)):
            cur = _dot(hc, w_ref[...])
            h_ref[...] = cur.astype(BF16)
            scr[0:FHALO, :] = _dot(hp, w_ref[...])
            scr[FHALO:FHALO + tt, :] = cur
            up = (dw_ref[0:1, :] * scr[FHALO - 2:FHALO - 2 + tt, :] + dw_ref[1:2, :] * scr[FHALO - 1:FHALO - 1 + tt, :]
                  + dw_ref[2:3, :] * cur + db_ref[...])
            up_ref[...] = up
            ups.append(up)
        g, u = ups
        act_ref[...] = (g * _sig(g) * u).astype(BF16)

    fb = tt // FHALO
    colg = lambda r: pl.BlockSpec((r, FFN_CB), lambda j, i: (0, j))
    colu = lambda r: pl.BlockSpec((r, FFN_CB), lambda j, i: (0, j + nj))
    tile = pl.BlockSpec((tt, FFN_CB), lambda j, i: (i, j))
    return pl.pallas_call(
        body, name="ffn_up", grid=(nj, t // tt),
        in_specs=[pl.BlockSpec((tt, D_MODEL), lambda j, i: (i, 0)),
                  pl.BlockSpec((FHALO, D_MODEL), lambda j, i: (jnp.maximum(i * fb - 1, 0), 0)),
                  pl.BlockSpec((None, D_MODEL, FFN_CB), lambda j, i: (j, 0, 0)),
                  pl.BlockSpec((None, D_MODEL, FFN_CB), lambda j, i: (j + nj, 0, 0)),
                  colg(3), colu(3), colg(1), colu(1)],
        out_specs=[tile] * 5,
        out_shape=[jax.ShapeDtypeStruct((t, D_FF), BF16), jax.ShapeDtypeStruct((t, D_FF), BF16),
                   jax.ShapeDtypeStruct((t, D_FF), F32), jax.ShapeDtypeStruct((t, D_FF), F32),
                   jax.ShapeDtypeStruct((t, D_FF), BF16)],
        scratch_shapes=[pltpu.VMEM((tt + FHALO, FFN_CB), F32), pltpu.VMEM((tt + FHALO, FFN_CB), F32)],
        compiler_params=_params(("parallel", "parallel")),
    )(h2, h2, w_up, w_up, dw, dw, db, db)


def _ffn_down(act, w_down, x1, target):
    t = act.shape[0]
    tt = _token_tile(t)

    def body(a_ref, w_ref, x1_ref, t_ref, dy_ref, loss_ref):
        err = x1_ref[...] + _dot(a_ref[...], w_ref[...]) - t_ref[...]
        dy_ref[...] = err * (1.0 / D_MODEL)

        @pl.when(pl.program_id(0) == 0)
        def _():
            loss_ref[...] = jnp.zeros_like(loss_ref)

        loss_ref[...] += jnp.sum(err * err, axis=0, keepdims=True)

    row = lambda w: pl.BlockSpec((tt, w), lambda i: (i, 0))
    return pl.pallas_call(
        body, name="ffn_down", grid=(t // tt,),
        in_specs=[row(D_FF), _full((D_FF, D_MODEL)), row(D_MODEL), row(D_MODEL)],
        out_specs=[row(D_MODEL), _full((1, D_MODEL))],
        out_shape=[jax.ShapeDtypeStruct((t, D_MODEL), F32), jax.ShapeDtypeStruct((1, D_MODEL), F32)],
        compiler_params=_params(("arbitrary",)),
    )(act, w_down, x1, target)


def _ffn_bwd_act(dy, w_down, up_g, up_u):
    t = dy.shape[0]
    tt = _token_tile(t)
    nj = D_FF // FFN_CB

    def body(dy_ref, wd_ref, g_ref, u_ref, dg_ref, du_ref, gbg_ref, gbu_ref):
        i = pl.program_id(1)
        d_act = _dot_nt(dy_ref[...].astype(BF16), wd_ref[...])
        g, u = g_ref[...], u_ref[...]
        s = _sig(g)
        d_u = d_act * (g * s)
        d_g = d_act * u * (s * (1.0 + g * (1.0 - s)))

        @pl.when(i == 0)
        def _():
            for r in (gbg_ref, gbu_ref):
                r[...] = jnp.zeros_like(r)

        for d, o_ref, gb_ref in ((d_g, dg_ref, gbg_ref), (d_u, du_ref, gbu_ref)):
            o_ref[...] = d.astype(BF16)
            gb_ref[...] += jnp.sum(d, axis=0, keepdims=True)

    tile = pl.BlockSpec((tt, FFN_CB), lambda j, i: (i, j))
    acc = pl.BlockSpec((1, FFN_CB), lambda j, i: (0, j))
    return pl.pallas_call(
        body, name="ffn_bwd_act", grid=(nj, t // tt),
        in_specs=[pl.BlockSpec((tt, D_MODEL), lambda j, i: (i, 0)), pl.BlockSpec((FFN_CB, D_MODEL), lambda j, i: (j, 0)),
                  tile, tile],
        out_specs=[tile, tile, acc, acc],
        out_shape=[jax.ShapeDtypeStruct((t, D_FF), BF16), jax.ShapeDtypeStruct((t, D_FF), BF16),
                   jax.ShapeDtypeStruct((1, D_FF), F32), jax.ShapeDtypeStruct((1, D_FF), F32)],
        compiler_params=_params(("parallel", "arbitrary")),
    )(dy, w_down, up_g, up_u)


def _ffn_bwd_conv(dg, du, hg, hu, dw):
    t = dg.shape[0]
    tt = _token_tile(t)
    nj = D_FF // FFN_CB
    ni = t // tt

    def body(gc_ref, gn_ref, uc_ref, un_ref, hg_ref, hu_ref, dwg_ref, dwu_ref, og_ref, ou_ref, gwg_ref, gwu_ref, scr):
        i = pl.program_id(1)
        last = (i < ni - 1).astype(F32)

        @pl.when(i == 0)
        def _():
            gwg_ref[...] = jnp.zeros_like(gwg_ref)
            gwu_ref[...] = jnp.zeros_like(gwu_ref)

        for c_ref, n_ref, h_ref, dw_ref, o_ref, gw_ref in ((gc_ref, gn_ref, hg_ref, dwg_ref, og_ref, gwg_ref),
                                                           (uc_ref, un_ref, hu_ref, dwu_ref, ou_ref, gwu_ref)):
            scr[0:tt, :] = c_ref[...].astype(F32)
            scr[tt:tt + FHALO, :] = n_ref[...].astype(F32) * last
            sums = None
            for r0, cs in _chunks(tt, FFN_CB):
                shifted = [scr[r0 + d:r0 + d + ROW_CHUNK, cs] for d in (2, 1, 0)]
                o_ref[r0:r0 + ROW_CHUNK, cs] = (dw_ref[0:1, cs] * shifted[0] + dw_ref[1:2, cs] * shifted[1]
                                                + dw_ref[2:3, cs] * shifted[2]).astype(BF16)
                hw = h_ref[r0:r0 + ROW_CHUNK, cs].astype(F32)
                prods = [hw * d for d in shifted]
                sums = prods if r0 == 0 else [a + b for a, b in zip(sums, prods)]
                if r0 == tt - ROW_CHUNK:
                    gw_ref[:, cs] += _rows_to_tile([jnp.sum(a, axis=0, keepdims=True) for a in sums], 8)

    fb = tt // FHALO
    tile = pl.BlockSpec((tt, FFN_CB), lambda j, i: (i, j))
    nxt = pl.BlockSpec((FHALO, FFN_CB), lambda j, i: (jnp.minimum((i + 1) * fb, t // FHALO - 1), j))
    acc = pl.BlockSpec((8, FFN_CB), lambda j, i: (0, j))
    return pl.pallas_call(
        body, name="ffn_bwd_conv", grid=(nj, ni),
        in_specs=[tile, nxt, tile, nxt, tile, tile, pl.BlockSpec((3, FFN_CB), lambda j, i: (0, j)),
                  pl.BlockSpec((3, FFN_CB), lambda j, i: (0, j + nj))],
        out_specs=[tile, tile, acc, acc],
        out_shape=[jax.ShapeDtypeStruct((t, D_FF), BF16), jax.ShapeDtypeStruct((t, D_FF), BF16),
                   jax.ShapeDtypeStruct((8, D_FF), F32), jax.ShapeDtypeStruct((8, D_FF), F32)],
        scratch_shapes=[pltpu.VMEM((tt + FHALO, FFN_CB), F32)],
        compiler_params=_params(("parallel", "arbitrary")),
    )(dg, dg, du, du, hg, hu, dw, dw)


def _ffn_bwd_in(dhg, dhu, w_up, x1, dy, g_ffn, riders=()):
    t = x1.shape[0]
    tt = _token_tile(t)

    def body(dg_ref, du_ref, w_ref, x1_ref, dy_ref, g_ref, dx_ref, gg_ref):
        d_h2 = (_dot_nt(dg_ref[:, 0:FFN_CB], w_ref[0]) + _dot_nt(dg_ref[:, FFN_CB:], w_ref[1])
                + _dot_nt(du_ref[:, 0:FFN_CB], w_ref[2]) + _dot_nt(du_ref[:, FFN_CB:], w_ref[3]))
        x1 = x1_ref[...]
        r = lax.rsqrt(jnp.mean(x1 * x1, axis=-1, keepdims=True) + EPS)
        xh = x1 * r
        gd = d_h2 * g_ref[...]
        dx_ref[...] = dy_ref[...] + r * (gd - xh * jnp.mean(gd * xh, axis=-1, keepdims=True))

        @pl.when(pl.program_id(0) == 0)
        def _():
            gg_ref[...] = jnp.zeros_like(gg_ref)

        gg_ref[...] += jnp.sum(d_h2 * xh, axis=0, keepdims=True)

    row = lambda w: pl.BlockSpec((tt, w), lambda i: (i, 0))
    return _call(
        body, (dhg, dhu, w_up, x1, dy, g_ffn), name="ffn_bwd_in", grid=(t // tt,), riders=riders,
        in_specs=[row(D_FF), row(D_FF), _full((4, D_MODEL, FFN_CB)), row(D_MODEL), row(D_MODEL), _full((1, D_MODEL))],
        out_specs=[row(D_MODEL), _full((1, D_MODEL))],
        out_shape=[jax.ShapeDtypeStruct((t, D_MODEL), F32), jax.ShapeDtypeStruct((1, D_MODEL), F32)])


def _grad_weight(a, b, nj, mb, name, lead=None, into=None, offset=0):
    t, m = a.shape
    n = b.shape[1]
    nb_ = n // nj
    tt = GRAD_TILE if t % GRAD_TILE == 0 else _token_tile(t)

    def body(*refs):
        a_ref, b_ref, o_ref = refs[0], refs[1], refs[-1]

        @pl.when(pl.program_id(2) == 0)
        def _():
            o_ref[...] = jnp.zeros_like(o_ref)

        o_ref[0] += _dot_tn(a_ref[...].astype(BF16), b_ref[...].astype(BF16))

    in_specs = [pl.BlockSpec((tt, mb), lambda j, mi, i: (i, mi)), pl.BlockSpec((tt, nb_), lambda j, mi, i: (i, j))]
    out_shape = jax.ShapeDtypeStruct((lead or nj, m, nb_) if into is None else into.shape, F32)
    return pl.pallas_call(
        body, name=name, grid=(nj, m // mb, t // tt),
        in_specs=in_specs if into is None else in_specs + [pl.BlockSpec(memory_space=pl.ANY)],
        out_specs=pl.BlockSpec((1, mb, nb_), lambda j, mi, i: (j + offset, mi, 0)),
        out_shape=out_shape,
        input_output_aliases={} if into is None else {2: 0},
        compiler_params=_params(("parallel", "parallel", "arbitrary")),
    )(*((a, b) if into is None else (a, b, into)))


def _out_proj_bwd(dx1, w_out):
    t = dx1.shape[0]
    tt = _wide_tile(t)

    def body(d_ref, w_ref, dm_ref, gb_ref):
        d = d_ref[...]
        dm_ref[...] = _dot_nt(d.astype(BF16), w_ref[...])

        @pl.when(pl.program_id(0) == 0)
        def _():
            gb_ref[...] = jnp.zeros_like(gb_ref)

        gb_ref[...] += jnp.sum(d, axis=0, keepdims=True)

    row = pl.BlockSpec((tt, D_MODEL), lambda i: (i, 0))
    return pl.pallas_call(
        body, name="out_proj_bwd", grid=(t // tt,),
        in_specs=[row, _full((D_MODEL, D_MODEL))],
        out_specs=[row, _full((1, D_MODEL))],
        out_shape=[jax.ShapeDtypeStruct((t, D_MODEL), F32), jax.ShapeDtypeStruct((1, D_MODEL), F32)],
        compiler_params=_params(("arbitrary",)),
    )(dx1, w_out)


def _conv_bwd(dmix, y, pc, cw, ln_g, ln_b, riders=()):
    t = y.shape[0]
    tt = _token_tile(t)
    ni = t // tt
    ncb = CONV_W // LANES

    def body(dc_ref, dcn_ref, y_ref, yn_ref, pc_ref, pcp_ref, w_ref, g_ref, bb_ref,
             dp_ref, gw_ref, gb_ref, gg_ref, gbb_ref, gbin_ref, scr_d, scr_c, scr_o, shf_d, shf_c):
        i = pl.program_id(0)
        lo = _lo_mask()

        @pl.when(i == 0)
        def _():
            for r in (gw_ref, gb_ref, gg_ref, gbb_ref, gbin_ref):
                r[...] = jnp.zeros_like(r)

        def norm_bwd(dc, yv, cs):
            yh, rstd = _group_norm_stats(yv, lo)
            z = yh * g_ref[:, cs] + bb_ref[:, cs]
            s = _sig(z)
            dz = dc * (s * (1.0 + z * (1.0 - s)))
            dyh = dz * g_ref[:, cs]
            d_y = rstd * (dyh - _half_mean(dyh, lo) - yh * _half_mean(dyh * yh, lo))
            return d_y, dz, yh

        for p in range(ncb):
            cs = slice(p * LANES, (p + 1) * LANES)
            d_y, dz, yh = norm_bwd(dc_ref[:, cs], y_ref[:, cs], cs)
            scr_d[0:tt, cs] = d_y
            gg_ref[:, cs] += jnp.sum(dz * yh, axis=0, keepdims=True)
            gbb_ref[:, cs] += jnp.sum(dz, axis=0, keepdims=True)
            gb_ref[:, cs] += jnp.sum(d_y, axis=0, keepdims=True)
            d_yn, _, _ = norm_bwd(dcn_ref[:, cs], yn_ref[:, cs], cs)
            scr_d[tt:tt + HALO, cs] = d_yn * (i < ni - 1).astype(F32)
        scr_c[0:HALO, :] = _glu(pcp_ref[...]) * (i > 0).astype(F32)
        scr_c[HALO:HALO + tt, :] = _glu(pc_ref[...])
        _shifted_copies(scr_d, shf_d, tt)
        _shifted_copies(scr_c, shf_c, tt)

        rid = lax.broadcasted_iota(jnp.int32, (HALO, LANES), 0)
        for cbk in range(ncb):
            cs = slice(cbk * LANES, (cbk + 1) * LANES)
            for rb in range(tt // ROW_CHUNK):
                r0 = rb * ROW_CHUNK
                acc = jnp.zeros((ROW_CHUNK, LANES), F32)
                for k in range(CONV_K):
                    acc = acc + w_ref[k:k + 1, cs] * _tap_rows(scr_d, shf_d, 30 - k, r0, cs)
                scr_o[r0:r0 + ROW_CHUNK, cs] = acc
            gwt = jnp.zeros((HALO, LANES), F32)
            for k in range(CONV_K):
                acc = jnp.zeros((ROW_CHUNK, LANES), F32)
                for rb in range(tt // ROW_CHUNK):
                    r0 = rb * ROW_CHUNK
                    acc = acc + scr_d[r0:r0 + ROW_CHUNK, cs] * _tap_rows(scr_c, shf_c, 2 + k, r0, cs)
                gwt = jnp.where(rid == k, jnp.sum(acc, axis=0, keepdims=True), gwt)
            gw_ref[:, cs] += gwt
        d_c0 = scr_o[...]
        a = pc_ref[:, 0:CONV_W]
        s = _sig(pc_ref[:, CONV_W:])
        d_a = d_c0 * s
        d_gate = d_c0 * a * s * (1.0 - s)
        dp_ref[:, 0:CONV_W] = d_a.astype(BF16)
        dp_ref[:, CONV_W:] = d_gate.astype(BF16)
        gbin_ref[:, 0:CONV_W] += jnp.sum(d_a, axis=0, keepdims=True)
        gbin_ref[:, CONV_W:] += jnp.sum(d_gate, axis=0, keepdims=True)

    hb = tt // HALO
    nxt = lambda col: pl.BlockSpec((HALO, CONV_W), lambda i: (jnp.minimum((i + 1) * hb, t // HALO - 1), col))
    return _call(
        body, (dmix, dmix, y, y, pc, pc, cw, ln_g, ln_b), name="conv_bwd", grid=(ni,), riders=riders,
        in_specs=[pl.BlockSpec((tt, CONV_W), lambda i: (i, 1)), nxt(1),
                  pl.BlockSpec((tt, CONV_W), lambda i: (i, 0)), nxt(0),
                  pl.BlockSpec((tt, 2 * CONV_W), lambda i: (i, 0)),
                  pl.BlockSpec((HALO, 2 * CONV_W), lambda i: (jnp.maximum(i * hb - 1, 0), 0)),
                  _full((CONV_K, CONV_W)), _full((1, CONV_W)), _full((1, CONV_W))],
        out_specs=[pl.BlockSpec((tt, 2 * CONV_W), lambda i: (i, 0)), _full((HALO, CONV_W)), _full((1, CONV_W)),
                   _full((1, CONV_W)), _full((1, CONV_W)), _full((1, 2 * CONV_W))],
        out_shape=[jax.ShapeDtypeStruct((t, 2 * CONV_W), BF16), jax.ShapeDtypeStruct((HALO, CONV_W), F32),
                   jax.ShapeDtypeStruct((1, CONV_W), F32), jax.ShapeDtypeStruct((1, CONV_W), F32),
                   jax.ShapeDtypeStruct((1, CONV_W), F32), jax.ShapeDtypeStruct((1, 2 * CONV_W), F32)],
        scratch=[pltpu.VMEM((tt + HALO, CONV_W), F32), pltpu.VMEM((tt + HALO, CONV_W), F32),
                 pltpu.VMEM((tt, CONV_W), F32), _shift_scratch(tt), _shift_scratch(tt)])


def _attn_bwd(qkvn, dmix, lse, sinks, bias, riders=()):
    t = qkvn.shape[0]
    nb = t // BLOCK
    bps = _attn_blocks_per_step(nb)

    def body(sink_ref, q_ref, kvc_ref, kvp_ref, bias_ref, do_ref, lse_ref, dq_ref, dcur_ref, dprev_ref, ds_ref):
        i = pl.program_id(0)
        lo = _lo_mask()
        lane1 = lax.broadcasted_iota(jnp.int32, (1, LANES), 1)
        lane = lax.broadcasted_iota(jnp.int32, (BLOCK, LANES), 1)
        dsink = jnp.zeros((1, LANES), F32)
        for b in range(bps):
            rows_b = slice(b * BLOCK, (b + 1) * BLOCK)
            k_dup, v_dup = _attn_keys(_attn_window(b, kvc_ref, kvp_ref))
            lse_t = lse_ref[rows_b, :]
            dqs = [jnp.zeros((BLOCK, LANES), F32) for _ in range(4)]
            dkv = []
            for kk in range(N_HEADS // GROUP):
                s, qs = _group_scores(q_ref, rows_b, kk, lo, k_dup, _attn_bias_of(bias_ref, b, kk))
                lse = jnp.concatenate([jnp.sum(jnp.where(lane == GROUP * kk + a, lse_t, 0.0), axis=-1, keepdims=True)
                                       for a in range(GROUP)], axis=0)
                prob = jnp.exp(s - lse)
                dos = _stack_heads(do_ref, rows_b, kk, lo, BF16)
                dp = _dot_nt(dos, v_dup[kk])
                dsum = jnp.sum(prob * dp, axis=-1, keepdims=True)
                dsb = (prob * (dp - dsum) * 0.125).astype(BF16)
                sink = _per_head_column([sink_ref[GROUP * kk + a] for a in range(GROUP)])
                dsk = -jnp.exp(sink - lse) * dsum
                dq = _dot(dsb, k_dup[kk])
                for a in range(GROUP):
                    h = GROUP * kk + a
                    rows = slice(a * BLOCK, (a + 1) * BLOCK)
                    hm = lo if h % 2 == 0 else jnp.logical_not(lo)
                    dqs[h // 2] = dqs[h // 2] + jnp.where(hm, dq[rows], 0.0)
                    dsink = dsink + jnp.where(lane1 == h, jnp.sum(dsk[rows], axis=0, keepdims=True), 0.0)
                dk_x = _dot_tn(dsb, qs)
                dv_x = _dot_tn(prob.astype(BF16), dos)
                dkv.append((dk_x + pltpu.roll(dk_x, HEAD_DIM, 1), dv_x + pltpu.roll(dv_x, HEAD_DIM, 1)))
            for p in range(4):
                dq_ref[rows_b, p * LANES:(p + 1) * LANES] = dqs[p]
            dk = jnp.where(lo, dkv[0][0], dkv[1][0])
            dv = jnp.where(lo, dkv[0][1], dkv[1][1])
            dprev_ref[rows_b, 0:LANES] = dk[0:BLOCK]
            dprev_ref[rows_b, LANES:] = dv[0:BLOCK]
            dcur_ref[rows_b, 0:LANES] = dk[BLOCK:]
            dcur_ref[rows_b, LANES:] = dv[BLOCK:]

        @pl.when(i == 0)
        def _():
            ds_ref[...] = jnp.zeros_like(ds_ref)

        ds_ref[...] += dsink

    blk = lambda w: pl.BlockSpec((bps * BLOCK, w), lambda i: (i, 0))
    return _call(
        body, (sinks, qkvn, qkvn, qkvn, bias, dmix, lse), name="attn_bwd", grid=(nb // bps,), riders=riders,
        in_specs=[pl.BlockSpec(memory_space=pltpu.SMEM)] + _attn_specs(bps) + [blk(Q_COLS), blk(LANES)],
        out_specs=[blk(Q_COLS), blk(2 * LANES), blk(2 * LANES), _full((1, LANES))],
        out_shape=[jax.ShapeDtypeStruct((t, Q_COLS), F32), jax.ShapeDtypeStruct((t, 2 * LANES), F32),
                   jax.ShapeDtypeStruct((t, 2 * LANES), F32), jax.ShapeDtypeStruct((1, LANES), F32)])


def _qk_norm_bwd(dqn, dcur, dprev, pq, gq2, gk2, riders=()):
    t = dqn.shape[0]
    nb = t // BLOCK
    tt = _token_tile(t)
    ni = t // tt

    def body(dq_ref, dc_ref, dt_ref, dn_ref, pq_ref, gq_ref, gk_ref, dp_ref, gbin_ref, gg_ref):
        i = pl.program_id(0)
        lo = _lo_mask()
        nxt = dn_ref[...] * (i < ni - 1).astype(F32)
        from_next = nxt if tt == BLOCK else jnp.concatenate([dt_ref[BLOCK:, :], nxt], axis=0)
        dkv = dc_ref[...] + from_next

        @pl.when(i == 0)
        def _():
            gbin_ref[...] = jnp.zeros_like(gbin_ref)
            gg_ref[...] = jnp.zeros_like(gg_ref)

        for p in range(5):
            cs = slice(p * LANES, (p + 1) * LANES)
            seg = pq_ref[:, cs]
            dn = dq_ref[:, cs] if p < 4 else dkv[:, 0:LANES]
            gain = gq_ref[...] if p < 4 else gk_ref[...]
            rr = lax.rsqrt(_half_mean(seg * seg, lo) + EPS)
            xh = seg * rr
            gd = dn * gain
            d = rr * (gd - xh * _half_mean(gd * xh, lo))
            dp_ref[:, cs] = d.astype(BF16)
            gbin_ref[:, cs] += jnp.sum(d, axis=0, keepdims=True)
            gg_ref[:, cs] += jnp.sum(dn * xh, axis=0, keepdims=True)
        dv = dkv[:, LANES:]
        dp_ref[:, 640:768] = dv.astype(BF16)
        gbin_ref[:, 640:768] += jnp.sum(dv, axis=0, keepdims=True)

    blk = lambda w: pl.BlockSpec((tt, w), lambda i: (i, 0))
    per = tt // BLOCK
    return _call(
        body, (dqn, dcur, dprev, dprev, pq, gq2, gk2), name="qk_norm_bwd", grid=(ni,), riders=riders,
        in_specs=[blk(Q_COLS), blk(2 * LANES), blk(2 * LANES),
                  pl.BlockSpec((BLOCK, 2 * LANES), lambda i: (jnp.minimum((i + 1) * per, nb - 1), 0)),
                  blk(QKV_COLS), _full((1, LANES)), _full((1, LANES))],
        out_specs=[blk(QKV_COLS), _full((1, QKV_COLS)), _full((1, 5 * LANES))],
        out_shape=[jax.ShapeDtypeStruct((t, QKV_COLS), BF16), jax.ShapeDtypeStruct((1, QKV_COLS), F32),
                   jax.ShapeDtypeStruct((1, 5 * LANES), F32)])


def _in_proj_bwd(dpq, dpc, w_in, x, dx1, g_mix):
    t = x.shape[0]
    tt = _wide_tile(t)

    def body(dq_ref, dc_ref, w_ref, x_ref, d1_ref, g_ref, gx_ref, gg_ref):
        d_h = _dot_nt(dq_ref[...], w_ref[:, 0:QKV_COLS]) + _dot_nt(dc_ref[...], w_ref[:, QKV_COLS:])
        xv = x_ref[...]
        r = lax.rsqrt(jnp.mean(xv * xv, axis=-1, keepdims=True) + EPS)
        xh = xv * r
        gd = d_h * g_ref[...]
        gx_ref[...] = d1_ref[...] + r * (gd - xh * jnp.mean(gd * xh, axis=-1, keepdims=True))

        @pl.when(pl.program_id(0) == 0)
        def _():
            gg_ref[...] = jnp.zeros_like(gg_ref)

        gg_ref[...] += jnp.sum(d_h * xh, axis=0, keepdims=True)

    row = lambda w: pl.BlockSpec((tt, w), lambda i: (i, 0))
    return pl.pallas_call(
        body, name="in_proj_bwd", grid=(t // tt,),
        in_specs=[row(QKV_COLS), row(2 * CONV_W), _full((D_MODEL, IN_COLS)), row(D_MODEL), row(D_MODEL), _full((1, D_MODEL))],
        out_specs=[row(D_MODEL), _full((1, D_MODEL))],
        out_shape=[jax.ShapeDtypeStruct((t, D_MODEL), F32), jax.ShapeDtypeStruct((1, D_MODEL), F32)],
        compiler_params=_params(("arbitrary",)),
    )(dpq, dpc, w_in, x, dx1, g_mix)


def _row_block(r):
    if r <= 256:
        return r
    return max(b for b in range(8, 257, 8) if r % b == 0)


def _adamw(w, g, m, v, name):
    r, c = w.shape
    rb = _row_block(r)

    def body(w_ref, g_ref, m_ref, v_ref, d_ref, nm_ref, nv_ref):
        gv = g_ref[...]
        nm = ADAM_B1 * m_ref[...] + (1.0 - ADAM_B1) * gv
        nv = ADAM_B2 * v_ref[...] + (1.0 - ADAM_B2) * (gv * gv)
        m_hat = nm / (1.0 - ADAM_B1 ** ADAM_STEP)
        v_hat = nv / (1.0 - ADAM_B2 ** ADAM_STEP)
        d_ref[...] = -ADAM_LR * (m_hat / (jnp.sqrt(v_hat) + ADAM_EPS) + ADAM_WD * w_ref[...])
        nm_ref[...] = nm
        nv_ref[...] = nv

    blk = pl.BlockSpec((rb, c), lambda i: (i, 0))
    shp = jax.ShapeDtypeStruct((r, c), F32)
    return pl.pallas_call(
        body, name=name, grid=(r // rb,), in_specs=[blk] * 4, out_specs=[blk] * 3, out_shape=[shp] * 3,
        compiler_params=_params(("parallel",)),
    )(w, g, m, v)


def _place():
    x, y, c = lax.axis_index("x"), lax.axis_index("y"), lax.axis_index("c")
    chips = [(1 - x, y), (x, 1 - y), (1 - x, 1 - y)]
    return x, y, c, chips


def _gather_all(v):
    r = v.shape[0]

    def body(v_ref, all_ref, sum_ref, send_sems, recv_sems):
        x, y, c, _ = _place()
        me = 4 * x + 2 * y + c
        all_ref[me] = v_ref[...]
        copies = []
        for k in range(1, 8):
            kx, ky, kc = (k >> 2) & 1, (k >> 1) & 1, k & 1
            peer = (x ^ kx, y ^ ky, c ^ kc)
            cp = pltpu.make_async_remote_copy(src_ref=v_ref, dst_ref=all_ref.at[me], send_sem=send_sems.at[k - 1],
                                              recv_sem=recv_sems.at[k - 1], device_id=peer, device_id_type=MESH)
            cp.start()
            copies.append((cp, 4 * peer[0] + 2 * peer[1] + peer[2]))
        for k, (cp, src_idx) in enumerate(copies):
            pltpu.make_async_remote_copy(src_ref=v_ref, dst_ref=all_ref.at[src_idx], send_sem=send_sems.at[k],
                                         recv_sem=recv_sems.at[k], device_id=(x, y, c), device_id_type=MESH).wait_recv()
        for cp, _ in copies:
            cp.wait_send()
        tot = all_ref[0]
        for d in range(1, 8):
            tot = tot + all_ref[d]
        sum_ref[...] = tot

    vm = pl.BlockSpec(memory_space=pltpu.VMEM)
    return pl.pallas_call(
        body, name="gather_all", in_specs=[vm], out_specs=[vm, vm],
        out_shape=[jax.ShapeDtypeStruct((8, r, LANES), v.dtype), jax.ShapeDtypeStruct((r, LANES), v.dtype)],
        scratch_shapes=[pltpu.SemaphoreType.DMA((7,)), pltpu.SemaphoreType.DMA((7,))],
        compiler_params=pltpu.CompilerParams(vmem_limit_bytes=VMEM_LIMIT),
    )(v)


def _remote(src, dst, send_sem, recv_sem, to):
    return pltpu.make_async_remote_copy(src_ref=src, dst_ref=dst, send_sem=send_sem, recv_sem=recv_sem,
                                        device_id=to, device_id_type=MESH)


def _dma_sems(*shape):
    return pltpu.SemaphoreType.DMA(shape)


def _gather_first(shards):
    n = len(shards)

    def copies(ins, outs, sems):
        x, y, c, chips = _place()
        me = 2 * x + y
        local = [pltpu.make_async_copy(ins[a], outs[a].at[me], sems[2].at[a]) for a in range(n)]
        sends = [_remote(ins[a].at[c], outs[a].at[me, c], sems[0].at[a, j], sems[1].at[a, j], (*chip, c))
                 for a in range(n) for j, chip in enumerate(chips)]
        lands = [_remote(ins[a].at[c], outs[a].at[2 * chip[0] + chip[1], c], sems[0].at[a, j], sems[1].at[a, j], (x, y, c))
                 for a in range(n) for j, chip in enumerate(chips)]
        return local, sends, lands

    def start(ins, outs, sems):
        local, sends, _ = copies(ins, outs, sems)
        for cp in local + sends:
            cp.start()

    def finish(ins, outs, sems):
        local, sends, lands = copies(ins, outs, sems)
        for cp in lands:
            cp.wait_recv()
        for cp in sends:
            cp.wait_send()
        for cp in local:
            cp.wait()

    return _Rider(shards, [jax.ShapeDtypeStruct((4,) + s.shape, s.dtype) for s in shards],
                  [_dma_sems(n, 3), _dma_sems(n, 3), _dma_sems(n)], start, finish)


def _gather_second(partials):
    n = len(partials)

    def copies(outs, sems):
        x, y, c, chips = _place()
        sends, lands = [], []
        for a in range(n):
            for j, chip in enumerate(chips):
                mine = outs[a].at[2 * chip[0] + chip[1], c]
                theirs = outs[a].at[2 * chip[0] + chip[1], 1 - c]
                sends.append(_remote(mine, mine, sems[0].at[a, j], sems[1].at[a, j], (x, y, 1 - c)))
                lands.append(_remote(theirs, theirs, sems[0].at[a, j], sems[1].at[a, j], (x, y, c)))
        return sends, lands

    def start(ins, outs, sems):
        for cp in copies(outs, sems)[0]:
            cp.start()

    def finish(ins, outs, sems):
        sends, lands = copies(outs, sems)
        for cp in lands:
            cp.wait_recv()
        for cp in sends:
            cp.wait_send()

    return _Rider(partials, [jax.ShapeDtypeStruct(p.shape, p.dtype) for p in partials],
                  [_dma_sems(n, 3), _dma_sems(n, 3)], start, finish, aliases={a: a for a in range(n)})


def _gather_small(v):
    def copies(ins, outs, sems):
        x, y, c, _ = _place()
        me = 4 * x + 2 * y + c
        local = pltpu.make_async_copy(ins[0], outs[0].at[me], sems[2].at[0])
        sends, lands = [], []
        for k in range(1, 8):
            peer = (x ^ ((k >> 2) & 1), y ^ ((k >> 1) & 1), c ^ (k & 1))
            sends.append(_remote(ins[0], outs[0].at[me], sems[0].at[k - 1], sems[1].at[k - 1], peer))
            lands.append(_remote(ins[0], outs[0].at[4 * peer[0] + 2 * peer[1] + peer[2]], sems[0].at[k - 1],
                                 sems[1].at[k - 1], (x, y, c)))
        return local, sends, lands

    def start(ins, outs, sems):
        local, sends, _ = copies(ins, outs, sems)
        for cp in [local] + sends:
            cp.start()

    def finish(ins, outs, sems):
        local, sends, lands = copies(ins, outs, sems)
        for cp in lands:
            cp.wait_recv()
        for cp in sends:
            cp.wait_send()
        local.wait()

    return _Rider([v], [jax.ShapeDtypeStruct((8,) + v.shape, v.dtype)], [_dma_sems(7), _dma_sems(7), _dma_sems(1)],
                  start, finish)


def _swap_halves(grads):
    n = len(grads)

    def copies(ins, outs, sems):
        x, y, c, _ = _place()
        return [_remote(ins[a].at[j, 1 - c], outs[a].at[j], sems[0].at[a, j], sems[1].at[a, j], (x, y, 1 - c))
                for a in range(n) for j in range(4)]

    def start(ins, outs, sems):
        for cp in copies(ins, outs, sems):
            cp.start()

    def finish(ins, outs, sems):
        for cp in copies(ins, outs, sems):
            cp.wait()

    return _Rider(grads, [jax.ShapeDtypeStruct((4,) + g.shape[2:], g.dtype) for g in grads],
                  [_dma_sems(n, 4), _dma_sems(n, 4)], start, finish)


def _add_sibling(g, got, c_idx, name):
    _, _, h, c = g.shape

    def body(s_ref, a_ref, b_ref, o_ref):
        o_ref[...] = (a_ref[...] + b_ref[...]).astype(BF16)

    return pl.pallas_call(
        body, name=name,
        grid_spec=pltpu.PrefetchScalarGridSpec(
            num_scalar_prefetch=1, grid=(4,),
            in_specs=[pl.BlockSpec((None, None, h, c), lambda j, s: (j, s[0], 0, 0)),
                      pl.BlockSpec((None, h, c), lambda j, s: (j, 0, 0))],
            out_specs=pl.BlockSpec((None, h, c), lambda j, s: (j, 0, 0))),
        out_shape=jax.ShapeDtypeStruct((4, h, c), BF16),
        compiler_params=_params(("parallel",)),
    )(c_idx, g, got)


def _exchange_chips(parts):
    n = len(parts)

    def copies(ins, outs, sems):
        x, y, c, chips = _place()
        return [_remote(ins[a].at[2 * chip[0] + chip[1]], outs[a].at[j], sems[0].at[a, j], sems[1].at[a, j], (*chip, c))
                for a in range(n) for j, chip in enumerate(chips)]

    def start(ins, outs, sems):
        for cp in copies(ins, outs, sems):
            cp.start()

    def finish(ins, outs, sems):
        for cp in copies(ins, outs, sems):
            cp.wait()

    return _Rider(parts, [jax.ShapeDtypeStruct((3,) + p.shape[1:], p.dtype) for p in parts],
                  [_dma_sems(n, 3), _dma_sems(n, 3)], start, finish)


def _add_chips(part, got, chip_idx, name):
    _, h, c = part.shape

    def body(s_ref, a_ref, b_ref, o_ref):
        o_ref[...] = ((a_ref[...].astype(F32) + b_ref[0].astype(F32)) + b_ref[1].astype(F32)) + b_ref[2].astype(F32)

    return pl.pallas_call(
        body, name=name,
        grid_spec=pltpu.PrefetchScalarGridSpec(
            num_scalar_prefetch=1, grid=(1,),
            in_specs=[pl.BlockSpec((None, h, c), lambda i, s: (s[0], 0, 0)),
                      pl.BlockSpec((3, h, c), lambda i, s: (0, 0, 0))],
            out_specs=pl.BlockSpec((h, c), lambda i, s: (0, 0))),
        out_shape=jax.ShapeDtypeStruct((h, c), F32),
        compiler_params=_params(("arbitrary",)),
    )(chip_idx, part, got)


def _join_halves(halves):
    n = len(halves)

    def copies(ins, outs, sems):
        x, y, c, _ = _place()
        local = [pltpu.make_async_copy(ins[a], outs[a].at[c], sems[2].at[a]) for a in range(n)]
        sends = [_remote(ins[a], outs[a].at[c], sems[0].at[a], sems[1].at[a], (x, y, 1 - c)) for a in range(n)]
        lands = [_remote(ins[a], outs[a].at[1 - c], sems[0].at[a], sems[1].at[a], (x, y, c)) for a in range(n)]
        return local, sends, lands

    def start(ins, outs, sems):
        local, sends, _ = copies(ins, outs, sems)
        for cp in local + sends:
            cp.start()

    def finish(ins, outs, sems):
        local, sends, lands = copies(ins, outs, sems)
        for cp in lands:
            cp.wait_recv()
        for cp in sends:
            cp.wait_send()
        for cp in local:
            cp.wait()

    return _Rider(halves, [jax.ShapeDtypeStruct((2,) + h.shape, h.dtype) for h in halves],
                  [_dma_sems(n), _dma_sems(n), _dma_sems(n)], start, finish)


def _pack(parts):
    flat = []
    for p in parts:
        p = p.reshape(-1).astype(F32)
        flat.append(jnp.pad(p, (0, (-p.shape[0]) % LANES)))
    v = jnp.concatenate(flat)
    v = jnp.pad(v, (0, (-v.shape[0]) % (8 * LANES)))
    return v.reshape(-1, LANES)


def _unpack(v, shapes):
    flat = v.reshape(-1)
    out, off = [], 0
    for s in shapes:
        n = 1
        for d in s:
            n *= d
        out.append(flat[off:off + n].reshape(s))
        off += n + (-n) % LANES
    return out


def kernel(x, mix_norm_gain, w_in, b_in, q_norm_gain, k_norm_gain, attn_sinks, conv_dw_w, conv_dw_b, conv_norm_gain, conv_norm_bias, w_out, b_out, ffn_norm_gain, w_up, ffn_dw_w, ffn_dw_b, w_down, loss_target, m_mix_norm_gain, m_w_in, m_b_in, m_q_norm_gain, m_k_norm_gain, m_attn_sinks, m_conv_dw_w, m_conv_dw_b, m_conv_norm_gain, m_conv_norm_bias, m_w_out, m_b_out, m_ffn_norm_gain, m_w_up, m_ffn_dw_w, m_ffn_dw_b, m_w_down, v_mix_norm_gain, v_w_in, v_b_in, v_q_norm_gain, v_k_norm_gain, v_attn_sinks, v_conv_dw_w, v_conv_dw_b, v_conv_norm_gain, v_conv_norm_bias, v_w_out, v_b_out, v_ffn_norm_gain, v_w_up, v_ffn_dw_w, v_ffn_dw_b, v_w_down):
    t = x.shape[1]
    xi, yi, ci = lax.axis_index("x"), lax.axis_index("y"), lax.axis_index("c")
    chip = 2 * xi + yi
    c_idx = jnp.reshape(ci, (1,)).astype(jnp.int32)
    chip_idx = jnp.reshape(chip, (1,)).astype(jnp.int32)
    x2 = x.reshape(t, D_MODEL)
    tgt = loss_target.reshape(t, D_MODEL)

    big = [w_in, w_out, w_up, w_down]
    halves = [w.astype(BF16).reshape(2, w.shape[0] // 2, w.shape[1]) for w in big]
    h_wi, h_wo, h_wu, h_wd = halves
    (p_wi,) = _run_riders([_gather_first([h_wi])], "gather_w_in_first")
    (g_wi,) = _run_riders([_gather_second([p_wi])], "gather_w_in_second")
    wi = jnp.concatenate([g_wi[j].reshape(D_MODEL, IN_COLS // 4) for j in range(4)], axis=1)
    row = lambda a: a.reshape(1, -1)
    gq2 = row(jnp.concatenate([q_norm_gain, q_norm_gain]))
    gk2 = row(jnp.concatenate([k_norm_gain, k_norm_gain]))

    (h1, pq, pc, qkvn), (p_wo, small_w) = _fwd_in(x2, row(mix_norm_gain), wi, row(b_in), gq2, gk2,
                                                  riders=[_gather_first([h_wo]), _gather_small(_pack([conv_dw_w, ffn_dw_w]))])
    per_chip = [_unpack(small_w[4 * (j // 2) + 2 * (j % 2)], [conv_dw_w.shape, ffn_dw_w.shape]) for j in range(4)]
    cw = jnp.concatenate([p[0] for p in per_chip], axis=1)
    fw = jnp.concatenate([p[1] for p in per_chip], axis=1)
    bias = _attn_bias()
    (attn, lse), (g_wo, p_wu) = _attn_fwd(qkvn, attn_sinks, bias, riders=[_gather_second([p_wo]), _gather_first([h_wu])])
    (y_conv, c_act), (g_wu, p_wd) = _conv_fwd(pc, cw, row(conv_dw_b), row(conv_norm_gain), row(conv_norm_bias),
                                              riders=[_gather_second([p_wu]), _gather_first([h_wd])])
    wo = g_wo.reshape(D_MODEL, D_MODEL)
    (x1, h2), (g_wd,) = _out_proj(x2, attn, c_act, wo, row(b_out), row(ffn_norm_gain), riders=[_gather_second([p_wd])])
    wu = g_wu.reshape(4, D_MODEL, FFN_CB)
    wd = g_wd.reshape(D_FF, D_MODEL)
    hg, hu, up_g, up_u, act = _ffn_up(h2, wu, fw, row(ffn_dw_b))
    dy, loss_cols = _ffn_down(act, wd, x1, tgt)

    split = lambda g: g.reshape(4, 2, g.shape[1] // 2, g.shape[2])
    dg, du, gfb_g, gfb_u = _ffn_bwd_act(dy, wd, up_g, up_u)
    dhg, dhu, gfw_g, gfw_u = _ffn_bwd_conv(dg, du, hg, hu, fw)
    gw_down = _grad_weight(act, dy, 1, FFN_CB, "grad_w_down")
    gw_up = _grad_weight(h2, dhg, 2, D_MODEL, "grad_w_up_gate", lead=4)
    gw_up = _grad_weight(h2, dhu, 2, D_MODEL, "grad_w_up_lin", into=gw_up, offset=2)
    early = [split(gw_up), split(gw_down.reshape(4, D_FF // 4, D_MODEL))]
    early_names = ["w_up", "w_down"]
    (dx1, g_ffn_gain), got = _ffn_bwd_in(dhg, dhu, wu, x1, dy, row(ffn_norm_gain), riders=[_swap_halves(early)])
    early_part = [_add_sibling(g, r, c_idx, "add_sibling_" + nm_) for g, r, nm_ in zip(early, got, early_names)]
    dmix, g_b_out = _out_proj_bwd(dx1, wo)
    gw_out = jnp.concatenate([_grad_weight(attn, dx1, 1, Q_COLS, "grad_w_out_attn")[0],
                              _grad_weight(c_act, dx1, 1, CONV_W, "grad_w_out_conv")[0]], axis=0)
    mid = [split(gw_out.reshape(4, D_MODEL // 4, D_MODEL))]
    (dpc, g_cw, g_cb, g_lng, g_lnb, gbin_c), got2 = _conv_bwd(dmix, y_conv, pc, cw, row(conv_norm_gain), row(conv_norm_bias),
                                                              riders=[_exchange_chips(early_part), _swap_halves(mid)])
    early_red = [_add_chips(p, r, chip_idx, "add_chips_" + nm_) for p, r, nm_ in zip(early_part, got2[:2], early_names)]
    mid_part = [_add_sibling(mid[0], got2[2], c_idx, "add_sibling_w_out")]
    (dqn, dcur, dprev, g_sink), got3 = _attn_bwd(qkvn, dmix, lse, attn_sinks, bias,
                                                 riders=[_join_halves(early_red), _exchange_chips(mid_part)])
    early_g = got3[:2]
    mid_red = [_add_chips(mid_part[0], got3[2], chip_idx, "add_chips_w_out")]
    (dpq, gbin_q, g_qk), mid_g = _qk_norm_bwd(dqn, dcur, dprev, pq, gq2, gk2, riders=[_join_halves(mid_red)])
    grad_x, g_mix_gain = _in_proj_bwd(dpq, dpc, wi, x2, dx1, row(mix_norm_gain))
    gw_in = jnp.concatenate([_grad_weight(h1, dpq, 1, D_MODEL, "grad_w_in_qkv")[0],
                             _grad_weight(h1, dpc, 1, D_MODEL, "grad_w_in_conv")[0]], axis=1)

    g_qk = g_qk.reshape(5, 2, HEAD_DIM)
    small = [g_mix_gain, jnp.concatenate([gbin_q, gbin_c], axis=1), g_qk[:4].sum(axis=(0, 1)), g_qk[4].sum(axis=0),
             g_sink[0, :N_HEADS], g_cb, g_lng, g_lnb, g_b_out, g_ffn_gain, jnp.concatenate([gfb_g, gfb_u], axis=1),
             loss_cols, g_cw[:CONV_K], jnp.concatenate([gfw_g[:3], gfw_u[:3]], axis=1)]
    _, tot = _gather_all(_pack(small))
    rep_names = ["mix_norm_gain", "b_in", "q_norm_gain", "k_norm_gain", "attn_sinks", "conv_dw_b", "conv_norm_gain",
                 "conv_norm_bias", "b_out", "ffn_norm_gain", "ffn_dw_b"]
    rep_w = [mix_norm_gain, b_in, q_norm_gain, k_norm_gain, attn_sinks, conv_dw_b, conv_norm_gain, conv_norm_bias,
             b_out, ffn_norm_gain, ffn_dw_b]
    rep_m = [m_mix_norm_gain, m_b_in, m_q_norm_gain, m_k_norm_gain, m_attn_sinks, m_conv_dw_b, m_conv_norm_gain,
             m_conv_norm_bias, m_b_out, m_ffn_norm_gain, m_ffn_dw_b]
    rep_v = [v_mix_norm_gain, v_b_in, v_q_norm_gain, v_k_norm_gain, v_attn_sinks, v_conv_dw_b, v_conv_norm_gain,
             v_conv_norm_bias, v_b_out, v_ffn_norm_gain, v_ffn_dw_b]
    shapes = [w.shape for w in rep_w] + [(D_MODEL,), (CONV_K, CONV_W), (3, 2 * D_FF)]
    tot_parts = _unpack(tot, shapes)
    loss = (0.5 / D_MODEL) * jnp.sum(tot_parts[len(rep_w)])
    g_cw_full, g_fw_full = tot_parts[len(rep_w) + 1], tot_parts[len(rep_w) + 2]
    n_rep_rows = _pack(rep_w).shape[0]
    rep_d, rep_nm, rep_nv = _adamw(_pack(rep_w), tot[:n_rep_rows], _pack(rep_m), _pack(rep_v), "adamw_small")
    rep_shapes = [w.shape for w in rep_w]
    res = {}
    for nm_, g_, d_, m_, v_ in zip(rep_names, tot_parts, _unpack(rep_d, rep_shapes), _unpack(rep_nm, rep_shapes),
                                   _unpack(rep_nv, rep_shapes)):
        res[nm_] = (g_, d_, m_, v_)

    g_cw_mine = lax.dynamic_slice_in_dim(g_cw_full, chip * (CONV_W // 4), CONV_W // 4, axis=1)
    g_fw_mine = lax.dynamic_slice_in_dim(g_fw_full, chip * (2 * D_FF // 4), 2 * D_FF // 4, axis=1)
    res["conv_dw_w"] = (g_cw_mine, *_adamw(conv_dw_w, g_cw_mine, m_conv_dw_w, v_conv_dw_w, "adamw_conv_dw_w"))
    res["ffn_dw_w"] = (g_fw_mine, *_adamw(ffn_dw_w, g_fw_mine, m_ffn_dw_w, v_ffn_dw_w, "adamw_ffn_dw_w"))

    gw_in4 = gw_in.reshape(D_MODEL, 4, IN_COLS // 4).transpose(1, 0, 2)
    late = [split(gw_in4)]
    late_names = ["w_in"]
    got = _run_riders([_swap_halves(late)], "swap_halves_late")
    late_part = [_add_sibling(g, r, c_idx, "add_sibling_" + nm_) for g, r, nm_ in zip(late, got, late_names)]
    got2 = _run_riders([_exchange_chips(late_part)], "exchange_chips_late")
    late_red = [_add_chips(p, r, chip_idx, "add_chips_" + nm_) for p, r, nm_ in zip(late_part, got2, late_names)]
    late_g = _run_riders([_join_halves(late_red)], "join_halves_late")
    names = ["w_in", "w_out", "w_up", "w_down"]
    shard_g = list(late_g) + list(mid_g) + list(early_g)
    for nm_, w_, g_, m_, v_ in zip(names, big, shard_g, [m_w_in, m_w_out, m_w_up, m_w_down], [v_w_in, v_w_out, v_w_up, v_w_down]):
        g_ = g_.reshape(w_.shape)
        res[nm_] = (g_, *_adamw(w_, g_, m_, v_, "adamw_" + nm_))

    order = ["mix_norm_gain", "w_in", "b_in", "q_norm_gain", "k_norm_gain", "attn_sinks", "conv_dw_w", "conv_dw_b",
             "conv_norm_gain", "conv_norm_bias", "w_out", "b_out", "ffn_norm_gain", "w_up", "ffn_dw_w", "ffn_dw_b", "w_down"]
    return (loss, grad_x.reshape(x.shape), *[res[n][0] for n in order], *[res[n][1] for n in order],
            *[res[n][2] for n in order], *[res[n][3] for n in order])
```
